```python
import jax
import jax.numpy as jnp
from jax import lax
import numpy as np

D_MODEL = 2048
BATCH = 16
SEQ = 256
DEPTH = 4
DEC_BATCH = 4
DEC_SEQ = 4096
PAST_LEN = 256

GRID_W = 64
EPS = 1e-6
N_MOD = 9
FFN_HIDDEN = 5504
MLA_HEADS = 8
Q_RANK = 512
KV_RANK = 512
NOPE_DIM = 128
ROPE_DIM = 64
V_DIM = 128
QK_DIM = NOPE_DIM + ROPE_DIM
ROPE_BASE = 10000.0
Q_BLOCK = 128
MLSTM_HEADS = 4
MLSTM_HEAD_DIM = 256
MLSTM_WIDTH = MLSTM_HEADS * MLSTM_HEAD_DIM
N_DIR = 2
CHUNK = 128
POOL_WINDOWS = (2, 4, 8, 16)
POOL_GROUPS = 4
POOL_GROUP_DIM = 256
POOL_WIDTH = POOL_GROUPS * POOL_GROUP_DIM
N_BRANCH = 3
BRANCH_WIDTH = 1024
SPLIT_SIZES = (Q_RANK, KV_RANK, ROPE_DIM, MLSTM_WIDTH, MLSTM_WIDTH, MLSTM_WIDTH, MLSTM_WIDTH,
               N_DIR * 2 * MLSTM_HEADS, POOL_WIDTH, N_BRANCH * D_MODEL)
IN_COLS = sum(SPLIT_SIZES)

kernel_name = 'hybrid_mla_mlstm_pool_flow_step'


def _rmsnorm(x, w):
    xf = x.astype(jnp.float32)
    y = xf * lax.rsqrt(jnp.mean(xf * xf, axis=-1, keepdims=True) + EPS)
    return (y * w.astype(jnp.float32)).astype(x.dtype)


def _modulate(x, shift, scale):
    return x * (1.0 + scale) + shift


def _swiglu(x, w_gu, w_down):
    gate, up = jnp.split(x @ w_gu, 2, axis=-1)
    return (jax.nn.silu(gate) * up) @ w_down


def _split_cols(p):
    parts, start = [], 0
    for size in SPLIT_SIZES:
        parts.append(p[..., start:start + size])
        start += size
    return parts


def _rotate(x, ang):
    x1, x2 = jnp.split(x, 2, axis=-1)
    cos, sin = jnp.cos(ang), jnp.sin(ang)
    return jnp.concatenate([x1 * cos - x2 * sin, x2 * cos + x1 * sin], axis=-1)


def _rope_2d(x, ang_r, ang_c):
    xf = x.astype(jnp.float32)
    half = ROPE_DIM // 2
    out = jnp.concatenate([_rotate(xf[..., :half], ang_r), _rotate(xf[..., half:], ang_c)], axis=-1)
    return out.astype(x.dtype)


def _mla_kv(ckv_n, kpe, w_ukv):
    B, S, _ = ckv_n.shape
    kv = (ckv_n @ w_ukv).reshape(B, S, MLA_HEADS, NOPE_DIM + V_DIM)
    k = jnp.concatenate([kv[..., :NOPE_DIM],
                         jnp.broadcast_to(kpe[:, :, None, :], (B, S, MLA_HEADS, ROPE_DIM))], axis=-1)
    return k, kv[..., NOPE_DIM:]


def _block_attention(q, k, v):
    B, T, H, dk = q.shape
    dv = v.shape[-1]
    n_blk = T // Q_BLOCK
    qb = jnp.moveaxis(q.reshape(B, n_blk, Q_BLOCK, H, dk), 1, 0)
    scale = dk ** -0.5

    def attend(q_blk):
        s = jnp.einsum('bqhd,bshd->bhqs', q_blk, k, preferred_element_type=jnp.float32) * scale
        p = jax.nn.softmax(s, axis=-1).astype(v.dtype)
        return jnp.einsum('bhqs,bshd->bqhd', p, v)

    o = lax.map(attend, qb)
    return jnp.moveaxis(o, 0, 1).reshape(B, T, H, dv)


def _mlstm_chunkwise(q, k, v, logi, logf, state):
    B, H, T, DH = q.shape
    n_chunks = T // CHUNK

    def chunks(a):
        a = a.astype(jnp.float32)
        return jnp.moveaxis(a.reshape(a.shape[:2] + (n_chunks, CHUNK) + a.shape[3:]), 2, 0)

    lower = jnp.tril(jnp.ones((CHUNK, CHUNK), dtype=bool))

    def step(carry, xs):
        C, n, m = carry
        qc, kc, vc, ic, fc = xs
        b = jnp.cumsum(fc, axis=-1)
        a = b + m[..., None]
        dmat = jnp.where(lower, b[..., :, None] - b[..., None, :] + ic[..., None, :], -jnp.inf)
        m_t = jnp.maximum(a, jnp.max(dmat, axis=-1))
        w_intra = jnp.exp(dmat - m_t[..., None])
        w_inter = jnp.exp(a - m_t)
        s = jnp.einsum('bhtk,bhsk->bhts', qc, kc) * w_intra
        num = (w_inter[..., None] * jnp.einsum('bhvk,bhtk->bhtv', C, qc)
               + jnp.einsum('bhts,bhsv->bhtv', s, vc))
        den = w_inter * jnp.einsum('bhk,bhtk->bht', n, qc) + jnp.sum(s, axis=-1)
        h = num / jnp.maximum(jnp.abs(den), jnp.exp(-m_t))[..., None]
        b_end = b[..., -1]
        g = b_end[..., None] - b + ic
        m_new = jnp.maximum(b_end + m, jnp.max(g, axis=-1))
        w_pos = jnp.exp(g - m_new[..., None])
        w_carry = jnp.exp(b_end + m - m_new)
        C_new = w_carry[..., None, None] * C + jnp.einsum('bhs,bhsv,bhsk->bhvk', w_pos, vc, kc)
        n_new = w_carry[..., None] * n + jnp.einsum('bhs,bhsk->bhk', w_pos, kc)
        return (C_new, n_new, m_new), h

    init = tuple(s.astype(jnp.float32) for s in state)
    final, hs = lax.scan(step, init, (chunks(q), chunks(k), chunks(v), chunks(logi), chunks(logf)))
    h = jnp.moveaxis(hs, 0, 2).reshape(B, H, T, DH)
    return h, final


def _multiscale_pool(u, pool_w, pool_scale):
    B, T, _ = u.shape
    ug = u.astype(jnp.float32).reshape(B, T, POOL_GROUPS, POOL_GROUP_DIM)
    csum = jnp.concatenate([jnp.zeros((B, 1, POOL_GROUPS, POOL_GROUP_DIM), jnp.float32),
                            jnp.cumsum(ug, axis=1)], axis=1)
    t = jnp.arange(T)
    outs = []
    for g, w in enumerate(POOL_WINDOWS):
        lo = jnp.clip(t - w // 2, 0, T)
        hi = jnp.clip(t - w // 2 + w, 0, T)
        mean = (csum[:, hi, g] - csum[:, lo, g]) / (hi - lo).astype(jnp.float32)[:, None]
        outs.append(mean - ug[:, :, g])
    pooled = jnp.stack(outs, axis=2)
    y = jnp.einsum('btgc,gcd->btgd', pooled, pool_w.astype(jnp.float32))
    return (y.reshape(B, T, POOL_WIDTH) * pool_scale.astype(jnp.float32)).astype(u.dtype)


def _layer(x, cond, lw, ctx=None, ang=None):
    (w_mod, b_mod, norm_w, ffn_w_gu, ffn_w_down, w_in, q_norm_w, kv_norm_w, w_uq, w_ukv,
     gate_b, m_norm_w, pool_w, pool_scale, w_branch, w_out) = lw
    latent = ctx is not None
    B, T, _ = x.shape
    mod = (jax.nn.silu(cond) @ w_mod + b_mod)[:, None, :]
    sh1, sc1, g1, sh2, sc2, g2, sh3, sc3, g3 = jnp.split(mod, N_MOD, axis=-1)

    x = x + 0.5 * g1 * _swiglu(_modulate(_rmsnorm(x, norm_w[0]), sh1, sc1), ffn_w_gu[0], ffn_w_down[0])

    h = _modulate(_rmsnorm(x, norm_w[1]), sh2, sc2)
    c_q, c_kv, k_pe, m_q, m_k, m_v, m_o, m_gates, u_pool, br_gates = _split_cols(h @ w_in)

    q = (_rmsnorm(c_q, q_norm_w) @ w_uq).reshape(B, T, MLA_HEADS, QK_DIM)
    ckv_n = _rmsnorm(c_kv, kv_norm_w)
    if latent:
        ang_r, ang_c = ang
        q = jnp.concatenate([q[..., :NOPE_DIM],
                             _rope_2d(q[..., NOPE_DIM:], ang_r[:, None], ang_c[:, None])], axis=-1)
        k_lat, v_lat = _mla_kv(ckv_n, _rope_2d(k_pe, ang_r, ang_c), w_ukv)
        k_ctx, v_ctx = _mla_kv(ctx[0].astype(x.dtype), ctx[1].astype(x.dtype), w_ukv)
        k = jnp.concatenate([k_ctx, k_lat], axis=1)
        v = jnp.concatenate([v_ctx, v_lat], axis=1)
    else:
        k, v = _mla_kv(ckv_n, k_pe, w_ukv)
    y_a = _block_attention(q, k, v).reshape(B, T, MLA_HEADS * V_DIM)

    def heads(a):
        return a.reshape(B, T, MLSTM_HEADS, MLSTM_HEAD_DIM).transpose(0, 2, 1, 3)
    mq, mk, mv = heads(m_q), heads(m_k) * (MLSTM_HEAD_DIM ** -0.5), heads(m_v)
    gts = m_gates.astype(jnp.float32).reshape(B, T, N_DIR, 2, MLSTM_HEADS) + gate_b.astype(jnp.float32)
    logi = jnp.transpose(gts[:, :, :, 0], (0, 2, 3, 1))
    logf = jax.nn.log_sigmoid(jnp.transpose(gts[:, :, :, 1], (0, 2, 3, 1)))
    if latent:
        init_f = (ctx[2][:, 0], ctx[3][:, 0], ctx[4][:, 0])
        init_b = (ctx[2][:, 1], ctx[3][:, 1], ctx[4][:, 1])
    else:
        zero = (jnp.zeros((B, MLSTM_HEADS, MLSTM_HEAD_DIM, MLSTM_HEAD_DIM), jnp.float32),
                jnp.zeros((B, MLSTM_HEADS, MLSTM_HEAD_DIM), jnp.float32),
                jnp.zeros((B, MLSTM_HEADS), jnp.float32))
        init_f, init_b = zero, zero
    h_f, st_f = _mlstm_chunkwise(mq, mk, mv, logi[:, 0], logf[:, 0], init_f)
    h_b, st_b = _mlstm_chunkwise(jnp.flip(mq, 2), jnp.flip(mk, 2), jnp.flip(mv, 2),
                                 jnp.flip(logi[:, 1], -1), jnp.flip(logf[:, 1], -1), init_b)
    h_m = (h_f + jnp.flip(h_b, 2)).transpose(0, 2, 1, 3)
    h_m = _rmsnorm(h_m, m_norm_w.reshape(MLSTM_HEADS, MLSTM_HEAD_DIM)).reshape(B, T, MLSTM_WIDTH)
    y_b = (jax.nn.sigmoid(m_o.astype(jnp.float32)) * h_m).astype(x.dtype)

    y_c = _multiscale_pool(u_pool, pool_w, pool_scale)

    gates = jax.nn.sigmoid(br_gates).reshape(B, T, N_BRANCH, D_MODEL)
    merged = (gates[:, :, 0] * (y_a @ w_branch[0])
              + gates[:, :, 1] * (y_b @ w_branch[1])
              + gates[:, :, 2] * (y_c @ w_branch[2]))
    x = x + g2 * (merged @ w_out)

    x = x + 0.5 * g3 * _swiglu(_modulate(_rmsnorm(x, norm_w[2]), sh3, sc3), ffn_w_gu[1], ffn_w_down[1])
    if latent:
        return x
    return x, (ckv_n, k_pe,
               jnp.stack([st_f[0], st_b[0]], axis=1),
               jnp.stack([st_f[1], st_b[1]], axis=1),
               jnp.stack([st_f[2], st_b[2]], axis=1))


def setup_inputs(seed: int = 0) -> dict:
    key = jax.random.key(seed)
    ks = jax.random.split(key, 26)

    def nrm(k, shape, s):
        return jax.random.normal(k, shape, jnp.float32) * s

    H, DH = MLSTM_HEADS, MLSTM_HEAD_DIM
    gate_base = jnp.stack([jnp.zeros((H,), jnp.float32),
                           jnp.linspace(3.0, 6.0, H, dtype=jnp.float32)])
    return {
        'x_prompt': nrm(ks[0], (BATCH, SEQ, D_MODEL), 1.0),
        'x_sample': nrm(ks[1], (DEC_BATCH, DEC_SEQ, D_MODEL), 1.0),
        'cache_ckv': nrm(ks[2], (DEC_BATCH, DEPTH, PAST_LEN, KV_RANK), 1.0),
        'cache_kpe': nrm(ks[3], (DEC_BATCH, DEPTH, PAST_LEN, ROPE_DIM), 1.0),
        'state_C': nrm(ks[4], (DEC_BATCH, DEPTH, N_DIR, H, DH, DH), 0.05),
        'state_n': nrm(ks[5], (DEC_BATCH, DEPTH, N_DIR, H, DH), 0.1),
        'state_m': nrm(ks[6], (DEC_BATCH, DEPTH, N_DIR, H), 0.5),
        'c': nrm(ks[7], (DEC_BATCH, D_MODEL), 1.0),
        'c_ctx': nrm(ks[8], (D_MODEL,), 1.0),
        'w_mod': nrm(ks[9], (DEPTH, D_MODEL, N_MOD * D_MODEL), 0.5 * D_MODEL ** -0.5),
        'b_mod': nrm(ks[10], (DEPTH, N_MOD * D_MODEL), 0.02),
        'norm_w': 1.0 + nrm(ks[11], (DEPTH, 3, D_MODEL), 0.02),
        'ffn_w_gu': nrm(ks[12], (DEPTH, 2, D_MODEL, 2 * FFN_HIDDEN), D_MODEL ** -0.5),
        'ffn_w_down': nrm(ks[13], (DEPTH, 2, FFN_HIDDEN, D_MODEL), FFN_HIDDEN ** -0.5),
        'w_in': nrm(ks[14], (DEPTH, D_MODEL, IN_COLS), D_MODEL ** -0.5),
        'q_norm_w': 1.0 + nrm(ks[15], (DEPTH, Q_RANK), 0.02),
        'kv_norm_w': 1.0 + nrm(ks[16], (DEPTH, KV_RANK), 0.02),
        'w_uq': nrm(ks[17], (DEPTH, Q_RANK, MLA_HEADS * QK_DIM), Q_RANK ** -0.5),
        'w_ukv': nrm(ks[18], (DEPTH, KV_RANK, MLA_HEADS * (NOPE_DIM + V_DIM)), KV_RANK ** -0.5),
        'mlstm_gate_b': gate_base + nrm(ks[19], (DEPTH, N_DIR, 2, H), 0.1),
        'mlstm_norm_w': 1.0 + nrm(ks[20], (DEPTH, MLSTM_WIDTH), 0.02),
        'pool_w': nrm(ks[21], (DEPTH, POOL_GROUPS, POOL_GROUP_DIM, POOL_GROUP_DIM), POOL_GROUP_DIM ** -0.5),
        'pool_scale': 1.0 + nrm(ks[22], (DEPTH, POOL_WIDTH), 0.1),
        'w_branch': nrm(ks[23], (DEPTH, N_BRANCH, BRANCH_WIDTH, D_MODEL), BRANCH_WIDTH ** -0.5),
        'w_out': nrm(ks[24], (DEPTH, D_MODEL, D_MODEL), D_MODEL ** -0.5),
        'final_norm_w': 1.0 + nrm(ks[25], (D_MODEL,), 0.02),
    }


def reference(x_prompt, x_sample, cache_ckv, cache_kpe, state_C, state_n, state_m, c, c_ctx,
              w_mod, b_mod, norm_w, ffn_w_gu, ffn_w_down, w_in, q_norm_w, kv_norm_w, w_uq, w_ukv,
              mlstm_gate_b, mlstm_norm_w, pool_w, pool_scale, w_branch, w_out, final_norm_w):
    def layer_weights(l):
        return (w_mod[l], b_mod[l], norm_w[l], ffn_w_gu[l], ffn_w_down[l], w_in[l], q_norm_w[l],
                kv_norm_w[l], w_uq[l], w_ukv[l], mlstm_gate_b[l], mlstm_norm_w[l], pool_w[l],
                pool_scale[l], w_branch[l], w_out[l])

    xp = x_prompt
    ckv_l, kpe_l, C_l, n_l, m_l = [], [], [], [], []
    for l in range(DEPTH):
        xp, (ckv, kpe, C_, n_, m_) = _layer(xp, c_ctx[None, :], layer_weights(l))
        ckv_l.append(ckv)
        kpe_l.append(kpe)
        C_l.append(C_)
        n_l.append(n_)
        m_l.append(m_)
    y_prompt = _rmsnorm(xp, final_norm_w)
    new_ckv = jnp.stack(ckv_l, axis=1)
    new_kpe = jnp.stack(kpe_l, axis=1)
    new_C = jnp.stack(C_l, axis=1)
    new_n = jnp.stack(n_l, axis=1)
    new_m = jnp.stack(m_l, axis=1)

    rows = x_sample.shape[1] // GRID_W
    t = jnp.arange(rows * GRID_W)
    row = (t // GRID_W).astype(jnp.float32)
    col = (t % GRID_W).astype(jnp.float32)
    n_freq = ROPE_DIM // 4
    inv_freq = jnp.power(ROPE_BASE, -jnp.arange(n_freq, dtype=jnp.float32) / n_freq)
    ang = (row[:, None] * inv_freq, col[:, None] * inv_freq)
    xs = x_sample
    for l in range(DEPTH):
        ctx = (cache_ckv[:, l], cache_kpe[:, l], state_C[:, l], state_n[:, l], state_m[:, l])
        xs = _layer(xs, c, layer_weights(l), ctx=ctx, ang=ang)
    y_sample = _rmsnorm(xs, final_norm_w)
    return (y_prompt, y_sample, new_ckv, new_kpe, new_C, new_n, new_m)
```

```python
import functools

import numpy as np
import jax
import jax.numpy as jnp
from jax import lax
from jax.experimental import pallas as pl
from jax.experimental.pallas import tpu as pltpu

GRID_W = 64
EPS = 1e-6
N_MOD = 9
MLA_HEADS = 8
NOPE_DIM = 128
ROPE_DIM = 64
V_DIM = 128
QK_DIM = NOPE_DIM + ROPE_DIM
ROPE_BASE = 10000.0
MLSTM_HEADS = 4
N_DIR = 2
CHUNK = 128
POOL_WINDOWS = (2, 4, 8, 16)
POOL_GROUPS = 4
N_BRANCH = 3

LANES = 128
MXU_WIDTH = 256
VMEM_LIMIT_MB = 56
COND_ROWS = 8

F32 = jnp.float32
BF16 = jnp.bfloat16


def _params(sem):
    return pltpu.CompilerParams(dimension_semantics=sem, vmem_limit_bytes=VMEM_LIMIT_MB << 20)


def _round_up(n, m):
    return (n + m - 1) // m * m


def _lane_tile(n, cap):
    t = cap - cap % LANES
    while n % t:
        t -= LANES
    return t


def _sigmoid(x):
    return 1.0 / (1.0 + jnp.exp(-x))


def _log_sigmoid(x):
    return -(jnp.maximum(-x, 0.0) + jnp.log1p(jnp.exp(-jnp.abs(x))))


def _rms(x, w):
    return x * lax.rsqrt(jnp.mean(x * x, axis=-1, keepdims=True) + EPS) * w


def _dot(a, b):
    return jnp.dot(a, b, preferred_element_type=F32)


def _dot_nt(a, b):
    return lax.dot_general(a, b, (((1,), (1,)), ((), ())), preferred_element_type=F32)


def _dot_tn(a, b):
    return lax.dot_general(a, b, (((0,), (0,)), ((), ())), preferred_element_type=F32)


def _mod_kernel(c_ref, w_ref, b_ref, o_ref):
    c = c_ref[...]
    a = (c * _sigmoid(c)).astype(BF16)
    o_ref[...] = _dot(a, w_ref[...].astype(BF16)) + b_ref[...]


def _mod_all(cond, w_mod, b_mod):
    depth, d, nd = w_mod.shape
    tn = _lane_tile(nd, 1024)
    return pl.pallas_call(
        _mod_kernel,
        grid=(depth, nd // tn),
        in_specs=[pl.BlockSpec((COND_ROWS, d), lambda l, j: (0, 0)),
                  pl.BlockSpec((None, d, tn), lambda l, j: (l, 0, j)),
                  pl.BlockSpec((None, 1, tn), lambda l, j: (l, 0, j))],
        out_specs=pl.BlockSpec((None, COND_ROWS, tn), lambda l, j: (l, 0, j)),
        out_shape=jax.ShapeDtypeStruct((depth, COND_ROWS, nd), F32),
        compiler_params=_params(("parallel", "parallel")),
        name="adaln_mod",
    )(cond, w_mod, b_mod.reshape(depth, 1, nd))


class _Rows:
    def __init__(self, n_ctx, t_ctx, n_lat, t_lat):
        self.n_ctx, self.t_ctx, self.n_lat, self.t_lat = n_ctx, t_ctx, n_lat, t_lat
        self.n = n_ctx + n_lat

    def mod_row(self, row):
        return jnp.where(row < self.n_ctx, 0, 1 + (row - self.n_ctx) // self.t_lat)

    def tile(self, cap):
        t = min(cap, self.n_ctx, self.t_lat)
        assert self.n_ctx % t == 0 and self.t_lat % t == 0
        return t


def _mod_spec(rows, tm, layer, k, d):
    return pl.BlockSpec((None, None, None, 1, d),
                        lambda i, j: (layer, rows.mod_row(i * tm), k, 0, 0))


def _norm_mod_to(h_ref, x_ref, nw_ref, sh_ref, sc_ref):
    y = _rms(x_ref[...], nw_ref[...])
    h_ref[...] = (y * (1.0 + sc_ref[...]) + sh_ref[...]).astype(h_ref.dtype)


def _ffn_kernel(x_ref, sh_ref, sc_ref, g_ref, nw_ref, wg_ref, wu_ref, wd_ref, o_ref, h_ref, acc_ref):
    j = pl.program_id(1)

    @pl.when(j == 0)
    def _():
        _norm_mod_to(h_ref, x_ref, nw_ref, sh_ref, sc_ref)

    h = h_ref[...]
    g = _dot(h, wg_ref[...])
    u = _dot(h, wu_ref[...])
    a = (g * _sigmoid(g) * u).astype(BF16)
    d = _dot(a, wd_ref[...])

    @pl.when(j == 0)
    def _():
        acc_ref[...] = d

    @pl.when(j > 0)
    def _():
        acc_ref[...] += d

    @pl.when(j == pl.num_programs(1) - 1)
    def _():
        o_ref[...] = x_ref[...] + 0.5 * g_ref[...] * acc_ref[...]


def _ffn(x, mod, norm_w, w_gu, w_down, rows, layer, which):
    n, d = x.shape
    hp = w_down.shape[2]
    tm = rows.tile(512)
    th = 512 if hp % 512 == 0 else MXU_WIDTH
    k0 = 0 if which == 0 else 6
    return pl.pallas_call(
        _ffn_kernel,
        grid=(n // tm, hp // th),
        in_specs=[pl.BlockSpec((tm, d), lambda i, j: (i, 0)),
                  _mod_spec(rows, tm, layer, k0, d),
                  _mod_spec(rows, tm, layer, k0 + 1, d),
                  _mod_spec(rows, tm, layer, k0 + 2, d),
                  pl.BlockSpec((None, None, 1, d), lambda i, j: (layer, 2 * which, 0, 0)),
                  pl.BlockSpec((None, None, None, d, th), lambda i, j: (layer, which, 0, 0, j)),
                  pl.BlockSpec((None, None, None, d, th), lambda i, j: (layer, which, 1, 0, j)),
                  pl.BlockSpec((None, None, th, d), lambda i, j: (layer, which, j, 0))],
        out_specs=pl.BlockSpec((tm, d), lambda i, j: (i, 0)),
        out_shape=jax.ShapeDtypeStruct((n, d), F32),
        scratch_shapes=[pltpu.VMEM((tm, d), BF16), pltpu.VMEM((tm, d), F32)],
        compiler_params=_params(("parallel", "arbitrary")),
        name="ffn_half_step",
    )(x, mod, mod, mod, norm_w, w_gu, w_gu, w_down)


def _inproj_kernel(x_ref, sh_ref, sc_ref, nw_ref, w_ref, o_ref, h_ref):
    @pl.when(pl.program_id(1) == 0)
    def _():
        _norm_mod_to(h_ref, x_ref, nw_ref, sh_ref, sc_ref)

    o_ref[...] = _dot(h_ref[...], w_ref[...]).astype(o_ref.dtype)


def _inproj(x, mod, norm_w, w, rows, layer, out_dtype, tn):
    n, d = x.shape
    cols = w.shape[2]
    tm = rows.tile(512)
    return pl.pallas_call(
        _inproj_kernel,
        grid=(n // tm, cols // tn),
        in_specs=[pl.BlockSpec((tm, d), lambda i, j: (i, 0)),
                  _mod_spec(rows, tm, layer, 3, d),
                  _mod_spec(rows, tm, layer, 4, d),
                  pl.BlockSpec((None, None, 1, d), lambda i, j: (layer, 1, 0, 0)),
                  pl.BlockSpec((None, d, tn), lambda i, j: (layer, 0, j))],
        out_specs=pl.BlockSpec((tm, tn), lambda i, j: (i, j)),
        out_shape=jax.ShapeDtypeStruct((n, cols), out_dtype),
        scratch_shapes=[pltpu.VMEM((tm, d), BF16)],
        compiler_params=_params(("parallel", "arbitrary")),
        name="mixer_in_proj",
    )(x, mod, mod, norm_w, w)


def _mla_prep_kernel(rope, q_rank, kv_rank, p_ref, qnw_ref, kvnw_ref, wq_ref, wkv_ref, *rest):
    if rope:
        cos_ref, sin_ref, q_ref, k_ref, v_ref, ckv_ref = rest
    else:
        q_ref, k_ref, v_ref, ckv_ref = rest
    heads = MLA_HEADS
    p = p_ref[...]
    c_q = p[:, :q_rank]
    c_kv = p[:, q_rank:q_rank + kv_rank]
    o = q_rank + kv_rank
    k_pe = p[:, o:o + ROPE_DIM]
    k_pe_sw = p[:, o + ROPE_DIM:o + 2 * ROPE_DIM]

    qa = _dot(_rms(c_q, qnw_ref[...]).astype(BF16), wq_ref[...])
    ckv_n = _rms(c_kv, kvnw_ref[...])
    ckv_ref[...] = ckv_n
    kv = _dot(ckv_n.astype(BF16), wkv_ref[...])
    scale = QK_DIM ** -0.5
    if rope:
        cos = cos_ref[...]
        sin = sin_ref[...]
        k_pe = k_pe * cos + k_pe_sw * sin
    pe0 = heads * NOPE_DIM
    sw0 = pe0 + heads * ROPE_DIM
    for h in range(heads):
        q_pe = qa[:, pe0 + h * ROPE_DIM:pe0 + (h + 1) * ROPE_DIM]
        if rope:
            q_pe = q_pe * cos + qa[:, sw0 + h * ROPE_DIM:sw0 + (h + 1) * ROPE_DIM] * sin
        q_ref[h, :, :NOPE_DIM] = (qa[:, h * NOPE_DIM:(h + 1) * NOPE_DIM] * scale).astype(BF16)
        q_ref[h, :, NOPE_DIM:] = (q_pe * scale).astype(BF16)
        k_ref[h, :, :NOPE_DIM] = kv[:, h * NOPE_DIM:(h + 1) * NOPE_DIM].astype(BF16)
        k_ref[h, :, NOPE_DIM:] = k_pe.astype(BF16)
        v0 = heads * NOPE_DIM + h * V_DIM
        v_ref[h] = kv[:, v0:v0 + V_DIM].astype(BF16)


def _mla_prep(p32, q_norm_w, kv_norm_w, wq, wkv, layer, row0, batch, t, rope_tabs):
    ws = p32.shape[1]
    q_rank, kv_rank = q_norm_w.shape[-1], kv_norm_w.shape[-1]
    tm = min(256, t)
    nt = t // tm
    heads = MLA_HEADS
    rope = rope_tabs is not None
    in_specs = [pl.BlockSpec((tm, ws), lambda b, i: (row0 // tm + b * nt + i, 0)),
                pl.BlockSpec((None, 1, q_rank), lambda b, i: (layer, 0, 0)),
                pl.BlockSpec((None, 1, kv_rank), lambda b, i: (layer, 0, 0)),
                pl.BlockSpec((None,) + wq.shape[1:], lambda b, i: (layer, 0, 0)),
                pl.BlockSpec((None,) + wkv.shape[1:], lambda b, i: (layer, 0, 0))]
    args = [p32, q_norm_w, kv_norm_w, wq, wkv]
    if rope:
        in_specs += [pl.BlockSpec((tm, ROPE_DIM), lambda b, i: (i, 0))] * 2
        args += list(rope_tabs)
    return pl.pallas_call(
        functools.partial(_mla_prep_kernel, rope, q_rank, kv_rank),
        grid=(batch, nt),
        in_specs=in_specs,
        out_specs=[pl.BlockSpec((None, heads, tm, QK_DIM), lambda b, i: (b, 0, i, 0)),
                   pl.BlockSpec((None, heads, tm, QK_DIM), lambda b, i: (b, 0, i, 0)),
                   pl.BlockSpec((None, heads, tm, V_DIM), lambda b, i: (b, 0, i, 0)),
                   pl.BlockSpec((tm, kv_rank), lambda b, i: (b * nt + i, 0))],
        out_shape=[jax.ShapeDtypeStruct((batch, heads, t, QK_DIM), BF16),
                   jax.ShapeDtypeStruct((batch, heads, t, QK_DIM), BF16),
                   jax.ShapeDtypeStruct((batch, heads, t, V_DIM), BF16),
                   jax.ShapeDtypeStruct((batch * t, kv_rank), F32)],
        compiler_params=_params(("parallel", "parallel")),
        name="mla_prep",
    )(*args)


def _cache_kv_kernel(ckv_ref, kpe_ref, wkv_ref, k_ref, v_ref):
    heads = MLA_HEADS
    kv = _dot(ckv_ref[...].astype(BF16), wkv_ref[...])
    k_pe = kpe_ref[...].astype(BF16)
    for h in range(heads):
        k_ref[h, :, :NOPE_DIM] = kv[:, h * NOPE_DIM:(h + 1) * NOPE_DIM].astype(BF16)
        k_ref[h, :, NOPE_DIM:] = k_pe
        v0 = heads * NOPE_DIM + h * V_DIM
        v_ref[h] = kv[:, v0:v0 + V_DIM].astype(BF16)


def _cache_kv(cache_ckv, cache_kpe, wkv):
    batch, depth, past, kv_rank = cache_ckv.shape
    heads = MLA_HEADS
    return pl.pallas_call(
        _cache_kv_kernel,
        grid=(batch, depth),
        in_specs=[pl.BlockSpec((None, None, past, kv_rank), lambda b, l: (b, l, 0, 0)),
                  pl.BlockSpec((None, None, past, ROPE_DIM), lambda b, l: (b, l, 0, 0)),
                  pl.BlockSpec((None,) + wkv.shape[1:], lambda b, l: (l, 0, 0))],
        out_specs=[pl.BlockSpec((None, None, heads, past, QK_DIM), lambda b, l: (b, l, 0, 0, 0)),
                   pl.BlockSpec((None, None, heads, past, V_DIM), lambda b, l: (b, l, 0, 0, 0))],
        out_shape=[jax.ShapeDtypeStruct((batch, depth, heads, past, QK_DIM), BF16),
                   jax.ShapeDtypeStruct((batch, depth, heads, past, V_DIM), BF16)],
        compiler_params=_params(("parallel", "parallel")),
        name="mla_cache_kv",
    )(cache_ckv, cache_kpe, wkv)


def _attn_kernel(has_cache, q_ref, k_ref, v_ref, *rest):
    if has_cache:
        kc_ref, vc_ref, o_ref = rest
    else:
        (o_ref,) = rest
    q = q_ref[...]
    s = _dot_nt(q, k_ref[...])
    m = jnp.max(s, axis=-1, keepdims=True)
    if has_cache:
        sc = _dot_nt(q, kc_ref[...])
        m = jnp.maximum(m, jnp.max(sc, axis=-1, keepdims=True))
    p = jnp.exp(s - m)
    den = jnp.sum(p, axis=-1, keepdims=True)
    o = _dot(p.astype(BF16), v_ref[...])
    if has_cache:
        pc = jnp.exp(sc - m)
        den = den + jnp.sum(pc, axis=-1, keepdims=True)
        o = o + _dot(pc.astype(BF16), vc_ref[...])
    o_ref[...] = (o / den).astype(o_ref.dtype)


def _attention(q, k, v, cache, layer):
    batch, heads, t, _ = q.shape
    tq = min(256, t)
    nq = t // tq
    has_cache = cache is not None
    in_specs = [pl.BlockSpec((None, None, tq, QK_DIM), lambda b, h, i: (b, h, i, 0)),
                pl.BlockSpec((None, None, t, QK_DIM), lambda b, h, i: (b, h, 0, 0)),
                pl.BlockSpec((None, None, t, V_DIM), lambda b, h, i: (b, h, 0, 0))]
    args = [q, k, v]
    if has_cache:
        past = cache[0].shape[3]
        in_specs += [pl.BlockSpec((None, None, None, past, QK_DIM), lambda b, h, i: (b, layer, h, 0, 0)),
                     pl.BlockSpec((None, None, None, past, V_DIM), lambda b, h, i: (b, layer, h, 0, 0))]
        args += list(cache)
    return pl.pallas_call(
        functools.partial(_attn_kernel, has_cache),
        grid=(batch, heads, nq),
        in_specs=in_specs,
        out_specs=pl.BlockSpec((tq, V_DIM), lambda b, h, i: (b * nq + i, h)),
        out_shape=jax.ShapeDtypeStruct((batch * t, heads * V_DIM), BF16),
        compiler_params=_params(("parallel", "parallel", "parallel")),
        name="mla_attention",
    )(*args)


def _mlstm_kernel(has_init, dh, *refs):
    (qf_ref, kf_ref, vf_ref, qb_ref, kb_ref, vb_ref, gf_ref, gb_ref, gtf_ref, gtb_ref,
     brow_ref, bcol_ref) = refs[:12]
    refs = refs[12:]
    if has_init:
        c0_ref, n0_ref, m0_ref = refs[:3]
        refs = refs[3:]
    hf_ref, hb_ref, c_ref, n_ref, m_ref = refs
    heads = MLSTM_HEADS
    n_gate = N_DIR * 2 * heads
    step = pl.program_id(1)

    @pl.when(step == 0)
    def _():
        if has_init:
            c_ref[...] = c0_ref[...]
            n_ref[...] = n0_ref[...]
            m_ref[...] = m0_ref[...]
        else:
            c_ref[...] = jnp.zeros_like(c_ref)
            n_ref[...] = jnp.zeros_like(n_ref)
            m_ref[...] = jnp.zeros_like(m_ref)

    tok_t = lax.broadcasted_iota(jnp.int32, (CHUNK, CHUNK), 0)
    tok_s = lax.broadcasted_iota(jnp.int32, (CHUNK, CHUNK), 1)
    k_scale = dh ** -0.5
    for d in range(N_DIR):
        q_ref, k_ref, v_ref, g_ref, gt_ref, h_ref = (
            (qf_ref, kf_ref, vf_ref, gf_ref, gtf_ref, hf_ref) if d == 0 else
            (qb_ref, kb_ref, vb_ref, gb_ref, gtb_ref, hb_ref))
        seen = (tok_s <= tok_t) if d == 0 else (tok_s >= tok_t)
        seen_f = seen.astype(F32)
        seen_t_f = ((tok_t <= tok_s) if d == 0 else (tok_t >= tok_s)).astype(F32)
        pre_col = g_ref[:, :n_gate] + brow_ref[...]
        pre_row = gt_ref[...] + bcol_ref[...]
        logf_col = _log_sigmoid(pre_col)
        logf_row = _log_sigmoid(pre_row)
        for h in range(heads):
            ci = d * 2 * heads + h
            cf = ci + heads
            i_col = pre_col[:, ci:ci + 1]
            i_row = pre_row[ci:ci + 1, :]
            f_col = logf_col[:, cf:cf + 1]
            f_row = logf_row[cf:cf + 1, :]
            sid = d * heads + h
            m_prev = m_ref[:, sid:sid + 1]
            b_col = jnp.sum(seen_f * f_row, axis=1, keepdims=True)
            b_row = jnp.sum(seen_t_f * f_col, axis=0, keepdims=True)
            b_end = jnp.sum(f_row, axis=1, keepdims=True)
            a_col = b_col + m_prev
            dmat = jnp.where(seen, b_col - b_row + i_row, -jnp.inf)
            m_t = jnp.maximum(a_col, jnp.max(dmat, axis=1, keepdims=True))
            w_intra = jnp.exp(dmat - m_t)
            w_inter = jnp.exp(a_col - m_t)

            q = q_ref[:, h * dh:(h + 1) * dh]
            k = (k_ref[:, h * dh:(h + 1) * dh].astype(F32) * k_scale)
            k_bf = k.astype(BF16)
            v = v_ref[:, h * dh:(h + 1) * dh]
            c_prev = c_ref[d, h]
            n_prev = n_ref[d, h]

            s = _dot_nt(q, k_bf) * w_intra
            num = w_inter * _dot_nt(q, c_prev.astype(BF16)) + _dot(s.astype(BF16), v)
            nq = jnp.sum(q.astype(F32) * n_prev, axis=1, keepdims=True)
            den = w_inter * nq + jnp.sum(s, axis=1, keepdims=True)
            h_ref[:, h * dh:(h + 1) * dh] = num / jnp.maximum(jnp.abs(den), jnp.exp(-m_t))

            g_row = b_end - b_row + i_row
            g_col = b_end - b_col + i_col
            m_new = jnp.maximum(b_end + m_prev, jnp.max(g_row, axis=1, keepdims=True))
            w_pos = jnp.exp(g_col - m_new)
            w_carry = jnp.exp(b_end + m_prev - m_new)
            c_ref[d, h] = w_carry * c_prev + _dot_tn((v.astype(F32) * w_pos).astype(BF16), k_bf)
            n_ref[d, h] = w_carry * n_prev + jnp.sum(k * w_pos, axis=0, keepdims=True)
            m_ref[:, sid:sid + 1] = m_new


def _mlstm(pb, p32, gates_t, gate_b, state, layer, row0, batch, t, gate_blk, dh):
    heads = MLSTM_HEADS
    n_gate = N_DIR * 2 * heads
    nc = t // CHUNK
    blk0 = row0 // CHUNK
    w = heads * dh

    def tok(col):
        fwd = pl.BlockSpec((CHUNK, w), lambda b, c: (blk0 + b * nc + c, col))
        bwd = pl.BlockSpec((CHUNK, w), lambda b, c: (blk0 + b * nc + nc - 1 - c, col))
        return fwd, bwd

    (qf, qb), (kf, kb), (vf, vb) = tok(0), tok(1), tok(2)
    in_specs = [qf, kf, vf, qb, kb, vb,
                pl.BlockSpec((CHUNK, LANES), lambda b, c: (blk0 + b * nc + c, gate_blk)),
                pl.BlockSpec((CHUNK, LANES), lambda b, c: (blk0 + b * nc + nc - 1 - c, gate_blk)),
                pl.BlockSpec((n_gate, CHUNK), lambda b, c: (0, blk0 + b * nc + c)),
                pl.BlockSpec((n_gate, CHUNK), lambda b, c: (0, blk0 + b * nc + nc - 1 - c)),
                pl.BlockSpec((None, 1, n_gate), lambda b, c: (layer, 0, 0)),
                pl.BlockSpec((None, n_gate, 1), lambda b, c: (layer, 0, 0))]
    args = [pb] * 6 + [p32, p32, gates_t, gates_t,
                       gate_b.reshape(-1, 1, n_gate), gate_b.reshape(-1, n_gate, 1)]
    has_init = state is not None
    if has_init:
        c0, n0, m0 = state
        depth = c0.shape[1]
        in_specs += [pl.BlockSpec((None, None, N_DIR, heads, dh, dh), lambda b, c: (b, layer, 0, 0, 0, 0)),
                     pl.BlockSpec((None, None, N_DIR, heads, 1, dh), lambda b, c: (b, layer, 0, 0, 0, 0)),
                     pl.BlockSpec((None, None, 1, N_DIR * heads), lambda b, c: (b, layer, 0, 0))]
        args += [c0, n0.reshape(batch, depth, N_DIR, heads, 1, dh),
                 m0.reshape(batch, depth, 1, N_DIR * heads)]
    return pl.pallas_call(
        functools.partial(_mlstm_kernel, has_init, dh),
        grid=(batch, nc),
        in_specs=in_specs,
        out_specs=[pl.BlockSpec((CHUNK, w), lambda b, c: (b * nc + c, 0)),
                   pl.BlockSpec((CHUNK, w), lambda b, c: (b * nc + nc - 1 - c, 0)),
                   pl.BlockSpec((None, N_DIR, heads, dh, dh), lambda b, c: (b, 0, 0, 0, 0)),
                   pl.BlockSpec((None, N_DIR, heads, 1, dh), lambda b, c: (b, 0, 0, 0, 0)),
                   pl.BlockSpec((None, 1, N_DIR * heads), lambda b, c: (b, 0, 0))],
        out_shape=[jax.ShapeDtypeStruct((batch * t, w), F32),
                   jax.ShapeDtypeStruct((batch * t, w), F32),
                   jax.ShapeDtypeStruct((batch, N_DIR, heads, dh, dh), F32),
                   jax.ShapeDtypeStruct((batch, N_DIR, heads, 1, dh), F32),
                   jax.ShapeDtypeStruct((batch, 1, N_DIR * heads), F32)],
        compiler_params=_params(("parallel", "arbitrary")),
        name="mlstm_scan",
    )(*args)


def _mlstm_post_kernel(dh, hf_ref, hb_ref, o_ref, w_ref, y_ref):
    hm = hf_ref[...] + hb_ref[...]
    gate = _sigmoid(o_ref[...].astype(F32))
    w = w_ref[...]
    for h in range(MLSTM_HEADS):
        sl = slice(h * dh, (h + 1) * dh)
        y_ref[:, sl] = (gate[:, sl] * _rms(hm[:, sl], w[:, sl])).astype(y_ref.dtype)


def _mlstm_post(h_f, h_b, pb, m_norm_w, layer, row0, dh):
    n, w = h_f.shape
    tm = min(256, n)
    return pl.pallas_call(
        functools.partial(_mlstm_post_kernel, dh),
        grid=(n // tm,),
        in_specs=[pl.BlockSpec((tm, w), lambda i: (i, 0)),
                  pl.BlockSpec((tm, w), lambda i: (i, 0)),
                  pl.BlockSpec((tm, w), lambda i: (row0 // tm + i, 3)),
                  pl.BlockSpec((None, 1, w), lambda i: (layer, 0, 0))],
        out_specs=pl.BlockSpec((tm, w), lambda i: (i, 0)),
        out_shape=jax.ShapeDtypeStruct((n, w), BF16),
        compiler_params=_params(("parallel",)),
        name="mlstm_post",
    )(h_f, h_b, pb, m_norm_w)


POOL_TILE = 256


def _pool_bands():
    t = np.arange(POOL_TILE)[:, None]
    bands = np.zeros((POOL_GROUPS, 3, POOL_TILE, POOL_TILE), np.float32)
    for g, win in enumerate(POOL_WINDOWS):
        for part in range(3):
            s = np.arange(POOL_TILE)[None, :] + (part - 1) * POOL_TILE
            bands[g, part] = (s >= t - win // 2) & (s < t - win // 2 + win)
    return jnp.asarray(bands, BF16)


def _pool_kernel(t_seq, gd, up_ref, um_ref, un_ref, band_ref, pw_ref, ps_ref, y_ref):
    j = pl.program_id(1)
    has_prev = (j > 0).astype(F32)
    has_next = (j < pl.num_programs(1) - 1).astype(F32)
    tile = um_ref.shape[0]
    pos = j * tile + lax.broadcasted_iota(jnp.int32, (tile, 1), 0)
    for g, win in enumerate(POOL_WINDOWS):
        sl = slice(g * gd, (g + 1) * gd)
        u = um_ref[:, sl]
        acc = (_dot(band_ref[g, 1], u)
               + has_prev * _dot(band_ref[g, 0], up_ref[:, sl])
               + has_next * _dot(band_ref[g, 2], un_ref[:, sl]))
        lo = jnp.clip(pos - win // 2, 0, t_seq)
        hi = jnp.clip(pos - win // 2 + win, 0, t_seq)
        pooled = acc / (hi - lo).astype(F32) - u.astype(F32)
        y = _dot(pooled.astype(BF16), pw_ref[g]) * ps_ref[:, sl]
        y_ref[:, sl] = y.astype(y_ref.dtype)


def _pool(pb, bands, pool_w, pool_scale, layer, row0, batch, t):
    gd = pool_w.shape[-1]
    w = POOL_GROUPS * gd
    tile = POOL_TILE
    assert t % tile == 0
    nt = t // tile
    blk0 = row0 // tile

    def u_spec(shift):
        return pl.BlockSpec((tile, w), lambda b, j: (blk0 + b * nt + jnp.clip(j + shift, 0, nt - 1), 4))

    return pl.pallas_call(
        functools.partial(_pool_kernel, t, gd),
        grid=(batch, nt),
        in_specs=[u_spec(-1), u_spec(0), u_spec(1),
                  pl.BlockSpec(bands.shape, lambda b, j: (0, 0, 0, 0)),
                  pl.BlockSpec((None, POOL_GROUPS, gd, gd), lambda b, j: (layer, 0, 0, 0)),
                  pl.BlockSpec((None, 1, w), lambda b, j: (layer, 0, 0))],
        out_specs=pl.BlockSpec((tile, w), lambda b, j: (b * nt + j, 0)),
        out_shape=jax.ShapeDtypeStruct((batch * t, w), BF16),
        compiler_params=_params(("parallel", "parallel")),
        name="multiscale_pool",
    )(pb, pb, pb, bands, pool_w, pool_scale)


def _merge_kernel(ya_ref, yb_ref, yc_ref, ga_ref, gb_ref, gc_ref, w_ref, o_ref):
    acc = _sigmoid(ga_ref[...].astype(F32)) * _dot(ya_ref[...], w_ref[0])
    acc += _sigmoid(gb_ref[...].astype(F32)) * _dot(yb_ref[...], w_ref[1])
    acc += _sigmoid(gc_ref[...].astype(F32)) * _dot(yc_ref[...], w_ref[2])
    o_ref[...] = acc.astype(o_ref.dtype)


def _merge(y_a, y_b, y_c, pb, w_branch, rows, layer, gate_col0):
    n, bw = y_a.shape
    d = w_branch.shape[-1]
    tm = rows.tile(512)
    tn = min(512, d)
    g0 = gate_col0 // tn
    nd = d // tn

    def gate_spec(k):
        return pl.BlockSpec((tm, tn), lambda i, j: (i, g0 + k * nd + j))

    y_spec = pl.BlockSpec((tm, bw), lambda i, j: (i, 0))
    return pl.pallas_call(
        _merge_kernel,
        grid=(n // tm, nd),
        in_specs=[y_spec, y_spec, y_spec, gate_spec(0), gate_spec(1), gate_spec(2),
                  pl.BlockSpec((None, N_BRANCH, bw, tn), lambda i, j: (layer, 0, 0, j))],
        out_specs=pl.BlockSpec((tm, tn), lambda i, j: (i, j)),
        out_shape=jax.ShapeDtypeStruct((n, d), BF16),
        compiler_params=_params(("parallel", "arbitrary")),
        name="branch_merge",
    )(y_a, y_b, y_c, pb, pb, pb, w_branch)


def _outproj_kernel(m_ref, w_ref, x_ref, g_ref, o_ref):
    o_ref[...] = x_ref[...] + g_ref[...] * _dot(m_ref[...], w_ref[...])


def _outproj(merged, w_out, x, mod, rows, layer):
    n, d = x.shape
    tm = rows.tile(512)
    tn = min(512, d)
    return pl.pallas_call(
        _outproj_kernel,
        grid=(n // tm, d // tn),
        in_specs=[pl.BlockSpec((tm, d), lambda i, j: (i, 0)),
                  pl.BlockSpec((None, d, tn), lambda i, j: (layer, 0, j)),
                  pl.BlockSpec((tm, tn), lambda i, j: (i, j)),
                  pl.BlockSpec((None, None, None, 1, tn),
                               lambda i, j: (layer, rows.mod_row(i * tm), 5, 0, j))],
        out_specs=pl.BlockSpec((tm, tn), lambda i, j: (i, j)),
        out_shape=jax.ShapeDtypeStruct((n, d), F32),
        compiler_params=_params(("parallel", "arbitrary")),
        name="mixer_out_proj",
    )(merged, w_out, x, mod)


def _final_norm_kernel(x_ref, w_ref, o_ref):
    o_ref[...] = _rms(x_ref[...], w_ref[...])


def _final_norm(x, w, row0, n_rows):
    d = x.shape[1]
    tm = min(512, n_rows)
    return pl.pallas_call(
        _final_norm_kernel,
        grid=(n_rows // tm,),
        in_specs=[pl.BlockSpec((tm, d), lambda i: (row0 // tm + i, 0)),
                  pl.BlockSpec((1, d), lambda i: (0, 0))],
        out_specs=pl.BlockSpec((tm, d), lambda i: (i, 0)),
        out_shape=jax.ShapeDtypeStruct((n_rows, d), F32),
        compiler_params=_params(("parallel",)),
        name="final_norm",
    )(x, w.reshape(1, d))


def _rope_swap_index():
    quarter = ROPE_DIM // 4
    idx = np.arange(ROPE_DIM).reshape(2, 2, quarter)
    return idx[:, ::-1, :].reshape(-1)


def _rope_tables(t):
    pos = jnp.arange(t)
    row = (pos // GRID_W).astype(F32)
    col = (pos % GRID_W).astype(F32)
    n_freq = ROPE_DIM // 4
    inv_freq = jnp.power(ROPE_BASE, -jnp.arange(n_freq, dtype=F32) / n_freq)
    ang_r = row[:, None] * inv_freq
    ang_c = col[:, None] * inv_freq
    cos = jnp.concatenate([jnp.cos(ang_r), jnp.cos(ang_r), jnp.cos(ang_c), jnp.cos(ang_c)], axis=-1)
    sin = jnp.concatenate([-jnp.sin(ang_r), jnp.sin(ang_r), -jnp.sin(ang_c), jnp.sin(ang_c)], axis=-1)
    return cos, sin


def kernel(x_prompt, x_sample, cache_ckv, cache_kpe, state_C, state_n, state_m, c, c_ctx, w_mod, b_mod, norm_w, ffn_w_gu, ffn_w_down, w_in, q_norm_w, kv_norm_w, w_uq, w_ukv, mlstm_gate_b, mlstm_norm_w, pool_w, pool_scale, w_branch, w_out, final_norm_w):
    batch, seq, d = x_prompt.shape
    dec_batch, dec_seq, _ = x_sample.shape
    depth = w_mod.shape[0]
    q_rank, kv_rank = q_norm_w.shape[1], kv_norm_w.shape[1]
    heads = MLA_HEADS
    mw = mlstm_norm_w.shape[1]
    dh = mw // MLSTM_HEADS
    pw = pool_scale.shape[1]
    ffn_h = ffn_w_down.shape[2]
    n_gate = N_DIR * 2 * MLSTM_HEADS
    assert mw == pw == w_branch.shape[2] == heads * V_DIM
    rows = _Rows(batch * seq, seq, dec_batch * dec_seq, dec_seq)

    hp = _round_up(ffn_h, 512)
    wgu = ffn_w_gu.reshape(depth, 2, d, 2, ffn_h).transpose(0, 1, 3, 2, 4)
    wgu = jnp.pad(wgu, ((0, 0),) * 4 + ((0, hp - ffn_h),)).astype(BF16)
    wdn = jnp.pad(ffn_w_down, ((0, 0), (0, 0), (0, hp - ffn_h), (0, 0))).astype(BF16)

    sizes = (q_rank, kv_rank, ROPE_DIM, mw, mw, mw, mw, n_gate, pw, N_BRANCH * d)
    offs = np.concatenate([[0], np.cumsum(sizes)])
    part = lambda k: w_in[:, :, offs[k]:offs[k + 1]]
    swap = _rope_swap_index()
    small_cols = q_rank + kv_rank + 2 * ROPE_DIM + LANES
    w_small = jnp.concatenate(
        [part(0), part(1), part(2), part(2)[:, :, swap], part(7),
         jnp.zeros((depth, d, LANES - n_gate), F32)], axis=-1).astype(BF16)
    gate_blk = (q_rank + kv_rank + 2 * ROPE_DIM) // LANES
    w_big = jnp.concatenate([part(3), part(4), part(5), part(6), part(8), part(9)], axis=-1).astype(BF16)
    gate_col0 = 5 * mw

    wq4 = w_uq.reshape(depth, q_rank, heads, QK_DIM)
    wq_pe = wq4[..., NOPE_DIM:]
    wq = jnp.concatenate([wq4[..., :NOPE_DIM].reshape(depth, q_rank, -1),
                          wq_pe.reshape(depth, q_rank, -1),
                          wq_pe[..., swap].reshape(depth, q_rank, -1)], axis=-1).astype(BF16)
    wkv4 = w_ukv.reshape(depth, kv_rank, heads, NOPE_DIM + V_DIM)
    wkv = jnp.concatenate([wkv4[..., :NOPE_DIM].reshape(depth, kv_rank, -1),
                           wkv4[..., NOPE_DIM:].reshape(depth, kv_rank, -1)], axis=-1).astype(BF16)
    wbr = w_branch.astype(BF16)
    wout = w_out.astype(BF16)
    pwb = pool_w.astype(BF16)
    norm_w4 = norm_w.reshape(depth, 3, 1, d)
    qnw = q_norm_w.reshape(depth, 1, q_rank)
    kvnw = kv_norm_w.reshape(depth, 1, kv_rank)
    mnw = mlstm_norm_w.reshape(depth, 1, mw)
    psc = pool_scale.reshape(depth, 1, pw)
    bands = _pool_bands()
    rope_tabs = _rope_tables(dec_seq)

    cond = jnp.concatenate([c_ctx[None, :], c, jnp.zeros((COND_ROWS - 1 - dec_batch, d), F32)], axis=0)
    mod = _mod_all(cond, w_mod, b_mod).reshape(depth, COND_ROWS, N_MOD, 1, d)

    cache_kv = _cache_kv(cache_ckv, cache_kpe, wkv)

    x = jnp.concatenate([x_prompt.reshape(rows.n_ctx, d), x_sample.reshape(rows.n_lat, d)], axis=0)
    ckv_l, kpe_l, c_l, n_l, m_l = [], [], [], [], []
    for l in range(depth):
        x = _ffn(x, mod, norm_w4, wgu, wdn, rows, l, 0)

        p32 = _inproj(x, mod, norm_w4, w_small, rows, l, F32, small_cols)
        pb = _inproj(x, mod, norm_w4, w_big, rows, l, BF16, _lane_tile(w_big.shape[2], 1024))
        gates_t = p32[:, gate_blk * LANES:gate_blk * LANES + n_gate].T

        q_c, k_c, v_c, ckv_c = _mla_prep(p32, qnw, kvnw, wq, wkv, l, 0, batch, seq, None)
        q_s, k_s, v_s, _ = _mla_prep(p32, qnw, kvnw, wq, wkv, l, rows.n_ctx, dec_batch, dec_seq, rope_tabs)
        y_a = jnp.concatenate([_attention(q_c, k_c, v_c, None, l),
                               _attention(q_s, k_s, v_s, cache_kv, l)], axis=0)
        ckv_l.append(ckv_c.reshape(batch, seq, kv_rank))
        kpe_l.append(p32[:rows.n_ctx, q_rank + kv_rank:q_rank + kv_rank + ROPE_DIM].reshape(batch, seq, ROPE_DIM))

        hf_c, hb_c, c_c, n_c, m_c = _mlstm(pb, p32, gates_t, mlstm_gate_b, None, l, 0, batch, seq, gate_blk, dh)
        hf_s, hb_s, _, _, _ = _mlstm(pb, p32, gates_t, mlstm_gate_b, (state_C, state_n, state_m), l,
                                     rows.n_ctx, dec_batch, dec_seq, gate_blk, dh)
        y_b = jnp.concatenate([_mlstm_post(hf_c, hb_c, pb, mnw, l, 0, dh),
                               _mlstm_post(hf_s, hb_s, pb, mnw, l, rows.n_ctx, dh)], axis=0)
        c_l.append(c_c)
        n_l.append(n_c.reshape(batch, N_DIR, MLSTM_HEADS, dh))
        m_l.append(m_c.reshape(batch, N_DIR, MLSTM_HEADS))

        y_c = jnp.concatenate([_pool(pb, bands, pwb, psc, l, 0, batch, seq),
                               _pool(pb, bands, pwb, psc, l, rows.n_ctx, dec_batch, dec_seq)], axis=0)

        merged = _merge(y_a, y_b, y_c, pb, wbr, rows, l, gate_col0)
        x = _outproj(merged, wout, x, mod, rows, l)
        x = _ffn(x, mod, norm_w4, wgu, wdn, rows, l, 1)

    y_prompt = _final_norm(x, final_norm_w, 0, rows.n_ctx).reshape(batch, seq, d)
    y_sample = _final_norm(x, final_norm_w, rows.n_ctx, rows.n_lat).reshape(dec_batch, dec_seq, d)
    return (y_prompt, y_sample, jnp.stack(ckv_l, axis=1), jnp.stack(kpe_l, axis=1),
            jnp.stack(c_l, axis=1), jnp.stack(n_l, axis=1), jnp.stack(m_l, axis=1))
```

```python
import functools
import math

import numpy as np
import jax
import jax.numpy as jnp
from jax import lax
from jax.experimental import pallas as pl
from jax.experimental.pallas import tpu as pltpu

GRID_W = 64
EPS = 1e-6
N_MOD = 9
MLA_HEADS = 8
NOPE_DIM = 128
ROPE_DIM = 64
V_DIM = 128
QK_DIM = NOPE_DIM + ROPE_DIM
ROPE_BASE = 10000.0
MLSTM_HEADS = 4
N_DIR = 2
CHUNK = 128
POOL_WINDOWS = (2, 4, 8, 16)
POOL_GROUPS = 4
N_BRANCH = 3

LANES = 128
VMEM_LIMIT_MB = 56
COND_ROWS = 8
V_PAD = 2 * V_DIM

F32 = jnp.float32
BF16 = jnp.bfloat16


def _params(sem):
    return pltpu.CompilerParams(dimension_semantics=sem, vmem_limit_bytes=VMEM_LIMIT_MB << 20)


def _call(kernel, name, grid, in_specs, args, out_specs, out_shape, sem, bases=(), scratch=()):
    n_in = len(args)
    extra = [b for b in bases if b is not None]
    aliases = {}
    for k, b in enumerate(bases):
        if b is not None:
            aliases[n_in + len(aliases)] = k

    def body(*refs):
        kernel(*refs[:n_in], *refs[n_in + len(extra):])

    return pl.pallas_call(
        body if extra else kernel, grid=grid,
        in_specs=list(in_specs) + [pl.BlockSpec(memory_space=pl.ANY)] * len(extra),
        out_specs=out_specs, out_shape=out_shape, input_output_aliases=aliases,
        scratch_shapes=list(scratch), compiler_params=_params(sem), name=name)(*args, *extra)


def _round_up(n, m):
    return (n + m - 1) // m * m


def _lane_tile(n, cap):
    t = cap - cap % LANES
    while n % t:
        t -= LANES
    return t


def _sigmoid(x):
    return 1.0 / (1.0 + jnp.exp(-x))


def _log_sigmoid(x):
    return -(jnp.maximum(-x, 0.0) + jnp.log1p(jnp.exp(-jnp.abs(x))))


def _rms(x, w):
    return x * lax.rsqrt(jnp.mean(x * x, axis=-1, keepdims=True) + EPS) * w


def _dot(a, b):
    return jnp.dot(a, b, preferred_element_type=F32)


def _dot_nt(a, b):
    return lax.dot_general(a, b, (((1,), (1,)), ((), ())), preferred_element_type=F32)


def _dot_tn(a, b):
    return lax.dot_general(a, b, (((0,), (0,)), ((), ())), preferred_element_type=F32)


def _mod_kernel(c_ref, w_ref, b_ref, o_ref):
    c = c_ref[...]
    a = (c * _sigmoid(c)).astype(BF16)
    o_ref[...] = _dot(a, w_ref[...].astype(BF16)) + b_ref[...]


def _mod_all(cond, w_mod, b_mod):
    depth, d, nd = w_mod.shape
    tn = _lane_tile(nd, 1024)
    return _call(
        _mod_kernel, "adaln_mod", (depth, nd // tn),
        [pl.BlockSpec((COND_ROWS, d), lambda l, j: (0, 0)),
         pl.BlockSpec((None, d, tn), lambda l, j: (l, 0, j)),
         pl.BlockSpec((None, 1, tn), lambda l, j: (l, 0, j))],
        [cond, w_mod, b_mod.reshape(depth, 1, nd)],
        pl.BlockSpec((None, COND_ROWS, tn), lambda l, j: (l, 0, j)),
        jax.ShapeDtypeStruct((depth, COND_ROWS, nd), F32),
        ("parallel", "parallel"))


class _Rows:
    def __init__(self, n_ctx, t_ctx, n_lat, t_lat):
        self.n_ctx, self.t_ctx, self.n_lat, self.t_lat = n_ctx, t_ctx, n_lat, t_lat
        self.n = n_ctx + n_lat

    def mod_row(self, row):
        return jnp.where(row < self.n_ctx, 0, 1 + (row - self.n_ctx) // self.t_lat)

    def tile(self, cap):
        t = min(cap, self.n_ctx, self.t_lat)
        assert self.n_ctx % t == 0 and self.t_lat % t == 0
        return t


def _mod_spec(rows, tm, layer, k, d):
    return pl.BlockSpec((None, None, None, 1, d),
                        lambda i, j: (layer, rows.mod_row(i * tm), k, 0, 0))


def _norm_mod_to(h_ref, x_ref, nw_ref, sh_ref, sc_ref):
    y = _rms(x_ref[...], nw_ref[...])
    h_ref[...] = (y * (1.0 + sc_ref[...]) + sh_ref[...]).astype(h_ref.dtype)


def _ffn_kernel(x_ref, sh_ref, sc_ref, g_ref, nw_ref, wg_ref, wu_ref, wd_ref, o_ref, h_ref):
    j = pl.program_id(1)

    @pl.when(j == 0)
    def _():
        _norm_mod_to(h_ref, x_ref, nw_ref, sh_ref, sc_ref)
        o_ref[...] = jnp.zeros_like(o_ref)

    h = h_ref[...]
    g = _dot(h, wg_ref[...])
    u = _dot(h, wu_ref[...])
    a = (g * _sigmoid(g) * u).astype(BF16)
    o_ref[...] += _dot(a, wd_ref[...])

    @pl.when(j == pl.num_programs(1) - 1)
    def _():
        o_ref[...] = x_ref[...] + 0.5 * g_ref[...] * o_ref[...]


def _ffn(x, mod, norm_w, w_g, w_u, w_down, rows, layer, which):
    n, d = x.shape
    hp = w_down.shape[2]
    tm = rows.tile(512)
    th = _lane_tile(hp, 512)
    k0 = 0 if which == 0 else 6
    w_spec = pl.BlockSpec((None, None, d, th), lambda i, j: (layer, which, 0, j))
    return _call(
        _ffn_kernel, "ffn_half_step", (n // tm, hp // th),
        [pl.BlockSpec((tm, d), lambda i, j: (i, 0)),
         _mod_spec(rows, tm, layer, k0, d),
         _mod_spec(rows, tm, layer, k0 + 1, d),
         _mod_spec(rows, tm, layer, k0 + 2, d),
         pl.BlockSpec((None, None, 1, d), lambda i, j: (layer, 2 * which, 0, 0)),
         w_spec, w_spec,
         pl.BlockSpec((None, None, th, d), lambda i, j: (layer, which, j, 0))],
        [x, mod, mod, mod, norm_w, w_g, w_u, w_down],
        pl.BlockSpec((tm, d), lambda i, j: (i, 0)),
        jax.ShapeDtypeStruct((n, d), F32),
        ("parallel", "arbitrary"),
        scratch=[pltpu.VMEM((tm, d), BF16)])


def _inproj_kernel(x_ref, sh_ref, sc_ref, nw_ref, w_ref, o_ref, h_ref):
    @pl.when(pl.program_id(1) == 0)
    def _():
        _norm_mod_to(h_ref, x_ref, nw_ref, sh_ref, sc_ref)

    o_ref[...] = _dot(h_ref[...], w_ref[...]).astype(o_ref.dtype)


def _inproj(x, mod, norm_w, w, rows, layer, out_dtype, tn):
    n, d = x.shape
    cols = w.shape[2]
    tm = rows.tile(1024)
    return _call(
        _inproj_kernel, "mixer_in_proj", (n // tm, cols // tn),
        [pl.BlockSpec((tm, d), lambda i, j: (i, 0)),
         _mod_spec(rows, tm, layer, 3, d),
         _mod_spec(rows, tm, layer, 4, d),
         pl.BlockSpec((None, None, 1, d), lambda i, j: (layer, 1, 0, 0)),
         pl.BlockSpec((None, d, tn), lambda i, j: (layer, 0, j))],
        [x, mod, mod, norm_w, w],
        pl.BlockSpec((tm, tn), lambda i, j: (i, j)),
        jax.ShapeDtypeStruct((n, cols), out_dtype),
        ("parallel", "arbitrary"),
        scratch=[pltpu.VMEM((tm, d), BF16)])


def _ones_column(rows):
    lane = lax.broadcasted_iota(jnp.int32, (rows, V_PAD - V_DIM), 1)
    return (lane == 0).astype(BF16)


def _mla_prep_kernel(rope, q_rank, kv_rank, p_ref, qnw_ref, kvnw_ref, wq_ref, wkv_ref, *rest):
    if rope:
        cos_ref, sin_ref, q_ref, k_ref, v_ref = rest
    else:
        q_ref, k_ref, v_ref, ckv_ref, kpe_ref = rest
    heads = MLA_HEADS
    p = p_ref[...]
    c_q = p[:, :q_rank]
    c_kv = p[:, q_rank:q_rank + kv_rank]
    o = q_rank + kv_rank
    k_pe = p[:, o:o + ROPE_DIM]
    k_pe_sw = p[:, o + ROPE_DIM:o + 2 * ROPE_DIM]

    qa = _dot(_rms(c_q, qnw_ref[...]).astype(BF16), wq_ref[...])
    ckv_n = _rms(c_kv, kvnw_ref[...])
    kv = _dot(ckv_n.astype(BF16), wkv_ref[...])
    scale = QK_DIM ** -0.5 * math.log2(math.e)
    if rope:
        cos = cos_ref[...]
        sin = sin_ref[...]
        k_pe = k_pe * cos + k_pe_sw * sin
    else:
        ckv_ref[...] = ckv_n
        kpe_ref[...] = k_pe
    ones = _ones_column(p.shape[0])
    pe0 = heads * NOPE_DIM
    sw0 = pe0 + heads * ROPE_DIM
    for h in range(heads):
        q_pe = qa[:, pe0 + h * ROPE_DIM:pe0 + (h + 1) * ROPE_DIM]
        if rope:
            q_pe = q_pe * cos + qa[:, sw0 + h * ROPE_DIM:sw0 + (h + 1) * ROPE_DIM] * sin
        q_ref[h, :, :NOPE_DIM] = (qa[:, h * NOPE_DIM:(h + 1) * NOPE_DIM] * scale).astype(BF16)
        q_ref[h, :, NOPE_DIM:] = (q_pe * scale).astype(BF16)
        k_ref[h, :, :NOPE_DIM] = kv[:, h * NOPE_DIM:(h + 1) * NOPE_DIM].astype(BF16)
        k_ref[h, :, NOPE_DIM:] = k_pe.astype(BF16)
        v0 = heads * NOPE_DIM + h * V_DIM
        v_ref[h, :, :V_DIM] = kv[:, v0:v0 + V_DIM].astype(BF16)
        v_ref[h, :, V_DIM:] = ones


def _mla_prep(p32, q_norm_w, kv_norm_w, wq, wkv, layer, row0, batch, t, rope_tabs, bases):
    ws = p32.shape[1]
    q_rank, kv_rank = q_norm_w.shape[-1], kv_norm_w.shape[-1]
    depth = q_norm_w.shape[0]
    tm = min(256, t)
    nt = t // tm
    heads = MLA_HEADS
    rope = rope_tabs is not None
    in_specs = [pl.BlockSpec((tm, ws), lambda b, i: (row0 // tm + b * nt + i, 0)),
                pl.BlockSpec((None, 1, q_rank), lambda b, i: (layer, 0, 0)),
                pl.BlockSpec((None, 1, kv_rank), lambda b, i: (layer, 0, 0)),
                pl.BlockSpec((None,) + wq.shape[1:], lambda b, i: (layer, 0, 0)),
                pl.BlockSpec((None,) + wkv.shape[1:], lambda b, i: (layer, 0, 0))]
    args = [p32, q_norm_w, kv_norm_w, wq, wkv]
    head_spec = lambda width: pl.BlockSpec((None, heads, tm, width), lambda b, i: (b, 0, i, 0))
    out_specs = [head_spec(QK_DIM), head_spec(QK_DIM), head_spec(V_PAD)]
    out_shape = [jax.ShapeDtypeStruct((batch, heads, t, QK_DIM), BF16),
                 jax.ShapeDtypeStruct((batch, heads, t, QK_DIM), BF16),
                 jax.ShapeDtypeStruct((batch, heads, t, V_PAD), BF16)]
    if rope:
        in_specs += [pl.BlockSpec((tm, ROPE_DIM), lambda b, i: (i, 0))] * 2
        args += list(rope_tabs)
        all_bases = ()
    else:
        out_specs += [pl.BlockSpec((None, None, tm, kv_rank), lambda b, i: (b, layer, i, 0)),
                      pl.BlockSpec((None, None, tm, ROPE_DIM), lambda b, i: (b, layer, i, 0))]
        out_shape += [jax.ShapeDtypeStruct((batch, depth, t, kv_rank), F32),
                      jax.ShapeDtypeStruct((batch, depth, t, ROPE_DIM), F32)]
        all_bases = (None, None, None) + tuple(bases)
    return _call(functools.partial(_mla_prep_kernel, rope, q_rank, kv_rank), "mla_prep",
                 (batch, nt), in_specs, args, out_specs, out_shape, ("parallel", "parallel"),
                 bases=all_bases)


def _cache_kv_kernel(ckv_ref, kpe_ref, wkv_ref, k_ref, v_ref):
    heads = MLA_HEADS
    kv = _dot(ckv_ref[...].astype(BF16), wkv_ref[...])
    k_pe = kpe_ref[...].astype(BF16)
    ones = _ones_column(kv.shape[0])
    for h in range(heads):
        k_ref[h, :, :NOPE_DIM] = kv[:, h * NOPE_DIM:(h + 1) * NOPE_DIM].astype(BF16)
        k_ref[h, :, NOPE_DIM:] = k_pe
        v0 = heads * NOPE_DIM + h * V_DIM
        v_ref[h, :, :V_DIM] = kv[:, v0:v0 + V_DIM].astype(BF16)
        v_ref[h, :, V_DIM:] = ones


def _cache_kv(cache_ckv, cache_kpe, wkv):
    batch, depth, past, kv_rank = cache_ckv.shape
    heads = MLA_HEADS
    return _call(
        _cache_kv_kernel, "mla_cache_kv", (batch, depth),
        [pl.BlockSpec((None, None, past, kv_rank), lambda b, l: (b, l, 0, 0)),
         pl.BlockSpec((None, None, past, ROPE_DIM), lambda b, l: (b, l, 0, 0)),
         pl.BlockSpec((None,) + wkv.shape[1:], lambda b, l: (l, 0, 0))],
        [cache_ckv, cache_kpe, wkv],
        [pl.BlockSpec((None, None, heads, past, QK_DIM), lambda b, l: (b, l, 0, 0, 0)),
         pl.BlockSpec((None, None, heads, past, V_PAD), lambda b, l: (b, l, 0, 0, 0))],
        [jax.ShapeDtypeStruct((batch, depth, heads, past, QK_DIM), BF16),
         jax.ShapeDtypeStruct((batch, depth, heads, past, V_PAD), BF16)],
        ("parallel", "parallel"))


def _attn_kernel(past, q_ref, k_ref, v_ref, *rest):
    if past:
        kc_ref, vc_ref = rest[:2]
        rest = rest[2:]
    o_ref, s0_ref, s1_ref, p0_ref, p1_ref, m0_ref, m1_ref = rest
    tq = q_ref.shape[0] // 2

    @pl.when(pl.program_id(0) == 0)
    def _():
        for ref in (s0_ref, s1_ref, m0_ref, m1_ref):
            ref[...] = jnp.zeros_like(ref)
        for ref in (p0_ref, p1_ref):
            ref[...] = jnp.ones_like(ref)

    def scores(r, s_ref, m_ref):
        q = q_ref[r * tq:(r + 1) * tq, :]
        s = _dot_nt(q, k_ref[...])
        m = jnp.max(s, axis=-1, keepdims=True)
        if past:
            sc = _dot_nt(q, kc_ref[...])
            m = jnp.maximum(m, jnp.max(sc, axis=-1, keepdims=True))
            s_ref[:, :past] = sc
        s_ref[:, past:] = s
        m_ref[...] = m

    def probs(s_ref, m_ref, p_ref):
        p_ref[...] = jnp.exp2(s_ref[...] - m_ref[...]).astype(p_ref.dtype)

    def values(r, p_ref):
        o = _dot(p_ref[:, past:], v_ref[...])
        if past:
            o = o + _dot(p_ref[:, :past], vc_ref[...])
        o_ref[r * tq:(r + 1) * tq, :] = (o[:, :V_DIM] / o[:, V_DIM:V_DIM + 1]).astype(o_ref.dtype)

    values(0, p0_ref)
    scores(0, s0_ref, m0_ref)
    probs(s1_ref, m1_ref, p1_ref)
    values(1, p1_ref)
    scores(1, s1_ref, m1_ref)
    probs(s0_ref, m0_ref, p0_ref)


def _attention(q, k, v, cache, layer, n_rows, row0, base):
    batch, heads, t, _ = q.shape
    tq = min(256, t // 2)
    pair = 2 * tq
    npair = t // pair
    n_steps = batch * heads * npair
    past = cache[0].shape[3] if cache is not None else 0

    def where(g):
        g = jnp.clip(g, 0, n_steps - 1)
        return g // (heads * npair), (g // npair) % heads, g % npair

    def q_map(g):
        b, h, i = where(g)
        return b, h, i, 0

    def kv_map(shift):
        def index(g):
            b, h, _ = where(g + shift)
            return b, h, 0, 0
        return index

    def cache_map(shift):
        def index(g):
            b, h, _ = where(g + shift)
            return b, layer, h, 0, 0
        return index

    def o_map(g):
        b, h, i = where(g - 1)
        return row0 // pair + b * npair + i, h

    in_specs = [pl.BlockSpec((None, None, pair, QK_DIM), q_map),
                pl.BlockSpec((None, None, t, QK_DIM), kv_map(0)),
                pl.BlockSpec((None, None, t, V_PAD), kv_map(-1))]
    args = [q, k, v]
    if past:
        in_specs += [pl.BlockSpec((None, None, None, past, QK_DIM), cache_map(0)),
                     pl.BlockSpec((None, None, None, past, V_PAD), cache_map(-1))]
        args += list(cache)
    s_total = past + t
    return _call(
        functools.partial(_attn_kernel, past), "mla_attention", (n_steps + 1,),
        in_specs, args,
        pl.BlockSpec((pair, V_DIM), o_map),
        jax.ShapeDtypeStruct((n_rows, heads * V_DIM), BF16),
        ("arbitrary",), bases=(base,),
        scratch=[pltpu.VMEM((tq, s_total), F32), pltpu.VMEM((tq, s_total), F32),
                 pltpu.VMEM((tq, s_total), BF16), pltpu.VMEM((tq, s_total), BF16),
                 pltpu.VMEM((tq, 1), F32), pltpu.VMEM((tq, 1), F32)])


def _mlstm_kernel(has_init, dh, *refs):
    (qf_ref, kf_ref, vf_ref, qb_ref, kb_ref, vb_ref, gf_ref, gb_ref, gtf_ref, gtb_ref,
     brow_ref, bcol_ref) = refs[:12]
    refs = refs[12:]
    if has_init:
        c0_ref, n0_ref, m0_ref = refs[:3]
        refs = refs[3:]
    hf_ref, hb_ref, c_ref, n_ref, m_ref = refs
    heads = MLSTM_HEADS
    n_gate = N_DIR * 2 * heads
    step = pl.program_id(1)

    @pl.when(step == 0)
    def _():
        if has_init:
            c_ref[...] = c0_ref[...]
            n_ref[...] = n0_ref[...]
            m_ref[...] = m0_ref[...]
        else:
            c_ref[...] = jnp.zeros_like(c_ref)
            n_ref[...] = jnp.zeros_like(n_ref)
            m_ref[...] = jnp.zeros_like(m_ref)

    tok_t = lax.broadcasted_iota(jnp.int32, (CHUNK, CHUNK), 0)
    tok_s = lax.broadcasted_iota(jnp.int32, (CHUNK, CHUNK), 1)
    k_scale = dh ** -0.5
    for d in range(N_DIR):
        q_ref, k_ref, v_ref, g_ref, gt_ref, h_ref = (
            (qf_ref, kf_ref, vf_ref, gf_ref, gtf_ref, hf_ref) if d == 0 else
            (qb_ref, kb_ref, vb_ref, gb_ref, gtb_ref, hb_ref))
        seen = (tok_s <= tok_t) if d == 0 else (tok_s >= tok_t)
        seen_f = seen.astype(F32)
        seen_t_f = ((tok_t <= tok_s) if d == 0 else (tok_t >= tok_s)).astype(F32)
        pre_col = g_ref[:, :n_gate] + brow_ref[...]
        pre_row = gt_ref[...] + bcol_ref[...]
        logf_col = _log_sigmoid(pre_col)
        logf_row = _log_sigmoid(pre_row)
        for h in range(heads):
            ci = d * 2 * heads + h
            cf = ci + heads
            i_col = pre_col[:, ci:ci + 1]
            i_row = pre_row[ci:ci + 1, :]
            f_col = logf_col[:, cf:cf + 1]
            f_row = logf_row[cf:cf + 1, :]
            sid = d * heads + h
            m_prev = m_ref[:, sid:sid + 1]
            b_col = jnp.sum(seen_f * f_row, axis=1, keepdims=True)
            b_row = jnp.sum(seen_t_f * f_col, axis=0, keepdims=True)
            b_end = jnp.sum(f_row, axis=1, keepdims=True)
            a_col = b_col + m_prev
            dmat = jnp.where(seen, b_col - b_row + i_row, -jnp.inf)
            m_t = jnp.maximum(a_col, jnp.max(dmat, axis=1, keepdims=True))
            w_intra = jnp.exp(dmat - m_t)
            w_inter = jnp.exp(a_col - m_t)

            q = q_ref[:, h * dh:(h + 1) * dh]
            k = (k_ref[:, h * dh:(h + 1) * dh].astype(F32) * k_scale)
            k_bf = k.astype(BF16)
            v = v_ref[:, h * dh:(h + 1) * dh]
            c_prev = c_ref[d, h]
            n_prev = n_ref[d, h]

            s = _dot_nt(q, k_bf) * w_intra
            num = w_inter * _dot_nt(q, c_prev.astype(BF16)) + _dot(s.astype(BF16), v)
            nq = jnp.sum(q.astype(F32) * n_prev, axis=1, keepdims=True)
            den = w_inter * nq + jnp.sum(s, axis=1, keepdims=True)
            h_ref[:, h * dh:(h + 1) * dh] = num / jnp.maximum(jnp.abs(den), jnp.exp(-m_t))

            g_row = b_end - b_row + i_row
            g_col = b_end - b_col + i_col
            m_new = jnp.maximum(b_end + m_prev, jnp.max(g_row, axis=1, keepdims=True))
            w_pos = jnp.exp(g_col - m_new)
            w_carry = jnp.exp(b_end + m_prev - m_new)
            c_ref[d, h] = w_carry * c_prev + _dot_tn((v.astype(F32) * w_pos).astype(BF16), k_bf)
            n_ref[d, h] = w_carry * n_prev + jnp.sum(k * w_pos, axis=0, keepdims=True)
            m_ref[:, sid:sid + 1] = m_new


def _mlstm(pb, p32, gates_t, gate_b, state, layer, row0, batch, t, gate_blk, dh, bases):
    heads = MLSTM_HEADS
    n_gate = N_DIR * 2 * heads
    depth = gate_b.shape[0]
    nc = t // CHUNK
    blk0 = row0 // CHUNK
    w = heads * dh

    def tok(col):
        fwd = pl.BlockSpec((CHUNK, w), lambda b, c: (blk0 + b * nc + c, col))
        bwd = pl.BlockSpec((CHUNK, w), lambda b, c: (blk0 + b * nc + nc - 1 - c, col))
        return fwd, bwd

    (qf, qb), (kf, kb), (vf, vb) = tok(0), tok(1), tok(2)
    in_specs = [qf, kf, vf, qb, kb, vb,
                pl.BlockSpec((CHUNK, LANES), lambda b, c: (blk0 + b * nc + c, gate_blk)),
                pl.BlockSpec((CHUNK, LANES), lambda b, c: (blk0 + b * nc + nc - 1 - c, gate_blk)),
                pl.BlockSpec((n_gate, CHUNK), lambda b, c: (0, blk0 + b * nc + c)),
                pl.BlockSpec((n_gate, CHUNK), lambda b, c: (0, blk0 + b * nc + nc - 1 - c)),
                pl.BlockSpec((None, 1, n_gate), lambda b, c: (layer, 0, 0)),
                pl.BlockSpec((None, n_gate, 1), lambda b, c: (layer, 0, 0))]
    args = [pb] * 6 + [p32, p32, gates_t, gates_t,
                       gate_b.reshape(-1, 1, n_gate), gate_b.reshape(-1, n_gate, 1)]
    has_init = state is not None
    state_shapes = [(N_DIR, heads, dh, dh), (N_DIR, heads, 1, dh), (1, N_DIR * heads)]
    if has_init:
        c0, n0, m0 = state
        in_specs += [pl.BlockSpec((None, None) + shp, lambda b, c, z=(0,) * len(shp): (b, layer) + z)
                     for shp in state_shapes]
        args += [c0, n0.reshape((batch, depth) + state_shapes[1]), m0.reshape((batch, depth) + state_shapes[2])]
        st_specs = [pl.BlockSpec((None,) + shp, lambda b, c, z=(0,) * len(shp): (b,) + z) for shp in state_shapes]
        st_shapes = [jax.ShapeDtypeStruct((batch,) + shp, F32) for shp in state_shapes]
        all_bases = ()
    else:
        st_specs = [pl.BlockSpec((None, None) + shp, lambda b, c, z=(0,) * len(shp): (b, layer) + z)
                    for shp in state_shapes]
        st_shapes = [jax.ShapeDtypeStruct((batch, depth) + shp, F32) for shp in state_shapes]
        all_bases = (None, None) + tuple(bases)
    return _call(
        functools.partial(_mlstm_kernel, has_init, dh), "mlstm_scan", (batch, nc), in_specs, args,
        [pl.BlockSpec((CHUNK, w), lambda b, c: (b * nc + c, 0)),
         pl.BlockSpec((CHUNK, w), lambda b, c: (b * nc + nc - 1 - c, 0))] + st_specs,
        [jax.ShapeDtypeStruct((batch * t, w), F32),
         jax.ShapeDtypeStruct((batch * t, w), F32)] + st_shapes,
        ("parallel", "arbitrary"), bases=all_bases)


def _mlstm_post_kernel(dh, hf_ref, hb_ref, o_ref, w_ref, y_ref):
    hm = hf_ref[...] + hb_ref[...]
    gate = _sigmoid(o_ref[...].astype(F32))
    w = w_ref[...]
    for h in range(MLSTM_HEADS):
        sl = slice(h * dh, (h + 1) * dh)
        y_ref[:, sl] = (gate[:, sl] * _rms(hm[:, sl], w[:, sl])).astype(y_ref.dtype)


def _mlstm_post(h_f, h_b, pb, m_norm_w, layer, n_rows, row0, dh, base):
    n, w = h_f.shape
    tm = min(256, n)
    return _call(
        functools.partial(_mlstm_post_kernel, dh), "mlstm_post", (n // tm,),
        [pl.BlockSpec((tm, w), lambda i: (i, 0)),
         pl.BlockSpec((tm, w), lambda i: (i, 0)),
         pl.BlockSpec((tm, w), lambda i: (row0 // tm + i, 3)),
         pl.BlockSpec((None, 1, w), lambda i: (layer, 0, 0))],
        [h_f, h_b, pb, m_norm_w],
        pl.BlockSpec((tm, w), lambda i: (row0 // tm + i, 0)),
        jax.ShapeDtypeStruct((n_rows, w), BF16),
        ("parallel",), bases=(base,))


POOL_TILE = 256


def _pool_bands():
    t = np.arange(POOL_TILE)[:, None]
    bands = np.zeros((POOL_GROUPS, 3, POOL_TILE, POOL_TILE), np.float32)
    for g, win in enumerate(POOL_WINDOWS):
        for part in range(3):
            s = np.arange(POOL_TILE)[None, :] + (part - 1) * POOL_TILE
            bands[g, part] = (s >= t - win // 2) & (s < t - win // 2 + win)
    return jnp.asarray(bands, BF16)


def _pool_kernel(t_seq, gd, up_ref, um_ref, un_ref, band_ref, pw_ref, ps_ref, y_ref):
    j = pl.program_id(1)
    has_prev = (j > 0).astype(F32)
    has_next = (j < pl.num_programs(1) - 1).astype(F32)
    tile = um_ref.shape[0]
    pos = j * tile + lax.broadcasted_iota(jnp.int32, (tile, 1), 0)
    for g, win in enumerate(POOL_WINDOWS):
        sl = slice(g * gd, (g + 1) * gd)
        u = um_ref[:, sl]
        acc = (_dot(band_ref[g, 1], u)
               + has_prev * _dot(band_ref[g, 0], up_ref[:, sl])
               + has_next * _dot(band_ref[g, 2], un_ref[:, sl]))
        lo = jnp.clip(pos - win // 2, 0, t_seq)
        hi = jnp.clip(pos - win // 2 + win, 0, t_seq)
        pooled = acc / (hi - lo).astype(F32) - u.astype(F32)
        y = _dot(pooled.astype(BF16), pw_ref[g]) * ps_ref[:, sl]
        y_ref[:, sl] = y.astype(y_ref.dtype)


def _pool(pb, bands, pool_w, pool_scale, layer, n_rows, row0, batch, t, base):
    gd = pool_w.shape[-1]
    w = POOL_GROUPS * gd
    tile = POOL_TILE
    assert t % tile == 0
    nt = t // tile
    blk0 = row0 // tile

    def u_spec(shift):
        return pl.BlockSpec((tile, w), lambda b, j: (blk0 + b * nt + jnp.clip(j + shift, 0, nt - 1), 4))

    return _call(
        functools.partial(_pool_kernel, t, gd), "multiscale_pool", (batch, nt),
        [u_spec(-1), u_spec(0), u_spec(1),
         pl.BlockSpec(bands.shape, lambda b, j: (0, 0, 0, 0)),
         pl.BlockSpec((None, POOL_GROUPS, gd, gd), lambda b, j: (layer, 0, 0, 0)),
         pl.BlockSpec((None, 1, w), lambda b, j: (layer, 0, 0))],
        [pb, pb, pb, bands, pool_w, pool_scale],
        pl.BlockSpec((tile, w), lambda b, j: (blk0 + b * nt + j, 0)),
        jax.ShapeDtypeStruct((n_rows, w), BF16),
        ("parallel", "parallel"), bases=(base,))


def _merge_kernel(ya_ref, yb_ref, yc_ref, ga_ref, gb_ref, gc_ref, w_ref, o_ref):
    acc = _sigmoid(ga_ref[...].astype(F32)) * _dot(ya_ref[...], w_ref[0])
    acc += _sigmoid(gb_ref[...].astype(F32)) * _dot(yb_ref[...], w_ref[1])
    acc += _sigmoid(gc_ref[...].astype(F32)) * _dot(yc_ref[...], w_ref[2])
    o_ref[...] = acc.astype(o_ref.dtype)


def _merge(y_a, y_b, y_c, pb, w_branch, rows, layer, gate_col0):
    n, bw = y_a.shape
    d = w_branch.shape[-1]
    tm = rows.tile(1024)
    tn = min(512, d)
    g0 = gate_col0 // tn
    nd = d // tn

    def gate_spec(k):
        return pl.BlockSpec((tm, tn), lambda i, j: (i, g0 + k * nd + j))

    y_spec = pl.BlockSpec((tm, bw), lambda i, j: (i, 0))
    return _call(
        _merge_kernel, "branch_merge", (n // tm, nd),
        [y_spec, y_spec, y_spec, gate_spec(0), gate_spec(1), gate_spec(2),
         pl.BlockSpec((None, N_BRANCH, bw, tn), lambda i, j: (layer, 0, 0, j))],
        [y_a, y_b, y_c, pb, pb, pb, w_branch],
        pl.BlockSpec((tm, tn), lambda i, j: (i, j)),
        jax.ShapeDtypeStruct((n, d), BF16),
        ("parallel", "arbitrary"))


def _outproj_kernel(m_ref, w_ref, x_ref, g_ref, o_ref):
    o_ref[...] = x_ref[...] + g_ref[...] * _dot(m_ref[...], w_ref[...])


def _outproj(merged, w_out, x, mod, rows, layer):
    n, d = x.shape
    tm = rows.tile(1024)
    tn = min(1024, d)
    return _call(
        _outproj_kernel, "mixer_out_proj", (n // tm, d // tn),
        [pl.BlockSpec((tm, d), lambda i, j: (i, 0)),
         pl.BlockSpec((None, d, tn), lambda i, j: (layer, 0, j)),
         pl.BlockSpec((tm, tn), lambda i, j: (i, j)),
         pl.BlockSpec((None, None, None, 1, tn),
                      lambda i, j: (layer, rows.mod_row(i * tm), 5, 0, j))],
        [merged, w_out, x, mod],
        pl.BlockSpec((tm, tn), lambda i, j: (i, j)),
        jax.ShapeDtypeStruct((n, d), F32),
        ("parallel", "arbitrary"))


def _final_norm_kernel(x_ref, w_ref, o_ref):
    o_ref[...] = _rms(x_ref[...], w_ref[...])


def _final_norm(x, w, row0, n_rows):
    d = x.shape[1]
    tm = min(512, n_rows)
    return _call(
        _final_norm_kernel, "final_norm", (n_rows // tm,),
        [pl.BlockSpec((tm, d), lambda i: (row0 // tm + i, 0)),
         pl.BlockSpec((1, d), lambda i: (0, 0))],
        [x, w.reshape(1, d)],
        pl.BlockSpec((tm, d), lambda i: (i, 0)),
        jax.ShapeDtypeStruct((n_rows, d), F32),
        ("parallel",))


def _rope_swap_index():
    quarter = ROPE_DIM // 4
    idx = np.arange(ROPE_DIM).reshape(2, 2, quarter)
    return idx[:, ::-1, :].reshape(-1)


def _rope_tables(t):
    pos = jnp.arange(t)
    row = (pos // GRID_W).astype(F32)
    col = (pos % GRID_W).astype(F32)
    n_freq = ROPE_DIM // 4
    inv_freq = jnp.power(ROPE_BASE, -jnp.arange(n_freq, dtype=F32) / n_freq)
    ang_r = row[:, None] * inv_freq
    ang_c = col[:, None] * inv_freq
    cos = jnp.concatenate([jnp.cos(ang_r), jnp.cos(ang_r), jnp.cos(ang_c), jnp.cos(ang_c)], axis=-1)
    sin = jnp.concatenate([-jnp.sin(ang_r), jnp.sin(ang_r), -jnp.sin(ang_c), jnp.sin(ang_c)], axis=-1)
    return cos, sin


def kernel(x_prompt, x_sample, cache_ckv, cache_kpe, state_C, state_n, state_m, c, c_ctx, w_mod, b_mod, norm_w, ffn_w_gu, ffn_w_down, w_in, q_norm_w, kv_norm_w, w_uq, w_ukv, mlstm_gate_b, mlstm_norm_w, pool_w, pool_scale, w_branch, w_out, final_norm_w):
    batch, seq, d = x_prompt.shape
    dec_batch, dec_seq, _ = x_sample.shape
    depth = w_mod.shape[0]
    q_rank, kv_rank = q_norm_w.shape[1], kv_norm_w.shape[1]
    heads = MLA_HEADS
    mw = mlstm_norm_w.shape[1]
    dh = mw // MLSTM_HEADS
    pw = pool_scale.shape[1]
    ffn_h = ffn_w_down.shape[2]
    n_gate = N_DIR * 2 * MLSTM_HEADS
    assert mw == pw == w_branch.shape[2] == heads * V_DIM
    rows = _Rows(batch * seq, seq, dec_batch * dec_seq, dec_seq)
    n = rows.n

    hp = _round_up(ffn_h, 512)
    hid_pad = ((0, 0), (0, 0), (0, 0), (0, hp - ffn_h))
    w_g = jnp.pad(ffn_w_gu[..., :ffn_h], hid_pad).astype(BF16)
    w_u = jnp.pad(ffn_w_gu[..., ffn_h:], hid_pad).astype(BF16)
    wdn = jnp.pad(ffn_w_down, ((0, 0), (0, 0), (0, hp - ffn_h), (0, 0))).astype(BF16)

    sizes = (q_rank, kv_rank, ROPE_DIM, mw, mw, mw, mw, n_gate, pw, N_BRANCH * d)
    offs = np.concatenate([[0], np.cumsum(sizes)])
    swap = _rope_swap_index()
    small_cols = q_rank + kv_rank + 2 * ROPE_DIM + LANES
    w_small = jnp.concatenate(
        [w_in[:, :, :offs[3]], w_in[:, :, offs[2]:offs[3]][:, :, swap], w_in[:, :, offs[7]:offs[8]],
         jnp.zeros((depth, d, LANES - n_gate), F32)], axis=-1).astype(BF16)
    gate_blk = (q_rank + kv_rank + 2 * ROPE_DIM) // LANES
    w_big = jnp.concatenate([w_in[:, :, offs[3]:offs[7]], w_in[:, :, offs[8]:]], axis=-1).astype(BF16)
    gate_col0 = 5 * mw

    wq4 = w_uq.reshape(depth, q_rank, heads, QK_DIM)
    wq_pe = wq4[..., NOPE_DIM:]
    wq = jnp.concatenate([wq4[..., :NOPE_DIM].reshape(depth, q_rank, -1),
                          wq_pe.reshape(depth, q_rank, -1),
                          wq_pe[..., swap].reshape(depth, q_rank, -1)], axis=-1).astype(BF16)
    wkv4 = w_ukv.reshape(depth, kv_rank, heads, NOPE_DIM + V_DIM)
    wkv = jnp.concatenate([wkv4[..., :NOPE_DIM].reshape(depth, kv_rank, -1),
                           wkv4[..., NOPE_DIM:].reshape(depth, kv_rank, -1)], axis=-1).astype(BF16)
    wbr = w_branch.astype(BF16)
    wout = w_out.astype(BF16)
    pwb = pool_w.astype(BF16)
    norm_w4 = norm_w.reshape(depth, 3, 1, d)
    qnw = q_norm_w.reshape(depth, 1, q_rank)
    kvnw = kv_norm_w.reshape(depth, 1, kv_rank)
    mnw = mlstm_norm_w.reshape(depth, 1, mw)
    psc = pool_scale.reshape(depth, 1, pw)
    bands = _pool_bands()
    rope_tabs = _rope_tables(dec_seq)

    cond = jnp.concatenate([c_ctx[None, :], c, jnp.zeros((COND_ROWS - 1 - dec_batch, d), F32)], axis=0)
    mod = _mod_all(cond, w_mod, b_mod).reshape(depth, COND_ROWS, N_MOD, 1, d)

    cache_kv = _cache_kv(cache_ckv, cache_kpe, wkv)

    x = jnp.concatenate([x_prompt.reshape(rows.n_ctx, d), x_sample.reshape(rows.n_lat, d)], axis=0)
    new_cache = (None, None)
    new_state = (None, None, None)
    for l in range(depth):
        x = _ffn(x, mod, norm_w4, w_g, w_u, wdn, rows, l, 0)

        p32 = _inproj(x, mod, norm_w4, w_small, rows, l, F32, small_cols)
        pb = _inproj(x, mod, norm_w4, w_big, rows, l, BF16, _lane_tile(w_big.shape[2], 1024))
        gates_t = p32[:, gate_blk * LANES:gate_blk * LANES + n_gate].T

        q_c, k_c, v_c, *new_cache = _mla_prep(p32, qnw, kvnw, wq, wkv, l, 0, batch, seq, None, new_cache)
        q_s, k_s, v_s = _mla_prep(p32, qnw, kvnw, wq, wkv, l, rows.n_ctx, dec_batch, dec_seq, rope_tabs, None)
        y_a = _attention(q_c, k_c, v_c, None, l, n, 0, None)
        y_a = _attention(q_s, k_s, v_s, cache_kv, l, n, rows.n_ctx, y_a)

        hf_c, hb_c, *new_state = _mlstm(pb, p32, gates_t, mlstm_gate_b, None, l, 0, batch, seq,
                                        gate_blk, dh, new_state)
        hf_s, hb_s, _, _, _ = _mlstm(pb, p32, gates_t, mlstm_gate_b, (state_C, state_n, state_m), l,
                                     rows.n_ctx, dec_batch, dec_seq, gate_blk, dh, None)
        y_b = _mlstm_post(hf_c, hb_c, pb, mnw, l, n, 0, dh, None)
        y_b = _mlstm_post(hf_s, hb_s, pb, mnw, l, n, rows.n_ctx, dh, y_b)

        y_c = _pool(pb, bands, pwb, psc, l, n, 0, batch, seq, None)
        y_c = _pool(pb, bands, pwb, psc, l, n, rows.n_ctx, dec_batch, dec_seq, y_c)

        merged = _merge(y_a, y_b, y_c, pb, wbr, rows, l, gate_col0)
        x = _outproj(merged, wout, x, mod, rows, l)
        x = _ffn(x, mod, norm_w4, w_g, w_u, wdn, rows, l, 1)

    y_prompt = _final_norm(x, final_norm_w, 0, rows.n_ctx).reshape(batch, seq, d)
    y_sample = _final_norm(x, final_norm_w, rows.n_ctx, rows.n_lat).reshape(dec_batch, dec_seq, d)
    new_c, new_n, new_m = new_state
    return (y_prompt, y_sample, new_cache[0], new_cache[1], new_c,
            new_n.reshape(batch, depth, N_DIR, MLSTM_HEADS, dh),
            new_m.reshape(batch, depth, N_DIR, MLSTM_HEADS))
```

```python
import functools
import math

import numpy as np
import jax
import jax.numpy as jnp
from jax import lax
from jax.experimental import pallas as pl
from jax.experimental.pallas import tpu as pltpu

GRID_W = 64
EPS = 1e-6
N_MOD = 9
MLA_HEADS = 8
NOPE_DIM = 128
ROPE_DIM = 64
V_DIM = 128
QK_DIM = NOPE_DIM + ROPE_DIM
ROPE_BASE = 10000.0
MLSTM_HEADS = 4
N_DIR = 2
CHUNK = 128
POOL_WINDOWS = (2, 4, 8, 16)
POOL_GROUPS = 4
N_BRANCH = 3

LANES = 128
VMEM_LIMIT_MB = 56
COND_ROWS = 8
V_PAD = 2 * V_DIM

F32 = jnp.float32
BF16 = jnp.bfloat16


def _params(sem):
    return pltpu.CompilerParams(dimension_semantics=sem, vmem_limit_bytes=VMEM_LIMIT_MB << 20)


def _call(kernel, name, grid, in_specs, args, out_specs, out_shape, sem, bases=(), scratch=()):
    n_in = len(args)
    extra = [b for b in bases if b is not None]
    aliases = {}
    for k, b in enumerate(bases):
        if b is not None:
            aliases[n_in + len(aliases)] = k

    def body(*refs):
        kernel(*refs[:n_in], *refs[n_in + len(extra):])

    return pl.pallas_call(
        body if extra else kernel, grid=grid,
        in_specs=list(in_specs) + [pl.BlockSpec(memory_space=pl.ANY)] * len(extra),
        out_specs=out_specs, out_shape=out_shape, input_output_aliases=aliases,
        scratch_shapes=list(scratch), compiler_params=_params(sem), name=name)(*args, *extra)


def _round_up(n, m):
    return (n + m - 1) // m * m


def _lane_tile(n, cap):
    t = cap - cap % LANES
    while n % t:
        t -= LANES
    return t


def _sigmoid(x):
    return 1.0 / (1.0 + jnp.exp(-x))


def _log_sigmoid(x):
    return -(jnp.maximum(-x, 0.0) + jnp.log1p(jnp.exp(-jnp.abs(x))))


def _rms(x, w):
    return x * lax.rsqrt(jnp.mean(x * x, axis=-1, keepdims=True) + EPS) * w


def _dot(a, b):
    return jnp.dot(a, b, preferred_element_type=F32)


def _dot_nt(a, b):
    return lax.dot_general(a, b, (((1,), (1,)), ((), ())), preferred_element_type=F32)


def _mod_kernel(c_ref, w_ref, b_ref, o_ref):
    c = c_ref[...]
    a = (c * _sigmoid(c)).astype(BF16)
    o_ref[...] = _dot(a, w_ref[...].astype(BF16)) + b_ref[...]


def _mod_all(cond, w_mod, b_mod):
    depth, d, nd = w_mod.shape
    tn = _lane_tile(nd, 1024)
    return _call(
        _mod_kernel, "adaln_mod", (depth, nd // tn),
        [pl.BlockSpec((COND_ROWS, d), lambda l, j: (0, 0)),
         pl.BlockSpec((None, d, tn), lambda l, j: (l, 0, j)),
         pl.BlockSpec((None, 1, tn), lambda l, j: (l, 0, j))],
        [cond, w_mod, b_mod.reshape(depth, 1, nd)],
        pl.BlockSpec((None, COND_ROWS, tn), lambda l, j: (l, 0, j)),
        jax.ShapeDtypeStruct((depth, COND_ROWS, nd), F32),
        ("parallel", "parallel"))


class _Rows:
    def __init__(self, n_ctx, t_ctx, n_lat, t_lat):
        self.n_ctx, self.t_ctx, self.n_lat, self.t_lat = n_ctx, t_ctx, n_lat, t_lat
        self.n = n_ctx + n_lat

    def mod_row(self, row):
        return jnp.where(row < self.n_ctx, 0, 1 + (row - self.n_ctx) // self.t_lat)

    def tile(self, cap):
        t = min(cap, self.n_ctx, self.t_lat)
        assert self.n_ctx % t == 0 and self.t_lat % t == 0
        return t


def _mod_spec(rows, tm, layer, k, d):
    return pl.BlockSpec((None, None, None, 1, d),
                        lambda i, j: (layer, rows.mod_row(i * tm), k, 0, 0))


def _norm_mod_to(h_ref, x_ref, nw_ref, sh_ref, sc_ref):
    y = _rms(x_ref[...], nw_ref[...])
    h_ref[...] = (y * (1.0 + sc_ref[...]) + sh_ref[...]).astype(h_ref.dtype)


def _ffn_kernel(emit_next, x_ref, sh_ref, sc_ref, g_ref, nw_ref, wg_ref, wu_ref, wd_ref, *rest):
    if emit_next:
        sh2_ref, sc2_ref, nw2_ref, o_ref, h2_ref, h_ref = rest
    else:
        o_ref, h_ref = rest
    j = pl.program_id(1)

    @pl.when(j == 0)
    def _():
        _norm_mod_to(h_ref, x_ref, nw_ref, sh_ref, sc_ref)
        o_ref[...] = jnp.zeros_like(o_ref)

    h = h_ref[...]
    g = _dot(h, wg_ref[...])
    u = _dot(h, wu_ref[...])
    a = (g * _sigmoid(g) * u).astype(BF16)
    o_ref[...] += _dot(a, wd_ref[...])

    @pl.when(j == pl.num_programs(1) - 1)
    def _():
        o_ref[...] = x_ref[...] + 0.5 * g_ref[...] * o_ref[...]
        if emit_next:
            _norm_mod_to(h2_ref, o_ref, nw2_ref, sh2_ref, sc2_ref)


def _ffn(x, mod, norm_w, w_g, w_u, w_down, rows, layer, which):
    n, d = x.shape
    hp = w_down.shape[2]
    tm = rows.tile(512)
    th = _lane_tile(hp, 512)
    k0 = 0 if which == 0 else 6
    emit_next = which == 0
    w_spec = pl.BlockSpec((None, None, d, th), lambda i, j: (layer, which, 0, j))
    row_spec = pl.BlockSpec((tm, d), lambda i, j: (i, 0))
    norm_spec = lambda k: pl.BlockSpec((None, None, 1, d), lambda i, j: (layer, k, 0, 0))
    in_specs = [row_spec,
                _mod_spec(rows, tm, layer, k0, d),
                _mod_spec(rows, tm, layer, k0 + 1, d),
                _mod_spec(rows, tm, layer, k0 + 2, d),
                norm_spec(2 * which), w_spec, w_spec,
                pl.BlockSpec((None, None, th, d), lambda i, j: (layer, which, j, 0))]
    args = [x, mod, mod, mod, norm_w, w_g, w_u, w_down]
    out_specs, out_shape = row_spec, jax.ShapeDtypeStruct((n, d), F32)
    if emit_next:
        in_specs += [_mod_spec(rows, tm, layer, 3, d), _mod_spec(rows, tm, layer, 4, d), norm_spec(1)]
        args += [mod, mod, norm_w]
        out_specs, out_shape = [row_spec, row_spec], [out_shape, jax.ShapeDtypeStruct((n, d), BF16)]
    return _call(
        functools.partial(_ffn_kernel, emit_next), "ffn_half_step", (n // tm, hp // th),
        in_specs, args, out_specs, out_shape, ("parallel", "arbitrary"),
        scratch=[pltpu.VMEM((tm, d), BF16)])


def _inproj_kernel(h_ref, w_ref, o_ref):
    o_ref[...] = _dot(h_ref[...], w_ref[...]).astype(o_ref.dtype)


def _inproj(h, w, rows, layer, out_dtype, tn):
    n, d = h.shape
    cols = w.shape[2]
    tm = rows.tile(1024)
    return _call(
        _inproj_kernel, "mixer_in_proj", (n // tm, cols // tn),
        [pl.BlockSpec((tm, d), lambda i, j: (i, 0)),
         pl.BlockSpec((None, d, tn), lambda i, j: (layer, 0, j))],
        [h, w],
        pl.BlockSpec((tm, tn), lambda i, j: (i, j)),
        jax.ShapeDtypeStruct((n, cols), out_dtype),
        ("parallel", "arbitrary"))


def _inproj_t_kernel(h_ref, wt_ref, o_ref):
    o_ref[...] = _dot_nt(wt_ref[...], h_ref[...]).astype(o_ref.dtype)


def _inproj_t(h, w_t, rows, layer):
    n, d = h.shape
    cols = w_t.shape[1]
    tm = rows.tile(1024)
    tn = _lane_tile(cols, 1024)
    return _call(
        _inproj_t_kernel, "mixer_in_proj_t", (n // tm, cols // tn),
        [pl.BlockSpec((tm, d), lambda i, j: (i, 0)),
         pl.BlockSpec((None, tn, d), lambda i, j: (layer, j, 0))],
        [h, w_t],
        pl.BlockSpec((tn, tm), lambda i, j: (j, i)),
        jax.ShapeDtypeStruct((cols, n), BF16),
        ("parallel", "arbitrary"))


def _ones_column(rows):
    lane = lax.broadcasted_iota(jnp.int32, (rows, V_PAD - V_DIM), 1)
    return (lane == 0).astype(BF16)


def _mla_prep_kernel(rope, q_rank, kv_rank, p_ref, qnw_ref, kvnw_ref, wq_ref, wkv_ref, *rest):
    if rope:
        cos_ref, sin_ref, q_ref, k_ref, v_ref = rest
    else:
        q_ref, k_ref, v_ref, ckv_ref, kpe_ref = rest
    heads = MLA_HEADS
    p = p_ref[...]
    c_q = p[:, :q_rank]
    c_kv = p[:, q_rank:q_rank + kv_rank]
    o = q_rank + kv_rank
    k_pe = p[:, o:o + ROPE_DIM]
    k_pe_sw = p[:, o + ROPE_DIM:o + 2 * ROPE_DIM]

    qa = _dot(_rms(c_q, qnw_ref[...]).astype(BF16), wq_ref[...])
    ckv_n = _rms(c_kv, kvnw_ref[...])
    kv = _dot(ckv_n.astype(BF16), wkv_ref[...])
    scale = QK_DIM ** -0.5 * math.log2(math.e)
    if rope:
        cos = cos_ref[...]
        sin = sin_ref[...]
        k_pe = k_pe * cos + k_pe_sw * sin
    else:
        ckv_ref[...] = ckv_n
        kpe_ref[...] = k_pe
    ones = _ones_column(p.shape[0])
    pe0 = heads * NOPE_DIM
    sw0 = pe0 + heads * ROPE_DIM
    for h in range(heads):
        q_pe = qa[:, pe0 + h * ROPE_DIM:pe0 + (h + 1) * ROPE_DIM]
        if rope:
            q_pe = q_pe * cos + qa[:, sw0 + h * ROPE_DIM:sw0 + (h + 1) * ROPE_DIM] * sin
        q_ref[h, :, :NOPE_DIM] = (qa[:, h * NOPE_DIM:(h + 1) * NOPE_DIM] * scale).astype(BF16)
        q_ref[h, :, NOPE_DIM:] = (q_pe * scale).astype(BF16)
        k_ref[h, :, :NOPE_DIM] = kv[:, h * NOPE_DIM:(h + 1) * NOPE_DIM].astype(BF16)
        k_ref[h, :, NOPE_DIM:] = k_pe.astype(BF16)
        v0 = heads * NOPE_DIM + h * V_DIM
        v_ref[h, :, :V_DIM] = kv[:, v0:v0 + V_DIM].astype(BF16)
        v_ref[h, :, V_DIM:] = ones


def _mla_prep(p32, q_norm_w, kv_norm_w, wq, wkv, layer, row0, batch, t, rope_tabs, bases):
    ws = p32.shape[1]
    q_rank, kv_rank = q_norm_w.shape[-1], kv_norm_w.shape[-1]
    depth = q_norm_w.shape[0]
    tm = min(256, t)
    nt = t // tm
    heads = MLA_HEADS
    rope = rope_tabs is not None
    in_specs = [pl.BlockSpec((tm, ws), lambda b, i: (row0 // tm + b * nt + i, 0)),
                pl.BlockSpec((None, 1, q_rank), lambda b, i: (layer, 0, 0)),
                pl.BlockSpec((None, 1, kv_rank), lambda b, i: (layer, 0, 0)),
                pl.BlockSpec((None,) + wq.shape[1:], lambda b, i: (layer, 0, 0)),
                pl.BlockSpec((None,) + wkv.shape[1:], lambda b, i: (layer, 0, 0))]
    args = [p32, q_norm_w, kv_norm_w, wq, wkv]
    head_spec = lambda width: pl.BlockSpec((None, heads, tm, width), lambda b, i: (b, 0, i, 0))
    out_specs = [head_spec(QK_DIM), head_spec(QK_DIM), head_spec(V_PAD)]
    out_shape = [jax.ShapeDtypeStruct((batch, heads, t, QK_DIM), BF16),
                 jax.ShapeDtypeStruct((batch, heads, t, QK_DIM), BF16),
                 jax.ShapeDtypeStruct((batch, heads, t, V_PAD), BF16)]
    if rope:
        in_specs += [pl.BlockSpec((tm, ROPE_DIM), lambda b, i: (i, 0))] * 2
        args += list(rope_tabs)
        all_bases = ()
    else:
        out_specs += [pl.BlockSpec((None, None, tm, kv_rank), lambda b, i: (b, layer, i, 0)),
                      pl.BlockSpec((None, None, tm, ROPE_DIM), lambda b, i: (b, layer, i, 0))]
        out_shape += [jax.ShapeDtypeStruct((batch, depth, t, kv_rank), F32),
                      jax.ShapeDtypeStruct((batch, depth, t, ROPE_DIM), F32)]
        all_bases = (None, None, None) + tuple(bases)
    return _call(functools.partial(_mla_prep_kernel, rope, q_rank, kv_rank), "mla_prep",
                 (batch, nt), in_specs, args, out_specs, out_shape, ("parallel", "parallel"),
                 bases=all_bases)


def _cache_kv_kernel(ckv_ref, kpe_ref, wkv_ref, k_ref, v_ref):
    heads = MLA_HEADS
    kv = _dot(ckv_ref[...].astype(BF16), wkv_ref[...])
    k_pe = kpe_ref[...].astype(BF16)
    ones = _ones_column(kv.shape[0])
    for h in range(heads):
        k_ref[h, :, :NOPE_DIM] = kv[:, h * NOPE_DIM:(h + 1) * NOPE_DIM].astype(BF16)
        k_ref[h, :, NOPE_DIM:] = k_pe
        v0 = heads * NOPE_DIM + h * V_DIM
        v_ref[h, :, :V_DIM] = kv[:, v0:v0 + V_DIM].astype(BF16)
        v_ref[h, :, V_DIM:] = ones


def _cache_kv(cache_ckv, cache_kpe, wkv):
    batch, depth, past, kv_rank = cache_ckv.shape
    heads = MLA_HEADS
    return _call(
        _cache_kv_kernel, "mla_cache_kv", (batch, depth),
        [pl.BlockSpec((None, None, past, kv_rank), lambda b, l: (b, l, 0, 0)),
         pl.BlockSpec((None, None, past, ROPE_DIM), lambda b, l: (b, l, 0, 0)),
         pl.BlockSpec((None,) + wkv.shape[1:], lambda b, l: (l, 0, 0))],
        [cache_ckv, cache_kpe, wkv],
        [pl.BlockSpec((None, None, heads, past, QK_DIM), lambda b, l: (b, l, 0, 0, 0)),
         pl.BlockSpec((None, None, heads, past, V_PAD), lambda b, l: (b, l, 0, 0, 0))],
        [jax.ShapeDtypeStruct((batch, depth, heads, past, QK_DIM), BF16),
         jax.ShapeDtypeStruct((batch, depth, heads, past, V_PAD), BF16)],
        ("parallel", "parallel"))


def _attn_kernel(past, q_ref, k_ref, v_ref, *rest):
    if past:
        kc_ref, vc_ref = rest[:2]
        rest = rest[2:]
    o_ref, s0_ref, s1_ref, p0_ref, p1_ref, m0_ref, m1_ref = rest
    tq = q_ref.shape[0] // 2

    @pl.when(pl.program_id(0) == 0)
    def _():
        for ref in (s0_ref, s1_ref, m0_ref, m1_ref):
            ref[...] = jnp.zeros_like(ref)
        for ref in (p0_ref, p1_ref):
            ref[...] = jnp.ones_like(ref)

    def scores(r, s_ref, m_ref):
        q = q_ref[r * tq:(r + 1) * tq, :]
        s = _dot_nt(q, k_ref[...])
        m = jnp.max(s, axis=-1, keepdims=True)
        if past:
            sc = _dot_nt(q, kc_ref[...])
            m = jnp.maximum(m, jnp.max(sc, axis=-1, keepdims=True))
            s_ref[:, :past] = sc
        s_ref[:, past:] = s
        m_ref[...] = m

    def probs(s_ref, m_ref, p_ref):
        p_ref[...] = jnp.exp2(s_ref[...] - m_ref[...]).astype(p_ref.dtype)

    def values(r, p_ref):
        o = _dot(p_ref[:, past:], v_ref[...])
        if past:
            o = o + _dot(p_ref[:, :past], vc_ref[...])
        o_ref[r * tq:(r + 1) * tq, :] = (o[:, :V_DIM] / o[:, V_DIM:V_DIM + 1]).astype(o_ref.dtype)

    values(0, p0_ref)
    scores(0, s0_ref, m0_ref)
    probs(s1_ref, m1_ref, p1_ref)
    values(1, p1_ref)
    scores(1, s1_ref, m1_ref)
    probs(s0_ref, m0_ref, p0_ref)


def _attention(q, k, v, cache, layer, n_rows, row0, base):
    batch, heads, t, _ = q.shape
    tq = min(256, t // 2)
    pair = 2 * tq
    npair = t // pair
    n_steps = batch * heads * npair
    past = cache[0].shape[3] if cache is not None else 0

    def where(g):
        g = jnp.clip(g, 0, n_steps - 1)
        return g // (heads * npair), (g // npair) % heads, g % npair

    def q_map(g):
        b, h, i = where(g)
        return b, h, i, 0

    def kv_map(shift):
        def index(g):
            b, h, _ = where(g + shift)
            return b, h, 0, 0
        return index

    def cache_map(shift):
        def index(g):
            b, h, _ = where(g + shift)
            return b, layer, h, 0, 0
        return index

    def o_map(g):
        b, h, i = where(g - 1)
        return row0 // pair + b * npair + i, h

    in_specs = [pl.BlockSpec((None, None, pair, QK_DIM), q_map),
                pl.BlockSpec((None, None, t, QK_DIM), kv_map(0)),
                pl.BlockSpec((None, None, t, V_PAD), kv_map(-1))]
    args = [q, k, v]
    if past:
        in_specs += [pl.BlockSpec((None, None, None, past, QK_DIM), cache_map(0)),
                     pl.BlockSpec((None, None, None, past, V_PAD), cache_map(-1))]
        args += list(cache)
    s_total = past + t
    return _call(
        functools.partial(_attn_kernel, past), "mla_attention", (n_steps + 1,),
        in_specs, args,
        pl.BlockSpec((pair, V_DIM), o_map),
        jax.ShapeDtypeStruct((n_rows, heads * V_DIM), BF16),
        ("arbitrary",), bases=(base,),
        scratch=[pltpu.VMEM((tq, s_total), F32), pltpu.VMEM((tq, s_total), F32),
                 pltpu.VMEM((tq, s_total), BF16), pltpu.VMEM((tq, s_total), BF16),
                 pltpu.VMEM((tq, 1), F32), pltpu.VMEM((tq, 1), F32)])


def _split3(x):
    hi = x.astype(BF16)
    r = x - hi.astype(F32)
    mid = r.astype(BF16)
    return hi, mid, (r - mid.astype(F32)).astype(BF16)


def _mlstm_kernel(has_init, dh, *refs):
    (qf_ref, kf_ref, vf_ref, qb_ref, kb_ref, vb_ref, gf_ref, gb_ref, gtf_ref, gtb_ref,
     brow_ref, bcol_ref) = refs[:12]
    refs = refs[12:]
    if has_init:
        c0_ref, n0_ref, m0_ref = refs[:3]
        refs = refs[3:]
    hf_ref, hb_ref, c_ref, n_ref, m_ref = refs
    heads = MLSTM_HEADS
    n_gate = N_DIR * 2 * heads
    step = pl.program_id(1)

    @pl.when(step == 0)
    def _():
        if has_init:
            c_ref[...] = c0_ref[...]
            n_ref[...] = n0_ref[...]
            m_ref[...] = m0_ref[...]
        else:
            c_ref[...] = jnp.zeros_like(c_ref)
            n_ref[...] = jnp.zeros_like(n_ref)
            m_ref[...] = jnp.zeros_like(m_ref)

    tok0 = lax.broadcasted_iota(jnp.int32, (CHUNK, CHUNK), 0)
    tok1 = lax.broadcasted_iota(jnp.int32, (CHUNK, CHUNK), 1)
    k_scale = dh ** -0.5
    m_all = m_ref[...]
    m_out = m_all
    unit_lane = lax.broadcasted_iota(jnp.int32, m_all.shape, 1)
    for d in range(N_DIR):
        q_ref, k_ref, vt_ref, g_ref, gt_ref, h_ref = (
            (qf_ref, kf_ref, vf_ref, gf_ref, gtf_ref, hf_ref) if d == 0 else
            (qb_ref, kb_ref, vb_ref, gb_ref, gtb_ref, hb_ref))
        seen_t = (tok0 <= tok1) if d == 0 else (tok0 >= tok1)
        seen_t_bf = seen_t.astype(BF16)
        seen_bf = ((tok1 <= tok0) if d == 0 else (tok1 >= tok0)).astype(BF16)
        pre_col = g_ref[:, :n_gate] + brow_ref[...]
        pre_row = gt_ref[...] + bcol_ref[...]
        cum_col = sum(_dot(seen_bf, part) for part in _split3(_log_sigmoid(pre_col)))
        cum_row = sum(_dot(part, seen_t_bf) for part in _split3(_log_sigmoid(pre_row)))
        last = CHUNK - 1 if d == 0 else 0
        for h in range(heads):
            ci = d * 2 * heads + h
            cf = ci + heads
            sid = d * heads + h
            sl = slice(h * dh, (h + 1) * dh)
            c_col = pre_col[:, ci:ci + 1] - cum_col[:, cf:cf + 1]
            i_row = pre_row[ci:ci + 1, :]
            b_row = cum_row[cf:cf + 1, :]
            b_end = b_row[:, last:last + 1]
            m_prev = m_all[:, sid:sid + 1]
            a_row = b_row + m_prev
            dmat = jnp.where(seen_t, b_row + c_col, -jnp.inf)
            m_t = jnp.maximum(a_row, jnp.max(dmat, axis=0, keepdims=True))
            w_intra = jnp.exp(dmat - m_t)
            w_inter = jnp.exp(a_row - m_t)

            q = q_ref[:, sl]
            k_bf = (k_ref[:, sl].astype(F32) * k_scale).astype(BF16)
            v_t = vt_ref[sl, :]
            c_prev = c_ref[d, h]
            n_prev = n_ref[d, h]

            s_t = _dot_nt(k_bf, q) * w_intra
            num = w_inter * _dot_nt(c_prev.astype(BF16), q) + _dot(v_t, s_t.astype(BF16))
            n_rows = jnp.broadcast_to(n_prev, (8, dh)).astype(BF16)
            nq = _dot_nt(n_rows, q)[0:1, :]
            den = w_inter * nq + jnp.sum(s_t, axis=0, keepdims=True)
            h_ref[sl, :] = num / jnp.maximum(jnp.abs(den), jnp.exp(-m_t))

            g_row = b_end - b_row + i_row
            m_new = jnp.maximum(b_end + m_prev, jnp.max(g_row, axis=1, keepdims=True))
            w_pos = jnp.exp(g_row - m_new)
            w_carry = jnp.exp(b_end + m_prev - m_new)
            c_ref[d, h] = w_carry * c_prev + _dot((v_t.astype(F32) * w_pos).astype(BF16), k_bf)
            w_rows = jnp.broadcast_to(w_pos, (8, CHUNK)).astype(BF16)
            n_ref[d, h] = w_carry * n_prev + _dot(w_rows, k_bf)[0:1, :]
            m_out = jnp.where(unit_lane == sid, m_new, m_out)
    m_ref[...] = m_out


def _mlstm(pb, pb_t, p32, gates_t, gate_b, state, layer, row0, batch, t, gate_blk, dh, bases):
    heads = MLSTM_HEADS
    n_gate = N_DIR * 2 * heads
    depth = gate_b.shape[0]
    nc = t // CHUNK
    blk0 = row0 // CHUNK
    w = heads * dh
    fwd = lambda b, c: blk0 + b * nc + c
    bwd = lambda b, c: blk0 + b * nc + nc - 1 - c

    def tok(col, blk):
        return pl.BlockSpec((CHUNK, w), lambda b, c: (blk(b, c), col))

    def feat(blk):
        return pl.BlockSpec((w, CHUNK), lambda b, c: (0, blk(b, c)))

    in_specs = [tok(0, fwd), tok(1, fwd), feat(fwd), tok(0, bwd), tok(1, bwd), feat(bwd),
                pl.BlockSpec((CHUNK, LANES), lambda b, c: (fwd(b, c), gate_blk)),
                pl.BlockSpec((CHUNK, LANES), lambda b, c: (bwd(b, c), gate_blk)),
                pl.BlockSpec((n_gate, CHUNK), lambda b, c: (0, fwd(b, c))),
                pl.BlockSpec((n_gate, CHUNK), lambda b, c: (0, bwd(b, c))),
                pl.BlockSpec((None, 1, n_gate), lambda b, c: (layer, 0, 0)),
                pl.BlockSpec((None, n_gate, 1), lambda b, c: (layer, 0, 0))]
    args = [pb, pb, pb_t, pb, pb, pb_t, p32, p32, gates_t, gates_t,
            gate_b.reshape(-1, 1, n_gate), gate_b.reshape(-1, n_gate, 1)]
    has_init = state is not None
    state_shapes = [(N_DIR, heads, dh, dh), (N_DIR, heads, 1, dh), (1, N_DIR * heads)]
    if has_init:
        c0, n0, m0 = state
        in_specs += [pl.BlockSpec((None, None) + shp, lambda b, c, z=(0,) * len(shp): (b, layer) + z)
                     for shp in state_shapes]
        args += [c0, n0.reshape((batch, depth) + state_shapes[1]), m0.reshape((batch, depth) + state_shapes[2])]
        st_specs = [pl.BlockSpec((None,) + shp, lambda b, c, z=(0,) * len(shp): (b,) + z) for shp in state_shapes]
        st_shapes = [jax.ShapeDtypeStruct((batch,) + shp, F32) for shp in state_shapes]
        all_bases = ()
    else:
        st_specs = [pl.BlockSpec((None, None) + shp, lambda b, c, z=(0,) * len(shp): (b, layer) + z)
                    for shp in state_shapes]
        st_shapes = [jax.ShapeDtypeStruct((batch, depth) + shp, F32) for shp in state_shapes]
        all_bases = (None, None) + tuple(bases)
    return _call(
        functools.partial(_mlstm_kernel, has_init, dh), "mlstm_scan", (batch, nc), in_specs, args,
        [pl.BlockSpec((w, CHUNK), lambda b, c: (0, b * nc + c)),
         pl.BlockSpec((w, CHUNK), lambda b, c: (0, b * nc + nc - 1 - c))] + st_specs,
        [jax.ShapeDtypeStruct((w, batch * t), F32),
         jax.ShapeDtypeStruct((w, batch * t), F32)] + st_shapes,
        ("parallel", "arbitrary"), bases=all_bases)


def _mlstm_post_kernel(dh, hf_ref, hb_ref, o_ref, w_ref, y_ref):
    tm = hf_ref.shape[1]
    hm = hf_ref[...] + hb_ref[...]
    gate = _sigmoid(o_ref[...].astype(F32))
    w = w_ref[...]
    eye = (lax.broadcasted_iota(jnp.int32, (tm, tm), 0)
           == lax.broadcasted_iota(jnp.int32, (tm, tm), 1)).astype(BF16)
    for h in range(MLSTM_HEADS):
        sl = slice(h * dh, (h + 1) * dh)
        x = hm[sl, :]
        y = x * lax.rsqrt(jnp.mean(x * x, axis=0, keepdims=True) + EPS) * w[sl, :]
        y_t = (gate[sl, :] * y).astype(BF16)
        y_ref[:, sl] = _dot_nt(eye, y_t).astype(y_ref.dtype)


def _mlstm_post(h_f, h_b, pb_t, m_norm_w, layer, n_rows, row0, dh, base):
    w, n = h_f.shape
    tm = min(256, n)
    return _call(
        functools.partial(_mlstm_post_kernel, dh), "mlstm_post", (n // tm,),
        [pl.BlockSpec((w, tm), lambda i: (0, i)),
         pl.BlockSpec((w, tm), lambda i: (0, i)),
         pl.BlockSpec((w, tm), lambda i: (1, row0 // tm + i)),
         pl.BlockSpec((None, w, 1), lambda i: (layer, 0, 0))],
        [h_f, h_b, pb_t, m_norm_w],
        pl.BlockSpec((tm, w), lambda i: (row0 // tm + i, 0)),
        jax.ShapeDtypeStruct((n_rows, w), BF16),
        ("parallel",), bases=(base,))


POOL_TILE = 256


def _pool_bands():
    t = np.arange(POOL_TILE)[:, None]
    bands = np.zeros((POOL_GROUPS, 3, POOL_TILE, POOL_TILE), np.float32)
    for g, win in enumerate(POOL_WINDOWS):
        for part in range(3):
            s = np.arange(POOL_TILE)[None, :] + (part - 1) * POOL_TILE
            bands[g, part] = (s >= t - win // 2) & (s < t - win // 2 + win)
    return jnp.asarray(bands, BF16)


def _pool_kernel(t_seq, gd, up_ref, um_ref, un_ref, band_ref, pw_ref, ps_ref, y_ref):
    j = pl.program_id(1)
    has_prev = (j > 0).astype(F32)
    has_next = (j < pl.num_programs(1) - 1).astype(F32)
    tile = um_ref.shape[0]
    pos = j * tile + lax.broadcasted_iota(jnp.int32, (tile, 1), 0)
    for g, win in enumerate(POOL_WINDOWS):
        sl = slice(g * gd, (g + 1) * gd)
        u = um_ref[:, sl]
        acc = (_dot(band_ref[g, 1], u)
               + has_prev * _dot(band_ref[g, 0], up_ref[:, sl])
               + has_next * _dot(band_ref[g, 2], un_ref[:, sl]))
        lo = jnp.clip(pos - win // 2, 0, t_seq)
        hi = jnp.clip(pos - win // 2 + win, 0, t_seq)
        pooled = acc / (hi - lo).astype(F32) - u.astype(F32)
        y = _dot(pooled.astype(BF16), pw_ref[g]) * ps_ref[:, sl]
        y_ref[:, sl] = y.astype(y_ref.dtype)


def _pool(pb, bands, pool_w, pool_scale, layer, n_rows, row0, batch, t, base):
    gd = pool_w.shape[-1]
    w = POOL_GROUPS * gd
    tile = POOL_TILE
    assert t % tile == 0
    nt = t // tile
    blk0 = row0 // tile

    def u_spec(shift):
        return pl.BlockSpec((tile, w), lambda b, j: (blk0 + b * nt + jnp.clip(j + shift, 0, nt - 1), 2))

    return _call(
        functools.partial(_pool_kernel, t, gd), "multiscale_pool", (batch, nt),
        [u_spec(-1), u_spec(0), u_spec(1),
         pl.BlockSpec(bands.shape, lambda b, j: (0, 0, 0, 0)),
         pl.BlockSpec((None, POOL_GROUPS, gd, gd), lambda b, j: (layer, 0, 0, 0)),
         pl.BlockSpec((None, 1, w), lambda b, j: (layer, 0, 0))],
        [pb, pb, pb, bands, pool_w, pool_scale],
        pl.BlockSpec((tile, w), lambda b, j: (blk0 + b * nt + j, 0)),
        jax.ShapeDtypeStruct((n_rows, w), BF16),
        ("parallel", "parallel"), bases=(base,))


def _merge_kernel(ya_ref, yb_ref, yc_ref, ga_ref, gb_ref, gc_ref, w_ref, o_ref):
    acc = _sigmoid(ga_ref[...].astype(F32)) * _dot(ya_ref[...], w_ref[0])
    acc += _sigmoid(gb_ref[...].astype(F32)) * _dot(yb_ref[...], w_ref[1])
    acc += _sigmoid(gc_ref[...].astype(F32)) * _dot(yc_ref[...], w_ref[2])
    o_ref[...] = acc.astype(o_ref.dtype)


def _merge(y_a, y_b, y_c, pb, w_branch, rows, layer, gate_col0):
    n, bw = y_a.shape
    d = w_branch.shape[-1]
    tm = rows.tile(1024)
    tn = min(512, d)
    g0 = gate_col0 // tn
    nd = d // tn

    def gate_spec(k):
        return pl.BlockSpec((tm, tn), lambda i, j: (i, g0 + k * nd + j))

    y_spec = pl.BlockSpec((tm, bw), lambda i, j: (i, 0))
    return _call(
        _merge_kernel, "branch_merge", (n // tm, nd),
        [y_spec, y_spec, y_spec, gate_spec(0), gate_spec(1), gate_spec(2),
         pl.BlockSpec((None, N_BRANCH, bw, tn), lambda i, j: (layer, 0, 0, j))],
        [y_a, y_b, y_c, pb, pb, pb, w_branch],
        pl.BlockSpec((tm, tn), lambda i, j: (i, j)),
        jax.ShapeDtypeStruct((n, d), BF16),
        ("parallel", "arbitrary"))


def _outproj_kernel(m_ref, w_ref, x_ref, g_ref, o_ref):
    o_ref[...] = x_ref[...] + g_ref[...] * _dot(m_ref[...], w_ref[...])


def _outproj(merged, w_out, x, mod, rows, layer):
    n, d = x.shape
    tm = rows.tile(1024)
    tn = min(1024, d)
    return _call(
        _outproj_kernel, "mixer_out_proj", (n // tm, d // tn),
        [pl.BlockSpec((tm, d), lambda i, j: (i, 0)),
         pl.BlockSpec((None, d, tn), lambda i, j: (layer, 0, j)),
         pl.BlockSpec((tm, tn), lambda i, j: (i, j)),
         pl.BlockSpec((None, None, None, 1, tn),
                      lambda i, j: (layer, rows.mod_row(i * tm), 5, 0, j))],
        [merged, w_out, x, mod],
        pl.BlockSpec((tm, tn), lambda i, j: (i, j)),
        jax.ShapeDtypeStruct((n, d), F32),
        ("parallel", "arbitrary"))


def _final_norm_kernel(x_ref, w_ref, o_ref):
    o_ref[...] = _rms(x_ref[...], w_ref[...])


def _final_norm(x, w, row0, n_rows):
    d = x.shape[1]
    tm = min(512, n_rows)
    return _call(
        _final_norm_kernel, "final_norm", (n_rows // tm,),
        [pl.BlockSpec((tm, d), lambda i: (row0 // tm + i, 0)),
         pl.BlockSpec((1, d), lambda i: (0, 0))],
        [x, w.reshape(1, d)],
        pl.BlockSpec((tm, d), lambda i: (i, 0)),
        jax.ShapeDtypeStruct((n_rows, d), F32),
        ("parallel",))


def _rope_swap_index():
    quarter = ROPE_DIM // 4
    idx = np.arange(ROPE_DIM).reshape(2, 2, quarter)
    return idx[:, ::-1, :].reshape(-1)


def _rope_tables(t):
    pos = jnp.arange(t)
    row = (pos // GRID_W).astype(F32)
    col = (pos % GRID_W).astype(F32)
    n_freq = ROPE_DIM // 4
    inv_freq = jnp.power(ROPE_BASE, -jnp.arange(n_freq, dtype=F32) / n_freq)
    ang_r = row[:, None] * inv_freq
    ang_c = col[:, None] * inv_freq
    cos = jnp.concatenate([jnp.cos(ang_r), jnp.cos(ang_r), jnp.cos(ang_c), jnp.cos(ang_c)], axis=-1)
    sin = jnp.concatenate([-jnp.sin(ang_r), jnp.sin(ang_r), -jnp.sin(ang_c), jnp.sin(ang_c)], axis=-1)
    return cos, sin


def kernel(x_prompt, x_sample, cache_ckv, cache_kpe, state_C, state_n, state_m, c, c_ctx, w_mod, b_mod, norm_w, ffn_w_gu, ffn_w_down, w_in, q_norm_w, kv_norm_w, w_uq, w_ukv, mlstm_gate_b, mlstm_norm_w, pool_w, pool_scale, w_branch, w_out, final_norm_w):
    batch, seq, d = x_prompt.shape
    dec_batch, dec_seq, _ = x_sample.shape
    depth = w_mod.shape[0]
    q_rank, kv_rank = q_norm_w.shape[1], kv_norm_w.shape[1]
    heads = MLA_HEADS
    mw = mlstm_norm_w.shape[1]
    dh = mw // MLSTM_HEADS
    pw = pool_scale.shape[1]
    ffn_h = ffn_w_down.shape[2]
    n_gate = N_DIR * 2 * MLSTM_HEADS
    assert mw == pw == w_branch.shape[2] == heads * V_DIM
    rows = _Rows(batch * seq, seq, dec_batch * dec_seq, dec_seq)
    n = rows.n

    hp = _round_up(ffn_h, 512)
    hid_pad = ((0, 0), (0, 0), (0, 0), (0, hp - ffn_h))
    w_g = jnp.pad(ffn_w_gu[..., :ffn_h], hid_pad).astype(BF16)
    w_u = jnp.pad(ffn_w_gu[..., ffn_h:], hid_pad).astype(BF16)
    wdn = jnp.pad(ffn_w_down, ((0, 0), (0, 0), (0, hp - ffn_h), (0, 0))).astype(BF16)

    sizes = (q_rank, kv_rank, ROPE_DIM, mw, mw, mw, mw, n_gate, pw, N_BRANCH * d)
    offs = np.concatenate([[0], np.cumsum(sizes)])
    swap = _rope_swap_index()
    small_cols = q_rank + kv_rank + 2 * ROPE_DIM + LANES
    w_small = jnp.concatenate(
        [w_in[:, :, :offs[3]], w_in[:, :, offs[2]:offs[3]][:, :, swap], w_in[:, :, offs[7]:offs[8]],
         jnp.zeros((depth, d, LANES - n_gate), F32)], axis=-1).astype(BF16)
    gate_blk = (q_rank + kv_rank + 2 * ROPE_DIM) // LANES
    w_big = jnp.concatenate([w_in[:, :, offs[3]:offs[5]], w_in[:, :, offs[8]:]], axis=-1).astype(BF16)
    w_feat = jnp.swapaxes(w_in[:, :, offs[5]:offs[7]], 1, 2).astype(BF16)
    gate_col0 = 3 * mw

    wq4 = w_uq.reshape(depth, q_rank, heads, QK_DIM)
    wq_pe = wq4[..., NOPE_DIM:]
    wq = jnp.concatenate([wq4[..., :NOPE_DIM].reshape(depth, q_rank, -1),
                          wq_pe.reshape(depth, q_rank, -1),
                          wq_pe[..., swap].reshape(depth, q_rank, -1)], axis=-1).astype(BF16)
    wkv4 = w_ukv.reshape(depth, kv_rank, heads, NOPE_DIM + V_DIM)
    wkv = jnp.concatenate([wkv4[..., :NOPE_DIM].reshape(depth, kv_rank, -1),
                           wkv4[..., NOPE_DIM:].reshape(depth, kv_rank, -1)], axis=-1).astype(BF16)
    wbr = w_branch.astype(BF16)
    wout = w_out.astype(BF16)
    pwb = pool_w.astype(BF16)
    norm_w4 = norm_w.reshape(depth, 3, 1, d)
    qnw = q_norm_w.reshape(depth, 1, q_rank)
    kvnw = kv_norm_w.reshape(depth, 1, kv_rank)
    mnw = mlstm_norm_w.reshape(depth, mw, 1)
    psc = pool_scale.reshape(depth, 1, pw)
    bands = _pool_bands()
    rope_tabs = _rope_tables(dec_seq)

    cond = jnp.concatenate([c_ctx[None, :], c, jnp.zeros((COND_ROWS - 1 - dec_batch, d), F32)], axis=0)
    mod = _mod_all(cond, w_mod, b_mod).reshape(depth, COND_ROWS, N_MOD, 1, d)

    cache_kv = _cache_kv(cache_ckv, cache_kpe, wkv)

    x = jnp.concatenate([x_prompt.reshape(rows.n_ctx, d), x_sample.reshape(rows.n_lat, d)], axis=0)
    new_cache = (None, None)
    new_state = (None, None, None)
    for l in range(depth):
        x, h_mix = _ffn(x, mod, norm_w4, w_g, w_u, wdn, rows, l, 0)

        p32 = _inproj(h_mix, w_small, rows, l, F32, small_cols)
        pb = _inproj(h_mix, w_big, rows, l, BF16, _lane_tile(w_big.shape[2], 1024))
        pb_t = _inproj_t(h_mix, w_feat, rows, l)
        gates_t = p32[:, gate_blk * LANES:gate_blk * LANES + n_gate].T

        q_c, k_c, v_c, *new_cache = _mla_prep(p32, qnw, kvnw, wq, wkv, l, 0, batch, seq, None, new_cache)
        q_s, k_s, v_s = _mla_prep(p32, qnw, kvnw, wq, wkv, l, rows.n_ctx, dec_batch, dec_seq, rope_tabs, None)
        y_a = _attention(q_c, k_c, v_c, None, l, n, 0, None)
        y_a = _attention(q_s, k_s, v_s, cache_kv, l, n, rows.n_ctx, y_a)

        hf_c, hb_c, *new_state = _mlstm(pb, pb_t, p32, gates_t, mlstm_gate_b, None, l, 0, batch, seq,
                                        gate_blk, dh, new_state)
        hf_s, hb_s, _, _, _ = _mlstm(pb, pb_t, p32, gates_t, mlstm_gate_b, (state_C, state_n, state_m), l,
                                     rows.n_ctx, dec_batch, dec_seq, gate_blk, dh, None)
        y_b = _mlstm_post(hf_c, hb_c, pb_t, mnw, l, n, 0, dh, None)
        y_b = _mlstm_post(hf_s, hb_s, pb_t, mnw, l, n, rows.n_ctx, dh, y_b)

        y_c = _pool(pb, bands, pwb, psc, l, n, 0, batch, seq, None)
        y_c = _pool(pb, bands, pwb, psc, l, n, rows.n_ctx, dec_batch, dec_seq, y_c)

        merged = _merge(y_a, y_b, y_c, pb, wbr, rows, l, gate_col0)
        x = _outproj(merged, wout, x, mod, rows, l)
        x = _ffn(x, mod, norm_w4, w_g, w_u, wdn, rows, l, 1)

    y_prompt = _final_norm(x, final_norm_w, 0, rows.n_ctx).reshape(batch, seq, d)
    y_sample = _final_norm(x, final_norm_w, rows.n_ctx, rows.n_lat).reshape(dec_batch, dec_seq, d)
    new_c, new_n, new_m = new_state
    return (y_prompt, y_sample, new_cache[0], new_cache[1], new_c,
            new_n.reshape(batch, depth, N_DIR, MLSTM_HEADS, dh),
            new_m.reshape(batch, depth, N_DIR, MLSTM_HEADS))
```

```python
import functools
import math

import numpy as np
import jax
import jax.numpy as jnp
from jax import lax
from jax.experimental import pallas as pl
from jax.experimental.pallas import tpu as pltpu

GRID_W = 64
EPS = 1e-6
N_MOD = 9
MLA_HEADS = 8
NOPE_DIM = 128
ROPE_DIM = 64
V_DIM = 128
QK_DIM = NOPE_DIM + ROPE_DIM
ROPE_BASE = 10000.0
MLSTM_HEADS = 4
N_DIR = 2
CHUNK = 128
POOL_WINDOWS = (2, 4, 8, 16)
POOL_GROUPS = 4
N_BRANCH = 3

LANES = 128
VMEM_LIMIT_MB = 56
COND_ROWS = 8
V_PAD = 2 * V_DIM

F32 = jnp.float32
BF16 = jnp.bfloat16


def _params(sem):
    return pltpu.CompilerParams(dimension_semantics=sem, vmem_limit_bytes=VMEM_LIMIT_MB << 20)


def _call(kernel, name, grid, in_specs, args, out_specs, out_shape, sem, bases=(), scratch=()):
    n_in = len(args)
    extra = [b for b in bases if b is not None]
    aliases = {}
    for k, b in enumerate(bases):
        if b is not None:
            aliases[n_in + len(aliases)] = k

    def body(*refs):
        kernel(*refs[:n_in], *refs[n_in + len(extra):])

    return pl.pallas_call(
        body if extra else kernel, grid=grid,
        in_specs=list(in_specs) + [pl.BlockSpec(memory_space=pl.ANY)] * len(extra),
        out_specs=out_specs, out_shape=out_shape, input_output_aliases=aliases,
        scratch_shapes=list(scratch), compiler_params=_params(sem), name=name)(*args, *extra)


def _round_up(n, m):
    return (n + m - 1) // m * m


def _lane_tile(n, cap):
    t = cap - cap % LANES
    while n % t:
        t -= LANES
    return t


def _sigmoid(x):
    return 1.0 / (1.0 + jnp.exp(-x))


def _log_sigmoid(x):
    return -(jnp.maximum(-x, 0.0) + jnp.log1p(jnp.exp(-jnp.abs(x))))


def _rms(x, w):
    return x * lax.rsqrt(jnp.mean(x * x, axis=-1, keepdims=True) + EPS) * w


def _dot(a, b):
    return jnp.dot(a, b, preferred_element_type=F32)


def _dot_nt(a, b):
    return lax.dot_general(a, b, (((1,), (1,)), ((), ())), preferred_element_type=F32)


def _mod_kernel(c_ref, w_ref, b_ref, o_ref):
    c = c_ref[...]
    a = (c * _sigmoid(c)).astype(BF16)
    o_ref[...] = _dot(a, w_ref[...].astype(BF16)) + b_ref[...]


def _mod_all(cond, w_mod, b_mod):
    depth, d, nd = w_mod.shape
    tn = _lane_tile(nd, 1024)
    return _call(
        _mod_kernel, "adaln_mod", (depth, nd // tn),
        [pl.BlockSpec((COND_ROWS, d), lambda l, j: (0, 0)),
         pl.BlockSpec((None, d, tn), lambda l, j: (l, 0, j)),
         pl.BlockSpec((None, 1, tn), lambda l, j: (l, 0, j))],
        [cond, w_mod, b_mod.reshape(depth, 1, nd)],
        pl.BlockSpec((None, COND_ROWS, tn), lambda l, j: (l, 0, j)),
        jax.ShapeDtypeStruct((depth, COND_ROWS, nd), F32),
        ("parallel", "parallel"))


class _Rows:
    def __init__(self, n_ctx, t_ctx, n_lat, t_lat):
        self.n_ctx, self.t_ctx, self.n_lat, self.t_lat = n_ctx, t_ctx, n_lat, t_lat
        self.n = n_ctx + n_lat

    def mod_row(self, row):
        return jnp.where(row < self.n_ctx, 0, 1 + (row - self.n_ctx) // self.t_lat)

    def tile(self, cap):
        t = min(cap, self.n_ctx, self.t_lat)
        assert self.n_ctx % t == 0 and self.t_lat % t == 0
        return t


def _mod_spec(rows, tm, layer, k, d):
    return pl.BlockSpec((None, None, None, 1, d),
                        lambda i, j: (layer, rows.mod_row(i * tm), k, 0, 0))


def _norm_mod_to(h_ref, x_ref, nw_ref, sh_ref, sc_ref):
    y = _rms(x_ref[...], nw_ref[...])
    h_ref[...] = (y * (1.0 + sc_ref[...]) + sh_ref[...]).astype(h_ref.dtype)


FFN_PROLOGUE_CHUNKS = 8


def _ffn_kernel(emit_next, n_chunks, x_ref, xn_ref, sh_ref, sc_ref, shn_ref, scn_ref, g_ref, nw_ref,
                wg_ref, wu_ref, wd_ref, *rest):
    if emit_next:
        sh2_ref, sc2_ref, nw2_ref, o_ref, h2_ref, ha_ref, hb_ref = rest
    else:
        o_ref, ha_ref, hb_ref = rest
    i = pl.program_id(0)
    j = pl.program_id(1)
    tm = x_ref.shape[0]
    rows = tm // n_chunks

    @pl.when((i == 0) & (j == 0))
    def _():
        _norm_mod_to(ha_ref, x_ref, nw_ref, sh_ref, sc_ref)

    @pl.when(j == 0)
    def _():
        o_ref[...] = jnp.zeros_like(o_ref)

    def step(h_ref, h_next_ref):
        r0 = pl.multiple_of(jnp.minimum(j, n_chunks - 1) * rows, rows)
        y = _rms(xn_ref[pl.ds(r0, rows), :], nw_ref[...])
        h_next_ref[pl.ds(r0, rows), :] = (y * (1.0 + scn_ref[...]) + shn_ref[...]).astype(BF16)
        h = h_ref[...]
        g = _dot(h, wg_ref[...])
        u = _dot(h, wu_ref[...])
        a = (g * _sigmoid(g) * u).astype(BF16)
        o_ref[...] += _dot(a, wd_ref[...])

    @pl.when(i % 2 == 0)
    def _():
        step(ha_ref, hb_ref)

    @pl.when(i % 2 == 1)
    def _():
        step(hb_ref, ha_ref)

    @pl.when(j == pl.num_programs(1) - 1)
    def _():
        o_ref[...] = x_ref[...] + 0.5 * g_ref[...] * o_ref[...]
        if emit_next:
            _norm_mod_to(h2_ref, o_ref, nw2_ref, sh2_ref, sc2_ref)


def _ffn(x, mod, norm_w, w_g, w_u, w_down, rows, layer, which):
    n, d = x.shape
    hp = w_down.shape[2]
    tm = rows.tile(512)
    th = _lane_tile(hp, 512)
    n_tiles = n // tm
    n_chunks = FFN_PROLOGUE_CHUNKS
    while n_chunks > hp // th:
        n_chunks //= 2
    assert tm % (16 * n_chunks) == 0
    k0 = 0 if which == 0 else 6
    emit_next = which == 0
    nxt = lambda i: jnp.minimum(i + 1, n_tiles - 1)
    w_spec = pl.BlockSpec((None, None, d, th), lambda i, j: (layer, which, 0, j))
    row_spec = pl.BlockSpec((tm, d), lambda i, j: (i, 0))
    norm_spec = lambda k: pl.BlockSpec((None, None, 1, d), lambda i, j: (layer, k, 0, 0))
    next_mod_spec = lambda k: pl.BlockSpec((None, None, None, 1, d),
                                           lambda i, j: (layer, rows.mod_row(nxt(i) * tm), k, 0, 0))
    in_specs = [row_spec,
                pl.BlockSpec((tm, d), lambda i, j: (nxt(i), 0)),
                _mod_spec(rows, tm, layer, k0, d),
                _mod_spec(rows, tm, layer, k0 + 1, d),
                next_mod_spec(k0), next_mod_spec(k0 + 1),
                _mod_spec(rows, tm, layer, k0 + 2, d),
                norm_spec(2 * which), w_spec, w_spec,
                pl.BlockSpec((None, None, th, d), lambda i, j: (layer, which, j, 0))]
    args = [x, x, mod, mod, mod, mod, mod, norm_w, w_g, w_u, w_down]
    out_specs, out_shape = row_spec, jax.ShapeDtypeStruct((n, d), F32)
    if emit_next:
        in_specs += [_mod_spec(rows, tm, layer, 3, d), _mod_spec(rows, tm, layer, 4, d), norm_spec(1)]
        args += [mod, mod, norm_w]
        out_specs, out_shape = [row_spec, row_spec], [out_shape, jax.ShapeDtypeStruct((n, d), BF16)]
    return _call(
        functools.partial(_ffn_kernel, emit_next, n_chunks), "ffn_half_step", (n_tiles, hp // th),
        in_specs, args, out_specs, out_shape, ("arbitrary", "arbitrary"),
        scratch=[pltpu.VMEM((tm, d), BF16), pltpu.VMEM((tm, d), BF16)])


def _inproj_kernel(h_ref, w_ref, o_ref):
    o_ref[...] = _dot(h_ref[...], w_ref[...]).astype(o_ref.dtype)


def _inproj(h, w, rows, layer, out_dtype, tn):
    n, d = h.shape
    cols = w.shape[2]
    tm = rows.tile(1024)
    return _call(
        _inproj_kernel, "mixer_in_proj", (n // tm, cols // tn),
        [pl.BlockSpec((tm, d), lambda i, j: (i, 0)),
         pl.BlockSpec((None, d, tn), lambda i, j: (layer, 0, j))],
        [h, w],
        pl.BlockSpec((tm, tn), lambda i, j: (i, j)),
        jax.ShapeDtypeStruct((n, cols), out_dtype),
        ("parallel", "arbitrary"))


def _inproj_t_kernel(h_ref, wt_ref, o_ref):
    o_ref[...] = _dot_nt(wt_ref[...], h_ref[...]).astype(o_ref.dtype)


def _inproj_t(h, w_t, rows, layer):
    n, d = h.shape
    cols = w_t.shape[1]
    tm = rows.tile(1024)
    tn = _lane_tile(cols, 1024)
    return _call(
        _inproj_t_kernel, "mixer_in_proj_t", (n // tm, cols // tn),
        [pl.BlockSpec((tm, d), lambda i, j: (i, 0)),
         pl.BlockSpec((None, tn, d), lambda i, j: (layer, j, 0))],
        [h, w_t],
        pl.BlockSpec((tn, tm), lambda i, j: (j, i)),
        jax.ShapeDtypeStruct((cols, n), BF16),
        ("parallel", "arbitrary"))


def _ones_column(rows):
    lane = lax.broadcasted_iota(jnp.int32, (rows, V_PAD - V_DIM), 1)
    return (lane == 0).astype(BF16)


def _mla_prep_kernel(rope, q_rank, kv_rank, p_ref, qnw_ref, kvnw_ref, wq_ref, wkv_ref, *rest):
    if rope:
        cos_ref, sin_ref, q_ref, k_ref, v_ref = rest
    else:
        q_ref, k_ref, v_ref, ckv_ref, kpe_ref = rest
    heads = MLA_HEADS
    p = p_ref[...]
    c_q = p[:, :q_rank]
    c_kv = p[:, q_rank:q_rank + kv_rank]
    o = q_rank + kv_rank
    k_pe = p[:, o:o + ROPE_DIM]
    k_pe_sw = p[:, o + ROPE_DIM:o + 2 * ROPE_DIM]

    qa = _dot(_rms(c_q, qnw_ref[...]).astype(BF16), wq_ref[...])
    ckv_n = _rms(c_kv, kvnw_ref[...])
    kv = _dot(ckv_n.astype(BF16), wkv_ref[...])
    scale = QK_DIM ** -0.5 * math.log2(math.e)
    if rope:
        cos = cos_ref[...]
        sin = sin_ref[...]
        k_pe = k_pe * cos + k_pe_sw * sin
    else:
        ckv_ref[...] = ckv_n
        kpe_ref[...] = k_pe
    ones = _ones_column(p.shape[0])
    pe0 = heads * NOPE_DIM
    sw0 = pe0 + heads * ROPE_DIM
    for h in range(heads):
        q_pe = qa[:, pe0 + h * ROPE_DIM:pe0 + (h + 1) * ROPE_DIM]
        if rope:
            q_pe = q_pe * cos + qa[:, sw0 + h * ROPE_DIM:sw0 + (h + 1) * ROPE_DIM] * sin
        q_ref[h, :, :NOPE_DIM] = (qa[:, h * NOPE_DIM:(h + 1) * NOPE_DIM] * scale).astype(BF16)
        q_ref[h, :, NOPE_DIM:] = (q_pe * scale).astype(BF16)
        k_ref[h, :, :NOPE_DIM] = kv[:, h * NOPE_DIM:(h + 1) * NOPE_DIM].astype(BF16)
        k_ref[h, :, NOPE_DIM:] = k_pe.astype(BF16)
        v0 = heads * NOPE_DIM + h * V_DIM
        v_ref[h, :, :V_DIM] = kv[:, v0:v0 + V_DIM].astype(BF16)
        v_ref[h, :, V_DIM:] = ones


def _mla_prep(p32, q_norm_w, kv_norm_w, wq, wkv, layer, row0, batch, t, rope_tabs, bases):
    ws = p32.shape[1]
    q_rank, kv_rank = q_norm_w.shape[-1], kv_norm_w.shape[-1]
    depth = q_norm_w.shape[0]
    tm = min(256, t)
    nt = t // tm
    heads = MLA_HEADS
    rope = rope_tabs is not None
    in_specs = [pl.BlockSpec((tm, ws), lambda b, i: (row0 // tm + b * nt + i, 0)),
                pl.BlockSpec((None, 1, q_rank), lambda b, i: (layer, 0, 0)),
                pl.BlockSpec((None, 1, kv_rank), lambda b, i: (layer, 0, 0)),
                pl.BlockSpec((None,) + wq.shape[1:], lambda b, i: (layer, 0, 0)),
                pl.BlockSpec((None,) + wkv.shape[1:], lambda b, i: (layer, 0, 0))]
    args = [p32, q_norm_w, kv_norm_w, wq, wkv]
    head_spec = lambda width: pl.BlockSpec((None, heads, tm, width), lambda b, i: (b, 0, i, 0))
    out_specs = [head_spec(QK_DIM), head_spec(QK_DIM), head_spec(V_PAD)]
    out_shape = [jax.ShapeDtypeStruct((batch, heads, t, QK_DIM), BF16),
                 jax.ShapeDtypeStruct((batch, heads, t, QK_DIM), BF16),
                 jax.ShapeDtypeStruct((batch, heads, t, V_PAD), BF16)]
    if rope:
        in_specs += [pl.BlockSpec((tm, ROPE_DIM), lambda b, i: (i, 0))] * 2
        args += list(rope_tabs)
        all_bases = ()
    else:
        out_specs += [pl.BlockSpec((None, None, tm, kv_rank), lambda b, i: (b, layer, i, 0)),
                      pl.BlockSpec((None, None, tm, ROPE_DIM), lambda b, i: (b, layer, i, 0))]
        out_shape += [jax.ShapeDtypeStruct((batch, depth, t, kv_rank), F32),
                      jax.ShapeDtypeStruct((batch, depth, t, ROPE_DIM), F32)]
        all_bases = (None, None, None) + tuple(bases)
    return _call(functools.partial(_mla_prep_kernel, rope, q_rank, kv_rank), "mla_prep",
                 (batch, nt), in_specs, args, out_specs, out_shape, ("parallel", "parallel"),
                 bases=all_bases)


def _cache_kv_kernel(ckv_ref, kpe_ref, wkv_ref, k_ref, v_ref):
    heads = MLA_HEADS
    kv = _dot(ckv_ref[...].astype(BF16), wkv_ref[...])
    k_pe = kpe_ref[...].astype(BF16)
    ones = _ones_column(kv.shape[0])
    for h in range(heads):
        k_ref[h, :, :NOPE_DIM] = kv[:, h * NOPE_DIM:(h + 1) * NOPE_DIM].astype(BF16)
        k_ref[h, :, NOPE_DIM:] = k_pe
        v0 = heads * NOPE_DIM + h * V_DIM
        v_ref[h, :, :V_DIM] = kv[:, v0:v0 + V_DIM].astype(BF16)
        v_ref[h, :, V_DIM:] = ones


def _cache_kv(cache_ckv, cache_kpe, wkv):
    batch, depth, past, kv_rank = cache_ckv.shape
    heads = MLA_HEADS
    return _call(
        _cache_kv_kernel, "mla_cache_kv", (batch, depth),
        [pl.BlockSpec((None, None, past, kv_rank), lambda b, l: (b, l, 0, 0)),
         pl.BlockSpec((None, None, past, ROPE_DIM), lambda b, l: (b, l, 0, 0)),
         pl.BlockSpec((None,) + wkv.shape[1:], lambda b, l: (l, 0, 0))],
        [cache_ckv, cache_kpe, wkv],
        [pl.BlockSpec((None, None, heads, past, QK_DIM), lambda b, l: (b, l, 0, 0, 0)),
         pl.BlockSpec((None, None, heads, past, V_PAD), lambda b, l: (b, l, 0, 0, 0))],
        [jax.ShapeDtypeStruct((batch, depth, heads, past, QK_DIM), BF16),
         jax.ShapeDtypeStruct((batch, depth, heads, past, V_PAD), BF16)],
        ("parallel", "parallel"))


ATTN_KEY_CHUNK = 512


def _attn_kernel(past, q_ref, qn_ref, k_ref, kn_ref, v_ref, *rest):
    if past:
        kc_ref, kcn_ref, vc_ref = rest[:3]
        rest = rest[3:]
    else:
        kc_ref = kcn_ref = vc_ref = None
    o_ref, s0_ref, s1_ref, m0_ref, m1_ref = rest
    tq = qn_ref.shape[0]
    t = k_ref.shape[0]
    chunk = min(ATTN_KEY_CHUNK, t)

    def scores(q, keys_ref, cache_keys_ref, s_ref, m_ref):
        s = _dot_nt(q, keys_ref[...])
        m = jnp.max(s, axis=-1, keepdims=True)
        if past:
            sc = _dot_nt(q, cache_keys_ref[...])
            m = jnp.maximum(m, jnp.max(sc, axis=-1, keepdims=True))
            s_ref[:, :past] = sc
        s_ref[:, past:] = s
        m_ref[...] = m

    def values(r, s_ref, m_ref):
        m = m_ref[...]
        acc = None
        if past:
            acc = _dot(jnp.exp2(s_ref[:, :past] - m).astype(BF16), vc_ref[...])
        for c in range(0, t, chunk):
            p = jnp.exp2(s_ref[:, past + c:past + c + chunk] - m).astype(BF16)
            d = _dot(p, v_ref[c:c + chunk, :])
            acc = d if acc is None else acc + d
        o_ref[r * tq:(r + 1) * tq, :] = (acc[:, :V_DIM] / acc[:, V_DIM:V_DIM + 1]).astype(o_ref.dtype)

    @pl.when(pl.program_id(0) == 0)
    def _():
        scores(q_ref[:tq, :], k_ref, kc_ref, s0_ref, m0_ref)

    values(0, s0_ref, m0_ref)
    scores(q_ref[tq:, :], k_ref, kc_ref, s1_ref, m1_ref)
    values(1, s1_ref, m1_ref)
    scores(qn_ref[...], kn_ref, kcn_ref, s0_ref, m0_ref)


def _attention(q, k, v, cache, layer, n_rows, row0, base):
    batch, heads, t, _ = q.shape
    tq = min(512, t // 2)
    pair = 2 * tq
    npair = t // pair
    n_steps = batch * heads * npair
    past = cache[0].shape[3] if cache is not None else 0
    assert t % pair == 0 and row0 % pair == 0

    def where(tile):
        tile = jnp.minimum(tile, 2 * n_steps - 1)
        p = tile // 2
        return p // (heads * npair), (p // npair) % heads, 2 * (p % npair) + tile % 2

    def pair_map(g):
        b, h, i = where(2 * g)
        return b, h, i // 2, 0

    def next_map(g):
        b, h, i = where(2 * g + 2)
        return b, h, i, 0

    def kv_map(shift):
        def index(g):
            b, h, _ = where(2 * g + shift)
            return b, h, 0, 0
        return index

    def cache_map(shift):
        def index(g):
            b, h, _ = where(2 * g + shift)
            return b, layer, h, 0, 0
        return index

    def o_map(g):
        b, h, i = where(2 * g)
        return row0 // pair + b * npair + i // 2, h

    in_specs = [pl.BlockSpec((None, None, pair, QK_DIM), pair_map),
                pl.BlockSpec((None, None, tq, QK_DIM), next_map),
                pl.BlockSpec((None, None, t, QK_DIM), kv_map(0)),
                pl.BlockSpec((None, None, t, QK_DIM), kv_map(2)),
                pl.BlockSpec((None, None, t, V_PAD), kv_map(0))]
    args = [q, q, k, k, v]
    if past:
        in_specs += [pl.BlockSpec((None, None, None, past, QK_DIM), cache_map(0)),
                     pl.BlockSpec((None, None, None, past, QK_DIM), cache_map(2)),
                     pl.BlockSpec((None, None, None, past, V_PAD), cache_map(0))]
        args += [cache[0], cache[0], cache[1]]
    s_total = past + t
    return _call(
        functools.partial(_attn_kernel, past), "mla_attention", (n_steps,),
        in_specs, args,
        pl.BlockSpec((pair, V_DIM), o_map),
        jax.ShapeDtypeStruct((n_rows, heads * V_DIM), BF16),
        ("arbitrary",), bases=(base,),
        scratch=[pltpu.VMEM((tq, s_total), F32), pltpu.VMEM((tq, s_total), F32),
                 pltpu.VMEM((tq, 1), F32), pltpu.VMEM((tq, 1), F32)])


def _split3(x):
    hi = x.astype(BF16)
    r = x - hi.astype(F32)
    mid = r.astype(BF16)
    return hi, mid, (r - mid.astype(F32)).astype(BF16)


def _mlstm_kernel(has_init, dh, *refs):
    (qf_ref, kf_ref, vf_ref, qb_ref, kb_ref, vb_ref, gf_ref, gb_ref, gtf_ref, gtb_ref,
     brow_ref, bcol_ref) = refs[:12]
    refs = refs[12:]
    if has_init:
        c0_ref, n0_ref, m0_ref = refs[:3]
        refs = refs[3:]
    hf_ref, hb_ref, c_ref, n_ref, m_ref = refs
    heads = MLSTM_HEADS
    n_gate = N_DIR * 2 * heads
    step = pl.program_id(1)

    @pl.when(step == 0)
    def _():
        if has_init:
            c_ref[...] = c0_ref[...]
            n_ref[...] = n0_ref[...]
            m_ref[...] = m0_ref[...]
        else:
            c_ref[...] = jnp.zeros_like(c_ref)
            n_ref[...] = jnp.zeros_like(n_ref)
            m_ref[...] = jnp.zeros_like(m_ref)

    tok0 = lax.broadcasted_iota(jnp.int32, (CHUNK, CHUNK), 0)
    tok1 = lax.broadcasted_iota(jnp.int32, (CHUNK, CHUNK), 1)
    k_scale = dh ** -0.5
    m_all = m_ref[...]
    m_out = m_all
    unit_lane = lax.broadcasted_iota(jnp.int32, m_all.shape, 1)
    for d in range(N_DIR):
        q_ref, k_ref, vt_ref, g_ref, gt_ref, h_ref = (
            (qf_ref, kf_ref, vf_ref, gf_ref, gtf_ref, hf_ref) if d == 0 else
            (qb_ref, kb_ref, vb_ref, gb_ref, gtb_ref, hb_ref))
        seen_t = (tok0 <= tok1) if d == 0 else (tok0 >= tok1)
        seen_t_bf = seen_t.astype(BF16)
        seen_bf = ((tok1 <= tok0) if d == 0 else (tok1 >= tok0)).astype(BF16)
        pre_col = g_ref[:, :n_gate] + brow_ref[...]
        pre_row = gt_ref[...] + bcol_ref[...]
        cum_col = sum(_dot(seen_bf, part) for part in _split3(_log_sigmoid(pre_col)))
        cum_row = sum(_dot(part, seen_t_bf) for part in _split3(_log_sigmoid(pre_row)))
        last = CHUNK - 1 if d == 0 else 0
        for h in range(heads):
            ci = d * 2 * heads + h
            cf = ci + heads
            sid = d * heads + h
            sl = slice(h * dh, (h + 1) * dh)
            c_col = pre_col[:, ci:ci + 1] - cum_col[:, cf:cf + 1]
            i_row = pre_row[ci:ci + 1, :]
            b_row = cum_row[cf:cf + 1, :]
            b_end = b_row[:, last:last + 1]
            m_prev = m_all[:, sid:sid + 1]
            a_row = b_row + m_prev
            dmat = jnp.where(seen_t, b_row + c_col, -jnp.inf)
            m_t = jnp.maximum(a_row, jnp.max(dmat, axis=0, keepdims=True))
            w_intra = jnp.exp(dmat - m_t)
            w_inter = jnp.exp(a_row - m_t)

            q = q_ref[:, sl]
            k_bf = (k_ref[:, sl].astype(F32) * k_scale).astype(BF16)
            v_t = vt_ref[sl, :]
            c_prev = c_ref[d, h]
            n_prev = n_ref[d, h]

            s_t = _dot_nt(k_bf, q) * w_intra
            num = w_inter * _dot_nt(c_prev.astype(BF16), q) + _dot(v_t, s_t.astype(BF16))
            n_rows = jnp.broadcast_to(n_prev, (8, dh)).astype(BF16)
            nq = _dot_nt(n_rows, q)[0:1, :]
            den = w_inter * nq + jnp.sum(s_t, axis=0, keepdims=True)
            h_ref[sl, :] = num / jnp.maximum(jnp.abs(den), jnp.exp(-m_t))

            g_row = b_end - b_row + i_row
            m_new = jnp.maximum(b_end + m_prev, jnp.max(g_row, axis=1, keepdims=True))
            w_pos = jnp.exp(g_row - m_new)
            w_carry = jnp.exp(b_end + m_prev - m_new)
            c_ref[d, h] = w_carry * c_prev + _dot((v_t.astype(F32) * w_pos).astype(BF16), k_bf)
            w_rows = jnp.broadcast_to(w_pos, (8, CHUNK)).astype(BF16)
            n_ref[d, h] = w_carry * n_prev + _dot(w_rows, k_bf)[0:1, :]
            m_out = jnp.where(unit_lane == sid, m_new, m_out)
    m_ref[...] = m_out


def _mlstm(pb, pb_t, p32, gates_t, gate_b, state, layer, row0, batch, t, gate_blk, dh, bases):
    heads = MLSTM_HEADS
    n_gate = N_DIR * 2 * heads
    depth = gate_b.shape[0]
    nc = t // CHUNK
    blk0 = row0 // CHUNK
    w = heads * dh
    fwd = lambda b, c: blk0 + b * nc + c
    bwd = lambda b, c: blk0 + b * nc + nc - 1 - c

    def tok(col, blk):
        return pl.BlockSpec((CHUNK, w), lambda b, c: (blk(b, c), col))

    def feat(blk):
        return pl.BlockSpec((w, CHUNK), lambda b, c: (0, blk(b, c)))

    in_specs = [tok(0, fwd), tok(1, fwd), feat(fwd), tok(0, bwd), tok(1, bwd), feat(bwd),
                pl.BlockSpec((CHUNK, LANES), lambda b, c: (fwd(b, c), gate_blk)),
                pl.BlockSpec((CHUNK, LANES), lambda b, c: (bwd(b, c), gate_blk)),
                pl.BlockSpec((n_gate, CHUNK), lambda b, c: (0, fwd(b, c))),
                pl.BlockSpec((n_gate, CHUNK), lambda b, c: (0, bwd(b, c))),
                pl.BlockSpec((None, 1, n_gate), lambda b, c: (layer, 0, 0)),
                pl.BlockSpec((None, n_gate, 1), lambda b, c: (layer, 0, 0))]
    args = [pb, pb, pb_t, pb, pb, pb_t, p32, p32, gates_t, gates_t,
            gate_b.reshape(-1, 1, n_gate), gate_b.reshape(-1, n_gate, 1)]
    has_init = state is not None
    state_shapes = [(N_DIR, heads, dh, dh), (N_DIR, heads, 1, dh), (1, N_DIR * heads)]
    if has_init:
        c0, n0, m0 = state
        in_specs += [pl.BlockSpec((None, None) + shp, lambda b, c, z=(0,) * len(shp): (b, layer) + z)
                     for shp in state_shapes]
        args += [c0, n0.reshape((batch, depth) + state_shapes[1]), m0.reshape((batch, depth) + state_shapes[2])]
        st_specs = [pl.BlockSpec((None,) + shp, lambda b, c, z=(0,) * len(shp): (b,) + z) for shp in state_shapes]
        st_shapes = [jax.ShapeDtypeStruct((batch,) + shp, F32) for shp in state_shapes]
        all_bases = ()
    else:
        st_specs = [pl.BlockSpec((None, None) + shp, lambda b, c, z=(0,) * len(shp): (b, layer) + z)
                    for shp in state_shapes]
        st_shapes = [jax.ShapeDtypeStruct((batch, depth) + shp, F32) for shp in state_shapes]
        all_bases = (None, None) + tuple(bases)
    return _call(
        functools.partial(_mlstm_kernel, has_init, dh), "mlstm_scan", (batch, nc), in_specs, args,
        [pl.BlockSpec((w, CHUNK), lambda b, c: (0, b * nc + c)),
         pl.BlockSpec((w, CHUNK), lambda b, c: (0, b * nc + nc - 1 - c))] + st_specs,
        [jax.ShapeDtypeStruct((w, batch * t), F32),
         jax.ShapeDtypeStruct((w, batch * t), F32)] + st_shapes,
        ("parallel", "arbitrary"), bases=all_bases)


def _mlstm_post_kernel(dh, hf_ref, hb_ref, o_ref, w_ref, y_ref):
    tm = hf_ref.shape[1]
    hm = hf_ref[...] + hb_ref[...]
    gate = _sigmoid(o_ref[...].astype(F32))
    w = w_ref[...]
    eye = (lax.broadcasted_iota(jnp.int32, (tm, tm), 0)
           == lax.broadcasted_iota(jnp.int32, (tm, tm), 1)).astype(BF16)
    for h in range(MLSTM_HEADS):
        sl = slice(h * dh, (h + 1) * dh)
        x = hm[sl, :]
        y = x * lax.rsqrt(jnp.mean(x * x, axis=0, keepdims=True) + EPS) * w[sl, :]
        y_t = (gate[sl, :] * y).astype(BF16)
        y_ref[:, sl] = _dot_nt(eye, y_t).astype(y_ref.dtype)


def _mlstm_post(h_f, h_b, pb_t, m_norm_w, layer, n_rows, row0, dh, base):
    w, n = h_f.shape
    tm = min(256, n)
    return _call(
        functools.partial(_mlstm_post_kernel, dh), "mlstm_post", (n // tm,),
        [pl.BlockSpec((w, tm), lambda i: (0, i)),
         pl.BlockSpec((w, tm), lambda i: (0, i)),
         pl.BlockSpec((w, tm), lambda i: (1, row0 // tm + i)),
         pl.BlockSpec((None, w, 1), lambda i: (layer, 0, 0))],
        [h_f, h_b, pb_t, m_norm_w],
        pl.BlockSpec((tm, w), lambda i: (row0 // tm + i, 0)),
        jax.ShapeDtypeStruct((n_rows, w), BF16),
        ("parallel",), bases=(base,))


POOL_TILE = 256


def _pool_bands():
    t = np.arange(POOL_TILE)[:, None]
    bands = np.zeros((POOL_GROUPS, 3, POOL_TILE, POOL_TILE), np.float32)
    for g, win in enumerate(POOL_WINDOWS):
        for part in range(3):
            s = np.arange(POOL_TILE)[None, :] + (part - 1) * POOL_TILE
            bands[g, part] = (s >= t - win // 2) & (s < t - win // 2 + win)
    return jnp.asarray(bands, BF16)


def _pool_kernel(t_seq, gd, up_ref, um_ref, un_ref, band_ref, pw_ref, ps_ref, y_ref):
    j = pl.program_id(1)
    has_prev = (j > 0).astype(F32)
    has_next = (j < pl.num_programs(1) - 1).astype(F32)
    tile = um_ref.shape[0]
    pos = j * tile + lax.broadcasted_iota(jnp.int32, (tile, 1), 0)
    for g, win in enumerate(POOL_WINDOWS):
        sl = slice(g * gd, (g + 1) * gd)
        u = um_ref[:, sl]
        acc = (_dot(band_ref[g, 1], u)
               + has_prev * _dot(band_ref[g, 0], up_ref[:, sl])
               + has_next * _dot(band_ref[g, 2], un_ref[:, sl]))
        lo = jnp.clip(pos - win // 2, 0, t_seq)
        hi = jnp.clip(pos - win // 2 + win, 0, t_seq)
        pooled = acc / (hi - lo).astype(F32) - u.astype(F32)
        y = _dot(pooled.astype(BF16), pw_ref[g]) * ps_ref[:, sl]
        y_ref[:, sl] = y.astype(y_ref.dtype)


def _pool(pb, bands, pool_w, pool_scale, layer, n_rows, row0, batch, t, base):
    gd = pool_w.shape[-1]
    w = POOL_GROUPS * gd
    tile = POOL_TILE
    assert t % tile == 0
    nt = t // tile
    blk0 = row0 // tile

    def u_spec(shift):
        return pl.BlockSpec((tile, w), lambda b, j: (blk0 + b * nt + jnp.clip(j + shift, 0, nt - 1), 2))

    return _call(
        functools.partial(_pool_kernel, t, gd), "multiscale_pool", (batch, nt),
        [u_spec(-1), u_spec(0), u_spec(1),
         pl.BlockSpec(bands.shape, lambda b, j: (0, 0, 0, 0)),
         pl.BlockSpec((None, POOL_GROUPS, gd, gd), lambda b, j: (layer, 0, 0, 0)),
         pl.BlockSpec((None, 1, w), lambda b, j: (layer, 0, 0))],
        [pb, pb, pb, bands, pool_w, pool_scale],
        pl.BlockSpec((tile, w), lambda b, j: (blk0 + b * nt + j, 0)),
        jax.ShapeDtypeStruct((n_rows, w), BF16),
        ("parallel", "parallel"), bases=(base,))


def _merge_kernel(ya_ref, yb_ref, yc_ref, ga_ref, gb_ref, gc_ref, w_ref, o_ref):
    acc = _sigmoid(ga_ref[...].astype(F32)) * _dot(ya_ref[...], w_ref[0])
    acc += _sigmoid(gb_ref[...].astype(F32)) * _dot(yb_ref[...], w_ref[1])
    acc += _sigmoid(gc_ref[...].astype(F32)) * _dot(yc_ref[...], w_ref[2])
    o_ref[...] = acc.astype(o_ref.dtype)


def _merge(y_a, y_b, y_c, pb, w_branch, rows, layer, gate_col0):
    n, bw = y_a.shape
    d = w_branch.shape[-1]
    tm = rows.tile(1024)
    tn = min(512, d)
    g0 = gate_col0 // tn
    nd = d // tn

    def gate_spec(k):
        return pl.BlockSpec((tm, tn), lambda i, j: (i, g0 + k * nd + j))

    y_spec = pl.BlockSpec((tm, bw), lambda i, j: (i, 0))
    return _call(
        _merge_kernel, "branch_merge", (n // tm, nd),
        [y_spec, y_spec, y_spec, gate_spec(0), gate_spec(1), gate_spec(2),
         pl.BlockSpec((None, N_BRANCH, bw, tn), lambda i, j: (layer, 0, 0, j))],
        [y_a, y_b, y_c, pb, pb, pb, w_branch],
        pl.BlockSpec((tm, tn), lambda i, j: (i, j)),
        jax.ShapeDtypeStruct((n, d), BF16),
        ("parallel", "arbitrary"))


def _outproj_kernel(m_ref, w_ref, x_ref, g_ref, o_ref):
    o_ref[...] = x_ref[...] + g_ref[...] * _dot(m_ref[...], w_ref[...])


def _outproj(merged, w_out, x, mod, rows, layer):
    n, d = x.shape
    tm = rows.tile(1024)
    tn = min(1024, d)
    return _call(
        _outproj_kernel, "mixer_out_proj", (n // tm, d // tn),
        [pl.BlockSpec((tm, d), lambda i, j: (i, 0)),
         pl.BlockSpec((None, d, tn), lambda i, j: (layer, 0, j)),
         pl.BlockSpec((tm, tn), lambda i, j: (i, j)),
         pl.BlockSpec((None, None, None, 1, tn),
                      lambda i, j: (layer, rows.mod_row(i * tm), 5, 0, j))],
        [merged, w_out, x, mod],
        pl.BlockSpec((tm, tn), lambda i, j: (i, j)),
        jax.ShapeDtypeStruct((n, d), F32),
        ("parallel", "arbitrary"))


def _final_norm_kernel(x_ref, w_ref, o_ref):
    o_ref[...] = _rms(x_ref[...], w_ref[...])


def _final_norm(x, w, row0, n_rows):
    d = x.shape[1]
    tm = min(512, n_rows)
    return _call(
        _final_norm_kernel, "final_norm", (n_rows // tm,),
        [pl.BlockSpec((tm, d), lambda i: (row0 // tm + i, 0)),
         pl.BlockSpec((1, d), lambda i: (0, 0))],
        [x, w.reshape(1, d)],
        pl.BlockSpec((tm, d), lambda i: (i, 0)),
        jax.ShapeDtypeStruct((n_rows, d), F32),
        ("parallel",))


def _rope_swap_index():
    quarter = ROPE_DIM // 4
    idx = np.arange(ROPE_DIM).reshape(2, 2, quarter)
    return idx[:, ::-1, :].reshape(-1)


def _rope_tables(t):
    pos = jnp.arange(t)
    row = (pos // GRID_W).astype(F32)
    col = (pos % GRID_W).astype(F32)
    n_freq = ROPE_DIM // 4
    inv_freq = jnp.power(ROPE_BASE, -jnp.arange(n_freq, dtype=F32) / n_freq)
    ang_r = row[:, None] * inv_freq
    ang_c = col[:, None] * inv_freq
    cos = jnp.concatenate([jnp.cos(ang_r), jnp.cos(ang_r), jnp.cos(ang_c), jnp.cos(ang_c)], axis=-1)
    sin = jnp.concatenate([-jnp.sin(ang_r), jnp.sin(ang_r), -jnp.sin(ang_c), jnp.sin(ang_c)], axis=-1)
    return cos, sin


def kernel(x_prompt, x_sample, cache_ckv, cache_kpe, state_C, state_n, state_m, c, c_ctx, w_mod, b_mod, norm_w, ffn_w_gu, ffn_w_down, w_in, q_norm_w, kv_norm_w, w_uq, w_ukv, mlstm_gate_b, mlstm_norm_w, pool_w, pool_scale, w_branch, w_out, final_norm_w):
    batch, seq, d = x_prompt.shape
    dec_batch, dec_seq, _ = x_sample.shape
    depth = w_mod.shape[0]
    q_rank, kv_rank = q_norm_w.shape[1], kv_norm_w.shape[1]
    heads = MLA_HEADS
    mw = mlstm_norm_w.shape[1]
    dh = mw // MLSTM_HEADS
    pw = pool_scale.shape[1]
    ffn_h = ffn_w_down.shape[2]
    n_gate = N_DIR * 2 * MLSTM_HEADS
    assert mw == pw == w_branch.shape[2] == heads * V_DIM
    rows = _Rows(batch * seq, seq, dec_batch * dec_seq, dec_seq)
    n = rows.n

    hp = _round_up(ffn_h, 512)
    hid_pad = ((0, 0), (0, 0), (0, 0), (0, hp - ffn_h))
    w_g = jnp.pad(ffn_w_gu[..., :ffn_h], hid_pad).astype(BF16)
    w_u = jnp.pad(ffn_w_gu[..., ffn_h:], hid_pad).astype(BF16)
    wdn = jnp.pad(ffn_w_down, ((0, 0), (0, 0), (0, hp - ffn_h), (0, 0))).astype(BF16)

    sizes = (q_rank, kv_rank, ROPE_DIM, mw, mw, mw, mw, n_gate, pw, N_BRANCH * d)
    offs = np.concatenate([[0], np.cumsum(sizes)])
    swap = _rope_swap_index()
    small_cols = q_rank + kv_rank + 2 * ROPE_DIM + LANES
    w_small = jnp.concatenate(
        [w_in[:, :, :offs[3]], w_in[:, :, offs[2]:offs[3]][:, :, swap], w_in[:, :, offs[7]:offs[8]],
         jnp.zeros((depth, d, LANES - n_gate), F32)], axis=-1).astype(BF16)
    gate_blk = (q_rank + kv_rank + 2 * ROPE_DIM) // LANES
    w_big = jnp.concatenate([w_in[:, :, offs[3]:offs[5]], w_in[:, :, offs[8]:]], axis=-1).astype(BF16)
    w_feat = jnp.swapaxes(w_in[:, :, offs[5]:offs[7]], 1, 2).astype(BF16)
    gate_col0 = 3 * mw

    wq4 = w_uq.reshape(depth, q_rank, heads, QK_DIM)
    wq_pe = wq4[..., NOPE_DIM:]
    wq = jnp.concatenate([wq4[..., :NOPE_DIM].reshape(depth, q_rank, -1),
                          wq_pe.reshape(depth, q_rank, -1),
                          wq_pe[..., swap].reshape(depth, q_rank, -1)], axis=-1).astype(BF16)
    wkv4 = w_ukv.reshape(depth, kv_rank, heads, NOPE_DIM + V_DIM)
    wkv = jnp.concatenate([wkv4[..., :NOPE_DIM].reshape(depth, kv_rank, -1),
                           wkv4[..., NOPE_DIM:].reshape(depth, kv_rank, -1)], axis=-1).astype(BF16)
    wbr = w_branch.astype(BF16)
    wout = w_out.astype(BF16)
    pwb = pool_w.astype(BF16)
    norm_w4 = norm_w.reshape(depth, 3, 1, d)
    qnw = q_norm_w.reshape(depth, 1, q_rank)
    kvnw = kv_norm_w.reshape(depth, 1, kv_rank)
    mnw = mlstm_norm_w.reshape(depth, mw, 1)
    psc = pool_scale.reshape(depth, 1, pw)
    bands = _pool_bands()
    rope_tabs = _rope_tables(dec_seq)

    cond = jnp.concatenate([c_ctx[None, :], c, jnp.zeros((COND_ROWS - 1 - dec_batch, d), F32)], axis=0)
    mod = _mod_all(cond, w_mod, b_mod).reshape(depth, COND_ROWS, N_MOD, 1, d)

    cache_kv = _cache_kv(cache_ckv, cache_kpe, wkv)

    x = jnp.concatenate([x_prompt.reshape(rows.n_ctx, d), x_sample.reshape(rows.n_lat, d)], axis=0)
    new_cache = (None, None)
    new_state = (None, None, None)
    for l in range(depth):
        x, h_mix = _ffn(x, mod, norm_w4, w_g, w_u, wdn, rows, l, 0)

        p32 = _inproj(h_mix, w_small, rows, l, F32, small_cols)
        pb = _inproj(h_mix, w_big, rows, l, BF16, _lane_tile(w_big.shape[2], 1024))
        pb_t = _inproj_t(h_mix, w_feat, rows, l)
        gates_t = p32[:, gate_blk * LANES:gate_blk * LANES + n_gate].T

        q_c, k_c, v_c, *new_cache = _mla_prep(p32, qnw, kvnw, wq, wkv, l, 0, batch, seq, None, new_cache)
        q_s, k_s, v_s = _mla_prep(p32, qnw, kvnw, wq, wkv, l, rows.n_ctx, dec_batch, dec_seq, rope_tabs, None)
        y_a = _attention(q_c, k_c, v_c, None, l, n, 0, None)
        y_a = _attention(q_s, k_s, v_s, cache_kv, l, n, rows.n_ctx, y_a)

        hf_c, hb_c, *new_state = _mlstm(pb, pb_t, p32, gates_t, mlstm_gate_b, None, l, 0, batch, seq,
                                        gate_blk, dh, new_state)
        hf_s, hb_s, _, _, _ = _mlstm(pb, pb_t, p32, gates_t, mlstm_gate_b, (state_C, state_n, state_m), l,
                                     rows.n_ctx, dec_batch, dec_seq, gate_blk, dh, None)
        y_b = _mlstm_post(hf_c, hb_c, pb_t, mnw, l, n, 0, dh, None)
        y_b = _mlstm_post(hf_s, hb_s, pb_t, mnw, l, n, rows.n_ctx, dh, y_b)

        y_c = _pool(pb, bands, pwb, psc, l, n, 0, batch, seq, None)
        y_c = _pool(pb, bands, pwb, psc, l, n, rows.n_ctx, dec_batch, dec_seq, y_c)

        merged = _merge(y_a, y_b, y_c, pb, wbr, rows, l, gate_col0)
        x = _outproj(merged, wout, x, mod, rows, l)
        x = _ffn(x, mod, norm_w4, w_g, w_u, wdn, rows, l, 1)

    y_prompt = _final_norm(x, final_norm_w, 0, rows.n_ctx).reshape(batch, seq, d)
    y_sample = _final_norm(x, final_norm_w, rows.n_ctx, rows.n_lat).reshape(dec_batch, dec_seq, d)
    new_c, new_n, new_m = new_state
    return (y_prompt, y_sample, new_cache[0], new_cache[1], new_c,
            new_n.reshape(batch, depth, N_DIR, MLSTM_HEADS, dh),
            new_m.reshape(batch, depth, N_DIR, MLSTM_HEADS))
```

```python
import functools
import math

import numpy as np
import jax
import jax.numpy as jnp
from jax import lax
from jax.experimental import pallas as pl
from jax.experimental.pallas import tpu as pltpu

GRID_W = 64
EPS = 1e-6
N_MOD = 9
MLA_HEADS = 8
NOPE_DIM = 128
ROPE_DIM = 64
V_DIM = 128
QK_DIM = NOPE_DIM + ROPE_DIM
ROPE_BASE = 10000.0
MLSTM_HEADS = 4
N_DIR = 2
CHUNK = 128
POOL_WINDOWS = (2, 4, 8, 16)
POOL_GROUPS = 4
N_BRANCH = 3

LANES = 128
VMEM_LIMIT_MB = 56
COND_ROWS = 8
PACK_ROWS = 16
V_PAD = 2 * V_DIM

F32 = jnp.float32
BF16 = jnp.bfloat16


def _params(sem):
    return pltpu.CompilerParams(dimension_semantics=sem, vmem_limit_bytes=VMEM_LIMIT_MB << 20)


def _call(kernel, name, grid, in_specs, args, out_specs, out_shape, sem, bases=(), scratch=()):
    n_in = len(args)
    extra = [b for b in bases if b is not None]
    aliases = {}
    for k, b in enumerate(bases):
        if b is not None:
            aliases[n_in + len(aliases)] = k

    def body(*refs):
        kernel(*refs[:n_in], *refs[n_in + len(extra):])

    return pl.pallas_call(
        body if extra else kernel, grid=grid,
        in_specs=list(in_specs) + [pl.BlockSpec(memory_space=pl.ANY)] * len(extra),
        out_specs=out_specs, out_shape=out_shape, input_output_aliases=aliases,
        scratch_shapes=list(scratch), compiler_params=_params(sem), name=name)(*args, *extra)


def _round_up(n, m):
    return (n + m - 1) // m * m


def _lane_tile(n, cap):
    t = cap - cap % LANES
    while n % t:
        t -= LANES
    return t


def _sigmoid(x):
    return 1.0 / (1.0 + jnp.exp(-x))


def _log_sigmoid(x):
    return -(jnp.maximum(-x, 0.0) + jnp.log1p(jnp.exp(-jnp.abs(x))))


def _rms(x, w):
    return x * lax.rsqrt(jnp.mean(x * x, axis=-1, keepdims=True) + EPS) * w


def _dot(a, b):
    return jnp.dot(a, b, preferred_element_type=F32)


def _dot_nt(a, b):
    return lax.dot_general(a, b, (((1,), (1,)), ((), ())), preferred_element_type=F32)


def _mod_kernel(c_ref, w_ref, b_ref, o_ref):
    c = c_ref[...]
    a = (c * _sigmoid(c)).astype(BF16)
    o_ref[...] = _dot(a, w_ref[...].astype(BF16)) + b_ref[...]


def _mod_all(cond, w_mod, b_mod):
    depth, d, nd = w_mod.shape
    tn = _lane_tile(nd, 1024)
    return _call(
        _mod_kernel, "adaln_mod", (depth, nd // tn),
        [pl.BlockSpec((COND_ROWS, d), lambda l, j: (0, 0)),
         pl.BlockSpec((None, d, tn), lambda l, j: (l, 0, j)),
         pl.BlockSpec((None, 1, tn), lambda l, j: (l, 0, j))],
        [cond, w_mod, b_mod.reshape(depth, 1, nd)],
        pl.BlockSpec((None, COND_ROWS, tn), lambda l, j: (l, 0, j)),
        jax.ShapeDtypeStruct((depth, COND_ROWS, nd), F32),
        ("parallel", "parallel"))


class _Rows:
    def __init__(self, n_ctx, t_ctx, n_lat, t_lat):
        self.n_ctx, self.t_ctx, self.n_lat, self.t_lat = n_ctx, t_ctx, n_lat, t_lat
        self.n = n_ctx + n_lat

    def mod_row(self, row):
        return jnp.where(row < self.n_ctx, 0, 1 + (row - self.n_ctx) // self.t_lat)

    def tile(self, cap):
        t = min(cap, self.n_ctx, self.t_lat)
        assert self.n_ctx % t == 0 and self.t_lat % t == 0
        return t


def _mod_spec(rows, tm, layer, k, d):
    return pl.BlockSpec((None, None, None, 1, d),
                        lambda i, j: (layer, rows.mod_row(i * tm), k, 0, 0))


def _norm_mod_to(h_ref, x_ref, nw_ref, sh_ref, sc_ref):
    y = _rms(x_ref[...], nw_ref[...])
    h_ref[...] = (y * (1.0 + sc_ref[...]) + sh_ref[...]).astype(h_ref.dtype)


def _ffn_kernel(emit_next, x_ref, sh_ref, sc_ref, g_ref, nw_ref, wg_ref, wu_ref, wd_ref, *rest):
    if emit_next:
        sh2_ref, sc2_ref, nw2_ref, o_ref, h2_ref, h_ref = rest
    else:
        o_ref, h_ref = rest
    j = pl.program_id(1)

    @pl.when(j == 0)
    def _():
        _norm_mod_to(h_ref, x_ref, nw_ref, sh_ref, sc_ref)
        o_ref[...] = jnp.zeros_like(o_ref)

    h = h_ref[...]
    g = _dot(h, wg_ref[...])
    u = _dot(h, wu_ref[...])
    a = (g * _sigmoid(g) * u).astype(BF16)
    o_ref[...] += _dot(a, wd_ref[...])

    @pl.when(j == pl.num_programs(1) - 1)
    def _():
        o_ref[...] = x_ref[...] + 0.5 * g_ref[...] * o_ref[...]
        if emit_next:
            _norm_mod_to(h2_ref, o_ref, nw2_ref, sh2_ref, sc2_ref)


def _ffn(x, mod, norm_w, w_g, w_u, w_down, rows, layer, which):
    n, d = x.shape
    hp = w_down.shape[2]
    tm = rows.tile(512)
    th = _lane_tile(hp, 512)
    k0 = 0 if which == 0 else 6
    emit_next = which == 0
    w_spec = pl.BlockSpec((None, None, d, th), lambda i, j: (layer, which, 0, j))
    row_spec = pl.BlockSpec((tm, d), lambda i, j: (i, 0))
    norm_spec = lambda k: pl.BlockSpec((None, None, 1, d), lambda i, j: (layer, k, 0, 0))
    in_specs = [row_spec,
                _mod_spec(rows, tm, layer, k0, d),
                _mod_spec(rows, tm, layer, k0 + 1, d),
                _mod_spec(rows, tm, layer, k0 + 2, d),
                norm_spec(2 * which), w_spec, w_spec,
                pl.BlockSpec((None, None, th, d), lambda i, j: (layer, which, j, 0))]
    args = [x, mod, mod, mod, norm_w, w_g, w_u, w_down]
    out_specs, out_shape = row_spec, jax.ShapeDtypeStruct((n, d), F32)
    if emit_next:
        in_specs += [_mod_spec(rows, tm, layer, 3, d), _mod_spec(rows, tm, layer, 4, d), norm_spec(1)]
        args += [mod, mod, norm_w]
        out_specs, out_shape = [row_spec, row_spec], [out_shape, jax.ShapeDtypeStruct((n, d), BF16)]
    return _call(
        functools.partial(_ffn_kernel, emit_next), "ffn_half_step", (n // tm, hp // th),
        in_specs, args, out_specs, out_shape, ("parallel", "arbitrary"),
        scratch=[pltpu.VMEM((tm, d), BF16)])


def _inproj_kernel(h_ref, w_ref, o_ref):
    o_ref[...] = _dot(h_ref[...], w_ref[...]).astype(o_ref.dtype)


def _inproj(h, w, rows, layer, out_dtype, tn):
    n, d = h.shape
    cols = w.shape[2]
    tm = rows.tile(1024)
    return _call(
        _inproj_kernel, "mixer_in_proj", (n // tm, cols // tn),
        [pl.BlockSpec((tm, d), lambda i, j: (i, 0)),
         pl.BlockSpec((None, d, tn), lambda i, j: (layer, 0, j))],
        [h, w],
        pl.BlockSpec((tm, tn), lambda i, j: (i, j)),
        jax.ShapeDtypeStruct((n, cols), out_dtype),
        ("parallel", "arbitrary"))


def _inproj_t_kernel(h_ref, wt_ref, o_ref):
    o_ref[...] = _dot_nt(wt_ref[...], h_ref[...]).astype(o_ref.dtype)


def _inproj_t(h, w_t, rows, layer):
    n, d = h.shape
    cols = w_t.shape[1]
    tm = rows.tile(1024)
    tn = _lane_tile(cols, 1024)
    return _call(
        _inproj_t_kernel, "mixer_in_proj_t", (n // tm, cols // tn),
        [pl.BlockSpec((tm, d), lambda i, j: (i, 0)),
         pl.BlockSpec((None, tn, d), lambda i, j: (layer, j, 0))],
        [h, w_t],
        pl.BlockSpec((tn, tm), lambda i, j: (j, i)),
        jax.ShapeDtypeStruct((cols, n), BF16),
        ("parallel", "arbitrary"))


def _ones_column(rows):
    lane = lax.broadcasted_iota(jnp.int32, (rows, V_PAD - V_DIM), 1)
    return (lane == 0).astype(BF16)


def _mla_prep_kernel(rope, q_rank, kv_rank, p_ref, qnw_ref, kvnw_ref, wq_ref, wkv_ref, *rest):
    if rope:
        cos_ref, sin_ref, q_ref, k_ref, v_ref = rest
    else:
        q_ref, k_ref, v_ref, ckv_ref, kpe_ref = rest
    heads = MLA_HEADS
    p = p_ref[...]
    c_q = p[:, :q_rank]
    c_kv = p[:, q_rank:q_rank + kv_rank]
    o = q_rank + kv_rank
    k_pe = p[:, o:o + ROPE_DIM]
    k_pe_sw = p[:, o + ROPE_DIM:o + 2 * ROPE_DIM]

    qa = _dot(_rms(c_q, qnw_ref[...]).astype(BF16), wq_ref[...])
    ckv_n = _rms(c_kv, kvnw_ref[...])
    kv = _dot(ckv_n.astype(BF16), wkv_ref[...])
    scale = QK_DIM ** -0.5 * math.log2(math.e)
    if rope:
        cos = cos_ref[...]
        sin = sin_ref[...]
        k_pe = k_pe * cos + k_pe_sw * sin
    else:
        ckv_ref[...] = ckv_n
        kpe_ref[...] = k_pe
    ones = _ones_column(p.shape[0])
    pe0 = heads * NOPE_DIM
    sw0 = pe0 + heads * ROPE_DIM
    for h in range(heads):
        q_pe = qa[:, pe0 + h * ROPE_DIM:pe0 + (h + 1) * ROPE_DIM]
        if rope:
            q_pe = q_pe * cos + qa[:, sw0 + h * ROPE_DIM:sw0 + (h + 1) * ROPE_DIM] * sin
        q_ref[h, :, :NOPE_DIM] = (qa[:, h * NOPE_DIM:(h + 1) * NOPE_DIM] * scale).astype(BF16)
        q_ref[h, :, NOPE_DIM:] = (q_pe * scale).astype(BF16)
        k_ref[h, :, :NOPE_DIM] = kv[:, h * NOPE_DIM:(h + 1) * NOPE_DIM].astype(BF16)
        k_ref[h, :, NOPE_DIM:] = k_pe.astype(BF16)
        v0 = heads * NOPE_DIM + h * V_DIM
        v_ref[h, :, :V_DIM] = kv[:, v0:v0 + V_DIM].astype(BF16)
        v_ref[h, :, V_DIM:] = ones


def _mla_prep(p32, q_norm_w, kv_norm_w, wq, wkv, layer, row0, batch, t, rope_tabs, bases):
    ws = p32.shape[1]
    q_rank, kv_rank = q_norm_w.shape[-1], kv_norm_w.shape[-1]
    depth = q_norm_w.shape[0]
    tm = min(256, t)
    nt = t // tm
    heads = MLA_HEADS
    rope = rope_tabs is not None
    in_specs = [pl.BlockSpec((tm, ws), lambda b, i: (row0 // tm + b * nt + i, 0)),
                pl.BlockSpec((None, 1, q_rank), lambda b, i: (layer, 0, 0)),
                pl.BlockSpec((None, 1, kv_rank), lambda b, i: (layer, 0, 0)),
                pl.BlockSpec((None,) + wq.shape[1:], lambda b, i: (layer, 0, 0)),
                pl.BlockSpec((None,) + wkv.shape[1:], lambda b, i: (layer, 0, 0))]
    args = [p32, q_norm_w, kv_norm_w, wq, wkv]
    head_spec = lambda width: pl.BlockSpec((None, heads, tm, width), lambda b, i: (b, 0, i, 0))
    out_specs = [head_spec(QK_DIM), head_spec(QK_DIM), head_spec(V_PAD)]
    out_shape = [jax.ShapeDtypeStruct((batch, heads, t, QK_DIM), BF16),
                 jax.ShapeDtypeStruct((batch, heads, t, QK_DIM), BF16),
                 jax.ShapeDtypeStruct((batch, heads, t, V_PAD), BF16)]
    if rope:
        in_specs += [pl.BlockSpec((tm, ROPE_DIM), lambda b, i: (i, 0))] * 2
        args += list(rope_tabs)
        all_bases = ()
    else:
        out_specs += [pl.BlockSpec((None, None, tm, kv_rank), lambda b, i: (b, layer, i, 0)),
                      pl.BlockSpec((None, None, tm, ROPE_DIM), lambda b, i: (b, layer, i, 0))]
        out_shape += [jax.ShapeDtypeStruct((batch, depth, t, kv_rank), F32),
                      jax.ShapeDtypeStruct((batch, depth, t, ROPE_DIM), F32)]
        all_bases = (None, None, None) + tuple(bases)
    return _call(functools.partial(_mla_prep_kernel, rope, q_rank, kv_rank), "mla_prep",
                 (batch, nt), in_specs, args, out_specs, out_shape, ("parallel", "parallel"),
                 bases=all_bases)


def _cache_kv_kernel(ckv_ref, kpe_ref, wkv_ref, k_ref, v_ref):
    heads = MLA_HEADS
    kv = _dot(ckv_ref[...].astype(BF16), wkv_ref[...])
    k_pe = kpe_ref[...].astype(BF16)
    ones = _ones_column(kv.shape[0])
    for h in range(heads):
        k_ref[h, :, :NOPE_DIM] = kv[:, h * NOPE_DIM:(h + 1) * NOPE_DIM].astype(BF16)
        k_ref[h, :, NOPE_DIM:] = k_pe
        v0 = heads * NOPE_DIM + h * V_DIM
        v_ref[h, :, :V_DIM] = kv[:, v0:v0 + V_DIM].astype(BF16)
        v_ref[h, :, V_DIM:] = ones


def _cache_kv(cache_ckv, cache_kpe, wkv):
    batch, depth, past, kv_rank = cache_ckv.shape
    heads = MLA_HEADS
    return _call(
        _cache_kv_kernel, "mla_cache_kv", (batch, depth),
        [pl.BlockSpec((None, None, past, kv_rank), lambda b, l: (b, l, 0, 0)),
         pl.BlockSpec((None, None, past, ROPE_DIM), lambda b, l: (b, l, 0, 0)),
         pl.BlockSpec((None,) + wkv.shape[1:], lambda b, l: (l, 0, 0))],
        [cache_ckv, cache_kpe, wkv],
        [pl.BlockSpec((None, None, heads, past, QK_DIM), lambda b, l: (b, l, 0, 0, 0)),
         pl.BlockSpec((None, None, heads, past, V_PAD), lambda b, l: (b, l, 0, 0, 0))],
        [jax.ShapeDtypeStruct((batch, depth, heads, past, QK_DIM), BF16),
         jax.ShapeDtypeStruct((batch, depth, heads, past, V_PAD), BF16)],
        ("parallel", "parallel"))


ATTN_KEY_CHUNK = 512


def _attn_kernel(past, q_ref, qn_ref, k_ref, kn_ref, v_ref, *rest):
    if past:
        kc_ref, kcn_ref, vc_ref = rest[:3]
        rest = rest[3:]
    else:
        kc_ref = kcn_ref = vc_ref = None
    o_ref, s0_ref, s1_ref, m0_ref, m1_ref = rest
    tq = qn_ref.shape[0]
    t = k_ref.shape[0]
    chunk = min(ATTN_KEY_CHUNK, t)

    def scores(q, keys_ref, cache_keys_ref, s_ref, m_ref):
        s = _dot_nt(q, keys_ref[...])
        m = jnp.max(s, axis=-1, keepdims=True)
        if past:
            sc = _dot_nt(q, cache_keys_ref[...])
            m = jnp.maximum(m, jnp.max(sc, axis=-1, keepdims=True))
            s_ref[:, :past] = sc
        s_ref[:, past:] = s
        m_ref[...] = m

    def values(r, s_ref, m_ref):
        m = m_ref[...]
        acc = None
        if past:
            acc = _dot(jnp.exp2(s_ref[:, :past] - m).astype(BF16), vc_ref[...])
        for c in range(0, t, chunk):
            p = jnp.exp2(s_ref[:, past + c:past + c + chunk] - m).astype(BF16)
            d = _dot(p, v_ref[c:c + chunk, :])
            acc = d if acc is None else acc + d
        o_ref[r * tq:(r + 1) * tq, :] = (acc[:, :V_DIM] / acc[:, V_DIM:V_DIM + 1]).astype(o_ref.dtype)

    @pl.when(pl.program_id(0) == 0)
    def _():
        scores(q_ref[:tq, :], k_ref, kc_ref, s0_ref, m0_ref)

    values(0, s0_ref, m0_ref)
    scores(q_ref[tq:, :], k_ref, kc_ref, s1_ref, m1_ref)
    values(1, s1_ref, m1_ref)
    scores(qn_ref[...], kn_ref, kcn_ref, s0_ref, m0_ref)


def _attention(q, k, v, cache, layer, n_rows, row0, base):
    batch, heads, t, _ = q.shape
    tq = min(512, t // 2)
    pair = 2 * tq
    npair = t // pair
    n_steps = batch * heads * npair
    past = cache[0].shape[3] if cache is not None else 0
    assert t % pair == 0 and row0 % pair == 0

    def where(tile):
        tile = jnp.minimum(tile, 2 * n_steps - 1)
        p = tile // 2
        return p // (heads * npair), (p // npair) % heads, 2 * (p % npair) + tile % 2

    def pair_map(g):
        b, h, i = where(2 * g)
        return b, h, i // 2, 0

    def next_map(g):
        b, h, i = where(2 * g + 2)
        return b, h, i, 0

    def kv_map(shift):
        def index(g):
            b, h, _ = where(2 * g + shift)
            return b, h, 0, 0
        return index

    def cache_map(shift):
        def index(g):
            b, h, _ = where(2 * g + shift)
            return b, layer, h, 0, 0
        return index

    def o_map(g):
        b, h, i = where(2 * g)
        return row0 // pair + b * npair + i // 2, h

    in_specs = [pl.BlockSpec((None, None, pair, QK_DIM), pair_map),
                pl.BlockSpec((None, None, tq, QK_DIM), next_map),
                pl.BlockSpec((None, None, t, QK_DIM), kv_map(0)),
                pl.BlockSpec((None, None, t, QK_DIM), kv_map(2)),
                pl.BlockSpec((None, None, t, V_PAD), kv_map(0))]
    args = [q, q, k, k, v]
    if past:
        in_specs += [pl.BlockSpec((None, None, None, past, QK_DIM), cache_map(0)),
                     pl.BlockSpec((None, None, None, past, QK_DIM), cache_map(2)),
                     pl.BlockSpec((None, None, None, past, V_PAD), cache_map(0))]
        args += [cache[0], cache[0], cache[1]]
    s_total = past + t
    return _call(
        functools.partial(_attn_kernel, past), "mla_attention", (n_steps,),
        in_specs, args,
        pl.BlockSpec((pair, V_DIM), o_map),
        jax.ShapeDtypeStruct((n_rows, heads * V_DIM), BF16),
        ("arbitrary",), bases=(base,),
        scratch=[pltpu.VMEM((tq, s_total), F32), pltpu.VMEM((tq, s_total), F32),
                 pltpu.VMEM((tq, 1), F32), pltpu.VMEM((tq, 1), F32)])


def _split3(x):
    hi = x.astype(BF16)
    r = x - hi.astype(F32)
    mid = r.astype(BF16)
    return hi, mid, (r - mid.astype(F32)).astype(BF16)


def _mlstm_kernel(has_init, dh, *refs):
    (qf_ref, kf_ref, vf_ref, qb_ref, kb_ref, vb_ref, gf_ref, gb_ref, gtf_ref, gtb_ref,
     brow_ref, bcol_ref) = refs[:12]
    refs = refs[12:]
    if has_init:
        c0_ref, n0_ref, m0_ref = refs[:3]
        refs = refs[3:]
    hf_ref, hb_ref, c_ref, n_ref, m_ref = refs
    heads = MLSTM_HEADS
    n_gate = N_DIR * 2 * heads
    step = pl.program_id(1)

    @pl.when(step == 0)
    def _():
        if has_init:
            c_ref[...] = c0_ref[...]
            n_ref[...] = n0_ref[...]
            m_ref[...] = m0_ref[...]
        else:
            c_ref[...] = jnp.zeros_like(c_ref)
            n_ref[...] = jnp.zeros_like(n_ref)
            m_ref[...] = jnp.zeros_like(m_ref)

    tok0 = lax.broadcasted_iota(jnp.int32, (CHUNK, CHUNK), 0)
    tok1 = lax.broadcasted_iota(jnp.int32, (CHUNK, CHUNK), 1)
    k_scale = dh ** -0.5
    m_all = m_ref[...]
    m_out = m_all
    unit_lane = lax.broadcasted_iota(jnp.int32, m_all.shape, 1)
    for d in range(N_DIR):
        q_ref, k_ref, vt_ref, g_ref, gt_ref, h_ref = (
            (qf_ref, kf_ref, vf_ref, gf_ref, gtf_ref, hf_ref) if d == 0 else
            (qb_ref, kb_ref, vb_ref, gb_ref, gtb_ref, hb_ref))
        seen_t = (tok0 <= tok1) if d == 0 else (tok0 >= tok1)
        seen_t_bf = seen_t.astype(BF16)
        seen_bf = ((tok1 <= tok0) if d == 0 else (tok1 >= tok0)).astype(BF16)
        pre_col = g_ref[:, :n_gate] + brow_ref[...]
        pre_row = gt_ref[...] + bcol_ref[...]
        cum_col = sum(_dot(seen_bf, part) for part in _split3(_log_sigmoid(pre_col)))
        cum_row = sum(_dot(part, seen_t_bf) for part in _split3(_log_sigmoid(pre_row)))
        last = CHUNK - 1 if d == 0 else 0
        for h in range(heads):
            ci = d * 2 * heads + h
            cf = ci + heads
            sid = d * heads + h
            sl = slice(h * dh, (h + 1) * dh)
            c_col = pre_col[:, ci:ci + 1] - cum_col[:, cf:cf + 1]
            i_row = pre_row[ci:ci + 1, :]
            b_row = cum_row[cf:cf + 1, :]
            b_end = b_row[:, last:last + 1]
            m_prev = m_all[:, sid:sid + 1]
            a_row = b_row + m_prev
            dmat = jnp.where(seen_t, b_row + c_col, -jnp.inf)
            m_t = jnp.maximum(a_row, jnp.max(dmat, axis=0, keepdims=True))
            w_intra = jnp.exp(dmat - m_t)
            w_inter = jnp.exp(a_row - m_t)

            q = q_ref[:, sl]
            k_bf = (k_ref[:, sl].astype(F32) * k_scale).astype(BF16)
            v_t = vt_ref[sl, :]
            c_prev = c_ref[d, h]
            n_prev = n_ref[d, h]

            s_t = _dot_nt(k_bf, q) * w_intra
            n_rows = jnp.broadcast_to(n_prev, (PACK_ROWS, dh)).astype(BF16)
            cq = _dot_nt(jnp.concatenate([c_prev.astype(BF16), n_rows], axis=0), q)
            num = w_inter * cq[:dh, :] + _dot(v_t, s_t.astype(BF16))
            den = w_inter * cq[dh:dh + 1, :] + jnp.sum(s_t, axis=0, keepdims=True)
            h_ref[sl, :] = num / jnp.maximum(jnp.abs(den), jnp.exp(-m_t))

            g_row = b_end - b_row + i_row
            m_new = jnp.maximum(b_end + m_prev, jnp.max(g_row, axis=1, keepdims=True))
            w_pos = jnp.exp(g_row - m_new)
            w_carry = jnp.exp(b_end + m_prev - m_new)
            w_rows = jnp.broadcast_to(w_pos, (PACK_ROWS, CHUNK)).astype(BF16)
            upd = _dot(jnp.concatenate([(v_t.astype(F32) * w_pos).astype(BF16), w_rows], axis=0), k_bf)
            c_ref[d, h] = w_carry * c_prev + upd[:dh, :]
            n_ref[d, h] = w_carry * n_prev + upd[dh:dh + 1, :]
            m_out = jnp.where(unit_lane == sid, m_new, m_out)
    m_ref[...] = m_out


def _mlstm(pb, pb_t, p32, gates_t, gate_b, state, layer, row0, batch, t, gate_blk, dh, bases):
    heads = MLSTM_HEADS
    n_gate = N_DIR * 2 * heads
    depth = gate_b.shape[0]
    nc = t // CHUNK
    blk0 = row0 // CHUNK
    w = heads * dh
    fwd = lambda b, c: blk0 + b * nc + c
    bwd = lambda b, c: blk0 + b * nc + nc - 1 - c

    def tok(col, blk):
        return pl.BlockSpec((CHUNK, w), lambda b, c: (blk(b, c), col))

    def feat(blk):
        return pl.BlockSpec((w, CHUNK), lambda b, c: (0, blk(b, c)))

    in_specs = [tok(0, fwd), tok(1, fwd), feat(fwd), tok(0, bwd), tok(1, bwd), feat(bwd),
                pl.BlockSpec((CHUNK, LANES), lambda b, c: (fwd(b, c), gate_blk)),
                pl.BlockSpec((CHUNK, LANES), lambda b, c: (bwd(b, c), gate_blk)),
                pl.BlockSpec((n_gate, CHUNK), lambda b, c: (0, fwd(b, c))),
                pl.BlockSpec((n_gate, CHUNK), lambda b, c: (0, bwd(b, c))),
                pl.BlockSpec((None, 1, n_gate), lambda b, c: (layer, 0, 0)),
                pl.BlockSpec((None, n_gate, 1), lambda b, c: (layer, 0, 0))]
    args = [pb, pb, pb_t, pb, pb, pb_t, p32, p32, gates_t, gates_t,
            gate_b.reshape(-1, 1, n_gate), gate_b.reshape(-1, n_gate, 1)]
    has_init = state is not None
    state_shapes = [(N_DIR, heads, dh, dh), (N_DIR, heads, 1, dh), (1, N_DIR * heads)]
    if has_init:
        c0, n0, m0 = state
        in_specs += [pl.BlockSpec((None, None) + shp, lambda b, c, z=(0,) * len(shp): (b, layer) + z)
                     for shp in state_shapes]
        args += [c0, n0.reshape((batch, depth) + state_shapes[1]), m0.reshape((batch, depth) + state_shapes[2])]
        st_specs = [pl.BlockSpec((None,) + shp, lambda b, c, z=(0,) * len(shp): (b,) + z) for shp in state_shapes]
        st_shapes = [jax.ShapeDtypeStruct((batch,) + shp, F32) for shp in state_shapes]
        all_bases = ()
    else:
        st_specs = [pl.BlockSpec((None, None) + shp, lambda b, c, z=(0,) * len(shp): (b, layer) + z)
                    for shp in state_shapes]
        st_shapes = [jax.ShapeDtypeStruct((batch, depth) + shp, F32) for shp in state_shapes]
        all_bases = (None, None) + tuple(bases)
    return _call(
        functools.partial(_mlstm_kernel, has_init, dh), "mlstm_scan", (batch, nc), in_specs, args,
        [pl.BlockSpec((w, CHUNK), lambda b, c: (0, b * nc + c)),
         pl.BlockSpec((w, CHUNK), lambda b, c: (0, b * nc + nc - 1 - c))] + st_specs,
        [jax.ShapeDtypeStruct((w, batch * t), F32),
         jax.ShapeDtypeStruct((w, batch * t), F32)] + st_shapes,
        ("parallel", "arbitrary"), bases=all_bases)


def _mlstm_post_kernel(dh, hf_ref, hb_ref, o_ref, w_ref, y_ref):
    tm = hf_ref.shape[1]
    hm = hf_ref[...] + hb_ref[...]
    gate = _sigmoid(o_ref[...].astype(F32))
    w = w_ref[...]
    eye = (lax.broadcasted_iota(jnp.int32, (tm, tm), 0)
           == lax.broadcasted_iota(jnp.int32, (tm, tm), 1)).astype(BF16)
    for h in range(MLSTM_HEADS):
        sl = slice(h * dh, (h + 1) * dh)
        x = hm[sl, :]
        y = x * lax.rsqrt(jnp.mean(x * x, axis=0, keepdims=True) + EPS) * w[sl, :]
        y_t = (gate[sl, :] * y).astype(BF16)
        y_ref[:, sl] = _dot_nt(eye, y_t).astype(y_ref.dtype)


def _mlstm_post(h_f, h_b, pb_t, m_norm_w, layer, n_rows, row0, dh, base):
    w, n = h_f.shape
    tm = min(256, n)
    return _call(
        functools.partial(_mlstm_post_kernel, dh), "mlstm_post", (n // tm,),
        [pl.BlockSpec((w, tm), lambda i: (0, i)),
         pl.BlockSpec((w, tm), lambda i: (0, i)),
         pl.BlockSpec((w, tm), lambda i: (1, row0 // tm + i)),
         pl.BlockSpec((None, w, 1), lambda i: (layer, 0, 0))],
        [h_f, h_b, pb_t, m_norm_w],
        pl.BlockSpec((tm, w), lambda i: (row0 // tm + i, 0)),
        jax.ShapeDtypeStruct((n_rows, w), BF16),
        ("parallel",), bases=(base,))


POOL_TILE = 256


def _pool_bands():
    t = np.arange(POOL_TILE)[:, None]
    bands = np.zeros((POOL_GROUPS, 3, POOL_TILE, POOL_TILE), np.float32)
    for g, win in enumerate(POOL_WINDOWS):
        for part in range(3):
            s = np.arange(POOL_TILE)[None, :] + (part - 1) * POOL_TILE
            bands[g, part] = (s >= t - win // 2) & (s < t - win // 2 + win)
    return jnp.asarray(bands, BF16)


def _pool_kernel(t_seq, gd, up_ref, um_ref, un_ref, band_ref, pw_ref, ps_ref, y_ref):
    j = pl.program_id(1)
    has_prev = (j > 0).astype(F32)
    has_next = (j < pl.num_programs(1) - 1).astype(F32)
    tile = um_ref.shape[0]
    pos = j * tile + lax.broadcasted_iota(jnp.int32, (tile, 1), 0)
    for g, win in enumerate(POOL_WINDOWS):
        sl = slice(g * gd, (g + 1) * gd)
        u = um_ref[:, sl]
        acc = (_dot(band_ref[g, 1], u)
               + has_prev * _dot(band_ref[g, 0], up_ref[:, sl])
               + has_next * _dot(band_ref[g, 2], un_ref[:, sl]))
        lo = jnp.clip(pos - win // 2, 0, t_seq)
        hi = jnp.clip(pos - win // 2 + win, 0, t_seq)
        pooled = acc / (hi - lo).astype(F32) - u.astype(F32)
        y = _dot(pooled.astype(BF16), pw_ref[g]) * ps_ref[:, sl]
        y_ref[:, sl] = y.astype(y_ref.dtype)


def _pool(pb, bands, pool_w, pool_scale, layer, n_rows, row0, batch, t, base):
    gd = pool_w.shape[-1]
    w = POOL_GROUPS * gd
    tile = POOL_TILE
    assert t % tile == 0
    nt = t // tile
    blk0 = row0 // tile

    def u_spec(shift):
        return pl.BlockSpec((tile, w), lambda b, j: (blk0 + b * nt + jnp.clip(j + shift, 0, nt - 1), 2))

    return _call(
        functools.partial(_pool_kernel, t, gd), "multiscale_pool", (batch, nt),
        [u_spec(-1), u_spec(0), u_spec(1),
         pl.BlockSpec(bands.shape, lambda b, j: (0, 0, 0, 0)),
         pl.BlockSpec((None, POOL_GROUPS, gd, gd), lambda b, j: (layer, 0, 0, 0)),
         pl.BlockSpec((None, 1, w), lambda b, j: (layer, 0, 0))],
        [pb, pb, pb, bands, pool_w, pool_scale],
        pl.BlockSpec((tile, w), lambda b, j: (blk0 + b * nt + j, 0)),
        jax.ShapeDtypeStruct((n_rows, w), BF16),
        ("parallel", "parallel"), bases=(base,))


def _merge_kernel(ya_ref, yb_ref, yc_ref, ga_ref, gb_ref, gc_ref, w_ref, o_ref):
    acc = _sigmoid(ga_ref[...].astype(F32)) * _dot(ya_ref[...], w_ref[0])
    acc += _sigmoid(gb_ref[...].astype(F32)) * _dot(yb_ref[...], w_ref[1])
    acc += _sigmoid(gc_ref[...].astype(F32)) * _dot(yc_ref[...], w_ref[2])
    o_ref[...] = acc.astype(o_ref.dtype)


def _merge(y_a, y_b, y_c, pb, w_branch, rows, layer, gate_col0):
    n, bw = y_a.shape
    d = w_branch.shape[-1]
    tm = rows.tile(1024)
    tn = min(512, d)
    g0 = gate_col0 // tn
    nd = d // tn

    def gate_spec(k):
        return pl.BlockSpec((tm, tn), lambda i, j: (i, g0 + k * nd + j))

    y_spec = pl.BlockSpec((tm, bw), lambda i, j: (i, 0))
    return _call(
        _merge_kernel, "branch_merge", (n // tm, nd),
        [y_spec, y_spec, y_spec, gate_spec(0), gate_spec(1), gate_spec(2),
         pl.BlockSpec((None, N_BRANCH, bw, tn), lambda i, j: (layer, 0, 0, j))],
        [y_a, y_b, y_c, pb, pb, pb, w_branch],
        pl.BlockSpec((tm, tn), lambda i, j: (i, j)),
        jax.ShapeDtypeStruct((n, d), BF16),
        ("parallel", "arbitrary"))


def _outproj_kernel(m_ref, w_ref, x_ref, g_ref, o_ref):
    o_ref[...] = x_ref[...] + g_ref[...] * _dot(m_ref[...], w_ref[...])


def _outproj(merged, w_out, x, mod, rows, layer):
    n, d = x.shape
    tm = rows.tile(1024)
    tn = min(1024, d)
    return _call(
        _outproj_kernel, "mixer_out_proj", (n // tm, d // tn),
        [pl.BlockSpec((tm, d), lambda i, j: (i, 0)),
         pl.BlockSpec((None, d, tn), lambda i, j: (layer, 0, j)),
         pl.BlockSpec((tm, tn), lambda i, j: (i, j)),
         pl.BlockSpec((None, None, None, 1, tn),
                      lambda i, j: (layer, rows.mod_row(i * tm), 5, 0, j))],
        [merged, w_out, x, mod],
        pl.BlockSpec((tm, tn), lambda i, j: (i, j)),
        jax.ShapeDtypeStruct((n, d), F32),
        ("parallel", "arbitrary"))


def _final_norm_kernel(x_ref, w_ref, o_ref):
    o_ref[...] = _rms(x_ref[...], w_ref[...])


def _final_norm(x, w, row0, n_rows):
    d = x.shape[1]
    tm = min(512, n_rows)
    return _call(
        _final_norm_kernel, "final_norm", (n_rows // tm,),
        [pl.BlockSpec((tm, d), lambda i: (row0 // tm + i, 0)),
         pl.BlockSpec((1, d), lambda i: (0, 0))],
        [x, w.reshape(1, d)],
        pl.BlockSpec((tm, d), lambda i: (i, 0)),
        jax.ShapeDtypeStruct((n_rows, d), F32),
        ("parallel",))


def _rope_swap_index():
    quarter = ROPE_DIM // 4
    idx = np.arange(ROPE_DIM).reshape(2, 2, quarter)
    return idx[:, ::-1, :].reshape(-1)


def _rope_tables(t):
    pos = jnp.arange(t)
    row = (pos // GRID_W).astype(F32)
    col = (pos % GRID_W).astype(F32)
    n_freq = ROPE_DIM // 4
    inv_freq = jnp.power(ROPE_BASE, -jnp.arange(n_freq, dtype=F32) / n_freq)
    ang_r = row[:, None] * inv_freq
    ang_c = col[:, None] * inv_freq
    cos = jnp.concatenate([jnp.cos(ang_r), jnp.cos(ang_r), jnp.cos(ang_c), jnp.cos(ang_c)], axis=-1)
    sin = jnp.concatenate([-jnp.sin(ang_r), jnp.sin(ang_r), -jnp.sin(ang_c), jnp.sin(ang_c)], axis=-1)
    return cos, sin


def kernel(x_prompt, x_sample, cache_ckv, cache_kpe, state_C, state_n, state_m, c, c_ctx, w_mod, b_mod, norm_w, ffn_w_gu, ffn_w_down, w_in, q_norm_w, kv_norm_w, w_uq, w_ukv, mlstm_gate_b, mlstm_norm_w, pool_w, pool_scale, w_branch, w_out, final_norm_w):
    batch, seq, d = x_prompt.shape
    dec_batch, dec_seq, _ = x_sample.shape
    depth = w_mod.shape[0]
    q_rank, kv_rank = q_norm_w.shape[1], kv_norm_w.shape[1]
    heads = MLA_HEADS
    mw = mlstm_norm_w.shape[1]
    dh = mw // MLSTM_HEADS
    pw = pool_scale.shape[1]
    ffn_h = ffn_w_down.shape[2]
    n_gate = N_DIR * 2 * MLSTM_HEADS
    assert mw == pw == w_branch.shape[2] == heads * V_DIM
    rows = _Rows(batch * seq, seq, dec_batch * dec_seq, dec_seq)
    n = rows.n

    hp = _round_up(ffn_h, 512)
    col_zeros = jnp.zeros((depth, 2, d, hp - ffn_h), BF16)
    w_g = jnp.concatenate([ffn_w_gu[..., :ffn_h].astype(BF16), col_zeros], axis=-1)
    w_u = jnp.concatenate([ffn_w_gu[..., ffn_h:].astype(BF16), col_zeros], axis=-1)
    wdn = jnp.concatenate([ffn_w_down.astype(BF16), jnp.zeros((depth, 2, hp - ffn_h, d), BF16)], axis=2)

    sizes = (q_rank, kv_rank, ROPE_DIM, mw, mw, mw, mw, n_gate, pw, N_BRANCH * d)
    offs = np.concatenate([[0], np.cumsum(sizes)])
    swap = _rope_swap_index()
    small_cols = q_rank + kv_rank + 2 * ROPE_DIM + LANES
    w_small = jnp.concatenate(
        [w_in[:, :, :offs[3]], w_in[:, :, offs[2]:offs[3]][:, :, swap], w_in[:, :, offs[7]:offs[8]],
         jnp.zeros((depth, d, LANES - n_gate), F32)], axis=-1).astype(BF16)
    gate_blk = (q_rank + kv_rank + 2 * ROPE_DIM) // LANES
    w_big = jnp.concatenate([w_in[:, :, offs[3]:offs[5]], w_in[:, :, offs[8]:]], axis=-1).astype(BF16)
    w_feat = jnp.swapaxes(w_in[:, :, offs[5]:offs[7]].astype(BF16), 1, 2)
    gate_col0 = 3 * mw

    wq4 = w_uq.reshape(depth, q_rank, heads, QK_DIM)
    wq_pe = wq4[..., NOPE_DIM:]
    wq = jnp.concatenate([wq4[..., :NOPE_DIM].reshape(depth, q_rank, -1),
                          wq_pe.reshape(depth, q_rank, -1),
                          wq_pe[..., swap].reshape(depth, q_rank, -1)], axis=-1).astype(BF16)
    wkv4 = w_ukv.reshape(depth, kv_rank, heads, NOPE_DIM + V_DIM)
    wkv = jnp.concatenate([wkv4[..., :NOPE_DIM].reshape(depth, kv_rank, -1),
                           wkv4[..., NOPE_DIM:].reshape(depth, kv_rank, -1)], axis=-1).astype(BF16)
    wbr = w_branch.astype(BF16)
    wout = w_out.astype(BF16)
    pwb = pool_w.astype(BF16)
    norm_w4 = norm_w.reshape(depth, 3, 1, d)
    qnw = q_norm_w.reshape(depth, 1, q_rank)
    kvnw = kv_norm_w.reshape(depth, 1, kv_rank)
    mnw = mlstm_norm_w.reshape(depth, mw, 1)
    psc = pool_scale.reshape(depth, 1, pw)
    bands = _pool_bands()
    rope_tabs = _rope_tables(dec_seq)

    cond = jnp.concatenate([c_ctx[None, :], c, jnp.zeros((COND_ROWS - 1 - dec_batch, d), F32)], axis=0)
    mod = _mod_all(cond, w_mod, b_mod).reshape(depth, COND_ROWS, N_MOD, 1, d)

    cache_kv = _cache_kv(cache_ckv, cache_kpe, wkv)

    x = jnp.concatenate([x_prompt.reshape(rows.n_ctx, d), x_sample.reshape(rows.n_lat, d)], axis=0)
    new_cache = (None, None)
    new_state = (None, None, None)
    for l in range(depth):
        x, h_mix = _ffn(x, mod, norm_w4, w_g, w_u, wdn, rows, l, 0)

        p32 = _inproj(h_mix, w_small, rows, l, F32, small_cols)
        pb = _inproj(h_mix, w_big, rows, l, BF16, _lane_tile(w_big.shape[2], 1024))
        pb_t = _inproj_t(h_mix, w_feat, rows, l)
        gates_t = p32[:, gate_blk * LANES:gate_blk * LANES + n_gate].T

        q_c, k_c, v_c, *new_cache = _mla_prep(p32, qnw, kvnw, wq, wkv, l, 0, batch, seq, None, new_cache)
        q_s, k_s, v_s = _mla_prep(p32, qnw, kvnw, wq, wkv, l, rows.n_ctx, dec_batch, dec_seq, rope_tabs, None)
        y_a = _attention(q_c, k_c, v_c, None, l, n, 0, None)
        y_a = _attention(q_s, k_s, v_s, cache_kv, l, n, rows.n_ctx, y_a)

        hf_c, hb_c, *new_state = _mlstm(pb, pb_t, p32, gates_t, mlstm_gate_b, None, l, 0, batch, seq,
                                        gate_blk, dh, new_state)
        hf_s, hb_s, _, _, _ = _mlstm(pb, pb_t, p32, gates_t, mlstm_gate_b, (state_C, state_n, state_m), l,
                                     rows.n_ctx, dec_batch, dec_seq, gate_blk, dh, None)
        y_b = _mlstm_post(hf_c, hb_c, pb_t, mnw, l, n, 0, dh, None)
        y_b = _mlstm_post(hf_s, hb_s, pb_t, mnw, l, n, rows.n_ctx, dh, y_b)

        y_c = _pool(pb, bands, pwb, psc, l, n, 0, batch, seq, None)
        y_c = _pool(pb, bands, pwb, psc, l, n, rows.n_ctx, dec_batch, dec_seq, y_c)

        merged = _merge(y_a, y_b, y_c, pb, wbr, rows, l, gate_col0)
        x = _outproj(merged, wout, x, mod, rows, l)
        x = _ffn(x, mod, norm_w4, w_g, w_u, wdn, rows, l, 1)

    y_prompt = _final_norm(x, final_norm_w, 0, rows.n_ctx).reshape(batch, seq, d)
    y_sample = _final_norm(x, final_norm_w, rows.n_ctx, rows.n_lat).reshape(dec_batch, dec_seq, d)
    new_c, new_n, new_m = new_state
    return (y_prompt, y_sample, new_cache[0], new_cache[1], new_c,
            new_n.reshape(batch, depth, N_DIR, MLSTM_HEADS, dh),
            new_m.reshape(batch, depth, N_DIR, MLSTM_HEADS))
```

```python
import functools
import math

import numpy as np
import jax
import jax.numpy as jnp
from jax import lax
from jax.experimental import pallas as pl
from jax.experimental.pallas import tpu as pltpu

GRID_W = 64
EPS = 1e-6
N_MOD = 9
MLA_HEADS = 8
NOPE_DIM = 128
ROPE_DIM = 64
V_DIM = 128
QK_DIM = NOPE_DIM + ROPE_DIM
ROPE_BASE = 10000.0
MLSTM_HEADS = 4
N_DIR = 2
CHUNK = 128
POOL_WINDOWS = (2, 4, 8, 16)
POOL_GROUPS = 4
N_BRANCH = 3

LANES = 128
VMEM_LIMIT_MB = 56
COND_ROWS = 8
PACK_ROWS = 16
V_PAD = 2 * V_DIM

F32 = jnp.float32
BF16 = jnp.bfloat16


def _params(sem):
    return pltpu.CompilerParams(dimension_semantics=sem, vmem_limit_bytes=VMEM_LIMIT_MB << 20)


def _call(kernel, name, grid, in_specs, args, out_specs, out_shape, sem, bases=(), scratch=()):
    n_in = len(args)
    extra = [b for b in bases if b is not None]
    aliases = {}
    for k, b in enumerate(bases):
        if b is not None:
            aliases[n_in + len(aliases)] = k

    def body(*refs):
        kernel(*refs[:n_in], *refs[n_in + len(extra):])

    return pl.pallas_call(
        body if extra else kernel, grid=grid,
        in_specs=list(in_specs) + [pl.BlockSpec(memory_space=pl.ANY)] * len(extra),
        out_specs=out_specs, out_shape=out_shape, input_output_aliases=aliases,
        scratch_shapes=list(scratch), compiler_params=_params(sem), name=name)(*args, *extra)


def _round_up(n, m):
    return (n + m - 1) // m * m


def _lane_tile(n, cap):
    t = cap - cap % LANES
    while n % t:
        t -= LANES
    return t


def _sigmoid(x):
    return 1.0 / (1.0 + jnp.exp(-x))


def _log_sigmoid(x):
    return -(jnp.maximum(-x, 0.0) + jnp.log1p(jnp.exp(-jnp.abs(x))))


def _rms(x, w):
    return x * lax.rsqrt(jnp.mean(x * x, axis=-1, keepdims=True) + EPS) * w


def _dot(a, b):
    return jnp.dot(a, b, preferred_element_type=F32)


def _dot_nt(a, b):
    return lax.dot_general(a, b, (((1,), (1,)), ((), ())), preferred_element_type=F32)


def _mod_kernel(c_ref, w_ref, b_ref, o_ref):
    c = c_ref[...]
    a = (c * _sigmoid(c)).astype(BF16)
    o_ref[...] = _dot(a, w_ref[...].astype(BF16)) + b_ref[...]


def _mod_all(cond, w_mod, b_mod):
    depth, d, nd = w_mod.shape
    tn = _lane_tile(nd, 1024)
    return _call(
        _mod_kernel, "adaln_mod", (depth, nd // tn),
        [pl.BlockSpec((COND_ROWS, d), lambda l, j: (0, 0)),
         pl.BlockSpec((None, d, tn), lambda l, j: (l, 0, j)),
         pl.BlockSpec((None, 1, tn), lambda l, j: (l, 0, j))],
        [cond, w_mod, b_mod.reshape(depth, 1, nd)],
        pl.BlockSpec((None, COND_ROWS, tn), lambda l, j: (l, 0, j)),
        jax.ShapeDtypeStruct((depth, COND_ROWS, nd), F32),
        ("parallel", "parallel"))


class _Rows:
    def __init__(self, n_ctx, t_ctx, n_lat, t_lat):
        self.n_ctx, self.t_ctx, self.n_lat, self.t_lat = n_ctx, t_ctx, n_lat, t_lat
        self.n = n_ctx + n_lat

    def mod_row(self, row):
        return jnp.where(row < self.n_ctx, 0, 1 + (row - self.n_ctx) // self.t_lat)

    def tile(self, cap):
        t = min(cap, self.n_ctx, self.t_lat)
        assert self.n_ctx % t == 0 and self.t_lat % t == 0
        return t


def _mod_spec(rows, tm, layer, k, d):
    return pl.BlockSpec((None, None, None, 1, d),
                        lambda i, j: (layer, rows.mod_row(i * tm), k, 0, 0))


def _norm_mod_to(h_ref, x_ref, nw_ref, sh_ref, sc_ref):
    y = _rms(x_ref[...], nw_ref[...])
    h_ref[...] = (y * (1.0 + sc_ref[...]) + sh_ref[...]).astype(h_ref.dtype)


def _ffn_kernel(emit_next, ctx_tiles, *refs):
    x_refs, refs = (refs[:1], refs[1:]) if ctx_tiles is None else (refs[:2], refs[2:])
    sh_ref, sc_ref, g_ref, nw_ref, wg_ref, wu_ref, wd_ref = refs[:7]
    if emit_next:
        sh2_ref, sc2_ref, nw2_ref, o_ref, h2_ref, h_ref = refs[7:]
    else:
        o_ref, h_ref = refs[7:]
    i = pl.program_id(0)
    j = pl.program_id(1)

    def per_source(fn):
        if ctx_tiles is None:
            fn(x_refs[0])
        else:
            pl.when(i < ctx_tiles)(functools.partial(fn, x_refs[0]))
            pl.when(i >= ctx_tiles)(functools.partial(fn, x_refs[1]))

    def prologue(x_ref):
        _norm_mod_to(h_ref, x_ref, nw_ref, sh_ref, sc_ref)

    def epilogue(x_ref):
        o_ref[...] = x_ref[...] + 0.5 * g_ref[...] * o_ref[...]
        if emit_next:
            _norm_mod_to(h2_ref, o_ref, nw2_ref, sh2_ref, sc2_ref)

    @pl.when(j == 0)
    def _():
        per_source(prologue)
        o_ref[...] = jnp.zeros_like(o_ref)

    h = h_ref[...]
    g = _dot(h, wg_ref[...])
    u = _dot(h, wu_ref[...])
    a = (g * _sigmoid(g) * u).astype(BF16)
    o_ref[...] += _dot(a, wd_ref[...])

    @pl.when(j == pl.num_programs(1) - 1)
    def _():
        per_source(epilogue)


def _ffn(x, mod, norm_w, w_g, w_u, w_down, rows, layer, which):
    tm = rows.tile(512)
    if isinstance(x, tuple):
        ctx_tiles = rows.n_ctx // tm
        x_args = list(x)
        x_specs = [pl.BlockSpec((tm, x[0].shape[1]), lambda i, j: (jnp.minimum(i, ctx_tiles - 1), 0)),
                   pl.BlockSpec((tm, x[0].shape[1]), lambda i, j: (jnp.maximum(i - ctx_tiles, 0), 0))]
    else:
        ctx_tiles = None
        x_args = [x]
        x_specs = [pl.BlockSpec((tm, x.shape[1]), lambda i, j: (i, 0))]
    n, d = rows.n, x_args[0].shape[1]
    hp = w_down.shape[2]
    th = _lane_tile(hp, 512)
    k0 = 0 if which == 0 else 6
    emit_next = which == 0
    w_spec = pl.BlockSpec((None, None, d, th), lambda i, j: (layer, which, 0, j))
    row_spec = pl.BlockSpec((tm, d), lambda i, j: (i, 0))
    norm_spec = lambda k: pl.BlockSpec((None, None, 1, d), lambda i, j: (layer, k, 0, 0))
    in_specs = x_specs + [
        _mod_spec(rows, tm, layer, k0, d),
        _mod_spec(rows, tm, layer, k0 + 1, d),
        _mod_spec(rows, tm, layer, k0 + 2, d),
        norm_spec(2 * which), w_spec, w_spec,
        pl.BlockSpec((None, None, th, d), lambda i, j: (layer, which, j, 0))]
    args = x_args + [mod, mod, mod, norm_w, w_g, w_u, w_down]
    out_specs, out_shape = row_spec, jax.ShapeDtypeStruct((n, d), F32)
    if emit_next:
        in_specs += [_mod_spec(rows, tm, layer, 3, d), _mod_spec(rows, tm, layer, 4, d), norm_spec(1)]
        args += [mod, mod, norm_w]
        out_specs, out_shape = [row_spec, row_spec], [out_shape, jax.ShapeDtypeStruct((n, d), BF16)]
    return _call(
        functools.partial(_ffn_kernel, emit_next, ctx_tiles), "ffn_half_step", (n // tm, hp // th),
        in_specs, args, out_specs, out_shape, ("parallel", "arbitrary"),
        scratch=[pltpu.VMEM((tm, d), BF16)])


def _inproj_kernel(h_ref, w_ref, o_ref):
    o_ref[...] = _dot(h_ref[...], w_ref[...]).astype(o_ref.dtype)


def _inproj(h, w, rows, layer, out_dtype, tn):
    n, d = h.shape
    cols = w.shape[2]
    tm = rows.tile(1024)
    return _call(
        _inproj_kernel, "mixer_in_proj", (n // tm, cols // tn),
        [pl.BlockSpec((tm, d), lambda i, j: (i, 0)),
         pl.BlockSpec((None, d, tn), lambda i, j: (layer, 0, j))],
        [h, w],
        pl.BlockSpec((tm, tn), lambda i, j: (i, j)),
        jax.ShapeDtypeStruct((n, cols), out_dtype),
        ("parallel", "arbitrary"))


def _inproj_t_kernel(h_ref, wt_ref, o_ref):
    o_ref[...] = _dot_nt(wt_ref[...], h_ref[...]).astype(o_ref.dtype)


def _inproj_t(h, w_t, rows, layer):
    n, d = h.shape
    cols = w_t.shape[1]
    tm = rows.tile(1024)
    tn = _lane_tile(cols, 1024)
    return _call(
        _inproj_t_kernel, "mixer_in_proj_t", (n // tm, cols // tn),
        [pl.BlockSpec((tm, d), lambda i, j: (i, 0)),
         pl.BlockSpec((None, tn, d), lambda i, j: (layer, j, 0))],
        [h, w_t],
        pl.BlockSpec((tn, tm), lambda i, j: (j, i)),
        jax.ShapeDtypeStruct((cols, n), BF16),
        ("parallel", "arbitrary"))


def _ones_column(rows):
    lane = lax.broadcasted_iota(jnp.int32, (rows, V_PAD - V_DIM), 1)
    return (lane == 0).astype(BF16)


def _mla_prep_kernel(rope, q_rank, kv_rank, p_ref, qnw_ref, kvnw_ref, wq_ref, wkv_ref, *rest):
    if rope:
        cos_ref, sin_ref, q_ref, k_ref, v_ref = rest
    else:
        q_ref, k_ref, v_ref, ckv_ref, kpe_ref = rest
    heads = MLA_HEADS
    p = p_ref[...]
    c_q = p[:, :q_rank]
    c_kv = p[:, q_rank:q_rank + kv_rank]
    o = q_rank + kv_rank
    k_pe = p[:, o:o + ROPE_DIM]
    k_pe_sw = p[:, o + ROPE_DIM:o + 2 * ROPE_DIM]

    qa = _dot(_rms(c_q, qnw_ref[...]).astype(BF16), wq_ref[...])
    ckv_n = _rms(c_kv, kvnw_ref[...])
    kv = _dot(ckv_n.astype(BF16), wkv_ref[...])
    scale = QK_DIM ** -0.5 * math.log2(math.e)
    if rope:
        cos = cos_ref[...]
        sin = sin_ref[...]
        k_pe = k_pe * cos + k_pe_sw * sin
    else:
        ckv_ref[...] = ckv_n
        kpe_ref[...] = k_pe
    ones = _ones_column(p.shape[0])
    pe0 = heads * NOPE_DIM
    sw0 = pe0 + heads * ROPE_DIM
    for h in range(heads):
        q_pe = qa[:, pe0 + h * ROPE_DIM:pe0 + (h + 1) * ROPE_DIM]
        if rope:
            q_pe = q_pe * cos + qa[:, sw0 + h * ROPE_DIM:sw0 + (h + 1) * ROPE_DIM] * sin
        q_ref[h, :, :NOPE_DIM] = (qa[:, h * NOPE_DIM:(h + 1) * NOPE_DIM] * scale).astype(BF16)
        q_ref[h, :, NOPE_DIM:] = (q_pe * scale).astype(BF16)
        k_ref[h, :, :NOPE_DIM] = kv[:, h * NOPE_DIM:(h + 1) * NOPE_DIM].astype(BF16)
        k_ref[h, :, NOPE_DIM:] = k_pe.astype(BF16)
        v0 = heads * NOPE_DIM + h * V_DIM
        v_ref[h, :, :V_DIM] = kv[:, v0:v0 + V_DIM].astype(BF16)
        v_ref[h, :, V_DIM:] = ones


def _mla_prep(p32, q_norm_w, kv_norm_w, wq, wkv, layer, row0, batch, t, rope_tabs, bases):
    ws = p32.shape[1]
    q_rank, kv_rank = q_norm_w.shape[-1], kv_norm_w.shape[-1]
    depth = q_norm_w.shape[0]
    tm = min(256, t)
    nt = t // tm
    heads = MLA_HEADS
    rope = rope_tabs is not None
    in_specs = [pl.BlockSpec((tm, ws), lambda b, i: (row0 // tm + b * nt + i, 0)),
                pl.BlockSpec((None, 1, q_rank), lambda b, i: (layer, 0, 0)),
                pl.BlockSpec((None, 1, kv_rank), lambda b, i: (layer, 0, 0)),
                pl.BlockSpec((None,) + wq.shape[1:], lambda b, i: (layer, 0, 0)),
                pl.BlockSpec((None,) + wkv.shape[1:], lambda b, i: (layer, 0, 0))]
    args = [p32, q_norm_w, kv_norm_w, wq, wkv]
    head_spec = lambda width: pl.BlockSpec((None, heads, tm, width), lambda b, i: (b, 0, i, 0))
    out_specs = [head_spec(QK_DIM), head_spec(QK_DIM), head_spec(V_PAD)]
    out_shape = [jax.ShapeDtypeStruct((batch, heads, t, QK_DIM), BF16),
                 jax.ShapeDtypeStruct((batch, heads, t, QK_DIM), BF16),
                 jax.ShapeDtypeStruct((batch, heads, t, V_PAD), BF16)]
    if rope:
        in_specs += [pl.BlockSpec((tm, ROPE_DIM), lambda b, i: (i, 0))] * 2
        args += list(rope_tabs)
        all_bases = ()
    else:
        out_specs += [pl.BlockSpec((None, None, tm, kv_rank), lambda b, i: (b, layer, i, 0)),
                      pl.BlockSpec((None, None, tm, ROPE_DIM), lambda b, i: (b, layer, i, 0))]
        out_shape += [jax.ShapeDtypeStruct((batch, depth, t, kv_rank), F32),
                      jax.ShapeDtypeStruct((batch, depth, t, ROPE_DIM), F32)]
        all_bases = (None, None, None) + tuple(bases)
    return _call(functools.partial(_mla_prep_kernel, rope, q_rank, kv_rank), "mla_prep",
                 (batch, nt), in_specs, args, out_specs, out_shape, ("parallel", "parallel"),
                 bases=all_bases)


def _cache_kv_kernel(ckv_ref, kpe_ref, wkv_ref, k_ref, v_ref):
    heads = MLA_HEADS
    kv = _dot(ckv_ref[...].astype(BF16), wkv_ref[...])
    k_pe = kpe_ref[...].astype(BF16)
    ones = _ones_column(kv.shape[0])
    for h in range(heads):
        k_ref[h, :, :NOPE_DIM] = kv[:, h * NOPE_DIM:(h + 1) * NOPE_DIM].astype(BF16)
        k_ref[h, :, NOPE_DIM:] = k_pe
        v0 = heads * NOPE_DIM + h * V_DIM
        v_ref[h, :, :V_DIM] = kv[:, v0:v0 + V_DIM].astype(BF16)
        v_ref[h, :, V_DIM:] = ones


def _cache_kv(cache_ckv, cache_kpe, wkv):
    batch, depth, past, kv_rank = cache_ckv.shape
    heads = MLA_HEADS
    return _call(
        _cache_kv_kernel, "mla_cache_kv", (batch, depth),
        [pl.BlockSpec((None, None, past, kv_rank), lambda b, l: (b, l, 0, 0)),
         pl.BlockSpec((None, None, past, ROPE_DIM), lambda b, l: (b, l, 0, 0)),
         pl.BlockSpec((None,) + wkv.shape[1:], lambda b, l: (l, 0, 0))],
        [cache_ckv, cache_kpe, wkv],
        [pl.BlockSpec((None, None, heads, past, QK_DIM), lambda b, l: (b, l, 0, 0, 0)),
         pl.BlockSpec((None, None, heads, past, V_PAD), lambda b, l: (b, l, 0, 0, 0))],
        [jax.ShapeDtypeStruct((batch, depth, heads, past, QK_DIM), BF16),
         jax.ShapeDtypeStruct((batch, depth, heads, past, V_PAD), BF16)],
        ("parallel", "parallel"))


ATTN_KEY_CHUNK = 512


def _attn_kernel(past, q_ref, qn_ref, k_ref, kn_ref, v_ref, *rest):
    if past:
        kc_ref, kcn_ref, vc_ref = rest[:3]
        rest = rest[3:]
    else:
        kc_ref = kcn_ref = vc_ref = None
    o_ref, s0_ref, s1_ref, m0_ref, m1_ref = rest
    tq = qn_ref.shape[0]
    t = k_ref.shape[0]
    chunk = min(ATTN_KEY_CHUNK, t)

    def scores(q, keys_ref, cache_keys_ref, s_ref, m_ref):
        s = _dot_nt(q, keys_ref[...])
        m = jnp.max(s, axis=-1, keepdims=True)
        if past:
            sc = _dot_nt(q, cache_keys_ref[...])
            m = jnp.maximum(m, jnp.max(sc, axis=-1, keepdims=True))
            s_ref[:, :past] = sc
        s_ref[:, past:] = s
        m_ref[...] = m

    def values(r, s_ref, m_ref):
        m = m_ref[...]
        acc = None
        if past:
            acc = _dot(jnp.exp2(s_ref[:, :past] - m).astype(BF16), vc_ref[...])
        for c in range(0, t, chunk):
            p = jnp.exp2(s_ref[:, past + c:past + c + chunk] - m).astype(BF16)
            d = _dot(p, v_ref[c:c + chunk, :])
            acc = d if acc is None else acc + d
        o_ref[r * tq:(r + 1) * tq, :] = (acc[:, :V_DIM] / acc[:, V_DIM:V_DIM + 1]).astype(o_ref.dtype)

    @pl.when(pl.program_id(0) == 0)
    def _():
        scores(q_ref[:tq, :], k_ref, kc_ref, s0_ref, m0_ref)

    values(0, s0_ref, m0_ref)
    scores(q_ref[tq:, :], k_ref, kc_ref, s1_ref, m1_ref)
    values(1, s1_ref, m1_ref)
    scores(qn_ref[...], kn_ref, kcn_ref, s0_ref, m0_ref)


def _attention(q, k, v, cache, layer, n_rows, row0, base):
    batch, heads, t, _ = q.shape
    tq = min(512, t // 2)
    pair = 2 * tq
    npair = t // pair
    n_steps = batch * heads * npair
    past = cache[0].shape[3] if cache is not None else 0
    assert t % pair == 0 and row0 % pair == 0

    def where(tile):
        tile = jnp.minimum(tile, 2 * n_steps - 1)
        p = tile // 2
        return p // (heads * npair), (p // npair) % heads, 2 * (p % npair) + tile % 2

    def pair_map(g):
        b, h, i = where(2 * g)
        return b, h, i // 2, 0

    def next_map(g):
        b, h, i = where(2 * g + 2)
        return b, h, i, 0

    def kv_map(shift):
        def index(g):
            b, h, _ = where(2 * g + shift)
            return b, h, 0, 0
        return index

    def cache_map(shift):
        def index(g):
            b, h, _ = where(2 * g + shift)
            return b, layer, h, 0, 0
        return index

    def o_map(g):
        b, h, i = where(2 * g)
        return row0 // pair + b * npair + i // 2, h

    in_specs = [pl.BlockSpec((None, None, pair, QK_DIM), pair_map),
                pl.BlockSpec((None, None, tq, QK_DIM), next_map),
                pl.BlockSpec((None, None, t, QK_DIM), kv_map(0)),
                pl.BlockSpec((None, None, t, QK_DIM), kv_map(2)),
                pl.BlockSpec((None, None, t, V_PAD), kv_map(0))]
    args = [q, q, k, k, v]
    if past:
        in_specs += [pl.BlockSpec((None, None, None, past, QK_DIM), cache_map(0)),
                     pl.BlockSpec((None, None, None, past, QK_DIM), cache_map(2)),
                     pl.BlockSpec((None, None, None, past, V_PAD), cache_map(0))]
        args += [cache[0], cache[0], cache[1]]
    s_total = past + t
    return _call(
        functools.partial(_attn_kernel, past), "mla_attention", (n_steps,),
        in_specs, args,
        pl.BlockSpec((pair, V_DIM), o_map),
        jax.ShapeDtypeStruct((n_rows, heads * V_DIM), BF16),
        ("arbitrary",), bases=(base,),
        scratch=[pltpu.VMEM((tq, s_total), F32), pltpu.VMEM((tq, s_total), F32),
                 pltpu.VMEM((tq, 1), F32), pltpu.VMEM((tq, 1), F32)])


def _split3(x):
    hi = x.astype(BF16)
    r = x - hi.astype(F32)
    mid = r.astype(BF16)
    return hi, mid, (r - mid.astype(F32)).astype(BF16)


def _mlstm_kernel(has_init, dh, *refs):
    (qf_ref, kf_ref, vf_ref, qb_ref, kb_ref, vb_ref, gf_ref, gb_ref, gtf_ref, gtb_ref,
     brow_ref, bcol_ref) = refs[:12]
    refs = refs[12:]
    if has_init:
        c0_ref, n0_ref, m0_ref = refs[:3]
        refs = refs[3:]
    hf_ref, hb_ref, c_ref, n_ref, m_ref = refs
    heads = MLSTM_HEADS
    n_gate = N_DIR * 2 * heads
    step = pl.program_id(1)

    @pl.when(step == 0)
    def _():
        if has_init:
            c_ref[...] = c0_ref[...]
            n_ref[...] = n0_ref[...]
            m_ref[...] = m0_ref[...]
        else:
            c_ref[...] = jnp.zeros_like(c_ref)
            n_ref[...] = jnp.zeros_like(n_ref)
            m_ref[...] = jnp.zeros_like(m_ref)

    tok0 = lax.broadcasted_iota(jnp.int32, (CHUNK, CHUNK), 0)
    tok1 = lax.broadcasted_iota(jnp.int32, (CHUNK, CHUNK), 1)
    k_scale = dh ** -0.5
    m_all = m_ref[...]
    m_out = m_all
    unit_lane = lax.broadcasted_iota(jnp.int32, m_all.shape, 1)
    for d in range(N_DIR):
        q_ref, k_ref, vt_ref, g_ref, gt_ref, h_ref = (
            (qf_ref, kf_ref, vf_ref, gf_ref, gtf_ref, hf_ref) if d == 0 else
            (qb_ref, kb_ref, vb_ref, gb_ref, gtb_ref, hb_ref))
        seen_t = (tok0 <= tok1) if d == 0 else (tok0 >= tok1)
        seen_t_bf = seen_t.astype(BF16)
        seen_bf = ((tok1 <= tok0) if d == 0 else (tok1 >= tok0)).astype(BF16)
        pre_col = g_ref[:, :n_gate] + brow_ref[...]
        pre_row = gt_ref[...] + bcol_ref[...]
        cum_col = sum(_dot(seen_bf, part) for part in _split3(_log_sigmoid(pre_col)))
        cum_row = sum(_dot(part, seen_t_bf) for part in _split3(_log_sigmoid(pre_row)))
        last = CHUNK - 1 if d == 0 else 0
        for h in range(heads):
            ci = d * 2 * heads + h
            cf = ci + heads
            sid = d * heads + h
            sl = slice(h * dh, (h + 1) * dh)
            c_col = pre_col[:, ci:ci + 1] - cum_col[:, cf:cf + 1]
            i_row = pre_row[ci:ci + 1, :]
            b_row = cum_row[cf:cf + 1, :]
            b_end = b_row[:, last:last + 1]
            m_prev = m_all[:, sid:sid + 1]
            a_row = b_row + m_prev
            dmat = jnp.where(seen_t, b_row + c_col, -jnp.inf)
            m_t = jnp.maximum(a_row, jnp.max(dmat, axis=0, keepdims=True))
            w_intra = jnp.exp(dmat - m_t)
            w_inter = jnp.exp(a_row - m_t)

            q = q_ref[:, sl]
            k_bf = (k_ref[:, sl].astype(F32) * k_scale).astype(BF16)
            v_t = vt_ref[sl, :]
            c_prev = c_ref[d, h]
            n_prev = n_ref[d, h]

            s_t = _dot_nt(k_bf, q) * w_intra
            n_rows = jnp.broadcast_to(n_prev, (PACK_ROWS, dh)).astype(BF16)
            cq = _dot_nt(jnp.concatenate([c_prev.astype(BF16), n_rows], axis=0), q)
            num = w_inter * cq[:dh, :] + _dot(v_t, s_t.astype(BF16))
            den = w_inter * cq[dh:dh + 1, :] + jnp.sum(s_t, axis=0, keepdims=True)
            h_ref[sl, :] = num / jnp.maximum(jnp.abs(den), jnp.exp(-m_t))

            g_row = b_end - b_row + i_row
            m_new = jnp.maximum(b_end + m_prev, jnp.max(g_row, axis=1, keepdims=True))
            w_pos = jnp.exp(g_row - m_new)
            w_carry = jnp.exp(b_end + m_prev - m_new)
            w_rows = jnp.broadcast_to(w_pos, (PACK_ROWS, CHUNK)).astype(BF16)
            upd = _dot(jnp.concatenate([(v_t.astype(F32) * w_pos).astype(BF16), w_rows], axis=0), k_bf)
            c_ref[d, h] = w_carry * c_prev + upd[:dh, :]
            n_ref[d, h] = w_carry * n_prev + upd[dh:dh + 1, :]
            m_out = jnp.where(unit_lane == sid, m_new, m_out)
    m_ref[...] = m_out


def _mlstm(pb, pb_t, p32, gates_t, gate_b, state, layer, row0, batch, t, gate_blk, dh, bases):
    heads = MLSTM_HEADS
    n_gate = N_DIR * 2 * heads
    depth = gate_b.shape[0]
    nc = t // CHUNK
    blk0 = row0 // CHUNK
    w = heads * dh
    fwd = lambda b, c: blk0 + b * nc + c
    bwd = lambda b, c: blk0 + b * nc + nc - 1 - c

    def tok(col, blk):
        return pl.BlockSpec((CHUNK, w), lambda b, c: (blk(b, c), col))

    def feat(blk):
        return pl.BlockSpec((w, CHUNK), lambda b, c: (0, blk(b, c)))

    in_specs = [tok(0, fwd), tok(1, fwd), feat(fwd), tok(0, bwd), tok(1, bwd), feat(bwd),
                pl.BlockSpec((CHUNK, LANES), lambda b, c: (fwd(b, c), gate_blk)),
                pl.BlockSpec((CHUNK, LANES), lambda b, c: (bwd(b, c), gate_blk)),
                pl.BlockSpec((n_gate, CHUNK), lambda b, c: (0, fwd(b, c))),
                pl.BlockSpec((n_gate, CHUNK), lambda b, c: (0, bwd(b, c))),
                pl.BlockSpec((None, 1, n_gate), lambda b, c: (layer, 0, 0)),
                pl.BlockSpec((None, n_gate, 1), lambda b, c: (layer, 0, 0))]
    args = [pb, pb, pb_t, pb, pb, pb_t, p32, p32, gates_t, gates_t,
            gate_b.reshape(-1, 1, n_gate), gate_b.reshape(-1, n_gate, 1)]
    has_init = state is not None
    state_shapes = [(N_DIR, heads, dh, dh), (N_DIR, heads, 1, dh), (1, N_DIR * heads)]
    if has_init:
        c0, n0, m0 = state
        in_specs += [pl.BlockSpec((None, None) + shp, lambda b, c, z=(0,) * len(shp): (b, layer) + z)
                     for shp in state_shapes]
        args += [c0, n0.reshape((batch, depth) + state_shapes[1]), m0.reshape((batch, depth) + state_shapes[2])]
        st_specs = [pl.BlockSpec((None,) + shp, lambda b, c, z=(0,) * len(shp): (b,) + z) for shp in state_shapes]
        st_shapes = [jax.ShapeDtypeStruct((batch,) + shp, F32) for shp in state_shapes]
        all_bases = ()
    else:
        st_specs = [pl.BlockSpec((None, None) + shp, lambda b, c, z=(0,) * len(shp): (b, layer) + z)
                    for shp in state_shapes]
        st_shapes = [jax.ShapeDtypeStruct((batch, depth) + shp, F32) for shp in state_shapes]
        all_bases = (None, None) + tuple(bases)
    return _call(
        functools.partial(_mlstm_kernel, has_init, dh), "mlstm_scan", (batch, nc), in_specs, args,
        [pl.BlockSpec((w, CHUNK), lambda b, c: (0, b * nc + c)),
         pl.BlockSpec((w, CHUNK), lambda b, c: (0, b * nc + nc - 1 - c))] + st_specs,
        [jax.ShapeDtypeStruct((w, batch * t), F32),
         jax.ShapeDtypeStruct((w, batch * t), F32)] + st_shapes,
        ("parallel", "arbitrary"), bases=all_bases)


def _mlstm_post_kernel(dh, hf_ref, hb_ref, o_ref, w_ref, y_ref):
    tm = hf_ref.shape[1]
    hm = hf_ref[...] + hb_ref[...]
    gate = _sigmoid(o_ref[...].astype(F32))
    w = w_ref[...]
    eye = (lax.broadcasted_iota(jnp.int32, (tm, tm), 0)
           == lax.broadcasted_iota(jnp.int32, (tm, tm), 1)).astype(BF16)
    for h in range(MLSTM_HEADS):
        sl = slice(h * dh, (h + 1) * dh)
        x = hm[sl, :]
        y = x * lax.rsqrt(jnp.mean(x * x, axis=0, keepdims=True) + EPS) * w[sl, :]
        y_t = (gate[sl, :] * y).astype(BF16)
        y_ref[:, sl] = _dot_nt(eye, y_t).astype(y_ref.dtype)


def _mlstm_post(h_f, h_b, pb_t, m_norm_w, layer, n_rows, row0, dh, base):
    w, n = h_f.shape
    tm = min(256, n)
    return _call(
        functools.partial(_mlstm_post_kernel, dh), "mlstm_post", (n // tm,),
        [pl.BlockSpec((w, tm), lambda i: (0, i)),
         pl.BlockSpec((w, tm), lambda i: (0, i)),
         pl.BlockSpec((w, tm), lambda i: (1, row0 // tm + i)),
         pl.BlockSpec((None, w, 1), lambda i: (layer, 0, 0))],
        [h_f, h_b, pb_t, m_norm_w],
        pl.BlockSpec((tm, w), lambda i: (row0 // tm + i, 0)),
        jax.ShapeDtypeStruct((n_rows, w), BF16),
        ("parallel",), bases=(base,))


POOL_TILE = 256


def _pool_bands():
    t = np.arange(POOL_TILE)[:, None]
    bands = np.zeros((POOL_GROUPS, 3, POOL_TILE, POOL_TILE), np.float32)
    for g, win in enumerate(POOL_WINDOWS):
        for part in range(3):
            s = np.arange(POOL_TILE)[None, :] + (part - 1) * POOL_TILE
            bands[g, part] = (s >= t - win // 2) & (s < t - win // 2 + win)
    return jnp.asarray(bands, BF16)


def _pool_kernel(t_seq, gd, up_ref, um_ref, un_ref, band_ref, pw_ref, ps_ref, y_ref):
    j = pl.program_id(1)
    has_prev = (j > 0).astype(F32)
    has_next = (j < pl.num_programs(1) - 1).astype(F32)
    tile = um_ref.shape[0]
    pos = j * tile + lax.broadcasted_iota(jnp.int32, (tile, 1), 0)
    for g, win in enumerate(POOL_WINDOWS):
        sl = slice(g * gd, (g + 1) * gd)
        u = um_ref[:, sl]
        acc = (_dot(band_ref[g, 1], u)
               + has_prev * _dot(band_ref[g, 0], up_ref[:, sl])
               + has_next * _dot(band_ref[g, 2], un_ref[:, sl]))
        lo = jnp.clip(pos - win // 2, 0, t_seq)
        hi = jnp.clip(pos - win // 2 + win, 0, t_seq)
        pooled = acc / (hi - lo).astype(F32) - u.astype(F32)
        y = _dot(pooled.astype(BF16), pw_ref[g]) * ps_ref[:, sl]
        y_ref[:, sl] = y.astype(y_ref.dtype)


def _pool(pb, bands, pool_w, pool_scale, layer, n_rows, row0, batch, t, base):
    gd = pool_w.shape[-1]
    w = POOL_GROUPS * gd
    tile = POOL_TILE
    assert t % tile == 0
    nt = t // tile
    blk0 = row0 // tile

    def u_spec(shift):
        return pl.BlockSpec((tile, w), lambda b, j: (blk0 + b * nt + jnp.clip(j + shift, 0, nt - 1), 2))

    return _call(
        functools.partial(_pool_kernel, t, gd), "multiscale_pool", (batch, nt),
        [u_spec(-1), u_spec(0), u_spec(1),
         pl.BlockSpec(bands.shape, lambda b, j: (0, 0, 0, 0)),
         pl.BlockSpec((None, POOL_GROUPS, gd, gd), lambda b, j: (layer, 0, 0, 0)),
         pl.BlockSpec((None, 1, w), lambda b, j: (layer, 0, 0))],
        [pb, pb, pb, bands, pool_w, pool_scale],
        pl.BlockSpec((tile, w), lambda b, j: (blk0 + b * nt + j, 0)),
        jax.ShapeDtypeStruct((n_rows, w), BF16),
        ("parallel", "parallel"), bases=(base,))


def _merge_kernel(ya_ref, yb_ref, yc_ref, ga_ref, gb_ref, gc_ref, w_ref, o_ref):
    acc = _sigmoid(ga_ref[...].astype(F32)) * _dot(ya_ref[...], w_ref[0])
    acc += _sigmoid(gb_ref[...].astype(F32)) * _dot(yb_ref[...], w_ref[1])
    acc += _sigmoid(gc_ref[...].astype(F32)) * _dot(yc_ref[...], w_ref[2])
    o_ref[...] = acc.astype(o_ref.dtype)


def _merge(y_a, y_b, y_c, pb, w_branch, rows, layer, gate_col0):
    n, bw = y_a.shape
    d = w_branch.shape[-1]
    tm = rows.tile(1024)
    tn = min(512, d)
    g0 = gate_col0 // tn
    nd = d // tn

    def gate_spec(k):
        return pl.BlockSpec((tm, tn), lambda i, j: (i, g0 + k * nd + j))

    y_spec = pl.BlockSpec((tm, bw), lambda i, j: (i, 0))
    return _call(
        _merge_kernel, "branch_merge", (n // tm, nd),
        [y_spec, y_spec, y_spec, gate_spec(0), gate_spec(1), gate_spec(2),
         pl.BlockSpec((None, N_BRANCH, bw, tn), lambda i, j: (layer, 0, 0, j))],
        [y_a, y_b, y_c, pb, pb, pb, w_branch],
        pl.BlockSpec((tm, tn), lambda i, j: (i, j)),
        jax.ShapeDtypeStruct((n, d), BF16),
        ("parallel", "arbitrary"))


def _outproj_kernel(m_ref, w_ref, x_ref, g_ref, o_ref):
    o_ref[...] = x_ref[...] + g_ref[...] * _dot(m_ref[...], w_ref[...])


def _outproj(merged, w_out, x, mod, rows, layer):
    n, d = x.shape
    tm = rows.tile(1024)
    tn = min(1024, d)
    return _call(
        _outproj_kernel, "mixer_out_proj", (n // tm, d // tn),
        [pl.BlockSpec((tm, d), lambda i, j: (i, 0)),
         pl.BlockSpec((None, d, tn), lambda i, j: (layer, 0, j)),
         pl.BlockSpec((tm, tn), lambda i, j: (i, j)),
         pl.BlockSpec((None, None, None, 1, tn),
                      lambda i, j: (layer, rows.mod_row(i * tm), 5, 0, j))],
        [merged, w_out, x, mod],
        pl.BlockSpec((tm, tn), lambda i, j: (i, j)),
        jax.ShapeDtypeStruct((n, d), F32),
        ("parallel", "arbitrary"))


def _final_norm_kernel(x_ref, w_ref, o_ref):
    o_ref[...] = _rms(x_ref[...], w_ref[...])


def _final_norm(x, w, row0, n_rows):
    d = x.shape[1]
    tm = min(512, n_rows)
    return _call(
        _final_norm_kernel, "final_norm", (n_rows // tm,),
        [pl.BlockSpec((tm, d), lambda i: (row0 // tm + i, 0)),
         pl.BlockSpec((1, d), lambda i: (0, 0))],
        [x, w.reshape(1, d)],
        pl.BlockSpec((tm, d), lambda i: (i, 0)),
        jax.ShapeDtypeStruct((n_rows, d), F32),
        ("parallel",))


def _gate_up_prep_kernel(valid, g_ref, u_ref, og_ref, ou_ref):
    for src, dst in ((g_ref, og_ref), (u_ref, ou_ref)):
        dst[:, :valid] = src[...].astype(BF16)
        if dst.shape[1] > valid:
            dst[:, valid:] = jnp.zeros((dst.shape[0], dst.shape[1] - valid), BF16)


def _gate_up_prep(w_gu, hp):
    depth, n_ffn, d, h2 = w_gu.shape
    h = h2 // 2
    assert h % LANES == 0
    n_rows = depth * n_ffn * d
    tr = 256
    flat = w_gu.reshape(n_rows, h2)
    out = jax.ShapeDtypeStruct((n_rows, hp), BF16)
    w_g, w_u = _call(
        functools.partial(_gate_up_prep_kernel, h), "ffn_gate_up_prep", (n_rows // tr,),
        [pl.BlockSpec((tr, h), lambda r: (r, 0)), pl.BlockSpec((tr, h), lambda r: (r, 1))],
        [flat, flat],
        [pl.BlockSpec((tr, hp), lambda r: (r, 0))] * 2, [out, out], ("parallel",))
    return w_g.reshape(depth, n_ffn, d, hp), w_u.reshape(depth, n_ffn, d, hp)


def _down_prep_kernel(valid_blocks, w_ref, o_ref):
    o_ref[...] = jnp.where(pl.program_id(1) < valid_blocks, w_ref[...], 0.0).astype(BF16)


def _down_prep(w_down, hp):
    depth, n_ffn, h, d = w_down.shape
    tr = math.gcd(h, hp)
    assert tr % PACK_ROWS == 0
    valid_blocks = h // tr
    flat = w_down.reshape(depth * n_ffn, h, d)
    out = _call(
        functools.partial(_down_prep_kernel, valid_blocks), "ffn_down_prep", (depth * n_ffn, hp // tr),
        [pl.BlockSpec((None, tr, d), lambda a, j: (a, jnp.minimum(j, valid_blocks - 1), 0))], [flat],
        pl.BlockSpec((None, tr, d), lambda a, j: (a, j, 0)),
        jax.ShapeDtypeStruct((depth * n_ffn, hp, d), BF16), ("parallel", "parallel"))
    return out.reshape(depth, n_ffn, hp, d)


def _rope_swap_index():
    quarter = ROPE_DIM // 4
    idx = np.arange(ROPE_DIM).reshape(2, 2, quarter)
    return idx[:, ::-1, :].reshape(-1)


def _rope_tables(t):
    pos = jnp.arange(t)
    row = (pos // GRID_W).astype(F32)
    col = (pos % GRID_W).astype(F32)
    n_freq = ROPE_DIM // 4
    inv_freq = jnp.power(ROPE_BASE, -jnp.arange(n_freq, dtype=F32) / n_freq)
    ang_r = row[:, None] * inv_freq
    ang_c = col[:, None] * inv_freq
    cos = jnp.concatenate([jnp.cos(ang_r), jnp.cos(ang_r), jnp.cos(ang_c), jnp.cos(ang_c)], axis=-1)
    sin = jnp.concatenate([-jnp.sin(ang_r), jnp.sin(ang_r), -jnp.sin(ang_c), jnp.sin(ang_c)], axis=-1)
    return cos, sin


def kernel(x_prompt, x_sample, cache_ckv, cache_kpe, state_C, state_n, state_m, c, c_ctx, w_mod, b_mod, norm_w, ffn_w_gu, ffn_w_down, w_in, q_norm_w, kv_norm_w, w_uq, w_ukv, mlstm_gate_b, mlstm_norm_w, pool_w, pool_scale, w_branch, w_out, final_norm_w):
    batch, seq, d = x_prompt.shape
    dec_batch, dec_seq, _ = x_sample.shape
    depth = w_mod.shape[0]
    q_rank, kv_rank = q_norm_w.shape[1], kv_norm_w.shape[1]
    heads = MLA_HEADS
    mw = mlstm_norm_w.shape[1]
    dh = mw // MLSTM_HEADS
    pw = pool_scale.shape[1]
    ffn_h = ffn_w_down.shape[2]
    n_gate = N_DIR * 2 * MLSTM_HEADS
    assert mw == pw == w_branch.shape[2] == heads * V_DIM
    rows = _Rows(batch * seq, seq, dec_batch * dec_seq, dec_seq)
    n = rows.n

    hp = _round_up(ffn_h, 512)
    w_g, w_u = _gate_up_prep(ffn_w_gu, hp)
    wdn = _down_prep(ffn_w_down, hp)

    sizes = (q_rank, kv_rank, ROPE_DIM, mw, mw, mw, mw, n_gate, pw, N_BRANCH * d)
    offs = np.concatenate([[0], np.cumsum(sizes)])
    swap = _rope_swap_index()
    small_cols = q_rank + kv_rank + 2 * ROPE_DIM + LANES
    w_in16 = lax.optimization_barrier(w_in.astype(BF16))
    w_small = jnp.concatenate(
        [w_in16[:, :, :offs[3]], w_in16[:, :, offs[2]:offs[3]][:, :, swap], w_in16[:, :, offs[7]:offs[8]],
         jnp.zeros((depth, d, LANES - n_gate), BF16)], axis=-1)
    gate_blk = (q_rank + kv_rank + 2 * ROPE_DIM) // LANES
    w_big = jnp.concatenate([w_in16[:, :, offs[3]:offs[5]], w_in16[:, :, offs[8]:]], axis=-1)
    w_feat = jnp.swapaxes(w_in16[:, :, offs[5]:offs[7]], 1, 2)
    gate_col0 = 3 * mw

    wq4 = w_uq.reshape(depth, q_rank, heads, QK_DIM)
    wq_pe = wq4[..., NOPE_DIM:]
    wq = jnp.concatenate([wq4[..., :NOPE_DIM].reshape(depth, q_rank, -1),
                          wq_pe.reshape(depth, q_rank, -1),
                          wq_pe[..., swap].reshape(depth, q_rank, -1)], axis=-1).astype(BF16)
    wkv4 = w_ukv.reshape(depth, kv_rank, heads, NOPE_DIM + V_DIM)
    wkv = jnp.concatenate([wkv4[..., :NOPE_DIM].reshape(depth, kv_rank, -1),
                           wkv4[..., NOPE_DIM:].reshape(depth, kv_rank, -1)], axis=-1).astype(BF16)
    wbr = w_branch.astype(BF16)
    wout = w_out.astype(BF16)
    pwb = pool_w.astype(BF16)
    norm_w4 = norm_w.reshape(depth, 3, 1, d)
    qnw = q_norm_w.reshape(depth, 1, q_rank)
    kvnw = kv_norm_w.reshape(depth, 1, kv_rank)
    mnw = mlstm_norm_w.reshape(depth, mw, 1)
    psc = pool_scale.reshape(depth, 1, pw)
    bands = _pool_bands()
    rope_tabs = _rope_tables(dec_seq)

    cond = jnp.concatenate([c_ctx[None, :], c, jnp.zeros((COND_ROWS - 1 - dec_batch, d), F32)], axis=0)
    mod = _mod_all(cond, w_mod, b_mod).reshape(depth, COND_ROWS, N_MOD, 1, d)

    cache_kv = _cache_kv(cache_ckv, cache_kpe, wkv)

    x = (x_prompt.reshape(rows.n_ctx, d), x_sample.reshape(rows.n_lat, d))
    new_cache = (None, None)
    new_state = (None, None, None)
    for l in range(depth):
        x, h_mix = _ffn(x, mod, norm_w4, w_g, w_u, wdn, rows, l, 0)

        p32 = _inproj(h_mix, w_small, rows, l, F32, small_cols)
        pb = _inproj(h_mix, w_big, rows, l, BF16, _lane_tile(w_big.shape[2], 1024))
        pb_t = _inproj_t(h_mix, w_feat, rows, l)
        gates_t = p32[:, gate_blk * LANES:gate_blk * LANES + n_gate].T

        q_c, k_c, v_c, *new_cache = _mla_prep(p32, qnw, kvnw, wq, wkv, l, 0, batch, seq, None, new_cache)
        q_s, k_s, v_s = _mla_prep(p32, qnw, kvnw, wq, wkv, l, rows.n_ctx, dec_batch, dec_seq, rope_tabs, None)
        y_a = _attention(q_c, k_c, v_c, None, l, n, 0, None)
        y_a = _attention(q_s, k_s, v_s, cache_kv, l, n, rows.n_ctx, y_a)

        hf_c, hb_c, *new_state = _mlstm(pb, pb_t, p32, gates_t, mlstm_gate_b, None, l, 0, batch, seq,
                                        gate_blk, dh, new_state)
        hf_s, hb_s, _, _, _ = _mlstm(pb, pb_t, p32, gates_t, mlstm_gate_b, (state_C, state_n, state_m), l,
                                     rows.n_ctx, dec_batch, dec_seq, gate_blk, dh, None)
        y_b = _mlstm_post(hf_c, hb_c, pb_t, mnw, l, n, 0, dh, None)
        y_b = _mlstm_post(hf_s, hb_s, pb_t, mnw, l, n, rows.n_ctx, dh, y_b)

        y_c = _pool(pb, bands, pwb, psc, l, n, 0, batch, seq, None)
        y_c = _pool(pb, bands, pwb, psc, l, n, rows.n_ctx, dec_batch, dec_seq, y_c)

        merged = _merge(y_a, y_b, y_c, pb, wbr, rows, l, gate_col0)
        x = _outproj(merged, wout, x, mod, rows, l)
        x = _ffn(x, mod, norm_w4, w_g, w_u, wdn, rows, l, 1)

    y_prompt = _final_norm(x, final_norm_w, 0, rows.n_ctx).reshape(batch, seq, d)
    y_sample = _final_norm(x, final_norm_w, rows.n_ctx, rows.n_lat).reshape(dec_batch, dec_seq, d)
    new_c, new_n, new_m = new_state
    return (y_prompt, y_sample, new_cache[0], new_cache[1], new_c,
            new_n.reshape(batch, depth, N_DIR, MLSTM_HEADS, dh),
            new_m.reshape(batch, depth, N_DIR, MLSTM_HEADS))
```

```python
import functools
import math

import numpy as np
import jax
import jax.numpy as jnp
from jax import lax
from jax.experimental import pallas as pl
from jax.experimental.pallas import tpu as pltpu

GRID_W = 64
EPS = 1e-6
N_MOD = 9
MLA_HEADS = 8
NOPE_DIM = 128
ROPE_DIM = 64
V_DIM = 128
QK_DIM = NOPE_DIM + ROPE_DIM
ROPE_BASE = 10000.0
MLSTM_HEADS = 4
N_DIR = 2
CHUNK = 128
POOL_WINDOWS = (2, 4, 8, 16)
POOL_GROUPS = 4
N_BRANCH = 3

LANES = 128
VMEM_LIMIT_MB = 56
COND_ROWS = 8
PACK_ROWS = 16
V_PAD = 2 * V_DIM

F32 = jnp.float32
BF16 = jnp.bfloat16


def _params(sem):
    return pltpu.CompilerParams(dimension_semantics=sem, vmem_limit_bytes=VMEM_LIMIT_MB << 20)


def _call(kernel, name, grid, in_specs, args, out_specs, out_shape, sem, bases=(), scratch=()):
    n_in = len(args)
    extra = [b for b in bases if b is not None]
    aliases = {}
    for k, b in enumerate(bases):
        if b is not None:
            aliases[n_in + len(aliases)] = k

    def body(*refs):
        kernel(*refs[:n_in], *refs[n_in + len(extra):])

    return pl.pallas_call(
        body if extra else kernel, grid=grid,
        in_specs=list(in_specs) + [pl.BlockSpec(memory_space=pl.ANY)] * len(extra),
        out_specs=out_specs, out_shape=out_shape, input_output_aliases=aliases,
        scratch_shapes=list(scratch), compiler_params=_params(sem), name=name)(*args, *extra)


def _round_up(n, m):
    return (n + m - 1) // m * m


def _lane_tile(n, cap):
    t = cap - cap % LANES
    while n % t:
        t -= LANES
    return t


def _sigmoid(x):
    return 1.0 / (1.0 + jnp.exp(-x))


def _log_sigmoid(x):
    return -(jnp.maximum(-x, 0.0) + jnp.log1p(jnp.exp(-jnp.abs(x))))


def _rms(x, w):
    return x * lax.rsqrt(jnp.mean(x * x, axis=-1, keepdims=True) + EPS) * w


def _dot(a, b):
    return jnp.dot(a, b, preferred_element_type=F32)


def _dot_nt(a, b):
    return lax.dot_general(a, b, (((1,), (1,)), ((), ())), preferred_element_type=F32)


def _mod_kernel(c_ref, w_ref, b_ref, o_ref):
    c = c_ref[...]
    a = (c * _sigmoid(c)).astype(BF16)
    o_ref[...] = _dot(a, w_ref[...].astype(BF16)) + b_ref[...]


def _mod_all(cond, w_mod, b_mod):
    depth, d, nd = w_mod.shape
    tn = _lane_tile(nd, 1024)
    return _call(
        _mod_kernel, "adaln_mod", (depth, nd // tn),
        [pl.BlockSpec((COND_ROWS, d), lambda l, j: (0, 0)),
         pl.BlockSpec((None, d, tn), lambda l, j: (l, 0, j)),
         pl.BlockSpec((None, 1, tn), lambda l, j: (l, 0, j))],
        [cond, w_mod, b_mod.reshape(depth, 1, nd)],
        pl.BlockSpec((None, COND_ROWS, tn), lambda l, j: (l, 0, j)),
        jax.ShapeDtypeStruct((depth, COND_ROWS, nd), F32),
        ("parallel", "parallel"))


class _Rows:
    def __init__(self, n_ctx, t_ctx, n_lat, t_lat):
        self.n_ctx, self.t_ctx, self.n_lat, self.t_lat = n_ctx, t_ctx, n_lat, t_lat
        self.n = n_ctx + n_lat

    def mod_row(self, row):
        return jnp.where(row < self.n_ctx, 0, 1 + (row - self.n_ctx) // self.t_lat)

    def tile(self, cap):
        t = min(cap, self.n_ctx, self.t_lat)
        assert self.n_ctx % t == 0 and self.t_lat % t == 0
        return t


def _mod_spec(rows, tm, layer, k, d):
    return pl.BlockSpec((None, None, None, 1, d),
                        lambda i, j: (layer, rows.mod_row(i * tm), k, 0, 0))


def _norm_mod_to(h_ref, x_ref, nw_ref, sh_ref, sc_ref):
    y = _rms(x_ref[...], nw_ref[...])
    h_ref[...] = (y * (1.0 + sc_ref[...]) + sh_ref[...]).astype(h_ref.dtype)


def _ffn_kernel(emit_next, ctx_tiles, *refs):
    x_refs, refs = (refs[:1], refs[1:]) if ctx_tiles is None else (refs[:2], refs[2:])
    sh_ref, sc_ref, g_ref, nw_ref, wg_ref, wu_ref, wd_ref = refs[:7]
    if emit_next:
        sh2_ref, sc2_ref, nw2_ref, o_ref, h2_ref, h_ref = refs[7:]
    else:
        o_ref, h_ref = refs[7:]
    i = pl.program_id(0)
    j = pl.program_id(1)

    def per_source(fn):
        if ctx_tiles is None:
            fn(x_refs[0])
        else:
            pl.when(i < ctx_tiles)(functools.partial(fn, x_refs[0]))
            pl.when(i >= ctx_tiles)(functools.partial(fn, x_refs[1]))

    def prologue(x_ref):
        _norm_mod_to(h_ref, x_ref, nw_ref, sh_ref, sc_ref)

    def epilogue(x_ref):
        o_ref[...] = x_ref[...] + 0.5 * g_ref[...] * o_ref[...]
        if emit_next:
            _norm_mod_to(h2_ref, o_ref, nw2_ref, sh2_ref, sc2_ref)

    @pl.when(j == 0)
    def _():
        per_source(prologue)
        o_ref[...] = jnp.zeros_like(o_ref)

    h = h_ref[...]
    g = _dot(h, wg_ref[...])
    u = _dot(h, wu_ref[...])
    a = (g * _sigmoid(g) * u).astype(BF16)
    o_ref[...] += _dot(a, wd_ref[...])

    @pl.when(j == pl.num_programs(1) - 1)
    def _():
        per_source(epilogue)


def _ffn(x, mod, norm_w, w_g, w_u, w_down, rows, layer, which):
    tm = rows.tile(512)
    if isinstance(x, tuple):
        ctx_tiles = rows.n_ctx // tm
        x_args = list(x)
        x_specs = [pl.BlockSpec((tm, x[0].shape[1]), lambda i, j: (jnp.minimum(i, ctx_tiles - 1), 0)),
                   pl.BlockSpec((tm, x[0].shape[1]), lambda i, j: (jnp.maximum(i - ctx_tiles, 0), 0))]
    else:
        ctx_tiles = None
        x_args = [x]
        x_specs = [pl.BlockSpec((tm, x.shape[1]), lambda i, j: (i, 0))]
    n, d = rows.n, x_args[0].shape[1]
    hp = w_down.shape[2]
    th = _lane_tile(hp, 512)
    k0 = 0 if which == 0 else 6
    emit_next = which == 0
    w_spec = pl.BlockSpec((None, None, d, th), lambda i, j: (layer, which, 0, j))
    row_spec = pl.BlockSpec((tm, d), lambda i, j: (i, 0))
    norm_spec = lambda k: pl.BlockSpec((None, None, 1, d), lambda i, j: (layer, k, 0, 0))
    in_specs = x_specs + [
        _mod_spec(rows, tm, layer, k0, d),
        _mod_spec(rows, tm, layer, k0 + 1, d),
        _mod_spec(rows, tm, layer, k0 + 2, d),
        norm_spec(2 * which), w_spec, w_spec,
        pl.BlockSpec((None, None, th, d), lambda i, j: (layer, which, j, 0))]
    args = x_args + [mod, mod, mod, norm_w, w_g, w_u, w_down]
    out_specs, out_shape = row_spec, jax.ShapeDtypeStruct((n, d), F32)
    if emit_next:
        in_specs += [_mod_spec(rows, tm, layer, 3, d), _mod_spec(rows, tm, layer, 4, d), norm_spec(1)]
        args += [mod, mod, norm_w]
        out_specs, out_shape = [row_spec, row_spec], [out_shape, jax.ShapeDtypeStruct((n, d), BF16)]
    return _call(
        functools.partial(_ffn_kernel, emit_next, ctx_tiles), "ffn_half_step", (n // tm, hp // th),
        in_specs, args, out_specs, out_shape, ("parallel", "arbitrary"),
        scratch=[pltpu.VMEM((tm, d), BF16)])


def _inproj_kernel(h_ref, w_ref, o_ref):
    o_ref[...] = _dot(h_ref[...], w_ref[...]).astype(o_ref.dtype)


def _inproj(h, w, rows, layer, out_dtype, tn):
    n, d = h.shape
    cols = w.shape[2]
    tm = rows.tile(1024)
    return _call(
        _inproj_kernel, "mixer_in_proj", (n // tm, cols // tn),
        [pl.BlockSpec((tm, d), lambda i, j: (i, 0)),
         pl.BlockSpec((None, d, tn), lambda i, j: (layer, 0, j))],
        [h, w],
        pl.BlockSpec((tm, tn), lambda i, j: (i, j)),
        jax.ShapeDtypeStruct((n, cols), out_dtype),
        ("parallel", "arbitrary"))


def _inproj_t_kernel(h_ref, wt_ref, o_ref):
    o_ref[...] = _dot_nt(wt_ref[...], h_ref[...]).astype(o_ref.dtype)


def _inproj_t(h, w_t, rows, layer):
    n, d = h.shape
    cols = w_t.shape[1]
    tm = rows.tile(1024)
    tn = _lane_tile(cols, 1024)
    return _call(
        _inproj_t_kernel, "mixer_in_proj_t", (n // tm, cols // tn),
        [pl.BlockSpec((tm, d), lambda i, j: (i, 0)),
         pl.BlockSpec((None, tn, d), lambda i, j: (layer, j, 0))],
        [h, w_t],
        pl.BlockSpec((tn, tm), lambda i, j: (j, i)),
        jax.ShapeDtypeStruct((cols, n), BF16),
        ("parallel", "arbitrary"))


def _ones_column(rows):
    lane = lax.broadcasted_iota(jnp.int32, (rows, V_PAD - V_DIM), 1)
    return (lane == 0).astype(BF16)


def _mla_prep_kernel(rope, q_rank, kv_rank, p_ref, qnw_ref, kvnw_ref, wq_ref, wkv_ref, *rest):
    if rope:
        cos_ref, sin_ref, q_ref, k_ref, v_ref = rest
    else:
        q_ref, k_ref, v_ref, ckv_ref, kpe_ref = rest
    heads = MLA_HEADS
    p = p_ref[...]
    c_q = p[:, :q_rank]
    c_kv = p[:, q_rank:q_rank + kv_rank]
    o = q_rank + kv_rank
    k_pe = p[:, o:o + ROPE_DIM]
    k_pe_sw = p[:, o + ROPE_DIM:o + 2 * ROPE_DIM]

    qa = _dot(_rms(c_q, qnw_ref[...]).astype(BF16), wq_ref[...])
    ckv_n = _rms(c_kv, kvnw_ref[...])
    kv = _dot(ckv_n.astype(BF16), wkv_ref[...])
    scale = QK_DIM ** -0.5 * math.log2(math.e)
    if rope:
        cos = cos_ref[...]
        sin = sin_ref[...]
        k_pe = k_pe * cos + k_pe_sw * sin
    else:
        ckv_ref[...] = ckv_n
        kpe_ref[...] = k_pe
    ones = _ones_column(p.shape[0])
    pe0 = heads * NOPE_DIM
    sw0 = pe0 + heads * ROPE_DIM
    for h in range(heads):
        q_pe = qa[:, pe0 + h * ROPE_DIM:pe0 + (h + 1) * ROPE_DIM]
        if rope:
            q_pe = q_pe * cos + qa[:, sw0 + h * ROPE_DIM:sw0 + (h + 1) * ROPE_DIM] * sin
        q_ref[h, :, :NOPE_DIM] = (qa[:, h * NOPE_DIM:(h + 1) * NOPE_DIM] * scale).astype(BF16)
        q_ref[h, :, NOPE_DIM:] = (q_pe * scale).astype(BF16)
        k_ref[h, :, :NOPE_DIM] = kv[:, h * NOPE_DIM:(h + 1) * NOPE_DIM].astype(BF16)
        k_ref[h, :, NOPE_DIM:] = k_pe.astype(BF16)
        v0 = heads * NOPE_DIM + h * V_DIM
        v_ref[h, :, :V_DIM] = kv[:, v0:v0 + V_DIM].astype(BF16)
        v_ref[h, :, V_DIM:] = ones


def _mla_prep(p32, q_norm_w, kv_norm_w, wq, wkv, layer, row0, batch, t, rope_tabs, bases):
    ws = p32.shape[1]
    q_rank, kv_rank = q_norm_w.shape[-1], kv_norm_w.shape[-1]
    depth = q_norm_w.shape[0]
    tm = min(256, t)
    nt = t // tm
    heads = MLA_HEADS
    rope = rope_tabs is not None
    in_specs = [pl.BlockSpec((tm, ws), lambda b, i: (row0 // tm + b * nt + i, 0)),
                pl.BlockSpec((None, 1, q_rank), lambda b, i: (layer, 0, 0)),
                pl.BlockSpec((None, 1, kv_rank), lambda b, i: (layer, 0, 0)),
                pl.BlockSpec((None,) + wq.shape[1:], lambda b, i: (layer, 0, 0)),
                pl.BlockSpec((None,) + wkv.shape[1:], lambda b, i: (layer, 0, 0))]
    args = [p32, q_norm_w, kv_norm_w, wq, wkv]
    head_spec = lambda width: pl.BlockSpec((None, heads, tm, width), lambda b, i: (b, 0, i, 0))
    out_specs = [head_spec(QK_DIM), head_spec(QK_DIM), head_spec(V_PAD)]
    out_shape = [jax.ShapeDtypeStruct((batch, heads, t, QK_DIM), BF16),
                 jax.ShapeDtypeStruct((batch, heads, t, QK_DIM), BF16),
                 jax.ShapeDtypeStruct((batch, heads, t, V_PAD), BF16)]
    if rope:
        in_specs += [pl.BlockSpec((tm, ROPE_DIM), lambda b, i: (i, 0))] * 2
        args += list(rope_tabs)
        all_bases = ()
    else:
        out_specs += [pl.BlockSpec((None, None, tm, kv_rank), lambda b, i: (b, layer, i, 0)),
                      pl.BlockSpec((None, None, tm, ROPE_DIM), lambda b, i: (b, layer, i, 0))]
        out_shape += [jax.ShapeDtypeStruct((batch, depth, t, kv_rank), F32),
                      jax.ShapeDtypeStruct((batch, depth, t, ROPE_DIM), F32)]
        all_bases = (None, None, None) + tuple(bases)
    return _call(functools.partial(_mla_prep_kernel, rope, q_rank, kv_rank), "mla_prep",
                 (batch, nt), in_specs, args, out_specs, out_shape, ("parallel", "parallel"),
                 bases=all_bases)


def _cache_kv_kernel(ckv_ref, kpe_ref, wkv_ref, k_ref, v_ref):
    heads = MLA_HEADS
    kv = _dot(ckv_ref[...].astype(BF16), wkv_ref[...])
    k_pe = kpe_ref[...].astype(BF16)
    ones = _ones_column(kv.shape[0])
    for h in range(heads):
        k_ref[h, :, :NOPE_DIM] = kv[:, h * NOPE_DIM:(h + 1) * NOPE_DIM].astype(BF16)
        k_ref[h, :, NOPE_DIM:] = k_pe
        v0 = heads * NOPE_DIM + h * V_DIM
        v_ref[h, :, :V_DIM] = kv[:, v0:v0 + V_DIM].astype(BF16)
        v_ref[h, :, V_DIM:] = ones


def _cache_kv(cache_ckv, cache_kpe, wkv):
    batch, depth, past, kv_rank = cache_ckv.shape
    heads = MLA_HEADS
    return _call(
        _cache_kv_kernel, "mla_cache_kv", (batch, depth),
        [pl.BlockSpec((None, None, past, kv_rank), lambda b, l: (b, l, 0, 0)),
         pl.BlockSpec((None, None, past, ROPE_DIM), lambda b, l: (b, l, 0, 0)),
         pl.BlockSpec((None,) + wkv.shape[1:], lambda b, l: (l, 0, 0))],
        [cache_ckv, cache_kpe, wkv],
        [pl.BlockSpec((None, None, heads, past, QK_DIM), lambda b, l: (b, l, 0, 0, 0)),
         pl.BlockSpec((None, None, heads, past, V_PAD), lambda b, l: (b, l, 0, 0, 0))],
        [jax.ShapeDtypeStruct((batch, depth, heads, past, QK_DIM), BF16),
         jax.ShapeDtypeStruct((batch, depth, heads, past, V_PAD), BF16)],
        ("parallel", "parallel"))


ATTN_KEY_CHUNK = 512


def _attn_kernel(past, q_ref, qn_ref, k_ref, kn_ref, v_ref, *rest):
    if past:
        kc_ref, kcn_ref, vc_ref = rest[:3]
        rest = rest[3:]
    else:
        kc_ref = kcn_ref = vc_ref = None
    o_ref, s0_ref, s1_ref, m0_ref, m1_ref = rest
    tq = qn_ref.shape[0]
    t = k_ref.shape[0]
    chunk = min(ATTN_KEY_CHUNK, t)

    def scores(q, keys_ref, cache_keys_ref, s_ref, m_ref):
        s = _dot_nt(q, keys_ref[...])
        m = jnp.max(s, axis=-1, keepdims=True)
        if past:
            sc = _dot_nt(q, cache_keys_ref[...])
            m = jnp.maximum(m, jnp.max(sc, axis=-1, keepdims=True))
            s_ref[:, :past] = sc
        s_ref[:, past:] = s
        m_ref[...] = m

    def values(r, s_ref, m_ref):
        m = m_ref[...]
        acc = None
        if past:
            acc = _dot(jnp.exp2(s_ref[:, :past] - m).astype(BF16), vc_ref[...])
        for c in range(0, t, chunk):
            p = jnp.exp2(s_ref[:, past + c:past + c + chunk] - m).astype(BF16)
            d = _dot(p, v_ref[c:c + chunk, :])
            acc = d if acc is None else acc + d
        o_ref[r * tq:(r + 1) * tq, :] = (acc[:, :V_DIM] / acc[:, V_DIM:V_DIM + 1]).astype(o_ref.dtype)

    @pl.when(pl.program_id(0) == 0)
    def _():
        scores(q_ref[:tq, :], k_ref, kc_ref, s0_ref, m0_ref)

    values(0, s0_ref, m0_ref)
    scores(q_ref[tq:, :], k_ref, kc_ref, s1_ref, m1_ref)
    values(1, s1_ref, m1_ref)
    scores(qn_ref[...], kn_ref, kcn_ref, s0_ref, m0_ref)


def _attn_short_kernel(q_ref, k_ref, v_ref, o_ref):
    for h in range(q_ref.shape[0]):
        s = _dot_nt(q_ref[h], k_ref[h])
        m = jnp.max(s, axis=-1, keepdims=True)
        o = _dot(jnp.exp2(s - m).astype(BF16), v_ref[h])
        o_ref[:, h * V_DIM:(h + 1) * V_DIM] = (o[:, :V_DIM] / o[:, V_DIM:V_DIM + 1]).astype(o_ref.dtype)


ATTN_SHORT_SEQ = 256


def _attention(q, k, v, cache, layer, n_rows, row0, base):
    batch, heads, t, _ = q.shape
    if cache is None and t <= ATTN_SHORT_SEQ:
        assert row0 % t == 0
        head_block = lambda width: pl.BlockSpec((None, heads, t, width), lambda b: (b, 0, 0, 0))
        return _call(
            _attn_short_kernel, "mla_attention_short", (batch,),
            [head_block(QK_DIM), head_block(QK_DIM), head_block(V_PAD)], [q, k, v],
            pl.BlockSpec((t, heads * V_DIM), lambda b: (row0 // t + b, 0)),
            jax.ShapeDtypeStruct((n_rows, heads * V_DIM), BF16), ("parallel",), bases=(base,))
    tq = min(512, t // 2)
    pair = 2 * tq
    npair = t // pair
    n_steps = batch * heads * npair
    past = cache[0].shape[3] if cache is not None else 0
    assert t % pair == 0 and row0 % pair == 0

    def where(tile):
        tile = jnp.minimum(tile, 2 * n_steps - 1)
        p = tile // 2
        return p // (heads * npair), (p // npair) % heads, 2 * (p % npair) + tile % 2

    def pair_map(g):
        b, h, i = where(2 * g)
        return b, h, i // 2, 0

    def next_map(g):
        b, h, i = where(2 * g + 2)
        return b, h, i, 0

    def kv_map(shift):
        def index(g):
            b, h, _ = where(2 * g + shift)
            return b, h, 0, 0
        return index

    def cache_map(shift):
        def index(g):
            b, h, _ = where(2 * g + shift)
            return b, layer, h, 0, 0
        return index

    def o_map(g):
        b, h, i = where(2 * g)
        return row0 // pair + b * npair + i // 2, h

    in_specs = [pl.BlockSpec((None, None, pair, QK_DIM), pair_map),
                pl.BlockSpec((None, None, tq, QK_DIM), next_map),
                pl.BlockSpec((None, None, t, QK_DIM), kv_map(0)),
                pl.BlockSpec((None, None, t, QK_DIM), kv_map(2)),
                pl.BlockSpec((None, None, t, V_PAD), kv_map(0))]
    args = [q, q, k, k, v]
    if past:
        in_specs += [pl.BlockSpec((None, None, None, past, QK_DIM), cache_map(0)),
                     pl.BlockSpec((None, None, None, past, QK_DIM), cache_map(2)),
                     pl.BlockSpec((None, None, None, past, V_PAD), cache_map(0))]
        args += [cache[0], cache[0], cache[1]]
    s_total = past + t
    return _call(
        functools.partial(_attn_kernel, past), "mla_attention", (n_steps,),
        in_specs, args,
        pl.BlockSpec((pair, V_DIM), o_map),
        jax.ShapeDtypeStruct((n_rows, heads * V_DIM), BF16),
        ("arbitrary",), bases=(base,),
        scratch=[pltpu.VMEM((tq, s_total), F32), pltpu.VMEM((tq, s_total), F32),
                 pltpu.VMEM((tq, 1), F32), pltpu.VMEM((tq, 1), F32)])


def _split3(x):
    hi = x.astype(BF16)
    r = x - hi.astype(F32)
    mid = r.astype(BF16)
    return hi, mid, (r - mid.astype(F32)).astype(BF16)


def _mlstm_kernel(has_init, dh, *refs):
    (qf_ref, kf_ref, vf_ref, qb_ref, kb_ref, vb_ref, gf_ref, gb_ref, gtf_ref, gtb_ref,
     brow_ref, bcol_ref) = refs[:12]
    refs = refs[12:]
    if has_init:
        c0_ref, n0_ref, m0_ref = refs[:3]
        refs = refs[3:]
    hf_ref, hb_ref, c_ref, n_ref, m_ref = refs
    heads = MLSTM_HEADS
    n_gate = N_DIR * 2 * heads
    step = pl.program_id(1)

    @pl.when(step == 0)
    def _():
        if has_init:
            c_ref[...] = c0_ref[...]
            n_ref[...] = n0_ref[...]
            m_ref[...] = m0_ref[...]
        else:
            c_ref[...] = jnp.zeros_like(c_ref)
            n_ref[...] = jnp.zeros_like(n_ref)
            m_ref[...] = jnp.zeros_like(m_ref)

    tok0 = lax.broadcasted_iota(jnp.int32, (CHUNK, CHUNK), 0)
    tok1 = lax.broadcasted_iota(jnp.int32, (CHUNK, CHUNK), 1)
    k_scale = dh ** -0.5
    m_all = m_ref[...]
    m_out = m_all
    unit_lane = lax.broadcasted_iota(jnp.int32, m_all.shape, 1)
    for d in range(N_DIR):
        q_ref, k_ref, vt_ref, g_ref, gt_ref, h_ref = (
            (qf_ref, kf_ref, vf_ref, gf_ref, gtf_ref, hf_ref) if d == 0 else
            (qb_ref, kb_ref, vb_ref, gb_ref, gtb_ref, hb_ref))
        seen_t = (tok0 <= tok1) if d == 0 else (tok0 >= tok1)
        seen_t_bf = seen_t.astype(BF16)
        seen_bf = ((tok1 <= tok0) if d == 0 else (tok1 >= tok0)).astype(BF16)
        pre_col = g_ref[:, :n_gate] + brow_ref[...]
        pre_row = gt_ref[...] + bcol_ref[...]
        cum_col = sum(_dot(seen_bf, part) for part in _split3(_log_sigmoid(pre_col)))
        cum_row = sum(_dot(part, seen_t_bf) for part in _split3(_log_sigmoid(pre_row)))
        last = CHUNK - 1 if d == 0 else 0
        for h in range(heads):
            ci = d * 2 * heads + h
            cf = ci + heads
            sid = d * heads + h
            sl = slice(h * dh, (h + 1) * dh)
            c_col = pre_col[:, ci:ci + 1] - cum_col[:, cf:cf + 1]
            i_row = pre_row[ci:ci + 1, :]
            b_row = cum_row[cf:cf + 1, :]
            b_end = b_row[:, last:last + 1]
            m_prev = m_all[:, sid:sid + 1]
            a_row = b_row + m_prev
            dmat = jnp.where(seen_t, b_row + c_col, -jnp.inf)
            m_t = jnp.maximum(a_row, jnp.max(dmat, axis=0, keepdims=True))
            w_intra = jnp.exp(dmat - m_t)
            w_inter = jnp.exp(a_row - m_t)

            q = q_ref[:, sl]
            k_bf = (k_ref[:, sl].astype(F32) * k_scale).astype(BF16)
            v_t = vt_ref[sl, :]
            c_prev = c_ref[d, h]
            n_prev = n_ref[d, h]

            s_t = _dot_nt(k_bf, q) * w_intra
            n_rows = jnp.broadcast_to(n_prev, (PACK_ROWS, dh)).astype(BF16)
            cq = _dot_nt(jnp.concatenate([c_prev.astype(BF16), n_rows], axis=0), q)
            num = w_inter * cq[:dh, :] + _dot(v_t, s_t.astype(BF16))
            den = w_inter * cq[dh:dh + 1, :] + jnp.sum(s_t, axis=0, keepdims=True)
            h_ref[sl, :] = num / jnp.maximum(jnp.abs(den), jnp.exp(-m_t))

            g_row = b_end - b_row + i_row
            m_new = jnp.maximum(b_end + m_prev, jnp.max(g_row, axis=1, keepdims=True))
            w_pos = jnp.exp(g_row - m_new)
            w_carry = jnp.exp(b_end + m_prev - m_new)
            w_rows = jnp.broadcast_to(w_pos, (PACK_ROWS, CHUNK)).astype(BF16)
            upd = _dot(jnp.concatenate([(v_t.astype(F32) * w_pos).astype(BF16), w_rows], axis=0), k_bf)
            c_ref[d, h] = w_carry * c_prev + upd[:dh, :]
            n_ref[d, h] = w_carry * n_prev + upd[dh:dh + 1, :]
            m_out = jnp.where(unit_lane == sid, m_new, m_out)
    m_ref[...] = m_out


def _mlstm(pb, pb_t, p32, gates_t, gate_b, state, layer, row0, batch, t, gate_blk, dh, bases):
    heads = MLSTM_HEADS
    n_gate = N_DIR * 2 * heads
    depth = gate_b.shape[0]
    nc = t // CHUNK
    blk0 = row0 // CHUNK
    w = heads * dh
    fwd = lambda b, c: blk0 + b * nc + c
    bwd = lambda b, c: blk0 + b * nc + nc - 1 - c

    def tok(col, blk):
        return pl.BlockSpec((CHUNK, w), lambda b, c: (blk(b, c), col))

    def feat(blk):
        return pl.BlockSpec((w, CHUNK), lambda b, c: (0, blk(b, c)))

    in_specs = [tok(0, fwd), tok(1, fwd), feat(fwd), tok(0, bwd), tok(1, bwd), feat(bwd),
                pl.BlockSpec((CHUNK, LANES), lambda b, c: (fwd(b, c), gate_blk)),
                pl.BlockSpec((CHUNK, LANES), lambda b, c: (bwd(b, c), gate_blk)),
                pl.BlockSpec((n_gate, CHUNK), lambda b, c: (0, fwd(b, c))),
                pl.BlockSpec((n_gate, CHUNK), lambda b, c: (0, bwd(b, c))),
                pl.BlockSpec((None, 1, n_gate), lambda b, c: (layer, 0, 0)),
                pl.BlockSpec((None, n_gate, 1), lambda b, c: (layer, 0, 0))]
    args = [pb, pb, pb_t, pb, pb, pb_t, p32, p32, gates_t, gates_t,
            gate_b.reshape(-1, 1, n_gate), gate_b.reshape(-1, n_gate, 1)]
    has_init = state is not None
    state_shapes = [(N_DIR, heads, dh, dh), (N_DIR, heads, 1, dh), (1, N_DIR * heads)]
    if has_init:
        c0, n0, m0 = state
        in_specs += [pl.BlockSpec((None, None) + shp, lambda b, c, z=(0,) * len(shp): (b, layer) + z)
                     for shp in state_shapes]
        args += [c0, n0.reshape((batch, depth) + state_shapes[1]), m0.reshape((batch, depth) + state_shapes[2])]
        st_specs = [pl.BlockSpec((None,) + shp, lambda b, c, z=(0,) * len(shp): (b,) + z) for shp in state_shapes]
        st_shapes = [jax.ShapeDtypeStruct((batch,) + shp, F32) for shp in state_shapes]
        all_bases = ()
    else:
        st_specs = [pl.BlockSpec((None, None) + shp, lambda b, c, z=(0,) * len(shp): (b, layer) + z)
                    for shp in state_shapes]
        st_shapes = [jax.ShapeDtypeStruct((batch, depth) + shp, F32) for shp in state_shapes]
        all_bases = (None, None) + tuple(bases)
    return _call(
        functools.partial(_mlstm_kernel, has_init, dh), "mlstm_scan", (batch, nc), in_specs, args,
        [pl.BlockSpec((w, CHUNK), lambda b, c: (0, b * nc + c)),
         pl.BlockSpec((w, CHUNK), lambda b, c: (0, b * nc + nc - 1 - c))] + st_specs,
        [jax.ShapeDtypeStruct((w, batch * t), F32),
         jax.ShapeDtypeStruct((w, batch * t), F32)] + st_shapes,
        ("parallel", "arbitrary"), bases=all_bases)


def _mlstm_post_kernel(dh, hf_ref, hb_ref, o_ref, w_ref, y_ref):
    tm = hf_ref.shape[1]
    hm = hf_ref[...] + hb_ref[...]
    gate = _sigmoid(o_ref[...].astype(F32))
    w = w_ref[...]
    eye = (lax.broadcasted_iota(jnp.int32, (tm, tm), 0)
           == lax.broadcasted_iota(jnp.int32, (tm, tm), 1)).astype(BF16)
    for h in range(MLSTM_HEADS):
        sl = slice(h * dh, (h + 1) * dh)
        x = hm[sl, :]
        y = x * lax.rsqrt(jnp.mean(x * x, axis=0, keepdims=True) + EPS) * w[sl, :]
        y_t = (gate[sl, :] * y).astype(BF16)
        y_ref[:, sl] = _dot_nt(eye, y_t).astype(y_ref.dtype)


def _mlstm_post(h_f, h_b, pb_t, m_norm_w, layer, n_rows, row0, dh, base):
    w, n = h_f.shape
    tm = min(256, n)
    return _call(
        functools.partial(_mlstm_post_kernel, dh), "mlstm_post", (n // tm,),
        [pl.BlockSpec((w, tm), lambda i: (0, i)),
         pl.BlockSpec((w, tm), lambda i: (0, i)),
         pl.BlockSpec((w, tm), lambda i: (1, row0 // tm + i)),
         pl.BlockSpec((None, w, 1), lambda i: (layer, 0, 0))],
        [h_f, h_b, pb_t, m_norm_w],
        pl.BlockSpec((tm, w), lambda i: (row0 // tm + i, 0)),
        jax.ShapeDtypeStruct((n_rows, w), BF16),
        ("parallel",), bases=(base,))


POOL_TILE = 256


def _pool_bands():
    t = np.arange(POOL_TILE)[:, None]
    bands = np.zeros((POOL_GROUPS, 3, POOL_TILE, POOL_TILE), np.float32)
    for g, win in enumerate(POOL_WINDOWS):
        for part in range(3):
            s = np.arange(POOL_TILE)[None, :] + (part - 1) * POOL_TILE
            bands[g, part] = (s >= t - win // 2) & (s < t - win // 2 + win)
    return jnp.asarray(bands, BF16)


def _pool_kernel(t_seq, gd, up_ref, um_ref, un_ref, band_ref, pw_ref, ps_ref, y_ref):
    j = pl.program_id(1)
    has_prev = (j > 0).astype(F32)
    has_next = (j < pl.num_programs(1) - 1).astype(F32)
    tile = um_ref.shape[0]
    pos = j * tile + lax.broadcasted_iota(jnp.int32, (tile, 1), 0)
    for g, win in enumerate(POOL_WINDOWS):
        sl = slice(g * gd, (g + 1) * gd)
        u = um_ref[:, sl]
        acc = (_dot(band_ref[g, 1], u)
               + has_prev * _dot(band_ref[g, 0], up_ref[:, sl])
               + has_next * _dot(band_ref[g, 2], un_ref[:, sl]))
        lo = jnp.clip(pos - win // 2, 0, t_seq)
        hi = jnp.clip(pos - win // 2 + win, 0, t_seq)
        pooled = acc / (hi - lo).astype(F32) - u.astype(F32)
        y = _dot(pooled.astype(BF16), pw_ref[g]) * ps_ref[:, sl]
        y_ref[:, sl] = y.astype(y_ref.dtype)


def _pool(pb, bands, pool_w, pool_scale, layer, n_rows, row0, batch, t, base):
    gd = pool_w.shape[-1]
    w = POOL_GROUPS * gd
    tile = POOL_TILE
    assert t % tile == 0
    nt = t // tile
    blk0 = row0 // tile

    def u_spec(shift):
        return pl.BlockSpec((tile, w), lambda b, j: (blk0 + b * nt + jnp.clip(j + shift, 0, nt - 1), 2))

    return _call(
        functools.partial(_pool_kernel, t, gd), "multiscale_pool", (batch, nt),
        [u_spec(-1), u_spec(0), u_spec(1),
         pl.BlockSpec(bands.shape, lambda b, j: (0, 0, 0, 0)),
         pl.BlockSpec((None, POOL_GROUPS, gd, gd), lambda b, j: (layer, 0, 0, 0)),
         pl.BlockSpec((None, 1, w), lambda b, j: (layer, 0, 0))],
        [pb, pb, pb, bands, pool_w, pool_scale],
        pl.BlockSpec((tile, w), lambda b, j: (blk0 + b * nt + j, 0)),
        jax.ShapeDtypeStruct((n_rows, w), BF16),
        ("parallel", "parallel"), bases=(base,))


def _merge_kernel(ya_ref, yb_ref, yc_ref, ga_ref, gb_ref, gc_ref, w_ref, o_ref):
    acc = _sigmoid(ga_ref[...].astype(F32)) * _dot(ya_ref[...], w_ref[0])
    acc += _sigmoid(gb_ref[...].astype(F32)) * _dot(yb_ref[...], w_ref[1])
    acc += _sigmoid(gc_ref[...].astype(F32)) * _dot(yc_ref[...], w_ref[2])
    o_ref[...] = acc.astype(o_ref.dtype)


def _merge(y_a, y_b, y_c, pb, w_branch, rows, layer, gate_col0):
    n, bw = y_a.shape
    d = w_branch.shape[-1]
    tm = rows.tile(1024)
    tn = min(1024, d)
    g0 = gate_col0 // tn
    nd = d // tn

    def gate_spec(k):
        return pl.BlockSpec((tm, tn), lambda i, j: (i, g0 + k * nd + j))

    y_spec = pl.BlockSpec((tm, bw), lambda i, j: (i, 0))
    return _call(
        _merge_kernel, "branch_merge", (n // tm, nd),
        [y_spec, y_spec, y_spec, gate_spec(0), gate_spec(1), gate_spec(2),
         pl.BlockSpec((None, N_BRANCH, bw, tn), lambda i, j: (layer, 0, 0, j))],
        [y_a, y_b, y_c, pb, pb, pb, w_branch],
        pl.BlockSpec((tm, tn), lambda i, j: (i, j)),
        jax.ShapeDtypeStruct((n, d), BF16),
        ("parallel", "arbitrary"))


def _outproj_kernel(m_ref, w_ref, x_ref, g_ref, o_ref):
    o_ref[...] = x_ref[...] + g_ref[...] * _dot(m_ref[...], w_ref[...])


def _outproj(merged, w_out, x, mod, rows, layer):
    n, d = x.shape
    tm = rows.tile(1024)
    tn = min(1024, d)
    return _call(
        _outproj_kernel, "mixer_out_proj", (n // tm, d // tn),
        [pl.BlockSpec((tm, d), lambda i, j: (i, 0)),
         pl.BlockSpec((None, d, tn), lambda i, j: (layer, 0, j)),
         pl.BlockSpec((tm, tn), lambda i, j: (i, j)),
         pl.BlockSpec((None, None, None, 1, tn),
                      lambda i, j: (layer, rows.mod_row(i * tm), 5, 0, j))],
        [merged, w_out, x, mod],
        pl.BlockSpec((tm, tn), lambda i, j: (i, j)),
        jax.ShapeDtypeStruct((n, d), F32),
        ("parallel", "arbitrary"))


def _final_norm_kernel(x_ref, w_ref, o_ref):
    o_ref[...] = _rms(x_ref[...], w_ref[...])


def _final_norm(x, w, row0, n_rows):
    d = x.shape[1]
    tm = min(512, n_rows)
    return _call(
        _final_norm_kernel, "final_norm", (n_rows // tm,),
        [pl.BlockSpec((tm, d), lambda i: (row0 // tm + i, 0)),
         pl.BlockSpec((1, d), lambda i: (0, 0))],
        [x, w.reshape(1, d)],
        pl.BlockSpec((tm, d), lambda i: (i, 0)),
        jax.ShapeDtypeStruct((n_rows, d), F32),
        ("parallel",))


def _gate_up_prep_kernel(valid, g_ref, u_ref, og_ref, ou_ref):
    for src, dst in ((g_ref, og_ref), (u_ref, ou_ref)):
        dst[:, :valid] = src[...].astype(BF16)
        if dst.shape[1] > valid:
            dst[:, valid:] = jnp.zeros((dst.shape[0], dst.shape[1] - valid), BF16)


def _gate_up_prep(w_gu, hp):
    depth, n_ffn, d, h2 = w_gu.shape
    h = h2 // 2
    assert h % LANES == 0
    n_rows = depth * n_ffn * d
    tr = 256
    flat = w_gu.reshape(n_rows, h2)
    out = jax.ShapeDtypeStruct((n_rows, hp), BF16)
    w_g, w_u = _call(
        functools.partial(_gate_up_prep_kernel, h), "ffn_gate_up_prep", (n_rows // tr,),
        [pl.BlockSpec((tr, h), lambda r: (r, 0)), pl.BlockSpec((tr, h), lambda r: (r, 1))],
        [flat, flat],
        [pl.BlockSpec((tr, hp), lambda r: (r, 0))] * 2, [out, out], ("parallel",))
    return w_g.reshape(depth, n_ffn, d, hp), w_u.reshape(depth, n_ffn, d, hp)


def _down_prep_kernel(valid, w_ref, o_ref):
    o_ref[:valid, :] = w_ref[...].astype(BF16)
    if o_ref.shape[0] > valid:
        o_ref[valid:, :] = jnp.zeros((o_ref.shape[0] - valid, o_ref.shape[1]), BF16)


def _down_prep(w_down, hp):
    depth, n_ffn, h, d = w_down.shape
    assert h % PACK_ROWS == 0
    td = _lane_tile(d, 256)
    flat = w_down.reshape(depth * n_ffn, h, d)
    out = _call(
        functools.partial(_down_prep_kernel, h), "ffn_down_prep", (depth * n_ffn, d // td),
        [pl.BlockSpec((None, h, td), lambda a, j: (a, 0, j))], [flat],
        pl.BlockSpec((None, hp, td), lambda a, j: (a, 0, j)),
        jax.ShapeDtypeStruct((depth * n_ffn, hp, d), BF16), ("parallel", "parallel"))
    return out.reshape(depth, n_ffn, hp, d)


def _transpose_kernel(w_ref, o_ref):
    n = o_ref.shape[0]
    eye = (lax.broadcasted_iota(jnp.int32, (n, n), 0) == lax.broadcasted_iota(jnp.int32, (n, n), 1))
    o_ref[...] = _dot_nt(eye.astype(BF16), w_ref[...]).astype(BF16)


def _transpose_weights(w):
    depth, d, c = w.shape
    tc = _lane_tile(c, 256)
    return _call(
        _transpose_kernel, "weight_transpose", (depth, c // tc),
        [pl.BlockSpec((None, d, tc), lambda l, j: (l, 0, j))], [w],
        pl.BlockSpec((None, tc, d), lambda l, j: (l, j, 0)),
        jax.ShapeDtypeStruct((depth, c, d), BF16), ("parallel", "parallel"))


def _rope_swap_index():
    quarter = ROPE_DIM // 4
    idx = np.arange(ROPE_DIM).reshape(2, 2, quarter)
    return idx[:, ::-1, :].reshape(-1)


def _rope_tables(t):
    pos = jnp.arange(t)
    row = (pos // GRID_W).astype(F32)
    col = (pos % GRID_W).astype(F32)
    n_freq = ROPE_DIM // 4
    inv_freq = jnp.power(ROPE_BASE, -jnp.arange(n_freq, dtype=F32) / n_freq)
    ang_r = row[:, None] * inv_freq
    ang_c = col[:, None] * inv_freq
    cos = jnp.concatenate([jnp.cos(ang_r), jnp.cos(ang_r), jnp.cos(ang_c), jnp.cos(ang_c)], axis=-1)
    sin = jnp.concatenate([-jnp.sin(ang_r), jnp.sin(ang_r), -jnp.sin(ang_c), jnp.sin(ang_c)], axis=-1)
    return cos, sin


def kernel(x_prompt, x_sample, cache_ckv, cache_kpe, state_C, state_n, state_m, c, c_ctx, w_mod, b_mod, norm_w, ffn_w_gu, ffn_w_down, w_in, q_norm_w, kv_norm_w, w_uq, w_ukv, mlstm_gate_b, mlstm_norm_w, pool_w, pool_scale, w_branch, w_out, final_norm_w):
    batch, seq, d = x_prompt.shape
    dec_batch, dec_seq, _ = x_sample.shape
    depth = w_mod.shape[0]
    q_rank, kv_rank = q_norm_w.shape[1], kv_norm_w.shape[1]
    heads = MLA_HEADS
    mw = mlstm_norm_w.shape[1]
    dh = mw // MLSTM_HEADS
    pw = pool_scale.shape[1]
    ffn_h = ffn_w_down.shape[2]
    n_gate = N_DIR * 2 * MLSTM_HEADS
    assert mw == pw == w_branch.shape[2] == heads * V_DIM
    rows = _Rows(batch * seq, seq, dec_batch * dec_seq, dec_seq)
    n = rows.n

    hp = _round_up(ffn_h, 512)
    w_g, w_u = _gate_up_prep(ffn_w_gu, hp)
    wdn = _down_prep(ffn_w_down, hp)

    sizes = (q_rank, kv_rank, ROPE_DIM, mw, mw, mw, mw, n_gate, pw, N_BRANCH * d)
    offs = np.concatenate([[0], np.cumsum(sizes)])
    swap = _rope_swap_index()
    small_cols = q_rank + kv_rank + 2 * ROPE_DIM + LANES
    w_in16 = lax.optimization_barrier(w_in.astype(BF16))
    w_small = jnp.concatenate(
        [w_in16[:, :, :offs[3]], w_in16[:, :, offs[2]:offs[3]][:, :, swap], w_in16[:, :, offs[7]:offs[8]],
         jnp.zeros((depth, d, LANES - n_gate), BF16)], axis=-1)
    gate_blk = (q_rank + kv_rank + 2 * ROPE_DIM) // LANES
    w_big = jnp.concatenate([w_in16[:, :, offs[3]:offs[5]], w_in16[:, :, offs[8]:]], axis=-1)
    w_feat = _transpose_weights(w_in16[:, :, offs[5]:offs[7]])
    gate_col0 = 3 * mw

    wq4 = w_uq.reshape(depth, q_rank, heads, QK_DIM)
    wq_pe = wq4[..., NOPE_DIM:]
    wq = jnp.concatenate([wq4[..., :NOPE_DIM].reshape(depth, q_rank, -1),
                          wq_pe.reshape(depth, q_rank, -1),
                          wq_pe[..., swap].reshape(depth, q_rank, -1)], axis=-1).astype(BF16)
    wkv4 = w_ukv.reshape(depth, kv_rank, heads, NOPE_DIM + V_DIM)
    wkv = jnp.concatenate([wkv4[..., :NOPE_DIM].reshape(depth, kv_rank, -1),
                           wkv4[..., NOPE_DIM:].reshape(depth, kv_rank, -1)], axis=-1).astype(BF16)
    wbr = w_branch.astype(BF16)
    wout = w_out.astype(BF16)
    pwb = pool_w.astype(BF16)
    norm_w4 = norm_w.reshape(depth, 3, 1, d)
    qnw = q_norm_w.reshape(depth, 1, q_rank)
    kvnw = kv_norm_w.reshape(depth, 1, kv_rank)
    mnw = mlstm_norm_w.reshape(depth, mw, 1)
    psc = pool_scale.reshape(depth, 1, pw)
    bands = _pool_bands()
    rope_tabs = _rope_tables(dec_seq)

    cond = jnp.concatenate([c_ctx[None, :], c, jnp.zeros((COND_ROWS - 1 - dec_batch, d), F32)], axis=0)
    mod = _mod_all(cond, w_mod, b_mod).reshape(depth, COND_ROWS, N_MOD, 1, d)

    cache_kv = _cache_kv(cache_ckv, cache_kpe, wkv)

    x = (x_prompt.reshape(rows.n_ctx, d), x_sample.reshape(rows.n_lat, d))
    new_cache = (None, None)
    new_state = (None, None, None)
    for l in range(depth):
        x, h_mix = _ffn(x, mod, norm_w4, w_g, w_u, wdn, rows, l, 0)

        p32 = _inproj(h_mix, w_small, rows, l, F32, small_cols)
        pb = _inproj(h_mix, w_big, rows, l, BF16, _lane_tile(w_big.shape[2], 1024))
        pb_t = _inproj_t(h_mix, w_feat, rows, l)
        gates_t = p32[:, gate_blk * LANES:gate_blk * LANES + n_gate].T

        q_c, k_c, v_c, *new_cache = _mla_prep(p32, qnw, kvnw, wq, wkv, l, 0, batch, seq, None, new_cache)
        q_s, k_s, v_s = _mla_prep(p32, qnw, kvnw, wq, wkv, l, rows.n_ctx, dec_batch, dec_seq, rope_tabs, None)
        y_a = _attention(q_c, k_c, v_c, None, l, n, 0, None)
        y_a = _attention(q_s, k_s, v_s, cache_kv, l, n, rows.n_ctx, y_a)

        hf_c, hb_c, *new_state = _mlstm(pb, pb_t, p32, gates_t, mlstm_gate_b, None, l, 0, batch, seq,
                                        gate_blk, dh, new_state)
        hf_s, hb_s, _, _, _ = _mlstm(pb, pb_t, p32, gates_t, mlstm_gate_b, (state_C, state_n, state_m), l,
                                     rows.n_ctx, dec_batch, dec_seq, gate_blk, dh, None)
        y_b = _mlstm_post(hf_c, hb_c, pb_t, mnw, l, n, 0, dh, None)
        y_b = _mlstm_post(hf_s, hb_s, pb_t, mnw, l, n, rows.n_ctx, dh, y_b)

        y_c = _pool(pb, bands, pwb, psc, l, n, 0, batch, seq, None)
        y_c = _pool(pb, bands, pwb, psc, l, n, rows.n_ctx, dec_batch, dec_seq, y_c)

        merged = _merge(y_a, y_b, y_c, pb, wbr, rows, l, gate_col0)
        x = _outproj(merged, wout, x, mod, rows, l)
        x = _ffn(x, mod, norm_w4, w_g, w_u, wdn, rows, l, 1)

    y_prompt = _final_norm(x, final_norm_w, 0, rows.n_ctx).reshape(batch, seq, d)
    y_sample = _final_norm(x, final_norm_w, rows.n_ctx, rows.n_lat).reshape(dec_batch, dec_seq, d)
    new_c, new_n, new_m = new_state
    return (y_prompt, y_sample, new_cache[0], new_cache[1], new_c,
            new_n.reshape(batch, depth, N_DIR, MLSTM_HEADS, dh),
            new_m.reshape(batch, depth, N_DIR, MLSTM_HEADS))
```

```python
import functools
import math

import numpy as np
import jax
import jax.numpy as jnp
from jax import lax
from jax.experimental import pallas as pl
from jax.experimental.pallas import tpu as pltpu

GRID_W = 64
EPS = 1e-6
N_MOD = 9
MLA_HEADS = 8
NOPE_DIM = 128
ROPE_DIM = 64
V_DIM = 128
QK_DIM = NOPE_DIM + ROPE_DIM
ROPE_BASE = 10000.0
MLSTM_HEADS = 4
N_DIR = 2
CHUNK = 128
POOL_WINDOWS = (2, 4, 8, 16)
POOL_GROUPS = 4
N_BRANCH = 3

LANES = 128
VMEM_LIMIT_MB = 56
COND_ROWS = 8
PACK_ROWS = 16
V_PAD = 2 * V_DIM

F32 = jnp.float32
BF16 = jnp.bfloat16


def _params(sem):
    return pltpu.CompilerParams(dimension_semantics=sem, vmem_limit_bytes=VMEM_LIMIT_MB << 20)


def _call(kernel, name, grid, in_specs, args, out_specs, out_shape, sem, bases=(), scratch=()):
    n_in = len(args)
    extra = [b for b in bases if b is not None]
    aliases = {}
    for k, b in enumerate(bases):
        if b is not None:
            aliases[n_in + len(aliases)] = k

    def body(*refs):
        kernel(*refs[:n_in], *refs[n_in + len(extra):])

    return pl.pallas_call(
        body if extra else kernel, grid=grid,
        in_specs=list(in_specs) + [pl.BlockSpec(memory_space=pl.ANY)] * len(extra),
        out_specs=out_specs, out_shape=out_shape, input_output_aliases=aliases,
        scratch_shapes=list(scratch), compiler_params=_params(sem), name=name)(*args, *extra)


def _round_up(n, m):
    return (n + m - 1) // m * m


def _lane_tile(n, cap):
    t = cap - cap % LANES
    while n % t:
        t -= LANES
    return t


def _sigmoid(x):
    return 1.0 / (1.0 + jnp.exp(-x))


def _log_sigmoid(x):
    return -(jnp.maximum(-x, 0.0) + jnp.log1p(jnp.exp(-jnp.abs(x))))


def _rms(x, w):
    return x * lax.rsqrt(jnp.mean(x * x, axis=-1, keepdims=True) + EPS) * w


def _dot(a, b):
    return jnp.dot(a, b, preferred_element_type=F32)


def _dot_nt(a, b):
    return lax.dot_general(a, b, (((1,), (1,)), ((), ())), preferred_element_type=F32)


def _mod_kernel(c_ref, w_ref, b_ref, o_ref):
    c = c_ref[...]
    a = (c * _sigmoid(c)).astype(BF16)
    o_ref[...] = _dot(a, w_ref[...].astype(BF16)) + b_ref[...]


def _mod_all(cond, w_mod, b_mod):
    depth, d, nd = w_mod.shape
    tn = _lane_tile(nd, 1024)
    return _call(
        _mod_kernel, "adaln_mod", (depth, nd // tn),
        [pl.BlockSpec((COND_ROWS, d), lambda l, j: (0, 0)),
         pl.BlockSpec((None, d, tn), lambda l, j: (l, 0, j)),
         pl.BlockSpec((None, 1, tn), lambda l, j: (l, 0, j))],
        [cond, w_mod, b_mod.reshape(depth, 1, nd)],
        pl.BlockSpec((None, COND_ROWS, tn), lambda l, j: (l, 0, j)),
        jax.ShapeDtypeStruct((depth, COND_ROWS, nd), F32),
        ("parallel", "parallel"))


class _Rows:
    def __init__(self, n_ctx, t_ctx, n_lat, t_lat):
        self.n_ctx, self.t_ctx, self.n_lat, self.t_lat = n_ctx, t_ctx, n_lat, t_lat
        self.n = n_ctx + n_lat

    def mod_row(self, row):
        return jnp.where(row < self.n_ctx, 0, 1 + (row - self.n_ctx) // self.t_lat)

    def tile(self, cap):
        t = min(cap, self.n_ctx, self.t_lat)
        assert self.n_ctx % t == 0 and self.t_lat % t == 0
        return t


def _mod_spec(rows, tm, layer, k, d):
    return pl.BlockSpec((None, None, None, 1, d),
                        lambda i, j: (layer, rows.mod_row(i * tm), k, 0, 0))


def _norm_mod_to(h_ref, x_ref, nw_ref, sh_ref, sc_ref):
    y = _rms(x_ref[...], nw_ref[...])
    h_ref[...] = (y * (1.0 + sc_ref[...]) + sh_ref[...]).astype(h_ref.dtype)


def _ffn_kernel(emit_next, ctx_tiles, *refs):
    x_refs, refs = (refs[:1], refs[1:]) if ctx_tiles is None else (refs[:2], refs[2:])
    sh_ref, sc_ref, g_ref, nw_ref, wg_ref, wu_ref, wd_ref = refs[:7]
    if emit_next:
        sh2_ref, sc2_ref, nw2_ref, o_ref, h2_ref, h_ref = refs[7:]
    else:
        o_ref, h_ref = refs[7:]
    i = pl.program_id(0)
    j = pl.program_id(1)

    def per_source(fn):
        if ctx_tiles is None:
            fn(x_refs[0])
        else:
            pl.when(i < ctx_tiles)(functools.partial(fn, x_refs[0]))
            pl.when(i >= ctx_tiles)(functools.partial(fn, x_refs[1]))

    def prologue(x_ref):
        _norm_mod_to(h_ref, x_ref, nw_ref, sh_ref, sc_ref)

    def epilogue(x_ref):
        o_ref[...] = x_ref[...] + 0.5 * g_ref[...] * o_ref[...]
        if emit_next:
            _norm_mod_to(h2_ref, o_ref, nw2_ref, sh2_ref, sc2_ref)

    @pl.when(j == 0)
    def _():
        per_source(prologue)
        o_ref[...] = jnp.zeros_like(o_ref)

    h = h_ref[...]
    g = _dot(h, wg_ref[...])
    u = _dot(h, wu_ref[...])
    a = (g * _sigmoid(g) * u).astype(BF16)
    o_ref[...] += _dot(a, wd_ref[...])

    @pl.when(j == pl.num_programs(1) - 1)
    def _():
        per_source(epilogue)


def _ffn(x, mod, norm_w, w_g, w_u, w_down, rows, layer, which):
    tm = rows.tile(512)
    if isinstance(x, tuple):
        ctx_tiles = rows.n_ctx // tm
        x_args = list(x)
        x_specs = [pl.BlockSpec((tm, x[0].shape[1]), lambda i, j: (jnp.minimum(i, ctx_tiles - 1), 0)),
                   pl.BlockSpec((tm, x[0].shape[1]), lambda i, j: (jnp.maximum(i - ctx_tiles, 0), 0))]
    else:
        ctx_tiles = None
        x_args = [x]
        x_specs = [pl.BlockSpec((tm, x.shape[1]), lambda i, j: (i, 0))]
    n, d = rows.n, x_args[0].shape[1]
    hp = w_down.shape[2]
    th = _lane_tile(hp, 512)
    k0 = 0 if which == 0 else 6
    emit_next = which == 0
    w_spec = pl.BlockSpec((None, None, d, th), lambda i, j: (layer, which, 0, j))
    row_spec = pl.BlockSpec((tm, d), lambda i, j: (i, 0))
    norm_spec = lambda k: pl.BlockSpec((None, None, 1, d), lambda i, j: (layer, k, 0, 0))
    in_specs = x_specs + [
        _mod_spec(rows, tm, layer, k0, d),
        _mod_spec(rows, tm, layer, k0 + 1, d),
        _mod_spec(rows, tm, layer, k0 + 2, d),
        norm_spec(2 * which), w_spec, w_spec,
        pl.BlockSpec((None, None, th, d), lambda i, j: (layer, which, j, 0))]
    args = x_args + [mod, mod, mod, norm_w, w_g, w_u, w_down]
    out_specs, out_shape = row_spec, jax.ShapeDtypeStruct((n, d), F32)
    if emit_next:
        in_specs += [_mod_spec(rows, tm, layer, 3, d), _mod_spec(rows, tm, layer, 4, d), norm_spec(1)]
        args += [mod, mod, norm_w]
        out_specs, out_shape = [row_spec, row_spec], [out_shape, jax.ShapeDtypeStruct((n, d), BF16)]
    return _call(
        functools.partial(_ffn_kernel, emit_next, ctx_tiles), "ffn_half_step", (n // tm, hp // th),
        in_specs, args, out_specs, out_shape, ("parallel", "arbitrary"),
        scratch=[pltpu.VMEM((tm, d), BF16)])


def _inproj_kernel(h_ref, w_ref, o_ref):
    o_ref[...] = _dot(h_ref[...], w_ref[...]).astype(o_ref.dtype)


def _inproj(h, w, rows, layer, out_dtype, tn):
    n, d = h.shape
    cols = w.shape[2]
    tm = rows.tile(1024)
    return _call(
        _inproj_kernel, "mixer_in_proj", (n // tm, cols // tn),
        [pl.BlockSpec((tm, d), lambda i, j: (i, 0)),
         pl.BlockSpec((None, d, tn), lambda i, j: (layer, 0, j))],
        [h, w],
        pl.BlockSpec((tm, tn), lambda i, j: (i, j)),
        jax.ShapeDtypeStruct((n, cols), out_dtype),
        ("parallel", "arbitrary"))


def _inproj_t_kernel(h_ref, wt_ref, o_ref):
    o_ref[...] = _dot_nt(wt_ref[...], h_ref[...]).astype(o_ref.dtype)


def _inproj_t(h, w_t, rows, layer):
    n, d = h.shape
    cols = w_t.shape[1]
    tm = rows.tile(1024)
    tn = _lane_tile(cols, 1024)
    return _call(
        _inproj_t_kernel, "mixer_in_proj_t", (n // tm, cols // tn),
        [pl.BlockSpec((tm, d), lambda i, j: (i, 0)),
         pl.BlockSpec((None, tn, d), lambda i, j: (layer, j, 0))],
        [h, w_t],
        pl.BlockSpec((tn, tm), lambda i, j: (j, i)),
        jax.ShapeDtypeStruct((cols, n), BF16),
        ("parallel", "arbitrary"))


def _ones_column(rows):
    lane = lax.broadcasted_iota(jnp.int32, (rows, V_PAD - V_DIM), 1)
    return (lane == 0).astype(BF16)


def _mla_prep_kernel(rope, q_rank, kv_rank, p_ref, qnw_ref, kvnw_ref, wq_ref, wkv_ref, *rest):
    if rope:
        cos_ref, sin_ref, q_ref, k_ref, v_ref = rest
    else:
        q_ref, k_ref, v_ref, ckv_ref, kpe_ref = rest
    heads = MLA_HEADS
    p = p_ref[...]
    c_q = p[:, :q_rank]
    c_kv = p[:, q_rank:q_rank + kv_rank]
    o = q_rank + kv_rank
    k_pe = p[:, o:o + ROPE_DIM]
    k_pe_sw = p[:, o + ROPE_DIM:o + 2 * ROPE_DIM]

    qa = _dot(_rms(c_q, qnw_ref[...]).astype(BF16), wq_ref[...])
    ckv_n = _rms(c_kv, kvnw_ref[...])
    kv = _dot(ckv_n.astype(BF16), wkv_ref[...])
    scale = QK_DIM ** -0.5 * math.log2(math.e)
    if rope:
        cos = cos_ref[...]
        sin = sin_ref[...]
        k_pe = k_pe * cos + k_pe_sw * sin
    else:
        ckv_ref[...] = ckv_n
        kpe_ref[...] = k_pe
    ones = _ones_column(p.shape[0])
    pe0 = heads * NOPE_DIM
    sw0 = pe0 + heads * ROPE_DIM
    for h in range(heads):
        q_pe = qa[:, pe0 + h * ROPE_DIM:pe0 + (h + 1) * ROPE_DIM]
        if rope:
            q_pe = q_pe * cos + qa[:, sw0 + h * ROPE_DIM:sw0 + (h + 1) * ROPE_DIM] * sin
        q_ref[h, :, :NOPE_DIM] = (qa[:, h * NOPE_DIM:(h + 1) * NOPE_DIM] * scale).astype(BF16)
        q_ref[h, :, NOPE_DIM:] = (q_pe * scale).astype(BF16)
        k_ref[h, :, :NOPE_DIM] = kv[:, h * NOPE_DIM:(h + 1) * NOPE_DIM].astype(BF16)
        k_ref[h, :, NOPE_DIM:] = k_pe.astype(BF16)
        v0 = heads * NOPE_DIM + h * V_DIM
        v_ref[h, :, :V_DIM] = kv[:, v0:v0 + V_DIM].astype(BF16)
        v_ref[h, :, V_DIM:] = ones


def _mla_prep(p32, q_norm_w, kv_norm_w, wq, wkv, layer, row0, batch, t, rope_tabs, bases):
    ws = p32.shape[1]
    q_rank, kv_rank = q_norm_w.shape[-1], kv_norm_w.shape[-1]
    depth = q_norm_w.shape[0]
    tm = min(256, t)
    nt = t // tm
    heads = MLA_HEADS
    rope = rope_tabs is not None
    in_specs = [pl.BlockSpec((tm, ws), lambda b, i: (row0 // tm + b * nt + i, 0)),
                pl.BlockSpec((None, 1, q_rank), lambda b, i: (layer, 0, 0)),
                pl.BlockSpec((None, 1, kv_rank), lambda b, i: (layer, 0, 0)),
                pl.BlockSpec((None,) + wq.shape[1:], lambda b, i: (layer, 0, 0)),
                pl.BlockSpec((None,) + wkv.shape[1:], lambda b, i: (layer, 0, 0))]
    args = [p32, q_norm_w, kv_norm_w, wq, wkv]
    head_spec = lambda width: pl.BlockSpec((None, heads, tm, width), lambda b, i: (b, 0, i, 0))
    out_specs = [head_spec(QK_DIM), head_spec(QK_DIM), head_spec(V_PAD)]
    out_shape = [jax.ShapeDtypeStruct((batch, heads, t, QK_DIM), BF16),
                 jax.ShapeDtypeStruct((batch, heads, t, QK_DIM), BF16),
                 jax.ShapeDtypeStruct((batch, heads, t, V_PAD), BF16)]
    if rope:
        in_specs += [pl.BlockSpec((tm, ROPE_DIM), lambda b, i: (i, 0))] * 2
        args += list(rope_tabs)
        all_bases = ()
    else:
        out_specs += [pl.BlockSpec((None, None, tm, kv_rank), lambda b, i: (b, layer, i, 0)),
                      pl.BlockSpec((None, None, tm, ROPE_DIM), lambda b, i: (b, layer, i, 0))]
        out_shape += [jax.ShapeDtypeStruct((batch, depth, t, kv_rank), F32),
                      jax.ShapeDtypeStruct((batch, depth, t, ROPE_DIM), F32)]
        all_bases = (None, None, None) + tuple(bases)
    return _call(functools.partial(_mla_prep_kernel, rope, q_rank, kv_rank), "mla_prep",
                 (batch, nt), in_specs, args, out_specs, out_shape, ("parallel", "parallel"),
                 bases=all_bases)


def _cache_kv_kernel(ckv_ref, kpe_ref, wkv_ref, k_ref, v_ref):
    heads = MLA_HEADS
    kv = _dot(ckv_ref[...].astype(BF16), wkv_ref[...])
    k_pe = kpe_ref[...].astype(BF16)
    ones = _ones_column(kv.shape[0])
    for h in range(heads):
        k_ref[h, :, :NOPE_DIM] = kv[:, h * NOPE_DIM:(h + 1) * NOPE_DIM].astype(BF16)
        k_ref[h, :, NOPE_DIM:] = k_pe
        v0 = heads * NOPE_DIM + h * V_DIM
        v_ref[h, :, :V_DIM] = kv[:, v0:v0 + V_DIM].astype(BF16)
        v_ref[h, :, V_DIM:] = ones


def _cache_kv(cache_ckv, cache_kpe, wkv):
    batch, depth, past, kv_rank = cache_ckv.shape
    heads = MLA_HEADS
    return _call(
        _cache_kv_kernel, "mla_cache_kv", (batch, depth),
        [pl.BlockSpec((None, None, past, kv_rank), lambda b, l: (b, l, 0, 0)),
         pl.BlockSpec((None, None, past, ROPE_DIM), lambda b, l: (b, l, 0, 0)),
         pl.BlockSpec((None,) + wkv.shape[1:], lambda b, l: (l, 0, 0))],
        [cache_ckv, cache_kpe, wkv],
        [pl.BlockSpec((None, None, heads, past, QK_DIM), lambda b, l: (b, l, 0, 0, 0)),
         pl.BlockSpec((None, None, heads, past, V_PAD), lambda b, l: (b, l, 0, 0, 0))],
        [jax.ShapeDtypeStruct((batch, depth, heads, past, QK_DIM), BF16),
         jax.ShapeDtypeStruct((batch, depth, heads, past, V_PAD), BF16)],
        ("parallel", "parallel"))


ATTN_KEY_CHUNK = 512


def _attn_kernel(past, q_ref, qn_ref, k_ref, kn_ref, v_ref, *rest):
    if past:
        kc_ref, kcn_ref, vc_ref = rest[:3]
        rest = rest[3:]
    else:
        kc_ref = kcn_ref = vc_ref = None
    o_ref, s0_ref, s1_ref, m0_ref, m1_ref = rest
    tq = qn_ref.shape[0]
    t = k_ref.shape[0]
    chunk = min(ATTN_KEY_CHUNK, t)

    def scores(q, keys_ref, cache_keys_ref, s_ref, m_ref):
        s = _dot_nt(q, keys_ref[...])
        m = jnp.max(s, axis=-1, keepdims=True)
        if past:
            sc = _dot_nt(q, cache_keys_ref[...])
            m = jnp.maximum(m, jnp.max(sc, axis=-1, keepdims=True))
            s_ref[:, :past] = sc
        s_ref[:, past:] = s
        m_ref[...] = m

    def values(r, s_ref, m_ref):
        m = m_ref[...]
        acc = None
        if past:
            acc = _dot(jnp.exp2(s_ref[:, :past] - m).astype(BF16), vc_ref[...])
        for c in range(0, t, chunk):
            p = jnp.exp2(s_ref[:, past + c:past + c + chunk] - m).astype(BF16)
            d = _dot(p, v_ref[c:c + chunk, :])
            acc = d if acc is None else acc + d
        o_ref[r * tq:(r + 1) * tq, :] = (acc[:, :V_DIM] / acc[:, V_DIM:V_DIM + 1]).astype(o_ref.dtype)

    @pl.when(pl.program_id(0) == 0)
    def _():
        scores(q_ref[:tq, :], k_ref, kc_ref, s0_ref, m0_ref)

    values(0, s0_ref, m0_ref)
    scores(q_ref[tq:, :], k_ref, kc_ref, s1_ref, m1_ref)
    values(1, s1_ref, m1_ref)
    scores(qn_ref[...], kn_ref, kcn_ref, s0_ref, m0_ref)


def _attn_short_kernel(q_ref, k_ref, v_ref, o_ref):
    for h in range(q_ref.shape[0]):
        s = _dot_nt(q_ref[h], k_ref[h])
        m = jnp.max(s, axis=-1, keepdims=True)
        o = _dot(jnp.exp2(s - m).astype(BF16), v_ref[h])
        o_ref[:, h * V_DIM:(h + 1) * V_DIM] = (o[:, :V_DIM] / o[:, V_DIM:V_DIM + 1]).astype(o_ref.dtype)


ATTN_SHORT_SEQ = 256


def _attention(q, k, v, cache, layer, n_rows, row0, base):
    batch, heads, t, _ = q.shape
    if cache is None and t <= ATTN_SHORT_SEQ:
        assert row0 % t == 0
        head_block = lambda width: pl.BlockSpec((None, heads, t, width), lambda b: (b, 0, 0, 0))
        return _call(
            _attn_short_kernel, "mla_attention_short", (batch,),
            [head_block(QK_DIM), head_block(QK_DIM), head_block(V_PAD)], [q, k, v],
            pl.BlockSpec((t, heads * V_DIM), lambda b: (row0 // t + b, 0)),
            jax.ShapeDtypeStruct((n_rows, heads * V_DIM), BF16), ("parallel",), bases=(base,))
    tq = min(512, t // 2)
    pair = 2 * tq
    npair = t // pair
    n_steps = batch * heads * npair
    past = cache[0].shape[3] if cache is not None else 0
    assert t % pair == 0 and row0 % pair == 0

    def where(tile):
        tile = jnp.minimum(tile, 2 * n_steps - 1)
        p = tile // 2
        return p // (heads * npair), (p // npair) % heads, 2 * (p % npair) + tile % 2

    def pair_map(g):
        b, h, i = where(2 * g)
        return b, h, i // 2, 0

    def next_map(g):
        b, h, i = where(2 * g + 2)
        return b, h, i, 0

    def kv_map(shift):
        def index(g):
            b, h, _ = where(2 * g + shift)
            return b, h, 0, 0
        return index

    def cache_map(shift):
        def index(g):
            b, h, _ = where(2 * g + shift)
            return b, layer, h, 0, 0
        return index

    def o_map(g):
        b, h, i = where(2 * g)
        return row0 // pair + b * npair + i // 2, h

    in_specs = [pl.BlockSpec((None, None, pair, QK_DIM), pair_map),
                pl.BlockSpec((None, None, tq, QK_DIM), next_map),
                pl.BlockSpec((None, None, t, QK_DIM), kv_map(0)),
                pl.BlockSpec((None, None, t, QK_DIM), kv_map(2)),
                pl.BlockSpec((None, None, t, V_PAD), kv_map(0))]
    args = [q, q, k, k, v]
    if past:
        in_specs += [pl.BlockSpec((None, None, None, past, QK_DIM), cache_map(0)),
                     pl.BlockSpec((None, None, None, past, QK_DIM), cache_map(2)),
                     pl.BlockSpec((None, None, None, past, V_PAD), cache_map(0))]
        args += [cache[0], cache[0], cache[1]]
    s_total = past + t
    return _call(
        functools.partial(_attn_kernel, past), "mla_attention", (n_steps,),
        in_specs, args,
        pl.BlockSpec((pair, V_DIM), o_map),
        jax.ShapeDtypeStruct((n_rows, heads * V_DIM), BF16),
        ("arbitrary",), bases=(base,),
        scratch=[pltpu.VMEM((tq, s_total), F32), pltpu.VMEM((tq, s_total), F32),
                 pltpu.VMEM((tq, 1), F32), pltpu.VMEM((tq, 1), F32)])


MLSTM_CHUNKS_PER_STEP = 2


def _split3(x):
    hi = x.astype(BF16)
    r = x - hi.astype(F32)
    mid = r.astype(BF16)
    return hi, mid, (r - mid.astype(F32)).astype(BF16)


def _mlstm_kernel(has_init, dh, n_sub, *refs):
    (qf_ref, kf_ref, vf_ref, qb_ref, kb_ref, vb_ref, gf_ref, gb_ref, gtf_ref, gtb_ref,
     brow_ref, bcol_ref) = refs[:12]
    refs = refs[12:]
    if has_init:
        c0_ref, n0_ref, m0_ref = refs[:3]
        refs = refs[3:]
    hf_ref, hb_ref, c_ref, n_ref, m_ref = refs
    heads = MLSTM_HEADS
    n_gate = N_DIR * 2 * heads
    step = pl.program_id(1)

    @pl.when(step == 0)
    def _():
        if has_init:
            c_ref[...] = c0_ref[...]
            n_ref[...] = n0_ref[...]
            m_ref[...] = m0_ref[...]
        else:
            c_ref[...] = jnp.zeros_like(c_ref)
            n_ref[...] = jnp.zeros_like(n_ref)
            m_ref[...] = jnp.zeros_like(m_ref)

    tok0 = lax.broadcasted_iota(jnp.int32, (CHUNK, CHUNK), 0)
    tok1 = lax.broadcasted_iota(jnp.int32, (CHUNK, CHUNK), 1)
    k_scale = dh ** -0.5
    m_all = m_ref[...]
    m_out = m_all
    unit_lane = lax.broadcasted_iota(jnp.int32, m_all.shape, 1)
    for sub, d in [(sub, d) for sub in range(n_sub) for d in range(N_DIR)]:
        q_ref, k_ref, vt_ref, g_ref, gt_ref, h_ref = (
            (qf_ref, kf_ref, vf_ref, gf_ref, gtf_ref, hf_ref) if d == 0 else
            (qb_ref, kb_ref, vb_ref, gb_ref, gtb_ref, hb_ref))
        chunk_idx = sub if d == 0 else n_sub - 1 - sub
        tok = slice(chunk_idx * CHUNK, (chunk_idx + 1) * CHUNK)
        if d == 0:
            m_all = m_out
        seen_t = (tok0 <= tok1) if d == 0 else (tok0 >= tok1)
        seen_t_bf = seen_t.astype(BF16)
        seen_bf = ((tok1 <= tok0) if d == 0 else (tok1 >= tok0)).astype(BF16)
        pre_col = g_ref[tok, :n_gate] + brow_ref[...]
        pre_row = gt_ref[:, tok] + bcol_ref[...]
        cum_col = sum(_dot(seen_bf, part) for part in _split3(_log_sigmoid(pre_col)))
        cum_row = sum(_dot(part, seen_t_bf) for part in _split3(_log_sigmoid(pre_row)))
        last = CHUNK - 1 if d == 0 else 0
        for h in range(heads):
            ci = d * 2 * heads + h
            cf = ci + heads
            sid = d * heads + h
            sl = slice(h * dh, (h + 1) * dh)
            c_col = pre_col[:, ci:ci + 1] - cum_col[:, cf:cf + 1]
            i_row = pre_row[ci:ci + 1, :]
            b_row = cum_row[cf:cf + 1, :]
            b_end = b_row[:, last:last + 1]
            m_prev = m_all[:, sid:sid + 1]
            a_row = b_row + m_prev
            dmat = jnp.where(seen_t, b_row + c_col, -jnp.inf)
            m_t = jnp.maximum(a_row, jnp.max(dmat, axis=0, keepdims=True))
            w_intra = jnp.exp(dmat - m_t)
            w_inter = jnp.exp(a_row - m_t)

            q = q_ref[tok, sl]
            k_bf = (k_ref[tok, sl].astype(F32) * k_scale).astype(BF16)
            v_t = vt_ref[sl, tok]
            c_prev = c_ref[d, h]
            n_prev = n_ref[d, h]

            s_t = _dot_nt(k_bf, q) * w_intra
            n_rows = jnp.broadcast_to(n_prev, (PACK_ROWS, dh)).astype(BF16)
            cq = _dot_nt(jnp.concatenate([c_prev.astype(BF16), n_rows], axis=0), q)
            num = w_inter * cq[:dh, :] + _dot(v_t, s_t.astype(BF16))
            den = w_inter * cq[dh:dh + 1, :] + jnp.sum(s_t, axis=0, keepdims=True)
            h_ref[sl, tok] = num / jnp.maximum(jnp.abs(den), jnp.exp(-m_t))

            g_row = b_end - b_row + i_row
            m_new = jnp.maximum(b_end + m_prev, jnp.max(g_row, axis=1, keepdims=True))
            w_pos = jnp.exp(g_row - m_new)
            w_carry = jnp.exp(b_end + m_prev - m_new)
            w_rows = jnp.broadcast_to(w_pos, (PACK_ROWS, CHUNK)).astype(BF16)
            upd = _dot(jnp.concatenate([(v_t.astype(F32) * w_pos).astype(BF16), w_rows], axis=0), k_bf)
            c_ref[d, h] = w_carry * c_prev + upd[:dh, :]
            n_ref[d, h] = w_carry * n_prev + upd[dh:dh + 1, :]
            m_out = jnp.where(unit_lane == sid, m_new, m_out)
    m_ref[...] = m_out


def _mlstm(pb, pb_t, p32, gates_t, gate_b, state, layer, row0, batch, t, gate_blk, dh, bases):
    heads = MLSTM_HEADS
    n_gate = N_DIR * 2 * heads
    depth = gate_b.shape[0]
    n_sub = MLSTM_CHUNKS_PER_STEP if (t // CHUNK) % MLSTM_CHUNKS_PER_STEP == 0 else 1
    span = n_sub * CHUNK
    nc = t // span
    assert row0 % span == 0
    blk0 = row0 // span
    w = heads * dh
    fwd = lambda b, c: blk0 + b * nc + c
    bwd = lambda b, c: blk0 + b * nc + nc - 1 - c

    def tok(col, blk):
        return pl.BlockSpec((span, w), lambda b, c: (blk(b, c), col))

    def feat(blk):
        return pl.BlockSpec((w, span), lambda b, c: (0, blk(b, c)))

    in_specs = [tok(0, fwd), tok(1, fwd), feat(fwd), tok(0, bwd), tok(1, bwd), feat(bwd),
                pl.BlockSpec((span, LANES), lambda b, c: (fwd(b, c), gate_blk)),
                pl.BlockSpec((span, LANES), lambda b, c: (bwd(b, c), gate_blk)),
                pl.BlockSpec((n_gate, span), lambda b, c: (0, fwd(b, c))),
                pl.BlockSpec((n_gate, span), lambda b, c: (0, bwd(b, c))),
                pl.BlockSpec((None, 1, n_gate), lambda b, c: (layer, 0, 0)),
                pl.BlockSpec((None, n_gate, 1), lambda b, c: (layer, 0, 0))]
    args = [pb, pb, pb_t, pb, pb, pb_t, p32, p32, gates_t, gates_t,
            gate_b.reshape(-1, 1, n_gate), gate_b.reshape(-1, n_gate, 1)]
    has_init = state is not None
    state_shapes = [(N_DIR, heads, dh, dh), (N_DIR, heads, 1, dh), (1, N_DIR * heads)]
    if has_init:
        c0, n0, m0 = state
        in_specs += [pl.BlockSpec((None, None) + shp, lambda b, c, z=(0,) * len(shp): (b, layer) + z)
                     for shp in state_shapes]
        args += [c0, n0.reshape((batch, depth) + state_shapes[1]), m0.reshape((batch, depth) + state_shapes[2])]
        st_specs = [pl.BlockSpec((None,) + shp, lambda b, c, z=(0,) * len(shp): (b,) + z) for shp in state_shapes]
        st_shapes = [jax.ShapeDtypeStruct((batch,) + shp, F32) for shp in state_shapes]
        all_bases = ()
    else:
        st_specs = [pl.BlockSpec((None, None) + shp, lambda b, c, z=(0,) * len(shp): (b, layer) + z)
                    for shp in state_shapes]
        st_shapes = [jax.ShapeDtypeStruct((batch, depth) + shp, F32) for shp in state_shapes]
        all_bases = (None, None) + tuple(bases)
    return _call(
        functools.partial(_mlstm_kernel, has_init, dh, n_sub), "mlstm_scan", (batch, nc), in_specs, args,
        [pl.BlockSpec((w, span), lambda b, c: (0, b * nc + c)),
         pl.BlockSpec((w, span), lambda b, c: (0, b * nc + nc - 1 - c))] + st_specs,
        [jax.ShapeDtypeStruct((w, batch * t), F32),
         jax.ShapeDtypeStruct((w, batch * t), F32)] + st_shapes,
        ("parallel", "arbitrary"), bases=all_bases)


def _mlstm_post_kernel(dh, hf_ref, hb_ref, o_ref, w_ref, y_ref):
    tm = hf_ref.shape[1]
    hm = hf_ref[...] + hb_ref[...]
    gate = _sigmoid(o_ref[...].astype(F32))
    w = w_ref[...]
    eye = (lax.broadcasted_iota(jnp.int32, (tm, tm), 0)
           == lax.broadcasted_iota(jnp.int32, (tm, tm), 1)).astype(BF16)
    for h in range(MLSTM_HEADS):
        sl = slice(h * dh, (h + 1) * dh)
        x = hm[sl, :]
        y = x * lax.rsqrt(jnp.mean(x * x, axis=0, keepdims=True) + EPS) * w[sl, :]
        y_t = (gate[sl, :] * y).astype(BF16)
        y_ref[:, sl] = _dot_nt(eye, y_t).astype(y_ref.dtype)


def _mlstm_post(h_f, h_b, pb_t, m_norm_w, layer, n_rows, row0, dh, base):
    w, n = h_f.shape
    tm = min(256, n)
    return _call(
        functools.partial(_mlstm_post_kernel, dh), "mlstm_post", (n // tm,),
        [pl.BlockSpec((w, tm), lambda i: (0, i)),
         pl.BlockSpec((w, tm), lambda i: (0, i)),
         pl.BlockSpec((w, tm), lambda i: (1, row0 // tm + i)),
         pl.BlockSpec((None, w, 1), lambda i: (layer, 0, 0))],
        [h_f, h_b, pb_t, m_norm_w],
        pl.BlockSpec((tm, w), lambda i: (row0 // tm + i, 0)),
        jax.ShapeDtypeStruct((n_rows, w), BF16),
        ("parallel",), bases=(base,))


POOL_TILE = 256


def _pool_bands():
    t = np.arange(POOL_TILE)[:, None]
    bands = np.zeros((POOL_GROUPS, 3, POOL_TILE, POOL_TILE), np.float32)
    for g, win in enumerate(POOL_WINDOWS):
        for part in range(3):
            s = np.arange(POOL_TILE)[None, :] + (part - 1) * POOL_TILE
            bands[g, part] = (s >= t - win // 2) & (s < t - win // 2 + win)
    return jnp.asarray(bands, BF16)


def _pool_kernel(t_seq, gd, up_ref, um_ref, un_ref, band_ref, pw_ref, ps_ref, y_ref):
    j = pl.program_id(1)
    has_prev = (j > 0).astype(F32)
    has_next = (j < pl.num_programs(1) - 1).astype(F32)
    tile = um_ref.shape[0]
    pos = j * tile + lax.broadcasted_iota(jnp.int32, (tile, 1), 0)
    for g, win in enumerate(POOL_WINDOWS):
        sl = slice(g * gd, (g + 1) * gd)
        u = um_ref[:, sl]
        acc = (_dot(band_ref[g, 1], u)
               + has_prev * _dot(band_ref[g, 0], up_ref[:, sl])
               + has_next * _dot(band_ref[g, 2], un_ref[:, sl]))
        lo = jnp.clip(pos - win // 2, 0, t_seq)
        hi = jnp.clip(pos - win // 2 + win, 0, t_seq)
        pooled = acc / (hi - lo).astype(F32) - u.astype(F32)
        y = _dot(pooled.astype(BF16), pw_ref[g]) * ps_ref[:, sl]
        y_ref[:, sl] = y.astype(y_ref.dtype)


def _pool(pb, bands, pool_w, pool_scale, layer, n_rows, row0, batch, t, base):
    gd = pool_w.shape[-1]
    w = POOL_GROUPS * gd
    tile = POOL_TILE
    assert t % tile == 0
    nt = t // tile
    blk0 = row0 // tile

    def u_spec(shift):
        return pl.BlockSpec((tile, w), lambda b, j: (blk0 + b * nt + jnp.clip(j + shift, 0, nt - 1), 2))

    return _call(
        functools.partial(_pool_kernel, t, gd), "multiscale_pool", (batch, nt),
        [u_spec(-1), u_spec(0), u_spec(1),
         pl.BlockSpec(bands.shape, lambda b, j: (0, 0, 0, 0)),
         pl.BlockSpec((None, POOL_GROUPS, gd, gd), lambda b, j: (layer, 0, 0, 0)),
         pl.BlockSpec((None, 1, w), lambda b, j: (layer, 0, 0))],
        [pb, pb, pb, bands, pool_w, pool_scale],
        pl.BlockSpec((tile, w), lambda b, j: (blk0 + b * nt + j, 0)),
        jax.ShapeDtypeStruct((n_rows, w), BF16),
        ("parallel", "parallel"), bases=(base,))


def _merge_kernel(ya_ref, yb_ref, yc_ref, ga_ref, gb_ref, gc_ref, w_ref, o_ref):
    acc = _sigmoid(ga_ref[...].astype(F32)) * _dot(ya_ref[...], w_ref[0])
    acc += _sigmoid(gb_ref[...].astype(F32)) * _dot(yb_ref[...], w_ref[1])
    acc += _sigmoid(gc_ref[...].astype(F32)) * _dot(yc_ref[...], w_ref[2])
    o_ref[...] = acc.astype(o_ref.dtype)


def _merge(y_a, y_b, y_c, pb, w_branch, rows, layer, gate_col0):
    n, bw = y_a.shape
    d = w_branch.shape[-1]
    tm = rows.tile(1024)
    tn = min(1024, d)
    g0 = gate_col0 // tn
    nd = d // tn

    def gate_spec(k):
        return pl.BlockSpec((tm, tn), lambda i, j: (i, g0 + k * nd + j))

    y_spec = pl.BlockSpec((tm, bw), lambda i, j: (i, 0))
    return _call(
        _merge_kernel, "branch_merge", (n // tm, nd),
        [y_spec, y_spec, y_spec, gate_spec(0), gate_spec(1), gate_spec(2),
         pl.BlockSpec((None, N_BRANCH, bw, tn), lambda i, j: (layer, 0, 0, j))],
        [y_a, y_b, y_c, pb, pb, pb, w_branch],
        pl.BlockSpec((tm, tn), lambda i, j: (i, j)),
        jax.ShapeDtypeStruct((n, d), BF16),
        ("parallel", "arbitrary"))


def _outproj_kernel(m_ref, w_ref, x_ref, g_ref, o_ref):
    o_ref[...] = x_ref[...] + g_ref[...] * _dot(m_ref[...], w_ref[...])


def _outproj(merged, w_out, x, mod, rows, layer):
    n, d = x.shape
    tm = rows.tile(1024)
    tn = min(1024, d)
    return _call(
        _outproj_kernel, "mixer_out_proj", (n // tm, d // tn),
        [pl.BlockSpec((tm, d), lambda i, j: (i, 0)),
         pl.BlockSpec((None, d, tn), lambda i, j: (layer, 0, j)),
         pl.BlockSpec((tm, tn), lambda i, j: (i, j)),
         pl.BlockSpec((None, None, None, 1, tn),
                      lambda i, j: (layer, rows.mod_row(i * tm), 5, 0, j))],
        [merged, w_out, x, mod],
        pl.BlockSpec((tm, tn), lambda i, j: (i, j)),
        jax.ShapeDtypeStruct((n, d), F32),
        ("parallel", "arbitrary"))


def _final_norm_kernel(x_ref, w_ref, o_ref):
    o_ref[...] = _rms(x_ref[...], w_ref[...])


def _final_norm(x, w, row0, n_rows):
    d = x.shape[1]
    tm = min(512, n_rows)
    return _call(
        _final_norm_kernel, "final_norm", (n_rows // tm,),
        [pl.BlockSpec((tm, d), lambda i: (row0 // tm + i, 0)),
         pl.BlockSpec((1, d), lambda i: (0, 0))],
        [x, w.reshape(1, d)],
        pl.BlockSpec((tm, d), lambda i: (i, 0)),
        jax.ShapeDtypeStruct((n_rows, d), F32),
        ("parallel",))


def _gate_up_prep_kernel(valid, g_ref, u_ref, og_ref, ou_ref):
    for src, dst in ((g_ref, og_ref), (u_ref, ou_ref)):
        dst[:, :valid] = src[...].astype(BF16)
        if dst.shape[1] > valid:
            dst[:, valid:] = jnp.zeros((dst.shape[0], dst.shape[1] - valid), BF16)


def _gate_up_prep(w_gu, hp):
    depth, n_ffn, d, h2 = w_gu.shape
    h = h2 // 2
    assert h % LANES == 0
    n_rows = depth * n_ffn * d
    tr = 256
    flat = w_gu.reshape(n_rows, h2)
    out = jax.ShapeDtypeStruct((n_rows, hp), BF16)
    w_g, w_u = _call(
        functools.partial(_gate_up_prep_kernel, h), "ffn_gate_up_prep", (n_rows // tr,),
        [pl.BlockSpec((tr, h), lambda r: (r, 0)), pl.BlockSpec((tr, h), lambda r: (r, 1))],
        [flat, flat],
        [pl.BlockSpec((tr, hp), lambda r: (r, 0))] * 2, [out, out], ("parallel",))
    return w_g.reshape(depth, n_ffn, d, hp), w_u.reshape(depth, n_ffn, d, hp)


def _down_prep_kernel(valid, w_ref, o_ref):
    o_ref[:valid, :] = w_ref[...].astype(BF16)
    if o_ref.shape[0] > valid:
        o_ref[valid:, :] = jnp.zeros((o_ref.shape[0] - valid, o_ref.shape[1]), BF16)


def _down_prep(w_down, hp):
    depth, n_ffn, h, d = w_down.shape
    assert h % PACK_ROWS == 0
    td = _lane_tile(d, 256)
    flat = w_down.reshape(depth * n_ffn, h, d)
    out = _call(
        functools.partial(_down_prep_kernel, h), "ffn_down_prep", (depth * n_ffn, d // td),
        [pl.BlockSpec((None, h, td), lambda a, j: (a, 0, j))], [flat],
        pl.BlockSpec((None, hp, td), lambda a, j: (a, 0, j)),
        jax.ShapeDtypeStruct((depth * n_ffn, hp, d), BF16), ("parallel", "parallel"))
    return out.reshape(depth, n_ffn, hp, d)


def _transpose_kernel(w_ref, o_ref):
    n = o_ref.shape[0]
    eye = (lax.broadcasted_iota(jnp.int32, (n, n), 0) == lax.broadcasted_iota(jnp.int32, (n, n), 1))
    o_ref[...] = _dot_nt(eye.astype(BF16), w_ref[...]).astype(BF16)


def _transpose_weights(w):
    depth, d, c = w.shape
    tc = _lane_tile(c, 256)
    return _call(
        _transpose_kernel, "weight_transpose", (depth, c // tc),
        [pl.BlockSpec((None, d, tc), lambda l, j: (l, 0, j))], [w],
        pl.BlockSpec((None, tc, d), lambda l, j: (l, j, 0)),
        jax.ShapeDtypeStruct((depth, c, d), BF16), ("parallel", "parallel"))


def _rope_swap_index():
    quarter = ROPE_DIM // 4
    idx = np.arange(ROPE_DIM).reshape(2, 2, quarter)
    return idx[:, ::-1, :].reshape(-1)


def _rope_tables(t):
    pos = jnp.arange(t)
    row = (pos // GRID_W).astype(F32)
    col = (pos % GRID_W).astype(F32)
    n_freq = ROPE_DIM // 4
    inv_freq = jnp.power(ROPE_BASE, -jnp.arange(n_freq, dtype=F32) / n_freq)
    ang_r = row[:, None] * inv_freq
    ang_c = col[:, None] * inv_freq
    cos = jnp.concatenate([jnp.cos(ang_r), jnp.cos(ang_r), jnp.cos(ang_c), jnp.cos(ang_c)], axis=-1)
    sin = jnp.concatenate([-jnp.sin(ang_r), jnp.sin(ang_r), -jnp.sin(ang_c), jnp.sin(ang_c)], axis=-1)
    return cos, sin


def kernel(x_prompt, x_sample, cache_ckv, cache_kpe, state_C, state_n, state_m, c, c_ctx, w_mod, b_mod, norm_w, ffn_w_gu, ffn_w_down, w_in, q_norm_w, kv_norm_w, w_uq, w_ukv, mlstm_gate_b, mlstm_norm_w, pool_w, pool_scale, w_branch, w_out, final_norm_w):
    batch, seq, d = x_prompt.shape
    dec_batch, dec_seq, _ = x_sample.shape
    depth = w_mod.shape[0]
    q_rank, kv_rank = q_norm_w.shape[1], kv_norm_w.shape[1]
    heads = MLA_HEADS
    mw = mlstm_norm_w.shape[1]
    dh = mw // MLSTM_HEADS
    pw = pool_scale.shape[1]
    ffn_h = ffn_w_down.shape[2]
    n_gate = N_DIR * 2 * MLSTM_HEADS
    assert mw == pw == w_branch.shape[2] == heads * V_DIM
    rows = _Rows(batch * seq, seq, dec_batch * dec_seq, dec_seq)
    n = rows.n

    hp = _round_up(ffn_h, 512)
    w_g, w_u = _gate_up_prep(ffn_w_gu, hp)
    wdn = _down_prep(ffn_w_down, hp)

    sizes = (q_rank, kv_rank, ROPE_DIM, mw, mw, mw, mw, n_gate, pw, N_BRANCH * d)
    offs = np.concatenate([[0], np.cumsum(sizes)])
    swap = _rope_swap_index()
    small_cols = q_rank + kv_rank + 2 * ROPE_DIM + LANES
    w_in16 = lax.optimization_barrier(w_in.astype(BF16))
    w_small = jnp.concatenate(
        [w_in16[:, :, :offs[3]], w_in16[:, :, offs[2]:offs[3]][:, :, swap], w_in16[:, :, offs[7]:offs[8]],
         jnp.zeros((depth, d, LANES - n_gate), BF16)], axis=-1)
    gate_blk = (q_rank + kv_rank + 2 * ROPE_DIM) // LANES
    w_big = jnp.concatenate([w_in16[:, :, offs[3]:offs[5]], w_in16[:, :, offs[8]:]], axis=-1)
    w_feat = _transpose_weights(w_in16[:, :, offs[5]:offs[7]])
    gate_col0 = 3 * mw

    wq4 = w_uq.reshape(depth, q_rank, heads, QK_DIM)
    wq_pe = wq4[..., NOPE_DIM:]
    wq = jnp.concatenate([wq4[..., :NOPE_DIM].reshape(depth, q_rank, -1),
                          wq_pe.reshape(depth, q_rank, -1),
                          wq_pe[..., swap].reshape(depth, q_rank, -1)], axis=-1).astype(BF16)
    wkv4 = w_ukv.reshape(depth, kv_rank, heads, NOPE_DIM + V_DIM)
    wkv = jnp.concatenate([wkv4[..., :NOPE_DIM].reshape(depth, kv_rank, -1),
                           wkv4[..., NOPE_DIM:].reshape(depth, kv_rank, -1)], axis=-1).astype(BF16)
    wbr = w_branch.astype(BF16)
    wout = w_out.astype(BF16)
    pwb = pool_w.astype(BF16)
    norm_w4 = norm_w.reshape(depth, 3, 1, d)
    qnw = q_norm_w.reshape(depth, 1, q_rank)
    kvnw = kv_norm_w.reshape(depth, 1, kv_rank)
    mnw = mlstm_norm_w.reshape(depth, mw, 1)
    psc = pool_scale.reshape(depth, 1, pw)
    bands = _pool_bands()
    rope_tabs = _rope_tables(dec_seq)

    cond = jnp.concatenate([c_ctx[None, :], c, jnp.zeros((COND_ROWS - 1 - dec_batch, d), F32)], axis=0)
    mod = _mod_all(cond, w_mod, b_mod).reshape(depth, COND_ROWS, N_MOD, 1, d)

    cache_kv = _cache_kv(cache_ckv, cache_kpe, wkv)

    x = (x_prompt.reshape(rows.n_ctx, d), x_sample.reshape(rows.n_lat, d))
    new_cache = (None, None)
    new_state = (None, None, None)
    for l in range(depth):
        x, h_mix = _ffn(x, mod, norm_w4, w_g, w_u, wdn, rows, l, 0)

        p32 = _inproj(h_mix, w_small, rows, l, F32, small_cols)
        pb = _inproj(h_mix, w_big, rows, l, BF16, _lane_tile(w_big.shape[2], 1024))
        pb_t = _inproj_t(h_mix, w_feat, rows, l)
        gates_t = p32[:, gate_blk * LANES:gate_blk * LANES + n_gate].T

        q_c, k_c, v_c, *new_cache = _mla_prep(p32, qnw, kvnw, wq, wkv, l, 0, batch, seq, None, new_cache)
        q_s, k_s, v_s = _mla_prep(p32, qnw, kvnw, wq, wkv, l, rows.n_ctx, dec_batch, dec_seq, rope_tabs, None)
        y_a = _attention(q_c, k_c, v_c, None, l, n, 0, None)
        y_a = _attention(q_s, k_s, v_s, cache_kv, l, n, rows.n_ctx, y_a)

        hf_c, hb_c, *new_state = _mlstm(pb, pb_t, p32, gates_t, mlstm_gate_b, None, l, 0, batch, seq,
                                        gate_blk, dh, new_state)
        hf_s, hb_s, _, _, _ = _mlstm(pb, pb_t, p32, gates_t, mlstm_gate_b, (state_C, state_n, state_m), l,
                                     rows.n_ctx, dec_batch, dec_seq, gate_blk, dh, None)
        y_b = _mlstm_post(hf_c, hb_c, pb_t, mnw, l, n, 0, dh, None)
        y_b = _mlstm_post(hf_s, hb_s, pb_t, mnw, l, n, rows.n_ctx, dh, y_b)

        y_c = _pool(pb, bands, pwb, psc, l, n, 0, batch, seq, None)
        y_c = _pool(pb, bands, pwb, psc, l, n, rows.n_ctx, dec_batch, dec_seq, y_c)

        merged = _merge(y_a, y_b, y_c, pb, wbr, rows, l, gate_col0)
        x = _outproj(merged, wout, x, mod, rows, l)
        x = _ffn(x, mod, norm_w4, w_g, w_u, wdn, rows, l, 1)

    y_prompt = _final_norm(x, final_norm_w, 0, rows.n_ctx).reshape(batch, seq, d)
    y_sample = _final_norm(x, final_norm_w, rows.n_ctx, rows.n_lat).reshape(dec_batch, dec_seq, d)
    new_c, new_n, new_m = new_state
    return (y_prompt, y_sample, new_cache[0], new_cache[1], new_c,
            new_n.reshape(batch, depth, N_DIR, MLSTM_HEADS, dh),
            new_m.reshape(batch, depth, N_DIR, MLSTM_HEADS))
```

```python
import functools
import math

import numpy as np
import jax
import jax.numpy as jnp
from jax import lax
from jax.experimental import pallas as pl
from jax.experimental.pallas import tpu as pltpu

GRID_W = 64
EPS = 1e-6
N_MOD = 9
MLA_HEADS = 8
NOPE_DIM = 128
ROPE_DIM = 64
V_DIM = 128
QK_DIM = NOPE_DIM + ROPE_DIM
ROPE_BASE = 10000.0
MLSTM_HEADS = 4
N_DIR = 2
CHUNK = 128
POOL_WINDOWS = (2, 4, 8, 16)
POOL_GROUPS = 4
N_BRANCH = 3

LANES = 128
VMEM_LIMIT_MB = 56
COND_ROWS = 8
PACK_ROWS = 16
V_PAD = 2 * V_DIM

F32 = jnp.float32
BF16 = jnp.bfloat16


def _params(sem):
    return pltpu.CompilerParams(dimension_semantics=sem, vmem_limit_bytes=VMEM_LIMIT_MB << 20)


def _call(kernel, name, grid, in_specs, args, out_specs, out_shape, sem, bases=(), scratch=()):
    n_in = len(args)
    extra = [b for b in bases if b is not None]
    aliases = {}
    for k, b in enumerate(bases):
        if b is not None:
            aliases[n_in + len(aliases)] = k

    def body(*refs):
        kernel(*refs[:n_in], *refs[n_in + len(extra):])

    return pl.pallas_call(
        body if extra else kernel, grid=grid,
        in_specs=list(in_specs) + [pl.BlockSpec(memory_space=pl.ANY)] * len(extra),
        out_specs=out_specs, out_shape=out_shape, input_output_aliases=aliases,
        scratch_shapes=list(scratch), compiler_params=_params(sem), name=name)(*args, *extra)


def _round_up(n, m):
    return (n + m - 1) // m * m


def _lane_tile(n, cap):
    t = cap - cap % LANES
    while n % t:
        t -= LANES
    return t


def _sigmoid(x):
    return 1.0 / (1.0 + jnp.exp(-x))


def _log_sigmoid(x):
    return -(jnp.maximum(-x, 0.0) + jnp.log1p(jnp.exp(-jnp.abs(x))))


def _rms(x, w):
    return x * lax.rsqrt(jnp.mean(x * x, axis=-1, keepdims=True) + EPS) * w


def _dot(a, b):
    return jnp.dot(a, b, preferred_element_type=F32)


def _dot_nt(a, b):
    return lax.dot_general(a, b, (((1,), (1,)), ((), ())), preferred_element_type=F32)


def _mod_kernel(c_ref, w_ref, b_ref, o_ref):
    c = c_ref[...]
    a = (c * _sigmoid(c)).astype(BF16)
    o_ref[...] = _dot(a, w_ref[...].astype(BF16)) + b_ref[...]


def _mod_all(cond, w_mod, b_mod):
    depth, d, nd = w_mod.shape
    tn = _lane_tile(nd, 1024)
    return _call(
        _mod_kernel, "adaln_mod", (depth, nd // tn),
        [pl.BlockSpec((COND_ROWS, d), lambda l, j: (0, 0)),
         pl.BlockSpec((None, d, tn), lambda l, j: (l, 0, j)),
         pl.BlockSpec((None, 1, tn), lambda l, j: (l, 0, j))],
        [cond, w_mod, b_mod.reshape(depth, 1, nd)],
        pl.BlockSpec((None, COND_ROWS, tn), lambda l, j: (l, 0, j)),
        jax.ShapeDtypeStruct((depth, COND_ROWS, nd), F32),
        ("parallel", "parallel"))


class _Rows:
    def __init__(self, n_ctx, t_ctx, n_lat, t_lat):
        self.n_ctx, self.t_ctx, self.n_lat, self.t_lat = n_ctx, t_ctx, n_lat, t_lat
        self.n = n_ctx + n_lat

    def mod_row(self, row):
        return jnp.where(row < self.n_ctx, 0, 1 + (row - self.n_ctx) // self.t_lat)

    def tile(self, cap):
        t = min(cap, self.n_ctx, self.t_lat)
        assert self.n_ctx % t == 0 and self.t_lat % t == 0
        return t


def _mod_spec(rows, tm, layer, k, d):
    return pl.BlockSpec((None, None, None, 1, d),
                        lambda i, j: (layer, rows.mod_row(i * tm), k, 0, 0))


def _norm_mod_to(h_ref, x_ref, nw_ref, sh_ref, sc_ref):
    y = _rms(x_ref[...], nw_ref[...])
    h_ref[...] = (y * (1.0 + sc_ref[...]) + sh_ref[...]).astype(h_ref.dtype)


def _ffn_kernel(emit_next, ctx_tiles, *refs):
    x_refs, refs = (refs[:1], refs[1:]) if ctx_tiles is None else (refs[:2], refs[2:])
    sh_ref, sc_ref, g_ref, nw_ref, wg_ref, wu_ref, wd_ref = refs[:7]
    if emit_next:
        sh2_ref, sc2_ref, nw2_ref, o_ref, h2_ref, h_ref = refs[7:]
    else:
        o_ref, h_ref = refs[7:]
    i = pl.program_id(0)
    j = pl.program_id(1)

    def per_source(fn):
        if ctx_tiles is None:
            fn(x_refs[0])
        else:
            pl.when(i < ctx_tiles)(functools.partial(fn, x_refs[0]))
            pl.when(i >= ctx_tiles)(functools.partial(fn, x_refs[1]))

    def prologue(x_ref):
        _norm_mod_to(h_ref, x_ref, nw_ref, sh_ref, sc_ref)

    def epilogue(x_ref):
        o_ref[...] = x_ref[...] + 0.5 * g_ref[...] * o_ref[...]
        if emit_next:
            _norm_mod_to(h2_ref, o_ref, nw2_ref, sh2_ref, sc2_ref)

    @pl.when(j == 0)
    def _():
        per_source(prologue)
        o_ref[...] = jnp.zeros_like(o_ref)

    h = h_ref[...]
    g = _dot(h, wg_ref[...])
    u = _dot(h, wu_ref[...])
    a = (g * _sigmoid(g) * u).astype(BF16)
    o_ref[...] += _dot(a, wd_ref[...])

    @pl.when(j == pl.num_programs(1) - 1)
    def _():
        per_source(epilogue)


def _ffn(x, mod, norm_w, w_g, w_u, w_down, rows, layer, which):
    tm = rows.tile(512)
    if isinstance(x, tuple):
        ctx_tiles = rows.n_ctx // tm
        x_args = list(x)
        x_specs = [pl.BlockSpec((tm, x[0].shape[1]), lambda i, j: (jnp.minimum(i, ctx_tiles - 1), 0)),
                   pl.BlockSpec((tm, x[0].shape[1]), lambda i, j: (jnp.maximum(i - ctx_tiles, 0), 0))]
    else:
        ctx_tiles = None
        x_args = [x]
        x_specs = [pl.BlockSpec((tm, x.shape[1]), lambda i, j: (i, 0))]
    n, d = rows.n, x_args[0].shape[1]
    hp = w_down.shape[2]
    th = _lane_tile(hp, 512)
    k0 = 0 if which == 0 else 6
    emit_next = which == 0
    w_spec = pl.BlockSpec((None, None, d, th), lambda i, j: (layer, which, 0, j))
    row_spec = pl.BlockSpec((tm, d), lambda i, j: (i, 0))
    norm_spec = lambda k: pl.BlockSpec((None, None, 1, d), lambda i, j: (layer, k, 0, 0))
    in_specs = x_specs + [
        _mod_spec(rows, tm, layer, k0, d),
        _mod_spec(rows, tm, layer, k0 + 1, d),
        _mod_spec(rows, tm, layer, k0 + 2, d),
        norm_spec(2 * which), w_spec, w_spec,
        pl.BlockSpec((None, None, th, d), lambda i, j: (layer, which, j, 0))]
    args = x_args + [mod, mod, mod, norm_w, w_g, w_u, w_down]
    out_specs, out_shape = row_spec, jax.ShapeDtypeStruct((n, d), F32)
    if emit_next:
        in_specs += [_mod_spec(rows, tm, layer, 3, d), _mod_spec(rows, tm, layer, 4, d), norm_spec(1)]
        args += [mod, mod, norm_w]
        out_specs, out_shape = [row_spec, row_spec], [out_shape, jax.ShapeDtypeStruct((n, d), BF16)]
    return _call(
        functools.partial(_ffn_kernel, emit_next, ctx_tiles), "ffn_half_step", (n // tm, hp // th),
        in_specs, args, out_specs, out_shape, ("parallel", "arbitrary"),
        scratch=[pltpu.VMEM((tm, d), BF16)])


def _inproj_kernel(h_ref, w_ref, o_ref):
    o_ref[...] = _dot(h_ref[...], w_ref[...]).astype(o_ref.dtype)


def _inproj(h, w, rows, layer, out_dtype, tn):
    n, d = h.shape
    cols = w.shape[2]
    tm = rows.tile(1024)
    return _call(
        _inproj_kernel, "mixer_in_proj", (n // tm, cols // tn),
        [pl.BlockSpec((tm, d), lambda i, j: (i, 0)),
         pl.BlockSpec((None, d, tn), lambda i, j: (layer, 0, j))],
        [h, w],
        pl.BlockSpec((tm, tn), lambda i, j: (i, j)),
        jax.ShapeDtypeStruct((n, cols), out_dtype),
        ("parallel", "arbitrary"))


def _inproj_t_kernel(h_ref, wt_ref, o_ref):
    o_ref[...] = _dot_nt(wt_ref[...], h_ref[...]).astype(o_ref.dtype)


def _inproj_t(h, w_t, rows, layer):
    n, d = h.shape
    cols = w_t.shape[1]
    tm = rows.tile(1024)
    tn = _lane_tile(cols, 1024)
    return _call(
        _inproj_t_kernel, "mixer_in_proj_t", (n // tm, cols // tn),
        [pl.BlockSpec((tm, d), lambda i, j: (i, 0)),
         pl.BlockSpec((None, tn, d), lambda i, j: (layer, j, 0))],
        [h, w_t],
        pl.BlockSpec((tn, tm), lambda i, j: (j, i)),
        jax.ShapeDtypeStruct((cols, n), BF16),
        ("parallel", "arbitrary"))


def _ones_column(rows):
    lane = lax.broadcasted_iota(jnp.int32, (rows, V_PAD - V_DIM), 1)
    return (lane == 0).astype(BF16)


def _mla_prep_kernel(rope, q_rank, kv_rank, p_ref, qnw_ref, kvnw_ref, wq_ref, wkv_ref, *rest):
    if rope:
        cos_ref, sin_ref, q_ref, k_ref, v_ref = rest
    else:
        q_ref, k_ref, v_ref, ckv_ref, kpe_ref = rest
    heads = MLA_HEADS
    p = p_ref[...]
    c_q = p[:, :q_rank]
    c_kv = p[:, q_rank:q_rank + kv_rank]
    o = q_rank + kv_rank
    k_pe = p[:, o:o + ROPE_DIM]
    k_pe_sw = p[:, o + ROPE_DIM:o + 2 * ROPE_DIM]

    qa = _dot(_rms(c_q, qnw_ref[...]).astype(BF16), wq_ref[...])
    ckv_n = _rms(c_kv, kvnw_ref[...])
    kv = _dot(ckv_n.astype(BF16), wkv_ref[...])
    scale = QK_DIM ** -0.5 * math.log2(math.e)
    if rope:
        cos = cos_ref[...]
        sin = sin_ref[...]
        k_pe = k_pe * cos + k_pe_sw * sin
    else:
        ckv_ref[...] = ckv_n
        kpe_ref[...] = k_pe
    ones = _ones_column(p.shape[0])
    pe0 = heads * NOPE_DIM
    sw0 = pe0 + heads * ROPE_DIM
    for h in range(heads):
        q_pe = qa[:, pe0 + h * ROPE_DIM:pe0 + (h + 1) * ROPE_DIM]
        if rope:
            q_pe = q_pe * cos + qa[:, sw0 + h * ROPE_DIM:sw0 + (h + 1) * ROPE_DIM] * sin
        q_ref[h, :, :NOPE_DIM] = (qa[:, h * NOPE_DIM:(h + 1) * NOPE_DIM] * scale).astype(BF16)
        q_ref[h, :, NOPE_DIM:] = (q_pe * scale).astype(BF16)
        k_ref[h, :, :NOPE_DIM] = kv[:, h * NOPE_DIM:(h + 1) * NOPE_DIM].astype(BF16)
        k_ref[h, :, NOPE_DIM:] = k_pe.astype(BF16)
        v0 = heads * NOPE_DIM + h * V_DIM
        v_ref[h, :, :V_DIM] = kv[:, v0:v0 + V_DIM].astype(BF16)
        v_ref[h, :, V_DIM:] = ones


def _mla_prep(p32, q_norm_w, kv_norm_w, wq, wkv, layer, row0, batch, t, rope_tabs, bases):
    ws = p32.shape[1]
    q_rank, kv_rank = q_norm_w.shape[-1], kv_norm_w.shape[-1]
    depth = q_norm_w.shape[0]
    tm = min(256, t)
    nt = t // tm
    heads = MLA_HEADS
    rope = rope_tabs is not None
    in_specs = [pl.BlockSpec((tm, ws), lambda b, i: (row0 // tm + b * nt + i, 0)),
                pl.BlockSpec((None, 1, q_rank), lambda b, i: (layer, 0, 0)),
                pl.BlockSpec((None, 1, kv_rank), lambda b, i: (layer, 0, 0)),
                pl.BlockSpec((None,) + wq.shape[1:], lambda b, i: (layer, 0, 0)),
                pl.BlockSpec((None,) + wkv.shape[1:], lambda b, i: (layer, 0, 0))]
    args = [p32, q_norm_w, kv_norm_w, wq, wkv]
    head_spec = lambda width: pl.BlockSpec((None, heads, tm, width), lambda b, i: (b, 0, i, 0))
    out_specs = [head_spec(QK_DIM), head_spec(QK_DIM), head_spec(V_PAD)]
    out_shape = [jax.ShapeDtypeStruct((batch, heads, t, QK_DIM), BF16),
                 jax.ShapeDtypeStruct((batch, heads, t, QK_DIM), BF16),
                 jax.ShapeDtypeStruct((batch, heads, t, V_PAD), BF16)]
    if rope:
        in_specs += [pl.BlockSpec((tm, ROPE_DIM), lambda b, i: (i, 0))] * 2
        args += list(rope_tabs)
        all_bases = ()
    else:
        out_specs += [pl.BlockSpec((None, None, tm, kv_rank), lambda b, i: (b, layer, i, 0)),
                      pl.BlockSpec((None, None, tm, ROPE_DIM), lambda b, i: (b, layer, i, 0))]
        out_shape += [jax.ShapeDtypeStruct((batch, depth, t, kv_rank), F32),
                      jax.ShapeDtypeStruct((batch, depth, t, ROPE_DIM), F32)]
        all_bases = (None, None, None) + tuple(bases)
    return _call(functools.partial(_mla_prep_kernel, rope, q_rank, kv_rank), "mla_prep",
                 (batch, nt), in_specs, args, out_specs, out_shape, ("parallel", "parallel"),
                 bases=all_bases)


def _cache_kv_kernel(ckv_ref, kpe_ref, wkv_ref, k_ref, v_ref):
    heads = MLA_HEADS
    kv = _dot(ckv_ref[...].astype(BF16), wkv_ref[...])
    k_pe = kpe_ref[...].astype(BF16)
    ones = _ones_column(kv.shape[0])
    for h in range(heads):
        k_ref[h, :, :NOPE_DIM] = kv[:, h * NOPE_DIM:(h + 1) * NOPE_DIM].astype(BF16)
        k_ref[h, :, NOPE_DIM:] = k_pe
        v0 = heads * NOPE_DIM + h * V_DIM
        v_ref[h, :, :V_DIM] = kv[:, v0:v0 + V_DIM].astype(BF16)
        v_ref[h, :, V_DIM:] = ones


def _cache_kv(cache_ckv, cache_kpe, wkv):
    batch, depth, past, kv_rank = cache_ckv.shape
    heads = MLA_HEADS
    return _call(
        _cache_kv_kernel, "mla_cache_kv", (batch, depth),
        [pl.BlockSpec((None, None, past, kv_rank), lambda b, l: (b, l, 0, 0)),
         pl.BlockSpec((None, None, past, ROPE_DIM), lambda b, l: (b, l, 0, 0)),
         pl.BlockSpec((None,) + wkv.shape[1:], lambda b, l: (l, 0, 0))],
        [cache_ckv, cache_kpe, wkv],
        [pl.BlockSpec((None, None, heads, past, QK_DIM), lambda b, l: (b, l, 0, 0, 0)),
         pl.BlockSpec((None, None, heads, past, V_PAD), lambda b, l: (b, l, 0, 0, 0))],
        [jax.ShapeDtypeStruct((batch, depth, heads, past, QK_DIM), BF16),
         jax.ShapeDtypeStruct((batch, depth, heads, past, V_PAD), BF16)],
        ("parallel", "parallel"))


ATTN_KEY_CHUNK = 512


def _attn_kernel(past, q_ref, qn_ref, k_ref, kn_ref, v_ref, *rest):
    if past:
        kc_ref, kcn_ref, vc_ref = rest[:3]
        rest = rest[3:]
    else:
        kc_ref = kcn_ref = vc_ref = None
    o_ref, s0_ref, s1_ref, m0_ref, m1_ref = rest
    tq = qn_ref.shape[0]
    t = k_ref.shape[0]
    chunk = min(ATTN_KEY_CHUNK, t)

    def scores(q, keys_ref, cache_keys_ref, s_ref, m_ref):
        s = _dot_nt(q, keys_ref[...])
        m = jnp.max(s, axis=-1, keepdims=True)
        if past:
            sc = _dot_nt(q, cache_keys_ref[...])
            m = jnp.maximum(m, jnp.max(sc, axis=-1, keepdims=True))
            s_ref[:, :past] = sc
        s_ref[:, past:] = s
        m_ref[...] = m

    def values(r, s_ref, m_ref):
        m = m_ref[...]
        acc = None
        if past:
            acc = _dot(jnp.exp2(s_ref[:, :past] - m).astype(BF16), vc_ref[...])
        for c in range(0, t, chunk):
            p = jnp.exp2(s_ref[:, past + c:past + c + chunk] - m).astype(BF16)
            d = _dot(p, v_ref[c:c + chunk, :])
            acc = d if acc is None else acc + d
        o_ref[r * tq:(r + 1) * tq, :] = (acc[:, :V_DIM] / acc[:, V_DIM:V_DIM + 1]).astype(o_ref.dtype)

    @pl.when(pl.program_id(0) == 0)
    def _():
        scores(q_ref[:tq, :], k_ref, kc_ref, s0_ref, m0_ref)

    values(0, s0_ref, m0_ref)
    scores(q_ref[tq:, :], k_ref, kc_ref, s1_ref, m1_ref)
    values(1, s1_ref, m1_ref)
    scores(qn_ref[...], kn_ref, kcn_ref, s0_ref, m0_ref)


def _attn_short_kernel(q_ref, k_ref, v_ref, o_ref):
    heads = range(q_ref.shape[0])
    s = [_dot_nt(q_ref[h], k_ref[h]) for h in heads]
    p = [jnp.exp2(x - jnp.max(x, axis=-1, keepdims=True)).astype(BF16) for x in s]
    o = [_dot(p[h], v_ref[h]) for h in heads]
    for h in heads:
        o_ref[:, h * V_DIM:(h + 1) * V_DIM] = (o[h][:, :V_DIM] / o[h][:, V_DIM:V_DIM + 1]).astype(o_ref.dtype)


ATTN_SHORT_SEQ = 256


def _attention(q, k, v, cache, layer, n_rows, row0, base):
    batch, heads, t, _ = q.shape
    if cache is None and t <= ATTN_SHORT_SEQ:
        assert row0 % t == 0
        head_block = lambda width: pl.BlockSpec((None, heads, t, width), lambda b: (b, 0, 0, 0))
        return _call(
            _attn_short_kernel, "mla_attention_short", (batch,),
            [head_block(QK_DIM), head_block(QK_DIM), head_block(V_PAD)], [q, k, v],
            pl.BlockSpec((t, heads * V_DIM), lambda b: (row0 // t + b, 0)),
            jax.ShapeDtypeStruct((n_rows, heads * V_DIM), BF16), ("parallel",), bases=(base,))
    tq = min(512, t // 2)
    pair = 2 * tq
    npair = t // pair
    n_steps = batch * heads * npair
    past = cache[0].shape[3] if cache is not None else 0
    assert t % pair == 0 and row0 % pair == 0

    def where(tile):
        tile = jnp.minimum(tile, 2 * n_steps - 1)
        p = tile // 2
        return p // (heads * npair), (p // npair) % heads, 2 * (p % npair) + tile % 2

    def pair_map(g):
        b, h, i = where(2 * g)
        return b, h, i // 2, 0

    def next_map(g):
        b, h, i = where(2 * g + 2)
        return b, h, i, 0

    def kv_map(shift):
        def index(g):
            b, h, _ = where(2 * g + shift)
            return b, h, 0, 0
        return index

    def cache_map(shift):
        def index(g):
            b, h, _ = where(2 * g + shift)
            return b, layer, h, 0, 0
        return index

    def o_map(g):
        b, h, i = where(2 * g)
        return row0 // pair + b * npair + i // 2, h

    in_specs = [pl.BlockSpec((None, None, pair, QK_DIM), pair_map),
                pl.BlockSpec((None, None, tq, QK_DIM), next_map),
                pl.BlockSpec((None, None, t, QK_DIM), kv_map(0)),
                pl.BlockSpec((None, None, t, QK_DIM), kv_map(2)),
                pl.BlockSpec((None, None, t, V_PAD), kv_map(0))]
    args = [q, q, k, k, v]
    if past:
        in_specs += [pl.BlockSpec((None, None, None, past, QK_DIM), cache_map(0)),
                     pl.BlockSpec((None, None, None, past, QK_DIM), cache_map(2)),
                     pl.BlockSpec((None, None, None, past, V_PAD), cache_map(0))]
        args += [cache[0], cache[0], cache[1]]
    s_total = past + t
    return _call(
        functools.partial(_attn_kernel, past), "mla_attention", (n_steps,),
        in_specs, args,
        pl.BlockSpec((pair, V_DIM), o_map),
        jax.ShapeDtypeStruct((n_rows, heads * V_DIM), BF16),
        ("arbitrary",), bases=(base,),
        scratch=[pltpu.VMEM((tq, s_total), F32), pltpu.VMEM((tq, s_total), F32),
                 pltpu.VMEM((tq, 1), F32), pltpu.VMEM((tq, 1), F32)])


MLSTM_CHUNKS_PER_STEP = 2


def _split3(x):
    hi = x.astype(BF16)
    r = x - hi.astype(F32)
    mid = r.astype(BF16)
    return hi, mid, (r - mid.astype(F32)).astype(BF16)


def _mlstm_kernel(has_init, dh, n_sub, *refs):
    (qf_ref, kf_ref, vf_ref, qb_ref, kb_ref, vb_ref, gf_ref, gb_ref, gtf_ref, gtb_ref,
     brow_ref, bcol_ref) = refs[:12]
    refs = refs[12:]
    if has_init:
        c0_ref, n0_ref, m0_ref = refs[:3]
        refs = refs[3:]
    hf_ref, hb_ref, c_ref, n_ref, m_ref = refs
    heads = MLSTM_HEADS
    n_gate = N_DIR * 2 * heads
    step = pl.program_id(1)

    @pl.when(step == 0)
    def _():
        if has_init:
            c_ref[...] = c0_ref[...]
            n_ref[...] = n0_ref[...]
            m_ref[...] = m0_ref[...]
        else:
            c_ref[...] = jnp.zeros_like(c_ref)
            n_ref[...] = jnp.zeros_like(n_ref)
            m_ref[...] = jnp.zeros_like(m_ref)

    tok0 = lax.broadcasted_iota(jnp.int32, (CHUNK, CHUNK), 0)
    tok1 = lax.broadcasted_iota(jnp.int32, (CHUNK, CHUNK), 1)
    k_scale = dh ** -0.5
    m_all = m_ref[...]
    m_out = m_all
    unit_lane = lax.broadcasted_iota(jnp.int32, m_all.shape, 1)
    units = [(d, h) for d in range(N_DIR) for h in range(heads)]
    for sub in range(n_sub):
        m_all = m_out
        gate = {}
        for d in range(N_DIR):
            g_ref, gt_ref = (gf_ref, gtf_ref) if d == 0 else (gb_ref, gtb_ref)
            chunk_idx = sub if d == 0 else n_sub - 1 - sub
            tok = slice(chunk_idx * CHUNK, (chunk_idx + 1) * CHUNK)
            seen_t = (tok0 <= tok1) if d == 0 else (tok0 >= tok1)
            seen_t_bf = seen_t.astype(BF16)
            seen_bf = ((tok1 <= tok0) if d == 0 else (tok1 >= tok0)).astype(BF16)
            pre_col = g_ref[tok, :n_gate] + brow_ref[...]
            pre_row = gt_ref[:, tok] + bcol_ref[...]
            cum_col = sum(_dot(seen_bf, part) for part in _split3(_log_sigmoid(pre_col)))
            cum_row = sum(_dot(part, seen_t_bf) for part in _split3(_log_sigmoid(pre_row)))
            gate[d] = (tok, seen_t, pre_col, pre_row, cum_col, cum_row)

        st = {}
        for d, h in units:
            tok, seen_t, pre_col, pre_row, cum_col, cum_row = gate[d]
            q_ref, k_ref, vt_ref = (qf_ref, kf_ref, vf_ref) if d == 0 else (qb_ref, kb_ref, vb_ref)
            ci = d * 2 * heads + h
            cf = ci + heads
            sid = d * heads + h
            sl = slice(h * dh, (h + 1) * dh)
            last = CHUNK - 1 if d == 0 else 0
            c_col = pre_col[:, ci:ci + 1] - cum_col[:, cf:cf + 1]
            i_row = pre_row[ci:ci + 1, :]
            b_row = cum_row[cf:cf + 1, :]
            b_end = b_row[:, last:last + 1]
            m_prev = m_all[:, sid:sid + 1]
            a_row = b_row + m_prev
            dmat = jnp.where(seen_t, b_row + c_col, -jnp.inf)
            m_t = jnp.maximum(a_row, jnp.max(dmat, axis=0, keepdims=True))
            q = q_ref[tok, sl]
            k_bf = (k_ref[tok, sl].astype(F32) * k_scale).astype(BF16)
            st[d, h] = dict(tok=tok, sl=sl, sid=sid, i_row=i_row, b_row=b_row, b_end=b_end, m_prev=m_prev,
                            m_t=m_t, w_intra=jnp.exp(dmat - m_t), w_inter=jnp.exp(a_row - m_t),
                            q=q, k_bf=k_bf, v_t=vt_ref[sl, tok], c_prev=c_ref[d, h], n_prev=n_ref[d, h])

        for u in units:
            x = st[u]
            x["s_t"] = _dot_nt(x["k_bf"], x["q"]) * x["w_intra"]
            n_rows = jnp.broadcast_to(x["n_prev"], (PACK_ROWS, dh)).astype(BF16)
            x["cq"] = _dot_nt(jnp.concatenate([x["c_prev"].astype(BF16), n_rows], axis=0), x["q"])

        for d, h in units:
            x = st[d, h]
            h_ref = hf_ref if d == 0 else hb_ref
            num = x["w_inter"] * x["cq"][:dh, :] + _dot(x["v_t"], x["s_t"].astype(BF16))
            den = x["w_inter"] * x["cq"][dh:dh + 1, :] + jnp.sum(x["s_t"], axis=0, keepdims=True)
            h_ref[x["sl"], x["tok"]] = num / jnp.maximum(jnp.abs(den), jnp.exp(-x["m_t"]))

        for d, h in units:
            x = st[d, h]
            g_row = x["b_end"] - x["b_row"] + x["i_row"]
            m_new = jnp.maximum(x["b_end"] + x["m_prev"], jnp.max(g_row, axis=1, keepdims=True))
            w_pos = jnp.exp(g_row - m_new)
            w_carry = jnp.exp(x["b_end"] + x["m_prev"] - m_new)
            w_rows = jnp.broadcast_to(w_pos, (PACK_ROWS, CHUNK)).astype(BF16)
            upd = _dot(jnp.concatenate([(x["v_t"].astype(F32) * w_pos).astype(BF16), w_rows], axis=0), x["k_bf"])
            c_ref[d, h] = w_carry * x["c_prev"] + upd[:dh, :]
            n_ref[d, h] = w_carry * x["n_prev"] + upd[dh:dh + 1, :]
            m_out = jnp.where(unit_lane == x["sid"], m_new, m_out)
    m_ref[...] = m_out


def _mlstm(pb, pb_t, p32, gates_t, gate_b, state, layer, row0, batch, t, gate_blk, dh, bases):
    heads = MLSTM_HEADS
    n_gate = N_DIR * 2 * heads
    depth = gate_b.shape[0]
    n_sub = MLSTM_CHUNKS_PER_STEP if (t // CHUNK) % MLSTM_CHUNKS_PER_STEP == 0 else 1
    span = n_sub * CHUNK
    nc = t // span
    assert row0 % span == 0
    blk0 = row0 // span
    w = heads * dh
    fwd = lambda b, c: blk0 + b * nc + c
    bwd = lambda b, c: blk0 + b * nc + nc - 1 - c

    def tok(col, blk):
        return pl.BlockSpec((span, w), lambda b, c: (blk(b, c), col))

    def feat(blk):
        return pl.BlockSpec((w, span), lambda b, c: (0, blk(b, c)))

    in_specs = [tok(0, fwd), tok(1, fwd), feat(fwd), tok(0, bwd), tok(1, bwd), feat(bwd),
                pl.BlockSpec((span, LANES), lambda b, c: (fwd(b, c), gate_blk)),
                pl.BlockSpec((span, LANES), lambda b, c: (bwd(b, c), gate_blk)),
                pl.BlockSpec((n_gate, span), lambda b, c: (0, fwd(b, c))),
                pl.BlockSpec((n_gate, span), lambda b, c: (0, bwd(b, c))),
                pl.BlockSpec((None, 1, n_gate), lambda b, c: (layer, 0, 0)),
                pl.BlockSpec((None, n_gate, 1), lambda b, c: (layer, 0, 0))]
    args = [pb, pb, pb_t, pb, pb, pb_t, p32, p32, gates_t, gates_t,
            gate_b.reshape(-1, 1, n_gate), gate_b.reshape(-1, n_gate, 1)]
    has_init = state is not None
    state_shapes = [(N_DIR, heads, dh, dh), (N_DIR, heads, 1, dh), (1, N_DIR * heads)]
    if has_init:
        c0, n0, m0 = state
        in_specs += [pl.BlockSpec((None, None) + shp, lambda b, c, z=(0,) * len(shp): (b, layer) + z)
                     for shp in state_shapes]
        args += [c0, n0.reshape((batch, depth) + state_shapes[1]), m0.reshape((batch, depth) + state_shapes[2])]
        st_specs = [pl.BlockSpec((None,) + shp, lambda b, c, z=(0,) * len(shp): (b,) + z) for shp in state_shapes]
        st_shapes = [jax.ShapeDtypeStruct((batch,) + shp, F32) for shp in state_shapes]
        all_bases = ()
    else:
        st_specs = [pl.BlockSpec((None, None) + shp, lambda b, c, z=(0,) * len(shp): (b, layer) + z)
                    for shp in state_shapes]
        st_shapes = [jax.ShapeDtypeStruct((batch, depth) + shp, F32) for shp in state_shapes]
        all_bases = (None, None) + tuple(bases)
    return _call(
        functools.partial(_mlstm_kernel, has_init, dh, n_sub), "mlstm_scan", (batch, nc), in_specs, args,
        [pl.BlockSpec((w, span), lambda b, c: (0, b * nc + c)),
         pl.BlockSpec((w, span), lambda b, c: (0, b * nc + nc - 1 - c))] + st_specs,
        [jax.ShapeDtypeStruct((w, batch * t), F32),
         jax.ShapeDtypeStruct((w, batch * t), F32)] + st_shapes,
        ("parallel", "arbitrary"), bases=all_bases)


def _mlstm_post_kernel(dh, hf_ref, hb_ref, o_ref, w_ref, y_ref):
    tm = hf_ref.shape[1]
    hm = hf_ref[...] + hb_ref[...]
    gate = _sigmoid(o_ref[...].astype(F32))
    w = w_ref[...]
    eye = (lax.broadcasted_iota(jnp.int32, (tm, tm), 0)
           == lax.broadcasted_iota(jnp.int32, (tm, tm), 1)).astype(BF16)
    for h in range(MLSTM_HEADS):
        sl = slice(h * dh, (h + 1) * dh)
        x = hm[sl, :]
        y = x * lax.rsqrt(jnp.mean(x * x, axis=0, keepdims=True) + EPS) * w[sl, :]
        y_t = (gate[sl, :] * y).astype(BF16)
        y_ref[:, sl] = _dot_nt(eye, y_t).astype(y_ref.dtype)


def _mlstm_post(h_f, h_b, pb_t, m_norm_w, layer, n_rows, row0, dh, base):
    w, n = h_f.shape
    tm = min(256, n)
    return _call(
        functools.partial(_mlstm_post_kernel, dh), "mlstm_post", (n // tm,),
        [pl.BlockSpec((w, tm), lambda i: (0, i)),
         pl.BlockSpec((w, tm), lambda i: (0, i)),
         pl.BlockSpec((w, tm), lambda i: (1, row0 // tm + i)),
         pl.BlockSpec((None, w, 1), lambda i: (layer, 0, 0))],
        [h_f, h_b, pb_t, m_norm_w],
        pl.BlockSpec((tm, w), lambda i: (row0 // tm + i, 0)),
        jax.ShapeDtypeStruct((n_rows, w), BF16),
        ("parallel",), bases=(base,))


POOL_TILE = 256


def _pool_bands():
    t = np.arange(POOL_TILE)[:, None]
    bands = np.zeros((POOL_GROUPS, 3, POOL_TILE, POOL_TILE), np.float32)
    for g, win in enumerate(POOL_WINDOWS):
        for part in range(3):
            s = np.arange(POOL_TILE)[None, :] + (part - 1) * POOL_TILE
            bands[g, part] = (s >= t - win // 2) & (s < t - win // 2 + win)
    return jnp.asarray(bands, BF16)


def _pool_kernel(t_seq, gd, up_ref, um_ref, un_ref, band_ref, pw_ref, ps_ref, y_ref):
    j = pl.program_id(1)
    has_prev = (j > 0).astype(F32)
    has_next = (j < pl.num_programs(1) - 1).astype(F32)
    tile = um_ref.shape[0]
    pos = j * tile + lax.broadcasted_iota(jnp.int32, (tile, 1), 0)
    groups = range(len(POOL_WINDOWS))
    cols = [slice(g * gd, (g + 1) * gd) for g in groups]
    acc = [_dot(band_ref[g, 1], um_ref[:, cols[g]])
           + has_prev * _dot(band_ref[g, 0], up_ref[:, cols[g]])
           + has_next * _dot(band_ref[g, 2], un_ref[:, cols[g]]) for g in groups]
    pooled = []
    for g, win in enumerate(POOL_WINDOWS):
        lo = jnp.clip(pos - win // 2, 0, t_seq)
        hi = jnp.clip(pos - win // 2 + win, 0, t_seq)
        pooled.append((acc[g] / (hi - lo).astype(F32) - um_ref[:, cols[g]].astype(F32)).astype(BF16))
    y = [_dot(pooled[g], pw_ref[g]) * ps_ref[:, cols[g]] for g in groups]
    for g in groups:
        y_ref[:, cols[g]] = y[g].astype(y_ref.dtype)


def _pool(pb, bands, pool_w, pool_scale, layer, n_rows, row0, batch, t, base):
    gd = pool_w.shape[-1]
    w = POOL_GROUPS * gd
    tile = POOL_TILE
    assert t % tile == 0
    nt = t // tile
    blk0 = row0 // tile

    def u_spec(shift):
        return pl.BlockSpec((tile, w), lambda b, j: (blk0 + b * nt + jnp.clip(j + shift, 0, nt - 1), 2))

    return _call(
        functools.partial(_pool_kernel, t, gd), "multiscale_pool", (batch, nt),
        [u_spec(-1), u_spec(0), u_spec(1),
         pl.BlockSpec(bands.shape, lambda b, j: (0, 0, 0, 0)),
         pl.BlockSpec((None, POOL_GROUPS, gd, gd), lambda b, j: (layer, 0, 0, 0)),
         pl.BlockSpec((None, 1, w), lambda b, j: (layer, 0, 0))],
        [pb, pb, pb, bands, pool_w, pool_scale],
        pl.BlockSpec((tile, w), lambda b, j: (blk0 + b * nt + j, 0)),
        jax.ShapeDtypeStruct((n_rows, w), BF16),
        ("parallel", "parallel"), bases=(base,))


def _merge_kernel(ya_ref, yb_ref, yc_ref, ga_ref, gb_ref, gc_ref, w_ref, o_ref):
    branch = [_dot(y_ref[...], w_ref[k]) for k, y_ref in enumerate((ya_ref, yb_ref, yc_ref))]
    gates = [_sigmoid(g_ref[...].astype(F32)) for g_ref in (ga_ref, gb_ref, gc_ref)]
    o_ref[...] = (gates[0] * branch[0] + gates[1] * branch[1] + gates[2] * branch[2]).astype(o_ref.dtype)


def _merge(y_a, y_b, y_c, pb, w_branch, rows, layer, gate_col0):
    n, bw = y_a.shape
    d = w_branch.shape[-1]
    tm = rows.tile(1024)
    tn = min(1024, d)
    g0 = gate_col0 // tn
    nd = d // tn

    def gate_spec(k):
        return pl.BlockSpec((tm, tn), lambda i, j: (i, g0 + k * nd + j))

    y_spec = pl.BlockSpec((tm, bw), lambda i, j: (i, 0))
    return _call(
        _merge_kernel, "branch_merge", (n // tm, nd),
        [y_spec, y_spec, y_spec, gate_spec(0), gate_spec(1), gate_spec(2),
         pl.BlockSpec((None, N_BRANCH, bw, tn), lambda i, j: (layer, 0, 0, j))],
        [y_a, y_b, y_c, pb, pb, pb, w_branch],
        pl.BlockSpec((tm, tn), lambda i, j: (i, j)),
        jax.ShapeDtypeStruct((n, d), BF16),
        ("parallel", "arbitrary"))


def _outproj_kernel(m_ref, w_ref, x_ref, g_ref, o_ref):
    o_ref[...] = x_ref[...] + g_ref[...] * _dot(m_ref[...], w_ref[...])


def _outproj(merged, w_out, x, mod, rows, layer):
    n, d = x.shape
    tm = rows.tile(1024)
    tn = min(1024, d)
    return _call(
        _outproj_kernel, "mixer_out_proj", (n // tm, d // tn),
        [pl.BlockSpec((tm, d), lambda i, j: (i, 0)),
         pl.BlockSpec((None, d, tn), lambda i, j: (layer, 0, j)),
         pl.BlockSpec((tm, tn), lambda i, j: (i, j)),
         pl.BlockSpec((None, None, None, 1, tn),
                      lambda i, j: (layer, rows.mod_row(i * tm), 5, 0, j))],
        [merged, w_out, x, mod],
        pl.BlockSpec((tm, tn), lambda i, j: (i, j)),
        jax.ShapeDtypeStruct((n, d), F32),
        ("parallel", "arbitrary"))


def _final_norm_kernel(x_ref, w_ref, o_ref):
    o_ref[...] = _rms(x_ref[...], w_ref[...])


def _final_norm(x, w, row0, n_rows):
    d = x.shape[1]
    tm = min(512, n_rows)
    return _call(
        _final_norm_kernel, "final_norm", (n_rows // tm,),
        [pl.BlockSpec((tm, d), lambda i: (row0 // tm + i, 0)),
         pl.BlockSpec((1, d), lambda i: (0, 0))],
        [x, w.reshape(1, d)],
        pl.BlockSpec((tm, d), lambda i: (i, 0)),
        jax.ShapeDtypeStruct((n_rows, d), F32),
        ("parallel",))


def _gate_up_prep_kernel(valid, g_ref, u_ref, og_ref, ou_ref):
    for src, dst in ((g_ref, og_ref), (u_ref, ou_ref)):
        dst[:, :valid] = src[...].astype(BF16)
        if dst.shape[1] > valid:
            dst[:, valid:] = jnp.zeros((dst.shape[0], dst.shape[1] - valid), BF16)


def _gate_up_prep(w_gu, hp):
    depth, n_ffn, d, h2 = w_gu.shape
    h = h2 // 2
    assert h % LANES == 0
    n_rows = depth * n_ffn * d
    tr = 256
    flat = w_gu.reshape(n_rows, h2)
    out = jax.ShapeDtypeStruct((n_rows, hp), BF16)
    w_g, w_u = _call(
        functools.partial(_gate_up_prep_kernel, h), "ffn_gate_up_prep", (n_rows // tr,),
        [pl.BlockSpec((tr, h), lambda r: (r, 0)), pl.BlockSpec((tr, h), lambda r: (r, 1))],
        [flat, flat],
        [pl.BlockSpec((tr, hp), lambda r: (r, 0))] * 2, [out, out], ("parallel",))
    return w_g.reshape(depth, n_ffn, d, hp), w_u.reshape(depth, n_ffn, d, hp)


def _down_prep_kernel(valid, w_ref, o_ref):
    o_ref[:valid, :] = w_ref[...].astype(BF16)
    if o_ref.shape[0] > valid:
        o_ref[valid:, :] = jnp.zeros((o_ref.shape[0] - valid, o_ref.shape[1]), BF16)


def _down_prep(w_down, hp):
    depth, n_ffn, h, d = w_down.shape
    assert h % PACK_ROWS == 0
    td = _lane_tile(d, 256)
    flat = w_down.reshape(depth * n_ffn, h, d)
    out = _call(
        functools.partial(_down_prep_kernel, h), "ffn_down_prep", (depth * n_ffn, d // td),
        [pl.BlockSpec((None, h, td), lambda a, j: (a, 0, j))], [flat],
        pl.BlockSpec((None, hp, td), lambda a, j: (a, 0, j)),
        jax.ShapeDtypeStruct((depth * n_ffn, hp, d), BF16), ("parallel", "parallel"))
    return out.reshape(depth, n_ffn, hp, d)


def _transpose_kernel(w_ref, o_ref):
    n = o_ref.shape[0]
    eye = (lax.broadcasted_iota(jnp.int32, (n, n), 0) == lax.broadcasted_iota(jnp.int32, (n, n), 1))
    o_ref[...] = _dot_nt(eye.astype(BF16), w_ref[...]).astype(BF16)


def _transpose_weights(w):
    depth, d, c = w.shape
    tc = _lane_tile(c, 256)
    return _call(
        _transpose_kernel, "weight_transpose", (depth, c // tc),
        [pl.BlockSpec((None, d, tc), lambda l, j: (l, 0, j))], [w],
        pl.BlockSpec((None, tc, d), lambda l, j: (l, j, 0)),
        jax.ShapeDtypeStruct((depth, c, d), BF16), ("parallel", "parallel"))


def _rope_swap_index():
    quarter = ROPE_DIM // 4
    idx = np.arange(ROPE_DIM).reshape(2, 2, quarter)
    return idx[:, ::-1, :].reshape(-1)


def _rope_tables(t):
    pos = jnp.arange(t)
    row = (pos // GRID_W).astype(F32)
    col = (pos % GRID_W).astype(F32)
    n_freq = ROPE_DIM // 4
    inv_freq = jnp.power(ROPE_BASE, -jnp.arange(n_freq, dtype=F32) / n_freq)
    ang_r = row[:, None] * inv_freq
    ang_c = col[:, None] * inv_freq
    cos = jnp.concatenate([jnp.cos(ang_r), jnp.cos(ang_r), jnp.cos(ang_c), jnp.cos(ang_c)], axis=-1)
    sin = jnp.concatenate([-jnp.sin(ang_r), jnp.sin(ang_r), -jnp.sin(ang_c), jnp.sin(ang_c)], axis=-1)
    return cos, sin


def kernel(x_prompt, x_sample, cache_ckv, cache_kpe, state_C, state_n, state_m, c, c_ctx, w_mod, b_mod, norm_w, ffn_w_gu, ffn_w_down, w_in, q_norm_w, kv_norm_w, w_uq, w_ukv, mlstm_gate_b, mlstm_norm_w, pool_w, pool_scale, w_branch, w_out, final_norm_w):
    batch, seq, d = x_prompt.shape
    dec_batch, dec_seq, _ = x_sample.shape
    depth = w_mod.shape[0]
    q_rank, kv_rank = q_norm_w.shape[1], kv_norm_w.shape[1]
    heads = MLA_HEADS
    mw = mlstm_norm_w.shape[1]
    dh = mw // MLSTM_HEADS
    pw = pool_scale.shape[1]
    ffn_h = ffn_w_down.shape[2]
    n_gate = N_DIR * 2 * MLSTM_HEADS
    assert mw == pw == w_branch.shape[2] == heads * V_DIM
    rows = _Rows(batch * seq, seq, dec_batch * dec_seq, dec_seq)
    n = rows.n

    hp = _round_up(ffn_h, 512)
    w_g, w_u = _gate_up_prep(ffn_w_gu, hp)
    wdn = _down_prep(ffn_w_down, hp)

    sizes = (q_rank, kv_rank, ROPE_DIM, mw, mw, mw, mw, n_gate, pw, N_BRANCH * d)
    offs = np.concatenate([[0], np.cumsum(sizes)])
    swap = _rope_swap_index()
    small_cols = q_rank + kv_rank + 2 * ROPE_DIM + LANES
    w_in16 = lax.optimization_barrier(w_in.astype(BF16))
    w_small = jnp.concatenate(
        [w_in16[:, :, :offs[3]], w_in16[:, :, offs[2]:offs[3]][:, :, swap], w_in16[:, :, offs[7]:offs[8]],
         jnp.zeros((depth, d, LANES - n_gate), BF16)], axis=-1)
    gate_blk = (q_rank + kv_rank + 2 * ROPE_DIM) // LANES
    w_big = jnp.concatenate([w_in16[:, :, offs[3]:offs[5]], w_in16[:, :, offs[8]:]], axis=-1)
    w_feat = _transpose_weights(w_in16[:, :, offs[5]:offs[7]])
    gate_col0 = 3 * mw

    wq4 = w_uq.reshape(depth, q_rank, heads, QK_DIM)
    wq_pe = wq4[..., NOPE_DIM:]
    wq = jnp.concatenate([wq4[..., :NOPE_DIM].reshape(depth, q_rank, -1),
                          wq_pe.reshape(depth, q_rank, -1),
                          wq_pe[..., swap].reshape(depth, q_rank, -1)], axis=-1).astype(BF16)
    wkv4 = w_ukv.reshape(depth, kv_rank, heads, NOPE_DIM + V_DIM)
    wkv = jnp.concatenate([wkv4[..., :NOPE_DIM].reshape(depth, kv_rank, -1),
                           wkv4[..., NOPE_DIM:].reshape(depth, kv_rank, -1)], axis=-1).astype(BF16)
    wbr = w_branch.astype(BF16)
    wout = w_out.astype(BF16)
    pwb = pool_w.astype(BF16)
    norm_w4 = norm_w.reshape(depth, 3, 1, d)
    qnw = q_norm_w.reshape(depth, 1, q_rank)
    kvnw = kv_norm_w.reshape(depth, 1, kv_rank)
    mnw = mlstm_norm_w.reshape(depth, mw, 1)
    psc = pool_scale.reshape(depth, 1, pw)
    bands = _pool_bands()
    rope_tabs = _rope_tables(dec_seq)

    cond = jnp.concatenate([c_ctx[None, :], c, jnp.zeros((COND_ROWS - 1 - dec_batch, d), F32)], axis=0)
    mod = _mod_all(cond, w_mod, b_mod).reshape(depth, COND_ROWS, N_MOD, 1, d)

    cache_kv = _cache_kv(cache_ckv, cache_kpe, wkv)

    x = (x_prompt.reshape(rows.n_ctx, d), x_sample.reshape(rows.n_lat, d))
    new_cache = (None, None)
    new_state = (None, None, None)
    for l in range(depth):
        x, h_mix = _ffn(x, mod, norm_w4, w_g, w_u, wdn, rows, l, 0)

        p32 = _inproj(h_mix, w_small, rows, l, F32, small_cols)
        pb = _inproj(h_mix, w_big, rows, l, BF16, _lane_tile(w_big.shape[2], 1024))
        pb_t = _inproj_t(h_mix, w_feat, rows, l)
        gates_t = p32[:, gate_blk * LANES:gate_blk * LANES + n_gate].T

        q_c, k_c, v_c, *new_cache = _mla_prep(p32, qnw, kvnw, wq, wkv, l, 0, batch, seq, None, new_cache)
        q_s, k_s, v_s = _mla_prep(p32, qnw, kvnw, wq, wkv, l, rows.n_ctx, dec_batch, dec_seq, rope_tabs, None)
        y_a = _attention(q_c, k_c, v_c, None, l, n, 0, None)
        y_a = _attention(q_s, k_s, v_s, cache_kv, l, n, rows.n_ctx, y_a)

        hf_c, hb_c, *new_state = _mlstm(pb, pb_t, p32, gates_t, mlstm_gate_b, None, l, 0, batch, seq,
                                        gate_blk, dh, new_state)
        hf_s, hb_s, _, _, _ = _mlstm(pb, pb_t, p32, gates_t, mlstm_gate_b, (state_C, state_n, state_m), l,
                                     rows.n_ctx, dec_batch, dec_seq, gate_blk, dh, None)
        y_b = _mlstm_post(hf_c, hb_c, pb_t, mnw, l, n, 0, dh, None)
        y_b = _mlstm_post(hf_s, hb_s, pb_t, mnw, l, n, rows.n_ctx, dh, y_b)

        y_c = _pool(pb, bands, pwb, psc, l, n, 0, batch, seq, None)
        y_c = _pool(pb, bands, pwb, psc, l, n, rows.n_ctx, dec_batch, dec_seq, y_c)

        merged = _merge(y_a, y_b, y_c, pb, wbr, rows, l, gate_col0)
        x = _outproj(merged, wout, x, mod, rows, l)
        x = _ffn(x, mod, norm_w4, w_g, w_u, wdn, rows, l, 1)

    y_prompt = _final_norm(x, final_norm_w, 0, rows.n_ctx).reshape(batch, seq, d)
    y_sample = _final_norm(x, final_norm_w, rows.n_ctx, rows.n_lat).reshape(dec_batch, dec_seq, d)
    new_c, new_n, new_m = new_state
    return (y_prompt, y_sample, new_cache[0], new_cache[1], new_c,
            new_n.reshape(batch, depth, N_DIR, MLSTM_HEADS, dh),
            new_m.reshape(batch, depth, N_DIR, MLSTM_HEADS))
```

```python
import functools
import math

import numpy as np
import jax
import jax.numpy as jnp
from jax import lax
from jax.experimental import pallas as pl
from jax.experimental.pallas import tpu as pltpu

GRID_W = 64
EPS = 1e-6
N_MOD = 9
MLA_HEADS = 8
NOPE_DIM = 128
ROPE_DIM = 64
V_DIM = 128
QK_DIM = NOPE_DIM + ROPE_DIM
ROPE_BASE = 10000.0
MLSTM_HEADS = 4
N_DIR = 2
CHUNK = 128
POOL_WINDOWS = (2, 4, 8, 16)
POOL_GROUPS = 4
N_BRANCH = 3

LANES = 128
VMEM_LIMIT_MB = 56
COND_ROWS = 8
PACK_ROWS = 16
V_PAD = 2 * V_DIM

F32 = jnp.float32
BF16 = jnp.bfloat16


def _params(sem):
    return pltpu.CompilerParams(dimension_semantics=sem, vmem_limit_bytes=VMEM_LIMIT_MB << 20)


def _call(kernel, name, grid, in_specs, args, out_specs, out_shape, sem, bases=(), scratch=()):
    n_in = len(args)
    extra = [b for b in bases if b is not None]
    aliases = {}
    for k, b in enumerate(bases):
        if b is not None:
            aliases[n_in + len(aliases)] = k

    def body(*refs):
        kernel(*refs[:n_in], *refs[n_in + len(extra):])

    return pl.pallas_call(
        body if extra else kernel, grid=grid,
        in_specs=list(in_specs) + [pl.BlockSpec(memory_space=pl.ANY)] * len(extra),
        out_specs=out_specs, out_shape=out_shape, input_output_aliases=aliases,
        scratch_shapes=list(scratch), compiler_params=_params(sem), name=name)(*args, *extra)


def _round_up(n, m):
    return (n + m - 1) // m * m


def _lane_tile(n, cap):
    t = cap - cap % LANES
    while n % t:
        t -= LANES
    return t


def _sigmoid(x):
    return 1.0 / (1.0 + jnp.exp(-x))


def _log_sigmoid(x):
    return -(jnp.maximum(-x, 0.0) + jnp.log1p(jnp.exp(-jnp.abs(x))))


def _rms(x, w):
    return x * lax.rsqrt(jnp.mean(x * x, axis=-1, keepdims=True) + EPS) * w


def _dot(a, b):
    return jnp.dot(a, b, preferred_element_type=F32)


def _dot_nt(a, b):
    return lax.dot_general(a, b, (((1,), (1,)), ((), ())), preferred_element_type=F32)


def _mod_kernel(c_ref, w_ref, b_ref, o_ref):
    c = c_ref[...]
    a = (c * _sigmoid(c)).astype(BF16)
    o_ref[...] = _dot(a, w_ref[...].astype(BF16)) + b_ref[...]


def _mod_all(cond, w_mod, b_mod):
    depth, d, nd = w_mod.shape
    tn = _lane_tile(nd, 1024)
    return _call(
        _mod_kernel, "adaln_mod", (depth, nd // tn),
        [pl.BlockSpec((COND_ROWS, d), lambda l, j: (0, 0)),
         pl.BlockSpec((None, d, tn), lambda l, j: (l, 0, j)),
         pl.BlockSpec((None, 1, tn), lambda l, j: (l, 0, j))],
        [cond, w_mod, b_mod.reshape(depth, 1, nd)],
        pl.BlockSpec((None, COND_ROWS, tn), lambda l, j: (l, 0, j)),
        jax.ShapeDtypeStruct((depth, COND_ROWS, nd), F32),
        ("parallel", "parallel"))


class _Rows:
    def __init__(self, n_ctx, t_ctx, n_lat, t_lat):
        self.n_ctx, self.t_ctx, self.n_lat, self.t_lat = n_ctx, t_ctx, n_lat, t_lat
        self.n = n_ctx + n_lat

    def mod_row(self, row):
        return jnp.where(row < self.n_ctx, 0, 1 + (row - self.n_ctx) // self.t_lat)

    def tile(self, cap):
        t = min(cap, self.n_ctx, self.t_lat)
        assert self.n_ctx % t == 0 and self.t_lat % t == 0
        return t


def _mod_spec(rows, tm, layer, k, d):
    return pl.BlockSpec((None, None, None, 1, d),
                        lambda i, j: (layer, rows.mod_row(i * tm), k, 0, 0))


def _norm_mod_to(h_ref, x_ref, nw_ref, sh_ref, sc_ref):
    y = _rms(x_ref[...], nw_ref[...])
    h_ref[...] = (y * (1.0 + sc_ref[...]) + sh_ref[...]).astype(h_ref.dtype)


def _ffn_kernel(emit_next, ctx_tiles, *refs):
    x_refs, refs = (refs[:1], refs[1:]) if ctx_tiles is None else (refs[:2], refs[2:])
    sh_ref, sc_ref, g_ref, nw_ref, wg_ref, wu_ref, wd_ref = refs[:7]
    if emit_next:
        sh2_ref, sc2_ref, nw2_ref, o_ref, h2_ref, h_ref = refs[7:]
    else:
        o_ref, h_ref = refs[7:]
    i = pl.program_id(0)
    j = pl.program_id(1)

    def per_source(fn):
        if ctx_tiles is None:
            fn(x_refs[0])
        else:
            pl.when(i < ctx_tiles)(functools.partial(fn, x_refs[0]))
            pl.when(i >= ctx_tiles)(functools.partial(fn, x_refs[1]))

    def prologue(x_ref):
        _norm_mod_to(h_ref, x_ref, nw_ref, sh_ref, sc_ref)

    def epilogue(x_ref):
        o_ref[...] = x_ref[...] + 0.5 * g_ref[...] * o_ref[...]
        if emit_next:
            _norm_mod_to(h2_ref, o_ref, nw2_ref, sh2_ref, sc2_ref)

    @pl.when(j == 0)
    def _():
        per_source(prologue)
        o_ref[...] = jnp.zeros_like(o_ref)

    h = h_ref[...]
    g = _dot(h, wg_ref[...])
    u = _dot(h, wu_ref[...])
    a = (g * _sigmoid(g) * u).astype(BF16)
    o_ref[...] += _dot(a, wd_ref[...])

    @pl.when(j == pl.num_programs(1) - 1)
    def _():
        per_source(epilogue)


def _ffn(x, mod, norm_w, w_g, w_u, w_down, rows, layer, which):
    tm = rows.tile(512)
    if isinstance(x, tuple):
        ctx_tiles = rows.n_ctx // tm
        x_args = list(x)
        x_specs = [pl.BlockSpec((tm, x[0].shape[1]), lambda i, j: (jnp.minimum(i, ctx_tiles - 1), 0)),
                   pl.BlockSpec((tm, x[0].shape[1]), lambda i, j: (jnp.maximum(i - ctx_tiles, 0), 0))]
    else:
        ctx_tiles = None
        x_args = [x]
        x_specs = [pl.BlockSpec((tm, x.shape[1]), lambda i, j: (i, 0))]
    n, d = rows.n, x_args[0].shape[1]
    hp = w_down.shape[2]
    th = _lane_tile(hp, 512)
    k0 = 0 if which == 0 else 6
    emit_next = which == 0
    w_spec = pl.BlockSpec((None, None, d, th), lambda i, j: (layer, which, 0, j))
    row_spec = pl.BlockSpec((tm, d), lambda i, j: (i, 0))
    norm_spec = lambda k: pl.BlockSpec((None, None, 1, d), lambda i, j: (layer, k, 0, 0))
    in_specs = x_specs + [
        _mod_spec(rows, tm, layer, k0, d),
        _mod_spec(rows, tm, layer, k0 + 1, d),
        _mod_spec(rows, tm, layer, k0 + 2, d),
        norm_spec(2 * which), w_spec, w_spec,
        pl.BlockSpec((None, None, th, d), lambda i, j: (layer, which, j, 0))]
    args = x_args + [mod, mod, mod, norm_w, w_g, w_u, w_down]
    out_specs, out_shape = row_spec, jax.ShapeDtypeStruct((n, d), F32)
    if emit_next:
        in_specs += [_mod_spec(rows, tm, layer, 3, d), _mod_spec(rows, tm, layer, 4, d), norm_spec(1)]
        args += [mod, mod, norm_w]
        out_specs, out_shape = [row_spec, row_spec], [out_shape, jax.ShapeDtypeStruct((n, d), BF16)]
    return _call(
        functools.partial(_ffn_kernel, emit_next, ctx_tiles), "ffn_half_step", (n // tm, hp // th),
        in_specs, args, out_specs, out_shape, ("parallel", "arbitrary"),
        scratch=[pltpu.VMEM((tm, d), BF16)])


def _inproj_kernel(h_ref, w_ref, o_ref):
    o_ref[...] = _dot(h_ref[...], w_ref[...]).astype(o_ref.dtype)


def _inproj(h, w, rows, layer, out_dtype, tn):
    n, d = h.shape
    cols = w.shape[2]
    tm = rows.tile(1024)
    return _call(
        _inproj_kernel, "mixer_in_proj", (n // tm, cols // tn),
        [pl.BlockSpec((tm, d), lambda i, j: (i, 0)),
         pl.BlockSpec((None, d, tn), lambda i, j: (layer, 0, j))],
        [h, w],
        pl.BlockSpec((tm, tn), lambda i, j: (i, j)),
        jax.ShapeDtypeStruct((n, cols), out_dtype),
        ("parallel", "arbitrary"))


def _inproj_t_kernel(h_ref, wt_ref, o_ref):
    o_ref[...] = _dot_nt(wt_ref[...], h_ref[...]).astype(o_ref.dtype)


def _inproj_t(h, w_t, rows, layer):
    n, d = h.shape
    cols = w_t.shape[1]
    tm = rows.tile(1024)
    tn = _lane_tile(cols, 1024)
    return _call(
        _inproj_t_kernel, "mixer_in_proj_t", (n // tm, cols // tn),
        [pl.BlockSpec((tm, d), lambda i, j: (i, 0)),
         pl.BlockSpec((None, tn, d), lambda i, j: (layer, j, 0))],
        [h, w_t],
        pl.BlockSpec((tn, tm), lambda i, j: (j, i)),
        jax.ShapeDtypeStruct((cols, n), BF16),
        ("parallel", "arbitrary"))


def _ones_column(rows):
    lane = lax.broadcasted_iota(jnp.int32, (rows, V_PAD - V_DIM), 1)
    return (lane == 0).astype(BF16)


def _mla_prep_kernel(rope, q_rank, kv_rank, p_ref, qnw_ref, kvnw_ref, wq_ref, wkv_ref, *rest):
    if rope:
        cos_ref, sin_ref, q_ref, k_ref, v_ref = rest
    else:
        q_ref, k_ref, v_ref, ckv_ref, kpe_ref = rest
    heads = MLA_HEADS
    p = p_ref[...]
    c_q = p[:, :q_rank]
    c_kv = p[:, q_rank:q_rank + kv_rank]
    o = q_rank + kv_rank
    k_pe = p[:, o:o + ROPE_DIM]
    k_pe_sw = p[:, o + ROPE_DIM:o + 2 * ROPE_DIM]

    qa = _dot(_rms(c_q, qnw_ref[...]).astype(BF16), wq_ref[...])
    ckv_n = _rms(c_kv, kvnw_ref[...])
    kv = _dot(ckv_n.astype(BF16), wkv_ref[...])
    scale = QK_DIM ** -0.5 * math.log2(math.e)
    if rope:
        cos = cos_ref[...]
        sin = sin_ref[...]
        k_pe = k_pe * cos + k_pe_sw * sin
    else:
        ckv_ref[...] = ckv_n
        kpe_ref[...] = k_pe
    ones = _ones_column(p.shape[0])
    pe0 = heads * NOPE_DIM
    sw0 = pe0 + heads * ROPE_DIM
    for h in range(heads):
        q_pe = qa[:, pe0 + h * ROPE_DIM:pe0 + (h + 1) * ROPE_DIM]
        if rope:
            q_pe = q_pe * cos + qa[:, sw0 + h * ROPE_DIM:sw0 + (h + 1) * ROPE_DIM] * sin
        q_ref[h, :, :NOPE_DIM] = (qa[:, h * NOPE_DIM:(h + 1) * NOPE_DIM] * scale).astype(BF16)
        q_ref[h, :, NOPE_DIM:] = (q_pe * scale).astype(BF16)
        k_ref[h, :, :NOPE_DIM] = kv[:, h * NOPE_DIM:(h + 1) * NOPE_DIM].astype(BF16)
        k_ref[h, :, NOPE_DIM:] = k_pe.astype(BF16)
        v0 = heads * NOPE_DIM + h * V_DIM
        v_ref[h, :, :V_DIM] = kv[:, v0:v0 + V_DIM].astype(BF16)
        v_ref[h, :, V_DIM:] = ones


def _mla_prep(p32, q_norm_w, kv_norm_w, wq, wkv, layer, row0, batch, t, rope_tabs, bases):
    ws = p32.shape[1]
    q_rank, kv_rank = q_norm_w.shape[-1], kv_norm_w.shape[-1]
    depth = q_norm_w.shape[0]
    tm = min(256, t)
    nt = t // tm
    heads = MLA_HEADS
    rope = rope_tabs is not None
    in_specs = [pl.BlockSpec((tm, ws), lambda b, i: (row0 // tm + b * nt + i, 0)),
                pl.BlockSpec((None, 1, q_rank), lambda b, i: (layer, 0, 0)),
                pl.BlockSpec((None, 1, kv_rank), lambda b, i: (layer, 0, 0)),
                pl.BlockSpec((None,) + wq.shape[1:], lambda b, i: (layer, 0, 0)),
                pl.BlockSpec((None,) + wkv.shape[1:], lambda b, i: (layer, 0, 0))]
    args = [p32, q_norm_w, kv_norm_w, wq, wkv]
    head_spec = lambda width: pl.BlockSpec((None, heads, tm, width), lambda b, i: (b, 0, i, 0))
    out_specs = [head_spec(QK_DIM), head_spec(QK_DIM), head_spec(V_PAD)]
    out_shape = [jax.ShapeDtypeStruct((batch, heads, t, QK_DIM), BF16),
                 jax.ShapeDtypeStruct((batch, heads, t, QK_DIM), BF16),
                 jax.ShapeDtypeStruct((batch, heads, t, V_PAD), BF16)]
    if rope:
        in_specs += [pl.BlockSpec((tm, ROPE_DIM), lambda b, i: (i, 0))] * 2
        args += list(rope_tabs)
        all_bases = ()
    else:
        out_specs += [pl.BlockSpec((None, None, tm, kv_rank), lambda b, i: (b, layer, i, 0)),
                      pl.BlockSpec((None, None, tm, ROPE_DIM), lambda b, i: (b, layer, i, 0))]
        out_shape += [jax.ShapeDtypeStruct((batch, depth, t, kv_rank), F32),
                      jax.ShapeDtypeStruct((batch, depth, t, ROPE_DIM), F32)]
        all_bases = (None, None, None) + tuple(bases)
    return _call(functools.partial(_mla_prep_kernel, rope, q_rank, kv_rank), "mla_prep",
                 (batch, nt), in_specs, args, out_specs, out_shape, ("parallel", "parallel"),
                 bases=all_bases)


def _cache_kv_kernel(ckv_ref, kpe_ref, wkv_ref, k_ref, v_ref):
    heads = MLA_HEADS
    kv = _dot(ckv_ref[...].astype(BF16), wkv_ref[...])
    k_pe = kpe_ref[...].astype(BF16)
    ones = _ones_column(kv.shape[0])
    for h in range(heads):
        k_ref[h, :, :NOPE_DIM] = kv[:, h * NOPE_DIM:(h + 1) * NOPE_DIM].astype(BF16)
        k_ref[h, :, NOPE_DIM:] = k_pe
        v0 = heads * NOPE_DIM + h * V_DIM
        v_ref[h, :, :V_DIM] = kv[:, v0:v0 + V_DIM].astype(BF16)
        v_ref[h, :, V_DIM:] = ones


def _cache_kv(cache_ckv, cache_kpe, wkv):
    batch, depth, past, kv_rank = cache_ckv.shape
    heads = MLA_HEADS
    return _call(
        _cache_kv_kernel, "mla_cache_kv", (batch, depth),
        [pl.BlockSpec((None, None, past, kv_rank), lambda b, l: (b, l, 0, 0)),
         pl.BlockSpec((None, None, past, ROPE_DIM), lambda b, l: (b, l, 0, 0)),
         pl.BlockSpec((None,) + wkv.shape[1:], lambda b, l: (l, 0, 0))],
        [cache_ckv, cache_kpe, wkv],
        [pl.BlockSpec((None, None, heads, past, QK_DIM), lambda b, l: (b, l, 0, 0, 0)),
         pl.BlockSpec((None, None, heads, past, V_PAD), lambda b, l: (b, l, 0, 0, 0))],
        [jax.ShapeDtypeStruct((batch, depth, heads, past, QK_DIM), BF16),
         jax.ShapeDtypeStruct((batch, depth, heads, past, V_PAD), BF16)],
        ("parallel", "parallel"))


ATTN_KEY_CHUNK = 512


def _attn_kernel(past, q_ref, qn_ref, k_ref, kn_ref, v_ref, *rest):
    if past:
        kc_ref, kcn_ref, vc_ref = rest[:3]
        rest = rest[3:]
    else:
        kc_ref = kcn_ref = vc_ref = None
    o_ref, s0_ref, s1_ref, m0_ref, m1_ref = rest
    tq = qn_ref.shape[0]
    t = k_ref.shape[0]
    chunk = min(ATTN_KEY_CHUNK, t)

    def scores(q, keys_ref, cache_keys_ref, s_ref, m_ref):
        s = _dot_nt(q, keys_ref[...])
        m = jnp.max(s, axis=-1, keepdims=True)
        if past:
            sc = _dot_nt(q, cache_keys_ref[...])
            m = jnp.maximum(m, jnp.max(sc, axis=-1, keepdims=True))
            s_ref[:, :past] = sc
        s_ref[:, past:] = s
        m_ref[...] = m

    def values(r, s_ref, m_ref):
        m = m_ref[...]
        acc = None
        if past:
            acc = _dot(jnp.exp2(s_ref[:, :past] - m).astype(BF16), vc_ref[...])
        for c in range(0, t, chunk):
            p = jnp.exp2(s_ref[:, past + c:past + c + chunk] - m).astype(BF16)
            d = _dot(p, v_ref[c:c + chunk, :])
            acc = d if acc is None else acc + d
        o_ref[r * tq:(r + 1) * tq, :] = (acc[:, :V_DIM] / acc[:, V_DIM:V_DIM + 1]).astype(o_ref.dtype)

    @pl.when(pl.program_id(0) == 0)
    def _():
        scores(q_ref[:tq, :], k_ref, kc_ref, s0_ref, m0_ref)

    values(0, s0_ref, m0_ref)
    scores(q_ref[tq:, :], k_ref, kc_ref, s1_ref, m1_ref)
    values(1, s1_ref, m1_ref)
    scores(qn_ref[...], kn_ref, kcn_ref, s0_ref, m0_ref)


def _attn_short_kernel(q_ref, k_ref, v_ref, o_ref):
    heads = range(q_ref.shape[0])
    s = [_dot_nt(q_ref[h], k_ref[h]) for h in heads]
    p = [jnp.exp2(x - jnp.max(x, axis=-1, keepdims=True)).astype(BF16) for x in s]
    o = [_dot(p[h], v_ref[h]) for h in heads]
    for h in heads:
        o_ref[:, h * V_DIM:(h + 1) * V_DIM] = (o[h][:, :V_DIM] / o[h][:, V_DIM:V_DIM + 1]).astype(o_ref.dtype)


ATTN_SHORT_SEQ = 256


def _attention(q, k, v, cache, layer, n_rows, row0, base):
    batch, heads, t, _ = q.shape
    if cache is None and t <= ATTN_SHORT_SEQ:
        assert row0 % t == 0
        head_block = lambda width: pl.BlockSpec((None, heads, t, width), lambda b: (b, 0, 0, 0))
        return _call(
            _attn_short_kernel, "mla_attention_short", (batch,),
            [head_block(QK_DIM), head_block(QK_DIM), head_block(V_PAD)], [q, k, v],
            pl.BlockSpec((t, heads * V_DIM), lambda b: (row0 // t + b, 0)),
            jax.ShapeDtypeStruct((n_rows, heads * V_DIM), BF16), ("parallel",), bases=(base,))
    tq = min(512, t // 2)
    pair = 2 * tq
    npair = t // pair
    n_steps = batch * heads * npair
    past = cache[0].shape[3] if cache is not None else 0
    assert t % pair == 0 and row0 % pair == 0

    def where(tile):
        tile = jnp.minimum(tile, 2 * n_steps - 1)
        p = tile // 2
        return p // (heads * npair), (p // npair) % heads, 2 * (p % npair) + tile % 2

    def pair_map(g):
        b, h, i = where(2 * g)
        return b, h, i // 2, 0

    def next_map(g):
        b, h, i = where(2 * g + 2)
        return b, h, i, 0

    def kv_map(shift):
        def index(g):
            b, h, _ = where(2 * g + shift)
            return b, h, 0, 0
        return index

    def cache_map(shift):
        def index(g):
            b, h, _ = where(2 * g + shift)
            return b, layer, h, 0, 0
        return index

    def o_map(g):
        b, h, i = where(2 * g)
        return row0 // pair + b * npair + i // 2, h

    in_specs = [pl.BlockSpec((None, None, pair, QK_DIM), pair_map),
                pl.BlockSpec((None, None, tq, QK_DIM), next_map),
                pl.BlockSpec((None, None, t, QK_DIM), kv_map(0)),
                pl.BlockSpec((None, None, t, QK_DIM), kv_map(2)),
                pl.BlockSpec((None, None, t, V_PAD), kv_map(0))]
    args = [q, q, k, k, v]
    if past:
        in_specs += [pl.BlockSpec((None, None, None, past, QK_DIM), cache_map(0)),
                     pl.BlockSpec((None, None, None, past, QK_DIM), cache_map(2)),
                     pl.BlockSpec((None, None, None, past, V_PAD), cache_map(0))]
        args += [cache[0], cache[0], cache[1]]
    s_total = past + t
    return _call(
        functools.partial(_attn_kernel, past), "mla_attention", (n_steps,),
        in_specs, args,
        pl.BlockSpec((pair, V_DIM), o_map),
        jax.ShapeDtypeStruct((n_rows, heads * V_DIM), BF16),
        ("arbitrary",), bases=(base,),
        scratch=[pltpu.VMEM((tq, s_total), F32), pltpu.VMEM((tq, s_total), F32),
                 pltpu.VMEM((tq, 1), F32), pltpu.VMEM((tq, 1), F32)])


MLSTM_CHUNKS_PER_STEP = 2


def _split3(x):
    hi = x.astype(BF16)
    r = x - hi.astype(F32)
    mid = r.astype(BF16)
    return hi, mid, (r - mid.astype(F32)).astype(BF16)


def _mlstm_kernel(has_init, dh, n_sub, *refs):
    (qf_ref, kf_ref, vf_ref, qb_ref, kb_ref, vb_ref, gf_ref, gb_ref, gtf_ref, gtb_ref,
     brow_ref, bcol_ref) = refs[:12]
    refs = refs[12:]
    if has_init:
        c0_ref, n0_ref, m0_ref = refs[:3]
        refs = refs[3:]
    hf_ref, hb_ref, c_ref, n_ref, m_ref = refs
    heads = MLSTM_HEADS
    n_gate = N_DIR * 2 * heads
    step = pl.program_id(1)

    @pl.when(step == 0)
    def _():
        if has_init:
            c_ref[...] = c0_ref[...]
            n_ref[...] = n0_ref[...]
            m_ref[...] = m0_ref[...]
        else:
            c_ref[...] = jnp.zeros_like(c_ref)
            n_ref[...] = jnp.zeros_like(n_ref)
            m_ref[...] = jnp.zeros_like(m_ref)

    tok0 = lax.broadcasted_iota(jnp.int32, (CHUNK, CHUNK), 0)
    tok1 = lax.broadcasted_iota(jnp.int32, (CHUNK, CHUNK), 1)
    k_scale = dh ** -0.5
    m_all = m_ref[...]
    m_out = m_all
    unit_lane = lax.broadcasted_iota(jnp.int32, m_all.shape, 1)
    units = [(d, h) for d in range(N_DIR) for h in range(heads)]
    for sub in range(n_sub):
        m_all = m_out
        gate = {}
        for d in range(N_DIR):
            g_ref, gt_ref = (gf_ref, gtf_ref) if d == 0 else (gb_ref, gtb_ref)
            chunk_idx = sub if d == 0 else n_sub - 1 - sub
            tok = slice(chunk_idx * CHUNK, (chunk_idx + 1) * CHUNK)
            seen_t = (tok0 <= tok1) if d == 0 else (tok0 >= tok1)
            seen_t_bf = seen_t.astype(BF16)
            seen_bf = ((tok1 <= tok0) if d == 0 else (tok1 >= tok0)).astype(BF16)
            pre_col = g_ref[tok, :n_gate] + brow_ref[...]
            pre_row = gt_ref[:, tok] + bcol_ref[...]
            cum_col = sum(_dot(seen_bf, part) for part in _split3(_log_sigmoid(pre_col)))
            cum_row = sum(_dot(part, seen_t_bf) for part in _split3(_log_sigmoid(pre_row)))
            gate[d] = (tok, seen_t, pre_col, pre_row, cum_col, cum_row)

        st = {}
        for d, h in units:
            tok, seen_t, pre_col, pre_row, cum_col, cum_row = gate[d]
            q_ref, k_ref, vt_ref = (qf_ref, kf_ref, vf_ref) if d == 0 else (qb_ref, kb_ref, vb_ref)
            ci = d * 2 * heads + h
            cf = ci + heads
            sid = d * heads + h
            sl = slice(h * dh, (h + 1) * dh)
            last = CHUNK - 1 if d == 0 else 0
            c_col = pre_col[:, ci:ci + 1] - cum_col[:, cf:cf + 1]
            i_row = pre_row[ci:ci + 1, :]
            b_row = cum_row[cf:cf + 1, :]
            b_end = b_row[:, last:last + 1]
            m_prev = m_all[:, sid:sid + 1]
            a_row = b_row + m_prev
            dmat = jnp.where(seen_t, b_row + c_col, -jnp.inf)
            m_t = jnp.maximum(a_row, jnp.max(dmat, axis=0, keepdims=True))
            q = q_ref[tok, sl]
            k_bf = (k_ref[tok, sl].astype(F32) * k_scale).astype(BF16)
            st[d, h] = dict(tok=tok, sl=sl, sid=sid, i_row=i_row, b_row=b_row, b_end=b_end, m_prev=m_prev,
                            m_t=m_t, w_intra=jnp.exp(dmat - m_t), w_inter=jnp.exp(a_row - m_t),
                            q=q, k_bf=k_bf, v_t=vt_ref[sl, tok], c_prev=c_ref[d, h], n_prev=n_ref[d, h])

        for u in units:
            x = st[u]
            x["s_t"] = _dot_nt(x["k_bf"], x["q"]) * x["w_intra"]
            n_rows = jnp.broadcast_to(x["n_prev"], (PACK_ROWS, dh)).astype(BF16)
            x["cq"] = _dot_nt(jnp.concatenate([x["c_prev"].astype(BF16), n_rows], axis=0), x["q"])

        for d, h in units:
            x = st[d, h]
            h_ref = hf_ref if d == 0 else hb_ref
            num = x["w_inter"] * x["cq"][:dh, :] + _dot(x["v_t"], x["s_t"].astype(BF16))
            den = x["w_inter"] * x["cq"][dh:dh + 1, :] + jnp.sum(x["s_t"], axis=0, keepdims=True)
            h_ref[x["sl"], x["tok"]] = num / jnp.maximum(jnp.abs(den), jnp.exp(-x["m_t"]))

        for d, h in units:
            x = st[d, h]
            g_row = x["b_end"] - x["b_row"] + x["i_row"]
            m_new = jnp.maximum(x["b_end"] + x["m_prev"], jnp.max(g_row, axis=1, keepdims=True))
            w_pos = jnp.exp(g_row - m_new)
            w_carry = jnp.exp(x["b_end"] + x["m_prev"] - m_new)
            w_rows = jnp.broadcast_to(w_pos, (PACK_ROWS, CHUNK)).astype(BF16)
            upd = _dot(jnp.concatenate([(x["v_t"].astype(F32) * w_pos).astype(BF16), w_rows], axis=0), x["k_bf"])
            c_ref[d, h] = w_carry * x["c_prev"] + upd[:dh, :]
            n_ref[d, h] = w_carry * x["n_prev"] + upd[dh:dh + 1, :]
            m_out = jnp.where(unit_lane == x["sid"], m_new, m_out)
    m_ref[...] = m_out


def _mlstm(pb, pb_t, p32, gates_t, gate_b, state, layer, row0, batch, t, gate_blk, dh, bases):
    heads = MLSTM_HEADS
    n_gate = N_DIR * 2 * heads
    depth = gate_b.shape[0]
    n_sub = MLSTM_CHUNKS_PER_STEP if (t // CHUNK) % MLSTM_CHUNKS_PER_STEP == 0 else 1
    span = n_sub * CHUNK
    nc = t // span
    assert row0 % span == 0
    blk0 = row0 // span
    w = heads * dh
    fwd = lambda b, c: blk0 + b * nc + c
    bwd = lambda b, c: blk0 + b * nc + nc - 1 - c

    def tok(col, blk):
        return pl.BlockSpec((span, w), lambda b, c: (blk(b, c), col))

    def feat(blk):
        return pl.BlockSpec((w, span), lambda b, c: (0, blk(b, c)))

    in_specs = [tok(0, fwd), tok(1, fwd), feat(fwd), tok(0, bwd), tok(1, bwd), feat(bwd),
                pl.BlockSpec((span, LANES), lambda b, c: (fwd(b, c), gate_blk)),
                pl.BlockSpec((span, LANES), lambda b, c: (bwd(b, c), gate_blk)),
                pl.BlockSpec((n_gate, span), lambda b, c: (0, fwd(b, c))),
                pl.BlockSpec((n_gate, span), lambda b, c: (0, bwd(b, c))),
                pl.BlockSpec((None, 1, n_gate), lambda b, c: (layer, 0, 0)),
                pl.BlockSpec((None, n_gate, 1), lambda b, c: (layer, 0, 0))]
    args = [pb, pb, pb_t, pb, pb, pb_t, p32, p32, gates_t, gates_t,
            gate_b.reshape(-1, 1, n_gate), gate_b.reshape(-1, n_gate, 1)]
    has_init = state is not None
    state_shapes = [(N_DIR, heads, dh, dh), (N_DIR, heads, 1, dh), (1, N_DIR * heads)]
    if has_init:
        c0, n0, m0 = state
        in_specs += [pl.BlockSpec((None, None) + shp, lambda b, c, z=(0,) * len(shp): (b, layer) + z)
                     for shp in state_shapes]
        args += [c0, n0.reshape((batch, depth) + state_shapes[1]), m0.reshape((batch, depth) + state_shapes[2])]
        st_specs = [pl.BlockSpec((None,) + shp, lambda b, c, z=(0,) * len(shp): (b,) + z) for shp in state_shapes]
        st_shapes = [jax.ShapeDtypeStruct((batch,) + shp, F32) for shp in state_shapes]
        all_bases = ()
    else:
        st_specs = [pl.BlockSpec((None, None) + shp, lambda b, c, z=(0,) * len(shp): (b, layer) + z)
                    for shp in state_shapes]
        st_shapes = [jax.ShapeDtypeStruct((batch, depth) + shp, F32) for shp in state_shapes]
        all_bases = (None, None) + tuple(bases)
    return _call(
        functools.partial(_mlstm_kernel, has_init, dh, n_sub), "mlstm_scan", (batch, nc), in_specs, args,
        [pl.BlockSpec((w, span), lambda b, c: (0, b * nc + c)),
         pl.BlockSpec((w, span), lambda b, c: (0, b * nc + nc - 1 - c))] + st_specs,
        [jax.ShapeDtypeStruct((w, batch * t), F32),
         jax.ShapeDtypeStruct((w, batch * t), F32)] + st_shapes,
        ("parallel", "arbitrary"), bases=all_bases)


def _mlstm_post_kernel(dh, hf_ref, hb_ref, o_ref, w_ref, y_ref):
    tm = hf_ref.shape[1]
    hm = hf_ref[...] + hb_ref[...]
    gate = _sigmoid(o_ref[...].astype(F32))
    w = w_ref[...]
    eye = (lax.broadcasted_iota(jnp.int32, (tm, tm), 0)
           == lax.broadcasted_iota(jnp.int32, (tm, tm), 1)).astype(BF16)
    for h in range(MLSTM_HEADS):
        sl = slice(h * dh, (h + 1) * dh)
        x = hm[sl, :]
        y = x * lax.rsqrt(jnp.mean(x * x, axis=0, keepdims=True) + EPS) * w[sl, :]
        y_t = (gate[sl, :] * y).astype(BF16)
        y_ref[:, sl] = _dot_nt(eye, y_t).astype(y_ref.dtype)


def _mlstm_post(h_f, h_b, pb_t, m_norm_w, layer, n_rows, row0, dh, base):
    w, n = h_f.shape
    tm = min(256, n)
    return _call(
        functools.partial(_mlstm_post_kernel, dh), "mlstm_post", (n // tm,),
        [pl.BlockSpec((w, tm), lambda i: (0, i)),
         pl.BlockSpec((w, tm), lambda i: (0, i)),
         pl.BlockSpec((w, tm), lambda i: (1, row0 // tm + i)),
         pl.BlockSpec((None, w, 1), lambda i: (layer, 0, 0))],
        [h_f, h_b, pb_t, m_norm_w],
        pl.BlockSpec((tm, w), lambda i: (row0 // tm + i, 0)),
        jax.ShapeDtypeStruct((n_rows, w), BF16),
        ("parallel",), bases=(base,))


POOL_TILE = 256


def _pool_bands():
    t = np.arange(POOL_TILE)[:, None]
    bands = np.zeros((POOL_GROUPS, 3, POOL_TILE, POOL_TILE), np.float32)
    for g, win in enumerate(POOL_WINDOWS):
        for part in range(3):
            s = np.arange(POOL_TILE)[None, :] + (part - 1) * POOL_TILE
            bands[g, part] = (s >= t - win // 2) & (s < t - win // 2 + win)
    return jnp.asarray(bands, BF16)


def _pool_kernel(t_seq, gd, up_ref, um_ref, un_ref, band_ref, pw_ref, ps_ref, y_ref):
    j = pl.program_id(1)
    has_prev = (j > 0).astype(F32)
    has_next = (j < pl.num_programs(1) - 1).astype(F32)
    tile = um_ref.shape[0]
    pos = j * tile + lax.broadcasted_iota(jnp.int32, (tile, 1), 0)
    groups = range(len(POOL_WINDOWS))
    cols = [slice(g * gd, (g + 1) * gd) for g in groups]
    acc = [_dot(band_ref[g, 1], um_ref[:, cols[g]])
           + has_prev * _dot(band_ref[g, 0], up_ref[:, cols[g]])
           + has_next * _dot(band_ref[g, 2], un_ref[:, cols[g]]) for g in groups]
    pooled = []
    for g, win in enumerate(POOL_WINDOWS):
        lo = jnp.clip(pos - win // 2, 0, t_seq)
        hi = jnp.clip(pos - win // 2 + win, 0, t_seq)
        pooled.append((acc[g] / (hi - lo).astype(F32) - um_ref[:, cols[g]].astype(F32)).astype(BF16))
    y = [_dot(pooled[g], pw_ref[g]) * ps_ref[:, cols[g]] for g in groups]
    for g in groups:
        y_ref[:, cols[g]] = y[g].astype(y_ref.dtype)


def _pool(pb, bands, pool_w, pool_scale, layer, n_rows, row0, batch, t, base):
    gd = pool_w.shape[-1]
    w = POOL_GROUPS * gd
    tile = POOL_TILE
    assert t % tile == 0
    nt = t // tile
    blk0 = row0 // tile

    def u_spec(shift):
        return pl.BlockSpec((tile, w), lambda b, j: (blk0 + b * nt + jnp.clip(j + shift, 0, nt - 1), 2))

    return _call(
        functools.partial(_pool_kernel, t, gd), "multiscale_pool", (batch, nt),
        [u_spec(-1), u_spec(0), u_spec(1),
         pl.BlockSpec(bands.shape, lambda b, j: (0, 0, 0, 0)),
         pl.BlockSpec((None, POOL_GROUPS, gd, gd), lambda b, j: (layer, 0, 0, 0)),
         pl.BlockSpec((None, 1, w), lambda b, j: (layer, 0, 0))],
        [pb, pb, pb, bands, pool_w, pool_scale],
        pl.BlockSpec((tile, w), lambda b, j: (blk0 + b * nt + j, 0)),
        jax.ShapeDtypeStruct((n_rows, w), BF16),
        ("parallel", "parallel"), bases=(base,))


def _merge_kernel(ya_ref, yb_ref, yc_ref, ga_ref, gb_ref, gc_ref, w_ref, o_ref):
    branch = [_dot(y_ref[...], w_ref[k]) for k, y_ref in enumerate((ya_ref, yb_ref, yc_ref))]
    gates = [_sigmoid(g_ref[...].astype(F32)) for g_ref in (ga_ref, gb_ref, gc_ref)]
    o_ref[...] = (gates[0] * branch[0] + gates[1] * branch[1] + gates[2] * branch[2]).astype(o_ref.dtype)


def _merge(y_a, y_b, y_c, pb, w_branch, rows, layer, gate_col0):
    n, bw = y_a.shape
    d = w_branch.shape[-1]
    tm = rows.tile(1024)
    tn = min(1024, d)
    g0 = gate_col0 // tn
    nd = d // tn

    def gate_spec(k):
        return pl.BlockSpec((tm, tn), lambda i, j: (i, g0 + k * nd + j))

    y_spec = pl.BlockSpec((tm, bw), lambda i, j: (i, 0))
    return _call(
        _merge_kernel, "branch_merge", (n // tm, nd),
        [y_spec, y_spec, y_spec, gate_spec(0), gate_spec(1), gate_spec(2),
         pl.BlockSpec((None, N_BRANCH, bw, tn), lambda i, j: (layer, 0, 0, j))],
        [y_a, y_b, y_c, pb, pb, pb, w_branch],
        pl.BlockSpec((tm, tn), lambda i, j: (i, j)),
        jax.ShapeDtypeStruct((n, d), BF16),
        ("parallel", "arbitrary"))


def _outproj_kernel(m_ref, w_ref, x_ref, g_ref, o_ref):
    o_ref[...] = x_ref[...] + g_ref[...] * _dot(m_ref[...], w_ref[...])


def _outproj(merged, w_out, x, mod, rows, layer):
    n, d = x.shape
    tm = rows.tile(1024)
    tn = min(1024, d)
    return _call(
        _outproj_kernel, "mixer_out_proj", (n // tm, d // tn),
        [pl.BlockSpec((tm, d), lambda i, j: (i, 0)),
         pl.BlockSpec((None, d, tn), lambda i, j: (layer, 0, j)),
         pl.BlockSpec((tm, tn), lambda i, j: (i, j)),
         pl.BlockSpec((None, None, None, 1, tn),
                      lambda i, j: (layer, rows.mod_row(i * tm), 5, 0, j))],
        [merged, w_out, x, mod],
        pl.BlockSpec((tm, tn), lambda i, j: (i, j)),
        jax.ShapeDtypeStruct((n, d), F32),
        ("parallel", "arbitrary"))


def _final_norm_kernel(x_ref, w_ref, o_ref):
    o_ref[...] = _rms(x_ref[...], w_ref[...])


def _final_norm(x, w, row0, n_rows):
    d = x.shape[1]
    tm = min(512, n_rows)
    return _call(
        _final_norm_kernel, "final_norm", (n_rows // tm,),
        [pl.BlockSpec((tm, d), lambda i: (row0 // tm + i, 0)),
         pl.BlockSpec((1, d), lambda i: (0, 0))],
        [x, w.reshape(1, d)],
        pl.BlockSpec((tm, d), lambda i: (i, 0)),
        jax.ShapeDtypeStruct((n_rows, d), F32),
        ("parallel",))


def _gate_up_prep_kernel(valid, g_ref, u_ref, og_ref, ou_ref):
    for src, dst in ((g_ref, og_ref), (u_ref, ou_ref)):
        dst[:, :valid] = src[...].astype(BF16)
        if dst.shape[1] > valid:
            dst[:, valid:] = jnp.zeros((dst.shape[0], dst.shape[1] - valid), BF16)


def _gate_up_prep(w_gu, hp):
    depth, n_ffn, d, h2 = w_gu.shape
    h = h2 // 2
    assert h % LANES == 0
    n_rows = depth * n_ffn * d
    tr = 256
    flat = w_gu.reshape(n_rows, h2)
    out = jax.ShapeDtypeStruct((n_rows, hp), BF16)
    w_g, w_u = _call(
        functools.partial(_gate_up_prep_kernel, h), "ffn_gate_up_prep", (n_rows // tr,),
        [pl.BlockSpec((tr, h), lambda r: (r, 0)), pl.BlockSpec((tr, h), lambda r: (r, 1))],
        [flat, flat],
        [pl.BlockSpec((tr, hp), lambda r: (r, 0))] * 2, [out, out], ("parallel",))
    return w_g.reshape(depth, n_ffn, d, hp), w_u.reshape(depth, n_ffn, d, hp)


def _down_prep_kernel(valid, w_ref, o_ref):
    o_ref[:valid, :] = w_ref[...].astype(BF16)
    if o_ref.shape[0] > valid:
        o_ref[valid:, :] = jnp.zeros((o_ref.shape[0] - valid, o_ref.shape[1]), BF16)


def _down_prep(w_down, hp):
    depth, n_ffn, h, d = w_down.shape
    assert h % PACK_ROWS == 0
    td = _lane_tile(d, 256)
    flat = w_down.reshape(depth * n_ffn, h, d)
    out = _call(
        functools.partial(_down_prep_kernel, h), "ffn_down_prep", (depth * n_ffn, d // td),
        [pl.BlockSpec((None, h, td), lambda a, j: (a, 0, j))], [flat],
        pl.BlockSpec((None, hp, td), lambda a, j: (a, 0, j)),
        jax.ShapeDtypeStruct((depth * n_ffn, hp, d), BF16), ("parallel", "parallel"))
    return out.reshape(depth, n_ffn, hp, d)


def _window(ref, start, stop):
    lo = start - start % LANES
    hi = min(_round_up(stop, LANES), ref.shape[1])
    return ref[:, lo:hi][:, start - lo:stop - lo]


def _in_proj_prep_kernel(offs, n_gate, w_ref, small_ref, big_ref, feat_ref):
    tr = w_ref.shape[0]
    k_pe = _window(w_ref, offs[2], offs[3]).astype(BF16)
    quarter = ROPE_DIM // 4
    src = lax.broadcasted_iota(jnp.int32, (ROPE_DIM, ROPE_DIM), 0)
    dst = lax.broadcasted_iota(jnp.int32, (ROPE_DIM, ROPE_DIM), 1)
    swapped_dst = jnp.where((dst // quarter) % 2 == 0, dst + quarter, dst - quarter)
    select = (src == swapped_dst).astype(BF16)
    small_ref[:, :offs[2]] = w_ref[:, :offs[2]].astype(BF16)
    small_ref[:, offs[2]:offs[2] + 2 * ROPE_DIM] = jnp.concatenate(
        [k_pe, _dot(k_pe, select).astype(BF16)], axis=1)
    gates = _window(w_ref, offs[7], offs[8]).astype(BF16)
    small_ref[:, offs[2] + 2 * ROPE_DIM:] = jnp.concatenate(
        [gates, jnp.zeros((tr, LANES - n_gate), BF16)], axis=1)
    qk = offs[5] - offs[3]
    big_ref[:, :qk] = _window(w_ref, offs[3], offs[5]).astype(BF16)
    big_ref[:, qk:] = _window(w_ref, offs[8], offs[10]).astype(BF16)
    vo = _window(w_ref, offs[5], offs[7]).astype(BF16)
    cw = 2 * LANES
    eye = (lax.broadcasted_iota(jnp.int32, (cw, cw), 0)
           == lax.broadcasted_iota(jnp.int32, (cw, cw), 1)).astype(BF16)
    for c in range(0, vo.shape[1], cw):
        feat_ref[c:c + cw, :] = _dot_nt(eye, vo[:, c:c + cw]).astype(BF16)


def _in_proj_prep(w_in, offs, n_gate):
    depth, d, cols = w_in.shape
    offs = tuple(int(o) for o in offs)
    small_cols = offs[2] + 2 * ROPE_DIM + LANES
    big_cols = (offs[5] - offs[3]) + (offs[10] - offs[8])
    feat_rows = offs[7] - offs[5]
    assert offs[2] % LANES == 0 and feat_rows % (2 * LANES) == 0 and big_cols % LANES == 0
    tr = 256
    return _call(
        functools.partial(_in_proj_prep_kernel, offs, n_gate), "in_proj_weight_prep", (depth, d // tr),
        [pl.BlockSpec((None, tr, cols), lambda l, i: (l, i, 0))], [w_in],
        [pl.BlockSpec((None, tr, small_cols), lambda l, i: (l, i, 0)),
         pl.BlockSpec((None, tr, big_cols), lambda l, i: (l, i, 0)),
         pl.BlockSpec((None, feat_rows, tr), lambda l, i: (l, 0, i))],
        [jax.ShapeDtypeStruct((depth, d, small_cols), BF16),
         jax.ShapeDtypeStruct((depth, d, big_cols), BF16),
         jax.ShapeDtypeStruct((depth, feat_rows, d), BF16)],
        ("parallel", "parallel"))


def _rope_swap_index():
    quarter = ROPE_DIM // 4
    idx = np.arange(ROPE_DIM).reshape(2, 2, quarter)
    return idx[:, ::-1, :].reshape(-1)


def _rope_tables(t):
    pos = jnp.arange(t)
    row = (pos // GRID_W).astype(F32)
    col = (pos % GRID_W).astype(F32)
    n_freq = ROPE_DIM // 4
    inv_freq = jnp.power(ROPE_BASE, -jnp.arange(n_freq, dtype=F32) / n_freq)
    ang_r = row[:, None] * inv_freq
    ang_c = col[:, None] * inv_freq
    cos = jnp.concatenate([jnp.cos(ang_r), jnp.cos(ang_r), jnp.cos(ang_c), jnp.cos(ang_c)], axis=-1)
    sin = jnp.concatenate([-jnp.sin(ang_r), jnp.sin(ang_r), -jnp.sin(ang_c), jnp.sin(ang_c)], axis=-1)
    return cos, sin


def kernel(x_prompt, x_sample, cache_ckv, cache_kpe, state_C, state_n, state_m, c, c_ctx, w_mod, b_mod, norm_w, ffn_w_gu, ffn_w_down, w_in, q_norm_w, kv_norm_w, w_uq, w_ukv, mlstm_gate_b, mlstm_norm_w, pool_w, pool_scale, w_branch, w_out, final_norm_w):
    batch, seq, d = x_prompt.shape
    dec_batch, dec_seq, _ = x_sample.shape
    depth = w_mod.shape[0]
    q_rank, kv_rank = q_norm_w.shape[1], kv_norm_w.shape[1]
    heads = MLA_HEADS
    mw = mlstm_norm_w.shape[1]
    dh = mw // MLSTM_HEADS
    pw = pool_scale.shape[1]
    ffn_h = ffn_w_down.shape[2]
    n_gate = N_DIR * 2 * MLSTM_HEADS
    assert mw == pw == w_branch.shape[2] == heads * V_DIM
    rows = _Rows(batch * seq, seq, dec_batch * dec_seq, dec_seq)
    n = rows.n

    hp = _round_up(ffn_h, 512)
    w_g, w_u = _gate_up_prep(ffn_w_gu, hp)
    wdn = _down_prep(ffn_w_down, hp)

    sizes = (q_rank, kv_rank, ROPE_DIM, mw, mw, mw, mw, n_gate, pw, N_BRANCH * d)
    offs = np.concatenate([[0], np.cumsum(sizes)])
    swap = _rope_swap_index()
    small_cols = q_rank + kv_rank + 2 * ROPE_DIM + LANES
    w_small, w_big, w_feat = _in_proj_prep(w_in, offs, n_gate)
    gate_blk = (q_rank + kv_rank + 2 * ROPE_DIM) // LANES
    gate_col0 = 3 * mw

    wq4 = w_uq.reshape(depth, q_rank, heads, QK_DIM)
    wq_pe = wq4[..., NOPE_DIM:]
    wq = jnp.concatenate([wq4[..., :NOPE_DIM].reshape(depth, q_rank, -1),
                          wq_pe.reshape(depth, q_rank, -1),
                          wq_pe[..., swap].reshape(depth, q_rank, -1)], axis=-1).astype(BF16)
    wkv4 = w_ukv.reshape(depth, kv_rank, heads, NOPE_DIM + V_DIM)
    wkv = jnp.concatenate([wkv4[..., :NOPE_DIM].reshape(depth, kv_rank, -1),
                           wkv4[..., NOPE_DIM:].reshape(depth, kv_rank, -1)], axis=-1).astype(BF16)
    wbr = w_branch.astype(BF16)
    wout = w_out.astype(BF16)
    pwb = pool_w.astype(BF16)
    norm_w4 = norm_w.reshape(depth, 3, 1, d)
    qnw = q_norm_w.reshape(depth, 1, q_rank)
    kvnw = kv_norm_w.reshape(depth, 1, kv_rank)
    mnw = mlstm_norm_w.reshape(depth, mw, 1)
    psc = pool_scale.reshape(depth, 1, pw)
    bands = _pool_bands()
    rope_tabs = _rope_tables(dec_seq)

    cond = jnp.concatenate([c_ctx[None, :], c, jnp.zeros((COND_ROWS - 1 - dec_batch, d), F32)], axis=0)
    mod = _mod_all(cond, w_mod, b_mod).reshape(depth, COND_ROWS, N_MOD, 1, d)

    cache_kv = _cache_kv(cache_ckv, cache_kpe, wkv)

    x = (x_prompt.reshape(rows.n_ctx, d), x_sample.reshape(rows.n_lat, d))
    new_cache = (None, None)
    new_state = (None, None, None)
    for l in range(depth):
        x, h_mix = _ffn(x, mod, norm_w4, w_g, w_u, wdn, rows, l, 0)

        p32 = _inproj(h_mix, w_small, rows, l, F32, small_cols)
        pb = _inproj(h_mix, w_big, rows, l, BF16, _lane_tile(w_big.shape[2], 1024))
        pb_t = _inproj_t(h_mix, w_feat, rows, l)
        gates_t = p32[:, gate_blk * LANES:gate_blk * LANES + n_gate].T

        q_c, k_c, v_c, *new_cache = _mla_prep(p32, qnw, kvnw, wq, wkv, l, 0, batch, seq, None, new_cache)
        q_s, k_s, v_s = _mla_prep(p32, qnw, kvnw, wq, wkv, l, rows.n_ctx, dec_batch, dec_seq, rope_tabs, None)
        y_a = _attention(q_c, k_c, v_c, None, l, n, 0, None)
        y_a = _attention(q_s, k_s, v_s, cache_kv, l, n, rows.n_ctx, y_a)

        hf_c, hb_c, *new_state = _mlstm(pb, pb_t, p32, gates_t, mlstm_gate_b, None, l, 0, batch, seq,
                                        gate_blk, dh, new_state)
        hf_s, hb_s, _, _, _ = _mlstm(pb, pb_t, p32, gates_t, mlstm_gate_b, (state_C, state_n, state_m), l,
                                     rows.n_ctx, dec_batch, dec_seq, gate_blk, dh, None)
        y_b = _mlstm_post(hf_c, hb_c, pb_t, mnw, l, n, 0, dh, None)
        y_b = _mlstm_post(hf_s, hb_s, pb_t, mnw, l, n, rows.n_ctx, dh, y_b)

        y_c = _pool(pb, bands, pwb, psc, l, n, 0, batch, seq, None)
        y_c = _pool(pb, bands, pwb, psc, l, n, rows.n_ctx, dec_batch, dec_seq, y_c)

        merged = _merge(y_a, y_b, y_c, pb, wbr, rows, l, gate_col0)
        x = _outproj(merged, wout, x, mod, rows, l)
        x = _ffn(x, mod, norm_w4, w_g, w_u, wdn, rows, l, 1)

    y_prompt = _final_norm(x, final_norm_w, 0, rows.n_ctx).reshape(batch, seq, d)
    y_sample = _final_norm(x, final_norm_w, rows.n_ctx, rows.n_lat).reshape(dec_batch, dec_seq, d)
    new_c, new_n, new_m = new_state
    return (y_prompt, y_sample, new_cache[0], new_cache[1], new_c,
            new_n.reshape(batch, depth, N_DIR, MLSTM_HEADS, dh),
            new_m.reshape(batch, depth, N_DIR, MLSTM_HEADS))
```

```python
import functools
import math

import numpy as np
import jax
import jax.numpy as jnp
from jax import lax
from jax.experimental import pallas as pl
from jax.experimental.pallas import tpu as pltpu

GRID_W = 64
EPS = 1e-6
N_MOD = 9
MLA_HEADS = 8
NOPE_DIM = 128
ROPE_DIM = 64
V_DIM = 128
QK_DIM = NOPE_DIM + ROPE_DIM
ROPE_BASE = 10000.0
MLSTM_HEADS = 4
N_DIR = 2
CHUNK = 128
POOL_WINDOWS = (2, 4, 8, 16)
POOL_GROUPS = 4
N_BRANCH = 3

LANES = 128
VMEM_LIMIT_MB = 56
COND_ROWS = 8
PACK_ROWS = 16
V_PAD = 2 * V_DIM

F32 = jnp.float32
BF16 = jnp.bfloat16


def _params(sem):
    return pltpu.CompilerParams(dimension_semantics=sem, vmem_limit_bytes=VMEM_LIMIT_MB << 20)


def _call(kernel, name, grid, in_specs, args, out_specs, out_shape, sem, bases=(), scratch=()):
    n_in = len(args)
    extra = [b for b in bases if b is not None]
    aliases = {}
    for k, b in enumerate(bases):
        if b is not None:
            aliases[n_in + len(aliases)] = k

    def body(*refs):
        kernel(*refs[:n_in], *refs[n_in + len(extra):])

    return pl.pallas_call(
        body if extra else kernel, grid=grid,
        in_specs=list(in_specs) + [pl.BlockSpec(memory_space=pl.ANY)] * len(extra),
        out_specs=out_specs, out_shape=out_shape, input_output_aliases=aliases,
        scratch_shapes=list(scratch), compiler_params=_params(sem), name=name)(*args, *extra)


def _round_up(n, m):
    return (n + m - 1) // m * m


def _lane_tile(n, cap):
    t = cap - cap % LANES
    while n % t:
        t -= LANES
    return t


def _sigmoid(x):
    return 1.0 / (1.0 + jnp.exp(-x))


def _log_sigmoid(x):
    return -(jnp.maximum(-x, 0.0) + jnp.log1p(jnp.exp(-jnp.abs(x))))


def _rms(x, w):
    return x * lax.rsqrt(jnp.mean(x * x, axis=-1, keepdims=True) + EPS) * w


def _dot(a, b):
    return jnp.dot(a, b, preferred_element_type=F32)


def _dot_nt(a, b):
    return lax.dot_general(a, b, (((1,), (1,)), ((), ())), preferred_element_type=F32)


def _mod_kernel(c_ref, w_ref, b_ref, o_ref):
    c = c_ref[...]
    a = (c * _sigmoid(c)).astype(BF16)
    o_ref[...] = _dot(a, w_ref[...].astype(BF16)) + b_ref[...]


def _mod_all(cond, w_mod, b_mod):
    depth, d, nd = w_mod.shape
    tn = _lane_tile(nd, 1024)
    return _call(
        _mod_kernel, "adaln_mod", (depth, nd // tn),
        [pl.BlockSpec((COND_ROWS, d), lambda l, j: (0, 0)),
         pl.BlockSpec((None, d, tn), lambda l, j: (l, 0, j)),
         pl.BlockSpec((None, 1, tn), lambda l, j: (l, 0, j))],
        [cond, w_mod, b_mod.reshape(depth, 1, nd)],
        pl.BlockSpec((None, COND_ROWS, tn), lambda l, j: (l, 0, j)),
        jax.ShapeDtypeStruct((depth, COND_ROWS, nd), F32),
        ("parallel", "parallel"))


class _Rows:
    def __init__(self, n_ctx, t_ctx, n_lat, t_lat):
        self.n_ctx, self.t_ctx, self.n_lat, self.t_lat = n_ctx, t_ctx, n_lat, t_lat
        self.n = n_ctx + n_lat

    def mod_row(self, row):
        return jnp.where(row < self.n_ctx, 0, 1 + (row - self.n_ctx) // self.t_lat)

    def tile(self, cap):
        t = min(cap, self.n_ctx, self.t_lat)
        assert self.n_ctx % t == 0 and self.t_lat % t == 0
        return t


def _mod_spec(rows, tm, layer, k, d):
    return pl.BlockSpec((None, None, None, 1, d),
                        lambda i, j: (layer, rows.mod_row(i * tm), k, 0, 0))


def _norm_mod_to(h_ref, x_ref, nw_ref, sh_ref, sc_ref):
    y = _rms(x_ref[...], nw_ref[...])
    h_ref[...] = (y * (1.0 + sc_ref[...]) + sh_ref[...]).astype(h_ref.dtype)


def _ffn_kernel(emit_next, ctx_tiles, *refs):
    x_refs, refs = (refs[:1], refs[1:]) if ctx_tiles is None else (refs[:2], refs[2:])
    sh_ref, sc_ref, g_ref, nw_ref, wg_ref, wu_ref, wd_ref = refs[:7]
    if emit_next:
        sh2_ref, sc2_ref, nw2_ref, o_ref, h2_ref, h_ref = refs[7:]
    else:
        o_ref, h_ref = refs[7:]
    i = pl.program_id(0)
    j = pl.program_id(1)

    def per_source(fn):
        if ctx_tiles is None:
            fn(x_refs[0])
        else:
            pl.when(i < ctx_tiles)(functools.partial(fn, x_refs[0]))
            pl.when(i >= ctx_tiles)(functools.partial(fn, x_refs[1]))

    def prologue(x_ref):
        _norm_mod_to(h_ref, x_ref, nw_ref, sh_ref, sc_ref)

    def epilogue(x_ref):
        o_ref[...] = x_ref[...] + 0.5 * g_ref[...] * o_ref[...]
        if emit_next:
            _norm_mod_to(h2_ref, o_ref, nw2_ref, sh2_ref, sc2_ref)

    @pl.when(j == 0)
    def _():
        per_source(prologue)
        o_ref[...] = jnp.zeros_like(o_ref)

    h = h_ref[...]
    g = _dot(h, wg_ref[...])
    u = _dot(h, wu_ref[...])
    a = (g * _sigmoid(g) * u).astype(BF16)
    o_ref[...] += _dot(a, wd_ref[...])

    @pl.when(j == pl.num_programs(1) - 1)
    def _():
        per_source(epilogue)


def _ffn(x, mod, norm_w, w_g, w_u, w_down, rows, layer, which):
    tm = rows.tile(512)
    if isinstance(x, tuple):
        ctx_tiles = rows.n_ctx // tm
        x_args = list(x)
        x_specs = [pl.BlockSpec((tm, x[0].shape[1]), lambda i, j: (jnp.minimum(i, ctx_tiles - 1), 0)),
                   pl.BlockSpec((tm, x[0].shape[1]), lambda i, j: (jnp.maximum(i - ctx_tiles, 0), 0))]
    else:
        ctx_tiles = None
        x_args = [x]
        x_specs = [pl.BlockSpec((tm, x.shape[1]), lambda i, j: (i, 0))]
    n, d = rows.n, x_args[0].shape[1]
    hp = w_down.shape[2]
    th = _lane_tile(hp, 512)
    k0 = 0 if which == 0 else 6
    emit_next = which == 0
    w_spec = pl.BlockSpec((None, None, d, th), lambda i, j: (layer, which, 0, j))
    row_spec = pl.BlockSpec((tm, d), lambda i, j: (i, 0))
    norm_spec = lambda k: pl.BlockSpec((None, None, 1, d), lambda i, j: (layer, k, 0, 0))
    in_specs = x_specs + [
        _mod_spec(rows, tm, layer, k0, d),
        _mod_spec(rows, tm, layer, k0 + 1, d),
        _mod_spec(rows, tm, layer, k0 + 2, d),
        norm_spec(2 * which), w_spec, w_spec,
        pl.BlockSpec((None, None, th, d), lambda i, j: (layer, which, j, 0))]
    args = x_args + [mod, mod, mod, norm_w, w_g, w_u, w_down]
    out_specs, out_shape = row_spec, jax.ShapeDtypeStruct((n, d), F32)
    if emit_next:
        in_specs += [_mod_spec(rows, tm, layer, 3, d), _mod_spec(rows, tm, layer, 4, d), norm_spec(1)]
        args += [mod, mod, norm_w]
        out_specs, out_shape = [row_spec, row_spec], [out_shape, jax.ShapeDtypeStruct((n, d), BF16)]
    return _call(
        functools.partial(_ffn_kernel, emit_next, ctx_tiles), "ffn_half_step", (n // tm, hp // th),
        in_specs, args, out_specs, out_shape, ("parallel", "arbitrary"),
        scratch=[pltpu.VMEM((tm, d), BF16)])


def _inproj_kernel(h_ref, w_ref, o_ref):
    o_ref[...] = _dot(h_ref[...], w_ref[...]).astype(o_ref.dtype)


def _inproj(h, w, rows, layer, out_dtype, tn):
    n, d = h.shape
    cols = w.shape[2]
    tm = rows.tile(1024)
    return _call(
        _inproj_kernel, "mixer_in_proj", (n // tm, cols // tn),
        [pl.BlockSpec((tm, d), lambda i, j: (i, 0)),
         pl.BlockSpec((None, d, tn), lambda i, j: (layer, 0, j))],
        [h, w],
        pl.BlockSpec((tm, tn), lambda i, j: (i, j)),
        jax.ShapeDtypeStruct((n, cols), out_dtype),
        ("parallel", "arbitrary"))


def _inproj_t_kernel(h_ref, wt_ref, o_ref):
    o_ref[...] = _dot_nt(wt_ref[...], h_ref[...]).astype(o_ref.dtype)


def _inproj_t(h, w_t, rows, layer):
    n, d = h.shape
    cols = w_t.shape[1]
    tm = rows.tile(1024)
    tn = _lane_tile(cols, 1024)
    return _call(
        _inproj_t_kernel, "mixer_in_proj_t", (n // tm, cols // tn),
        [pl.BlockSpec((tm, d), lambda i, j: (i, 0)),
         pl.BlockSpec((None, tn, d), lambda i, j: (layer, j, 0))],
        [h, w_t],
        pl.BlockSpec((tn, tm), lambda i, j: (j, i)),
        jax.ShapeDtypeStruct((cols, n), BF16),
        ("parallel", "arbitrary"))


def _ones_column(rows):
    lane = lax.broadcasted_iota(jnp.int32, (rows, V_PAD - V_DIM), 1)
    return (lane == 0).astype(BF16)


def _mla_prep_kernel(rope, q_rank, kv_rank, p_ref, qnw_ref, kvnw_ref, wq_ref, wkv_ref, *rest):
    if rope:
        cos_ref, sin_ref, q_ref, k_ref, v_ref = rest
    else:
        q_ref, k_ref, v_ref, ckv_ref, kpe_ref = rest
    heads = MLA_HEADS
    p = p_ref[...]
    c_q = p[:, :q_rank]
    c_kv = p[:, q_rank:q_rank + kv_rank]
    o = q_rank + kv_rank
    k_pe = p[:, o:o + ROPE_DIM]
    k_pe_sw = p[:, o + ROPE_DIM:o + 2 * ROPE_DIM]

    qa = _dot(_rms(c_q, qnw_ref[...]).astype(BF16), wq_ref[...])
    ckv_n = _rms(c_kv, kvnw_ref[...])
    kv = _dot(ckv_n.astype(BF16), wkv_ref[...])
    scale = QK_DIM ** -0.5 * math.log2(math.e)
    if rope:
        cos = cos_ref[...]
        sin = sin_ref[...]
        k_pe = k_pe * cos + k_pe_sw * sin
    else:
        ckv_ref[...] = ckv_n
        kpe_ref[...] = k_pe
    ones = _ones_column(p.shape[0])
    pe0 = heads * NOPE_DIM
    sw0 = pe0 + heads * ROPE_DIM
    for h in range(heads):
        q_pe = qa[:, pe0 + h * ROPE_DIM:pe0 + (h + 1) * ROPE_DIM]
        if rope:
            q_pe = q_pe * cos + qa[:, sw0 + h * ROPE_DIM:sw0 + (h + 1) * ROPE_DIM] * sin
        q_ref[h, :, :NOPE_DIM] = (qa[:, h * NOPE_DIM:(h + 1) * NOPE_DIM] * scale).astype(BF16)
        q_ref[h, :, NOPE_DIM:] = (q_pe * scale).astype(BF16)
        k_ref[h, :, :NOPE_DIM] = kv[:, h * NOPE_DIM:(h + 1) * NOPE_DIM].astype(BF16)
        k_ref[h, :, NOPE_DIM:] = k_pe.astype(BF16)
        v0 = heads * NOPE_DIM + h * V_DIM
        v_ref[h, :, :V_DIM] = kv[:, v0:v0 + V_DIM].astype(BF16)
        v_ref[h, :, V_DIM:] = ones


def _mla_prep(p32, q_norm_w, kv_norm_w, wq, wkv, layer, row0, batch, t, rope_tabs, bases):
    ws = p32.shape[1]
    q_rank, kv_rank = q_norm_w.shape[-1], kv_norm_w.shape[-1]
    depth = q_norm_w.shape[0]
    tm = min(256, t)
    nt = t // tm
    heads = MLA_HEADS
    rope = rope_tabs is not None
    in_specs = [pl.BlockSpec((tm, ws), lambda b, i: (row0 // tm + b * nt + i, 0)),
                pl.BlockSpec((None, 1, q_rank), lambda b, i: (layer, 0, 0)),
                pl.BlockSpec((None, 1, kv_rank), lambda b, i: (layer, 0, 0)),
                pl.BlockSpec((None,) + wq.shape[1:], lambda b, i: (layer, 0, 0)),
                pl.BlockSpec((None,) + wkv.shape[1:], lambda b, i: (layer, 0, 0))]
    args = [p32, q_norm_w, kv_norm_w, wq, wkv]
    head_spec = lambda width: pl.BlockSpec((None, heads, tm, width), lambda b, i: (b, 0, i, 0))
    out_specs = [head_spec(QK_DIM), head_spec(QK_DIM), head_spec(V_PAD)]
    out_shape = [jax.ShapeDtypeStruct((batch, heads, t, QK_DIM), BF16),
                 jax.ShapeDtypeStruct((batch, heads, t, QK_DIM), BF16),
                 jax.ShapeDtypeStruct((batch, heads, t, V_PAD), BF16)]
    if rope:
        in_specs += [pl.BlockSpec((tm, ROPE_DIM), lambda b, i: (i, 0))] * 2
        args += list(rope_tabs)
        all_bases = ()
    else:
        out_specs += [pl.BlockSpec((None, None, tm, kv_rank), lambda b, i: (b, layer, i, 0)),
                      pl.BlockSpec((None, None, tm, ROPE_DIM), lambda b, i: (b, layer, i, 0))]
        out_shape += [jax.ShapeDtypeStruct((batch, depth, t, kv_rank), F32),
                      jax.ShapeDtypeStruct((batch, depth, t, ROPE_DIM), F32)]
        all_bases = (None, None, None) + tuple(bases)
    return _call(functools.partial(_mla_prep_kernel, rope, q_rank, kv_rank), "mla_prep",
                 (batch, nt), in_specs, args, out_specs, out_shape, ("parallel", "parallel"),
                 bases=all_bases)


def _cache_kv_kernel(ckv_ref, kpe_ref, wkv_ref, k_ref, v_ref):
    heads = MLA_HEADS
    kv = _dot(ckv_ref[...].astype(BF16), wkv_ref[...])
    k_pe = kpe_ref[...].astype(BF16)
    ones = _ones_column(kv.shape[0])
    for h in range(heads):
        k_ref[h, :, :NOPE_DIM] = kv[:, h * NOPE_DIM:(h + 1) * NOPE_DIM].astype(BF16)
        k_ref[h, :, NOPE_DIM:] = k_pe
        v0 = heads * NOPE_DIM + h * V_DIM
        v_ref[h, :, :V_DIM] = kv[:, v0:v0 + V_DIM].astype(BF16)
        v_ref[h, :, V_DIM:] = ones


def _cache_kv(cache_ckv, cache_kpe, wkv):
    batch, depth, past, kv_rank = cache_ckv.shape
    heads = MLA_HEADS
    return _call(
        _cache_kv_kernel, "mla_cache_kv", (batch, depth),
        [pl.BlockSpec((None, None, past, kv_rank), lambda b, l: (b, l, 0, 0)),
         pl.BlockSpec((None, None, past, ROPE_DIM), lambda b, l: (b, l, 0, 0)),
         pl.BlockSpec((None,) + wkv.shape[1:], lambda b, l: (l, 0, 0))],
        [cache_ckv, cache_kpe, wkv],
        [pl.BlockSpec((None, None, heads, past, QK_DIM), lambda b, l: (b, l, 0, 0, 0)),
         pl.BlockSpec((None, None, heads, past, V_PAD), lambda b, l: (b, l, 0, 0, 0))],
        [jax.ShapeDtypeStruct((batch, depth, heads, past, QK_DIM), BF16),
         jax.ShapeDtypeStruct((batch, depth, heads, past, V_PAD), BF16)],
        ("parallel", "parallel"))


ATTN_KEY_CHUNK = 512


def _attn_kernel(past, q_ref, qn_ref, k_ref, kn_ref, v_ref, *rest):
    if past:
        kc_ref, kcn_ref, vc_ref = rest[:3]
        rest = rest[3:]
    else:
        kc_ref = kcn_ref = vc_ref = None
    o_ref, s0_ref, s1_ref, m0_ref, m1_ref = rest
    tq = qn_ref.shape[0]
    t = k_ref.shape[0]
    chunk = min(ATTN_KEY_CHUNK, t)

    def scores(q, keys_ref, cache_keys_ref, s_ref, m_ref):
        s = _dot_nt(q, keys_ref[...])
        m = jnp.max(s, axis=-1, keepdims=True)
        if past:
            sc = _dot_nt(q, cache_keys_ref[...])
            m = jnp.maximum(m, jnp.max(sc, axis=-1, keepdims=True))
            s_ref[:, :past] = sc
        s_ref[:, past:] = s
        m_ref[...] = m

    def values(r, s_ref, m_ref):
        m = m_ref[...]
        acc = None
        if past:
            acc = _dot(jnp.exp2(s_ref[:, :past] - m).astype(BF16), vc_ref[...])
        for c in range(0, t, chunk):
            p = jnp.exp2(s_ref[:, past + c:past + c + chunk] - m).astype(BF16)
            d = _dot(p, v_ref[c:c + chunk, :])
            acc = d if acc is None else acc + d
        o_ref[r * tq:(r + 1) * tq, :] = (acc[:, :V_DIM] / acc[:, V_DIM:V_DIM + 1]).astype(o_ref.dtype)

    @pl.when(pl.program_id(0) == 0)
    def _():
        scores(q_ref[:tq, :], k_ref, kc_ref, s0_ref, m0_ref)

    values(0, s0_ref, m0_ref)
    scores(q_ref[tq:, :], k_ref, kc_ref, s1_ref, m1_ref)
    values(1, s1_ref, m1_ref)
    scores(qn_ref[...], kn_ref, kcn_ref, s0_ref, m0_ref)


def _attn_short_kernel(q_ref, k_ref, v_ref, o_ref):
    heads = range(q_ref.shape[0])
    s = [_dot_nt(q_ref[h], k_ref[h]) for h in heads]
    p = [jnp.exp2(x - jnp.max(x, axis=-1, keepdims=True)).astype(BF16) for x in s]
    o = [_dot(p[h], v_ref[h]) for h in heads]
    for h in heads:
        o_ref[:, h * V_DIM:(h + 1) * V_DIM] = (o[h][:, :V_DIM] / o[h][:, V_DIM:V_DIM + 1]).astype(o_ref.dtype)


ATTN_SHORT_SEQ = 256


def _attention(q, k, v, cache, layer, n_rows, row0, base):
    batch, heads, t, _ = q.shape
    if cache is None and t <= ATTN_SHORT_SEQ:
        assert row0 % t == 0
        head_block = lambda width: pl.BlockSpec((None, heads, t, width), lambda b: (b, 0, 0, 0))
        return _call(
            _attn_short_kernel, "mla_attention_short", (batch,),
            [head_block(QK_DIM), head_block(QK_DIM), head_block(V_PAD)], [q, k, v],
            pl.BlockSpec((t, heads * V_DIM), lambda b: (row0 // t + b, 0)),
            jax.ShapeDtypeStruct((n_rows, heads * V_DIM), BF16), ("parallel",), bases=(base,))
    tq = min(512, t // 2)
    pair = 2 * tq
    npair = t // pair
    n_steps = batch * heads * npair
    past = cache[0].shape[3] if cache is not None else 0
    assert t % pair == 0 and row0 % pair == 0

    def where(tile):
        tile = jnp.minimum(tile, 2 * n_steps - 1)
        p = tile // 2
        return p // (heads * npair), (p // npair) % heads, 2 * (p % npair) + tile % 2

    def pair_map(g):
        b, h, i = where(2 * g)
        return b, h, i // 2, 0

    def next_map(g):
        b, h, i = where(2 * g + 2)
        return b, h, i, 0

    def kv_map(shift):
        def index(g):
            b, h, _ = where(2 * g + shift)
            return b, h, 0, 0
        return index

    def cache_map(shift):
        def index(g):
            b, h, _ = where(2 * g + shift)
            return b, layer, h, 0, 0
        return index

    def o_map(g):
        b, h, i = where(2 * g)
        return row0 // pair + b * npair + i // 2, h

    in_specs = [pl.BlockSpec((None, None, pair, QK_DIM), pair_map),
                pl.BlockSpec((None, None, tq, QK_DIM), next_map),
                pl.BlockSpec((None, None, t, QK_DIM), kv_map(0)),
                pl.BlockSpec((None, None, t, QK_DIM), kv_map(2)),
                pl.BlockSpec((None, None, t, V_PAD), kv_map(0))]
    args = [q, q, k, k, v]
    if past:
        in_specs += [pl.BlockSpec((None, None, None, past, QK_DIM), cache_map(0)),
                     pl.BlockSpec((None, None, None, past, QK_DIM), cache_map(2)),
                     pl.BlockSpec((None, None, None, past, V_PAD), cache_map(0))]
        args += [cache[0], cache[0], cache[1]]
    s_total = past + t
    return _call(
        functools.partial(_attn_kernel, past), "mla_attention", (n_steps,),
        in_specs, args,
        pl.BlockSpec((pair, V_DIM), o_map),
        jax.ShapeDtypeStruct((n_rows, heads * V_DIM), BF16),
        ("arbitrary",), bases=(base,),
        scratch=[pltpu.VMEM((tq, s_total), F32), pltpu.VMEM((tq, s_total), F32),
                 pltpu.VMEM((tq, 1), F32), pltpu.VMEM((tq, 1), F32)])


MLSTM_CHUNKS_PER_STEP = 2


def _split3(x):
    hi = x.astype(BF16)
    r = x - hi.astype(F32)
    mid = r.astype(BF16)
    return hi, mid, (r - mid.astype(F32)).astype(BF16)


def _mlstm_kernel(has_init, dh, n_sub, *refs):
    (qf_ref, kf_ref, vf_ref, qb_ref, kb_ref, vb_ref, gf_ref, gb_ref, gtf_ref, gtb_ref,
     brow_ref, bcol_ref) = refs[:12]
    refs = refs[12:]
    if has_init:
        c0_ref, n0_ref, m0_ref = refs[:3]
        refs = refs[3:]
    hf_ref, hb_ref, c_ref, n_ref, m_ref = refs
    heads = MLSTM_HEADS
    n_gate = N_DIR * 2 * heads
    step = pl.program_id(1)

    @pl.when(step == 0)
    def _():
        if has_init:
            c_ref[...] = c0_ref[...]
            n_ref[...] = n0_ref[...]
            m_ref[...] = m0_ref[...]
        else:
            c_ref[...] = jnp.zeros_like(c_ref)
            n_ref[...] = jnp.zeros_like(n_ref)
            m_ref[...] = jnp.zeros_like(m_ref)

    tok0 = lax.broadcasted_iota(jnp.int32, (CHUNK, CHUNK), 0)
    tok1 = lax.broadcasted_iota(jnp.int32, (CHUNK, CHUNK), 1)
    k_scale = dh ** -0.5
    m_all = m_ref[...]
    m_out = m_all
    unit_lane = lax.broadcasted_iota(jnp.int32, m_all.shape, 1)
    units = [(d, h) for d in range(N_DIR) for h in range(heads)]
    for sub in range(n_sub):
        m_all = m_out
        gate = {}
        for d in range(N_DIR):
            g_ref, gt_ref = (gf_ref, gtf_ref) if d == 0 else (gb_ref, gtb_ref)
            chunk_idx = sub if d == 0 else n_sub - 1 - sub
            tok = slice(chunk_idx * CHUNK, (chunk_idx + 1) * CHUNK)
            seen_t = (tok0 <= tok1) if d == 0 else (tok0 >= tok1)
            seen_t_bf = seen_t.astype(BF16)
            seen_bf = ((tok1 <= tok0) if d == 0 else (tok1 >= tok0)).astype(BF16)
            pre_col = g_ref[tok, :n_gate] + brow_ref[...]
            pre_row = gt_ref[:, tok] + bcol_ref[...]
            cum_col = sum(_dot(seen_bf, part) for part in _split3(_log_sigmoid(pre_col)))
            cum_row = sum(_dot(part, seen_t_bf) for part in _split3(_log_sigmoid(pre_row)))
            gate[d] = (tok, seen_t, pre_col, pre_row, cum_col, cum_row)

        st = {}
        for d, h in units:
            tok, seen_t, pre_col, pre_row, cum_col, cum_row = gate[d]
            q_ref, k_ref, vt_ref = (qf_ref, kf_ref, vf_ref) if d == 0 else (qb_ref, kb_ref, vb_ref)
            ci = d * 2 * heads + h
            cf = ci + heads
            sid = d * heads + h
            sl = slice(h * dh, (h + 1) * dh)
            last = CHUNK - 1 if d == 0 else 0
            c_col = pre_col[:, ci:ci + 1] - cum_col[:, cf:cf + 1]
            i_row = pre_row[ci:ci + 1, :]
            b_row = cum_row[cf:cf + 1, :]
            b_end = b_row[:, last:last + 1]
            m_prev = m_all[:, sid:sid + 1]
            a_row = b_row + m_prev
            dmat = jnp.where(seen_t, b_row + c_col, -jnp.inf)
            m_t = jnp.maximum(a_row, jnp.max(dmat, axis=0, keepdims=True))
            q = q_ref[tok, sl]
            k_bf = (k_ref[tok, sl].astype(F32) * k_scale).astype(BF16)
            st[d, h] = dict(tok=tok, sl=sl, sid=sid, i_row=i_row, b_row=b_row, b_end=b_end, m_prev=m_prev,
                            m_t=m_t, w_intra=jnp.exp(dmat - m_t), w_inter=jnp.exp(a_row - m_t),
                            q=q, k_bf=k_bf, v_t=vt_ref[sl, tok], c_prev=c_ref[d, h], n_prev=n_ref[d, h])

        for u in units:
            x = st[u]
            x["s_t"] = _dot_nt(x["k_bf"], x["q"]) * x["w_intra"]
            n_rows = jnp.broadcast_to(x["n_prev"], (PACK_ROWS, dh)).astype(BF16)
            x["cq"] = _dot_nt(jnp.concatenate([x["c_prev"].astype(BF16), n_rows], axis=0), x["q"])

        for d, h in units:
            x = st[d, h]
            h_ref = hf_ref if d == 0 else hb_ref
            num = x["w_inter"] * x["cq"][:dh, :] + _dot(x["v_t"], x["s_t"].astype(BF16))
            den = x["w_inter"] * x["cq"][dh:dh + 1, :] + jnp.sum(x["s_t"], axis=0, keepdims=True)
            h_ref[x["sl"], x["tok"]] = num / jnp.maximum(jnp.abs(den), jnp.exp(-x["m_t"]))

        for d, h in units:
            x = st[d, h]
            g_row = x["b_end"] - x["b_row"] + x["i_row"]
            m_new = jnp.maximum(x["b_end"] + x["m_prev"], jnp.max(g_row, axis=1, keepdims=True))
            w_pos = jnp.exp(g_row - m_new)
            w_carry = jnp.exp(x["b_end"] + x["m_prev"] - m_new)
            w_rows = jnp.broadcast_to(w_pos, (PACK_ROWS, CHUNK)).astype(BF16)
            upd = _dot(jnp.concatenate([(x["v_t"].astype(F32) * w_pos).astype(BF16), w_rows], axis=0), x["k_bf"])
            c_ref[d, h] = w_carry * x["c_prev"] + upd[:dh, :]
            n_ref[d, h] = w_carry * x["n_prev"] + upd[dh:dh + 1, :]
            m_out = jnp.where(unit_lane == x["sid"], m_new, m_out)
    m_ref[...] = m_out


def _mlstm(pb, pb_t, p32, gates_t, gate_b, state, layer, row0, batch, t, gate_blk, dh, bases):
    heads = MLSTM_HEADS
    n_gate = N_DIR * 2 * heads
    depth = gate_b.shape[0]
    n_sub = MLSTM_CHUNKS_PER_STEP if (t // CHUNK) % MLSTM_CHUNKS_PER_STEP == 0 else 1
    span = n_sub * CHUNK
    nc = t // span
    assert row0 % span == 0
    blk0 = row0 // span
    w = heads * dh
    fwd = lambda b, c: blk0 + b * nc + c
    bwd = lambda b, c: blk0 + b * nc + nc - 1 - c

    def tok(col, blk):
        return pl.BlockSpec((span, w), lambda b, c: (blk(b, c), col))

    def feat(blk):
        return pl.BlockSpec((w, span), lambda b, c: (0, blk(b, c)))

    in_specs = [tok(0, fwd), tok(1, fwd), feat(fwd), tok(0, bwd), tok(1, bwd), feat(bwd),
                pl.BlockSpec((span, LANES), lambda b, c: (fwd(b, c), gate_blk)),
                pl.BlockSpec((span, LANES), lambda b, c: (bwd(b, c), gate_blk)),
                pl.BlockSpec((n_gate, span), lambda b, c: (0, fwd(b, c))),
                pl.BlockSpec((n_gate, span), lambda b, c: (0, bwd(b, c))),
                pl.BlockSpec((None, 1, n_gate), lambda b, c: (layer, 0, 0)),
                pl.BlockSpec((None, n_gate, 1), lambda b, c: (layer, 0, 0))]
    args = [pb, pb, pb_t, pb, pb, pb_t, p32, p32, gates_t, gates_t,
            gate_b.reshape(-1, 1, n_gate), gate_b.reshape(-1, n_gate, 1)]
    has_init = state is not None
    state_shapes = [(N_DIR, heads, dh, dh), (N_DIR, heads, 1, dh), (1, N_DIR * heads)]
    if has_init:
        c0, n0, m0 = state
        in_specs += [pl.BlockSpec((None, None) + shp, lambda b, c, z=(0,) * len(shp): (b, layer) + z)
                     for shp in state_shapes]
        args += [c0, n0.reshape((batch, depth) + state_shapes[1]), m0.reshape((batch, depth) + state_shapes[2])]
        st_specs = [pl.BlockSpec((None,) + shp, lambda b, c, z=(0,) * len(shp): (b,) + z) for shp in state_shapes]
        st_shapes = [jax.ShapeDtypeStruct((batch,) + shp, F32) for shp in state_shapes]
        all_bases = ()
    else:
        st_specs = [pl.BlockSpec((None, None) + shp, lambda b, c, z=(0,) * len(shp): (b, layer) + z)
                    for shp in state_shapes]
        st_shapes = [jax.ShapeDtypeStruct((batch, depth) + shp, F32) for shp in state_shapes]
        all_bases = (None, None) + tuple(bases)
    return _call(
        functools.partial(_mlstm_kernel, has_init, dh, n_sub), "mlstm_scan", (batch, nc), in_specs, args,
        [pl.BlockSpec((w, span), lambda b, c: (0, b * nc + c)),
         pl.BlockSpec((w, span), lambda b, c: (0, b * nc + nc - 1 - c))] + st_specs,
        [jax.ShapeDtypeStruct((w, batch * t), F32),
         jax.ShapeDtypeStruct((w, batch * t), F32)] + st_shapes,
        ("parallel", "arbitrary"), bases=all_bases)


def _mlstm_post_kernel(dh, hf_ref, hb_ref, o_ref, w_ref, y_ref):
    tm = hf_ref.shape[1]
    hm = hf_ref[...] + hb_ref[...]
    gate = _sigmoid(o_ref[...].astype(F32))
    w = w_ref[...]
    eye = (lax.broadcasted_iota(jnp.int32, (tm, tm), 0)
           == lax.broadcasted_iota(jnp.int32, (tm, tm), 1)).astype(BF16)
    for h in range(MLSTM_HEADS):
        sl = slice(h * dh, (h + 1) * dh)
        x = hm[sl, :]
        y = x * lax.rsqrt(jnp.mean(x * x, axis=0, keepdims=True) + EPS) * w[sl, :]
        y_t = (gate[sl, :] * y).astype(BF16)
        y_ref[:, sl] = _dot_nt(eye, y_t).astype(y_ref.dtype)


def _mlstm_post(h_f, h_b, pb_t, m_norm_w, layer, n_rows, row0, dh, base):
    w, n = h_f.shape
    tm = min(256, n)
    return _call(
        functools.partial(_mlstm_post_kernel, dh), "mlstm_post", (n // tm,),
        [pl.BlockSpec((w, tm), lambda i: (0, i)),
         pl.BlockSpec((w, tm), lambda i: (0, i)),
         pl.BlockSpec((w, tm), lambda i: (1, row0 // tm + i)),
         pl.BlockSpec((None, w, 1), lambda i: (layer, 0, 0))],
        [h_f, h_b, pb_t, m_norm_w],
        pl.BlockSpec((tm, w), lambda i: (row0 // tm + i, 0)),
        jax.ShapeDtypeStruct((n_rows, w), BF16),
        ("parallel",), bases=(base,))


POOL_TILE = 256


def _pool_bands():
    t = np.arange(POOL_TILE)[:, None]
    bands = np.zeros((POOL_GROUPS, 3, POOL_TILE, POOL_TILE), np.float32)
    for g, win in enumerate(POOL_WINDOWS):
        for part in range(3):
            s = np.arange(POOL_TILE)[None, :] + (part - 1) * POOL_TILE
            bands[g, part] = (s >= t - win // 2) & (s < t - win // 2 + win)
    return jnp.asarray(bands, BF16)


def _pool_kernel(t_seq, gd, up_ref, um_ref, un_ref, band_ref, pw_ref, ps_ref, y_ref):
    j = pl.program_id(1)
    has_prev = (j > 0).astype(F32)
    has_next = (j < pl.num_programs(1) - 1).astype(F32)
    tile = um_ref.shape[0]
    pos = j * tile + lax.broadcasted_iota(jnp.int32, (tile, 1), 0)
    groups = range(len(POOL_WINDOWS))
    cols = [slice(g * gd, (g + 1) * gd) for g in groups]
    acc = [_dot(band_ref[g, 1], um_ref[:, cols[g]])
           + has_prev * _dot(band_ref[g, 0], up_ref[:, cols[g]])
           + has_next * _dot(band_ref[g, 2], un_ref[:, cols[g]]) for g in groups]
    pooled = []
    for g, win in enumerate(POOL_WINDOWS):
        lo = jnp.clip(pos - win // 2, 0, t_seq)
        hi = jnp.clip(pos - win // 2 + win, 0, t_seq)
        pooled.append((acc[g] / (hi - lo).astype(F32) - um_ref[:, cols[g]].astype(F32)).astype(BF16))
    y = [_dot(pooled[g], pw_ref[g]) * ps_ref[:, cols[g]] for g in groups]
    for g in groups:
        y_ref[:, cols[g]] = y[g].astype(y_ref.dtype)


def _pool(pb, bands, pool_w, pool_scale, layer, n_rows, row0, batch, t, base):
    gd = pool_w.shape[-1]
    w = POOL_GROUPS * gd
    tile = POOL_TILE
    assert t % tile == 0
    nt = t // tile
    blk0 = row0 // tile

    def u_spec(shift):
        return pl.BlockSpec((tile, w), lambda b, j: (blk0 + b * nt + jnp.clip(j + shift, 0, nt - 1), 2))

    return _call(
        functools.partial(_pool_kernel, t, gd), "multiscale_pool", (batch, nt),
        [u_spec(-1), u_spec(0), u_spec(1),
         pl.BlockSpec(bands.shape, lambda b, j: (0, 0, 0, 0)),
         pl.BlockSpec((None, POOL_GROUPS, gd, gd), lambda b, j: (layer, 0, 0, 0)),
         pl.BlockSpec((None, 1, w), lambda b, j: (layer, 0, 0))],
        [pb, pb, pb, bands, pool_w, pool_scale],
        pl.BlockSpec((tile, w), lambda b, j: (blk0 + b * nt + j, 0)),
        jax.ShapeDtypeStruct((n_rows, w), BF16),
        ("parallel", "parallel"), bases=(base,))


def _merge_kernel(ya_ref, yb_ref, yc_ref, ga_ref, gb_ref, gc_ref, w_ref, o_ref):
    branch = [_dot(y_ref[...], w_ref[k]) for k, y_ref in enumerate((ya_ref, yb_ref, yc_ref))]
    gates = [_sigmoid(g_ref[...].astype(F32)) for g_ref in (ga_ref, gb_ref, gc_ref)]
    o_ref[...] = (gates[0] * branch[0] + gates[1] * branch[1] + gates[2] * branch[2]).astype(o_ref.dtype)


def _merge(y_a, y_b, y_c, pb, w_branch, rows, layer, gate_col0):
    n, bw = y_a.shape
    d = w_branch.shape[-1]
    tm = rows.tile(1024)
    tn = min(1024, d)
    g0 = gate_col0 // tn
    nd = d // tn

    def gate_spec(k):
        return pl.BlockSpec((tm, tn), lambda i, j: (i, g0 + k * nd + j))

    y_spec = pl.BlockSpec((tm, bw), lambda i, j: (i, 0))
    return _call(
        _merge_kernel, "branch_merge", (n // tm, nd),
        [y_spec, y_spec, y_spec, gate_spec(0), gate_spec(1), gate_spec(2),
         pl.BlockSpec((None, N_BRANCH, bw, tn), lambda i, j: (layer, 0, 0, j))],
        [y_a, y_b, y_c, pb, pb, pb, w_branch],
        pl.BlockSpec((tm, tn), lambda i, j: (i, j)),
        jax.ShapeDtypeStruct((n, d), BF16),
        ("parallel", "arbitrary"))


def _outproj_kernel(m_ref, w_ref, x_ref, g_ref, o_ref):
    o_ref[...] = x_ref[...] + g_ref[...] * _dot(m_ref[...], w_ref[...])


def _outproj(merged, w_out, x, mod, rows, layer):
    n, d = x.shape
    tm = rows.tile(1024)
    tn = min(1024, d)
    return _call(
        _outproj_kernel, "mixer_out_proj", (n // tm, d // tn),
        [pl.BlockSpec((tm, d), lambda i, j: (i, 0)),
         pl.BlockSpec((None, d, tn), lambda i, j: (layer, 0, j)),
         pl.BlockSpec((tm, tn), lambda i, j: (i, j)),
         pl.BlockSpec((None, None, None, 1, tn),
                      lambda i, j: (layer, rows.mod_row(i * tm), 5, 0, j))],
        [merged, w_out, x, mod],
        pl.BlockSpec((tm, tn), lambda i, j: (i, j)),
        jax.ShapeDtypeStruct((n, d), F32),
        ("parallel", "arbitrary"))


def _final_norm_kernel(x_ref, w_ref, o_ref):
    o_ref[...] = _rms(x_ref[...], w_ref[...])


def _final_norm(x, w, row0, n_rows):
    d = x.shape[1]
    tm = min(512, n_rows)
    return _call(
        _final_norm_kernel, "final_norm", (n_rows // tm,),
        [pl.BlockSpec((tm, d), lambda i: (row0 // tm + i, 0)),
         pl.BlockSpec((1, d), lambda i: (0, 0))],
        [x, w.reshape(1, d)],
        pl.BlockSpec((tm, d), lambda i: (i, 0)),
        jax.ShapeDtypeStruct((n_rows, d), F32),
        ("parallel",))


def _gate_up_prep_kernel(valid, g_ref, u_ref, og_ref, ou_ref):
    for src, dst in ((g_ref, og_ref), (u_ref, ou_ref)):
        dst[:, :valid] = src[...].astype(BF16)
        if dst.shape[1] > valid:
            dst[:, valid:] = jnp.zeros((dst.shape[0], dst.shape[1] - valid), BF16)


def _gate_up_prep(w_gu, hp):
    depth, n_ffn, d, h2 = w_gu.shape
    h = h2 // 2
    assert h % LANES == 0
    n_rows = depth * n_ffn * d
    tr = 256
    flat = w_gu.reshape(n_rows, h2)
    out = jax.ShapeDtypeStruct((n_rows, hp), BF16)
    w_g, w_u = _call(
        functools.partial(_gate_up_prep_kernel, h), "ffn_gate_up_prep", (n_rows // tr,),
        [pl.BlockSpec((tr, h), lambda r: (r, 0)), pl.BlockSpec((tr, h), lambda r: (r, 1))],
        [flat, flat],
        [pl.BlockSpec((tr, hp), lambda r: (r, 0))] * 2, [out, out], ("parallel",))
    return w_g.reshape(depth, n_ffn, d, hp), w_u.reshape(depth, n_ffn, d, hp)


def _down_prep_kernel(valid, w_ref, o_ref):
    o_ref[:valid, :] = w_ref[...].astype(BF16)
    if o_ref.shape[0] > valid:
        o_ref[valid:, :] = jnp.zeros((o_ref.shape[0] - valid, o_ref.shape[1]), BF16)


def _down_prep(w_down, hp):
    depth, n_ffn, h, d = w_down.shape
    assert h % PACK_ROWS == 0
    td = _lane_tile(d, 256)
    flat = w_down.reshape(depth * n_ffn, h, d)
    out = _call(
        functools.partial(_down_prep_kernel, h), "ffn_down_prep", (depth * n_ffn, d // td),
        [pl.BlockSpec((None, h, td), lambda a, j: (a, 0, j))], [flat],
        pl.BlockSpec((None, hp, td), lambda a, j: (a, 0, j)),
        jax.ShapeDtypeStruct((depth * n_ffn, hp, d), BF16), ("parallel", "parallel"))
    return out.reshape(depth, n_ffn, hp, d)


def _window(ref, start, stop):
    lo = start - start % LANES
    hi = min(_round_up(stop, LANES), ref.shape[1])
    return ref[:, lo:hi][:, start - lo:stop - lo]


def _in_proj_prep_kernel(offs, n_gate, w_ref, small_ref, big_ref, feat_ref):
    tr = w_ref.shape[0]
    k_pe = _window(w_ref, offs[2], offs[3]).astype(BF16)
    quarter = ROPE_DIM // 4
    src = lax.broadcasted_iota(jnp.int32, (ROPE_DIM, ROPE_DIM), 0)
    dst = lax.broadcasted_iota(jnp.int32, (ROPE_DIM, ROPE_DIM), 1)
    swapped_dst = jnp.where((dst // quarter) % 2 == 0, dst + quarter, dst - quarter)
    select = (src == swapped_dst).astype(BF16)
    small_ref[:, :offs[2]] = w_ref[:, :offs[2]].astype(BF16)
    small_ref[:, offs[2]:offs[2] + 2 * ROPE_DIM] = jnp.concatenate(
        [k_pe, _dot(k_pe, select).astype(BF16)], axis=1)
    gates = _window(w_ref, offs[7], offs[8]).astype(BF16)
    small_ref[:, offs[2] + 2 * ROPE_DIM:] = jnp.concatenate(
        [gates, jnp.zeros((tr, LANES - n_gate), BF16)], axis=1)
    qk = offs[5] - offs[3]
    big_ref[:, :qk] = _window(w_ref, offs[3], offs[5]).astype(BF16)
    big_ref[:, qk:] = _window(w_ref, offs[8], offs[10]).astype(BF16)
    vo = _window(w_ref, offs[5], offs[7]).astype(BF16)
    cw = 2 * LANES
    eye = (lax.broadcasted_iota(jnp.int32, (cw, cw), 0)
           == lax.broadcasted_iota(jnp.int32, (cw, cw), 1)).astype(BF16)
    for c in range(0, vo.shape[1], cw):
        feat_ref[c:c + cw, :] = _dot_nt(eye, vo[:, c:c + cw]).astype(BF16)


def _in_proj_prep(w_in, offs, n_gate):
    depth, d, cols = w_in.shape
    offs = tuple(int(o) for o in offs)
    small_cols = offs[2] + 2 * ROPE_DIM + LANES
    big_cols = (offs[5] - offs[3]) + (offs[10] - offs[8])
    feat_rows = offs[7] - offs[5]
    assert offs[2] % LANES == 0 and feat_rows % (2 * LANES) == 0 and big_cols % LANES == 0
    tr = 256
    return _call(
        functools.partial(_in_proj_prep_kernel, offs, n_gate), "in_proj_weight_prep", (depth, d // tr),
        [pl.BlockSpec((None, tr, cols), lambda l, i: (l, i, 0))], [w_in],
        [pl.BlockSpec((None, tr, small_cols), lambda l, i: (l, i, 0)),
         pl.BlockSpec((None, tr, big_cols), lambda l, i: (l, i, 0)),
         pl.BlockSpec((None, feat_rows, tr), lambda l, i: (l, 0, i))],
        [jax.ShapeDtypeStruct((depth, d, small_cols), BF16),
         jax.ShapeDtypeStruct((depth, d, big_cols), BF16),
         jax.ShapeDtypeStruct((depth, feat_rows, d), BF16)],
        ("parallel", "parallel"))


def _rope_swap_index():
    quarter = ROPE_DIM // 4
    idx = np.arange(ROPE_DIM).reshape(2, 2, quarter)
    return idx[:, ::-1, :].reshape(-1)


def _rope_tables(t):
    pos = jnp.arange(t)
    row = (pos // GRID_W).astype(F32)
    col = (pos % GRID_W).astype(F32)
    n_freq = ROPE_DIM // 4
    inv_freq = jnp.power(ROPE_BASE, -jnp.arange(n_freq, dtype=F32) / n_freq)
    ang_r = row[:, None] * inv_freq
    ang_c = col[:, None] * inv_freq
    cos = jnp.concatenate([jnp.cos(ang_r), jnp.cos(ang_r), jnp.cos(ang_c), jnp.cos(ang_c)], axis=-1)
    sin = jnp.concatenate([-jnp.sin(ang_r), jnp.sin(ang_r), -jnp.sin(ang_c), jnp.sin(ang_c)], axis=-1)
    return cos, sin


def kernel(x_prompt, x_sample, cache_ckv, cache_kpe, state_C, state_n, state_m, c, c_ctx, w_mod, b_mod, norm_w, ffn_w_gu, ffn_w_down, w_in, q_norm_w, kv_norm_w, w_uq, w_ukv, mlstm_gate_b, mlstm_norm_w, pool_w, pool_scale, w_branch, w_out, final_norm_w):
    batch, seq, d = x_prompt.shape
    dec_batch, dec_seq, _ = x_sample.shape
    depth = w_mod.shape[0]
    q_rank, kv_rank = q_norm_w.shape[1], kv_norm_w.shape[1]
    heads = MLA_HEADS
    mw = mlstm_norm_w.shape[1]
    dh = mw // MLSTM_HEADS
    pw = pool_scale.shape[1]
    ffn_h = ffn_w_down.shape[2]
    n_gate = N_DIR * 2 * MLSTM_HEADS
    assert mw == pw == w_branch.shape[2] == heads * V_DIM
    rows = _Rows(batch * seq, seq, dec_batch * dec_seq, dec_seq)
    n = rows.n

    hp = _round_up(ffn_h, 512)
    w_g, w_u = _gate_up_prep(ffn_w_gu, hp)
    wdn = _down_prep(ffn_w_down, hp)

    sizes = (q_rank, kv_rank, ROPE_DIM, mw, mw, mw, mw, n_gate, pw, N_BRANCH * d)
    offs = np.concatenate([[0], np.cumsum(sizes)])
    swap = _rope_swap_index()
    small_cols = q_rank + kv_rank + 2 * ROPE_DIM + LANES
    w_in16 = jnp.pad(w_in.astype(BF16), ((0, 0), (0, 0), (0, _round_up(w_in.shape[2], LANES) - w_in.shape[2])))
    w_small, w_big, w_feat = _in_proj_prep(w_in16, offs, n_gate)
    gate_blk = (q_rank + kv_rank + 2 * ROPE_DIM) // LANES
    gate_col0 = 3 * mw

    wq4 = w_uq.reshape(depth, q_rank, heads, QK_DIM)
    wq_pe = wq4[..., NOPE_DIM:]
    wq = jnp.concatenate([wq4[..., :NOPE_DIM].reshape(depth, q_rank, -1),
                          wq_pe.reshape(depth, q_rank, -1),
                          wq_pe[..., swap].reshape(depth, q_rank, -1)], axis=-1).astype(BF16)
    wkv4 = w_ukv.reshape(depth, kv_rank, heads, NOPE_DIM + V_DIM)
    wkv = jnp.concatenate([wkv4[..., :NOPE_DIM].reshape(depth, kv_rank, -1),
                           wkv4[..., NOPE_DIM:].reshape(depth, kv_rank, -1)], axis=-1).astype(BF16)
    wbr = w_branch.astype(BF16)
    wout = w_out.astype(BF16)
    pwb = pool_w.astype(BF16)
    norm_w4 = norm_w.reshape(depth, 3, 1, d)
    qnw = q_norm_w.reshape(depth, 1, q_rank)
    kvnw = kv_norm_w.reshape(depth, 1, kv_rank)
    mnw = mlstm_norm_w.reshape(depth, mw, 1)
    psc = pool_scale.reshape(depth, 1, pw)
    bands = _pool_bands()
    rope_tabs = _rope_tables(dec_seq)

    cond = jnp.concatenate([c_ctx[None, :], c, jnp.zeros((COND_ROWS - 1 - dec_batch, d), F32)], axis=0)
    mod = _mod_all(cond, w_mod, b_mod).reshape(depth, COND_ROWS, N_MOD, 1, d)

    cache_kv = _cache_kv(cache_ckv, cache_kpe, wkv)

    x = (x_prompt.reshape(rows.n_ctx, d), x_sample.reshape(rows.n_lat, d))
    new_cache = (None, None)
    new_state = (None, None, None)
    for l in range(depth):
        x, h_mix = _ffn(x, mod, norm_w4, w_g, w_u, wdn, rows, l, 0)

        p32 = _inproj(h_mix, w_small, rows, l, F32, small_cols)
        pb = _inproj(h_mix, w_big, rows, l, BF16, _lane_tile(w_big.shape[2], 1024))
        pb_t = _inproj_t(h_mix, w_feat, rows, l)
        gates_t = p32[:, gate_blk * LANES:gate_blk * LANES + n_gate].T

        q_c, k_c, v_c, *new_cache = _mla_prep(p32, qnw, kvnw, wq, wkv, l, 0, batch, seq, None, new_cache)
        q_s, k_s, v_s = _mla_prep(p32, qnw, kvnw, wq, wkv, l, rows.n_ctx, dec_batch, dec_seq, rope_tabs, None)
        y_a = _attention(q_c, k_c, v_c, None, l, n, 0, None)
        y_a = _attention(q_s, k_s, v_s, cache_kv, l, n, rows.n_ctx, y_a)

        hf_c, hb_c, *new_state = _mlstm(pb, pb_t, p32, gates_t, mlstm_gate_b, None, l, 0, batch, seq,
                                        gate_blk, dh, new_state)
        hf_s, hb_s, _, _, _ = _mlstm(pb, pb_t, p32, gates_t, mlstm_gate_b, (state_C, state_n, state_m), l,
                                     rows.n_ctx, dec_batch, dec_seq, gate_blk, dh, None)
        y_b = _mlstm_post(hf_c, hb_c, pb_t, mnw, l, n, 0, dh, None)
        y_b = _mlstm_post(hf_s, hb_s, pb_t, mnw, l, n, rows.n_ctx, dh, y_b)

        y_c = _pool(pb, bands, pwb, psc, l, n, 0, batch, seq, None)
        y_c = _pool(pb, bands, pwb, psc, l, n, rows.n_ctx, dec_batch, dec_seq, y_c)

        merged = _merge(y_a, y_b, y_c, pb, wbr, rows, l, gate_col0)
        x = _outproj(merged, wout, x, mod, rows, l)
        x = _ffn(x, mod, norm_w4, w_g, w_u, wdn, rows, l, 1)

    y_prompt = _final_norm(x, final_norm_w, 0, rows.n_ctx).reshape(batch, seq, d)
    y_sample = _final_norm(x, final_norm_w, rows.n_ctx, rows.n_lat).reshape(dec_batch, dec_seq, d)
    new_c, new_n, new_m = new_state
    return (y_prompt, y_sample, new_cache[0], new_cache[1], new_c,
            new_n.reshape(batch, depth, N_DIR, MLSTM_HEADS, dh),
            new_m.reshape(batch, depth, N_DIR, MLSTM_HEADS))
```

```python
import functools
import math

import numpy as np
import jax
import jax.numpy as jnp
from jax import lax
from jax.experimental import pallas as pl
from jax.experimental.pallas import tpu as pltpu

GRID_W = 64
EPS = 1e-6
N_MOD = 9
MLA_HEADS = 8
NOPE_DIM = 128
ROPE_DIM = 64
V_DIM = 128
QK_DIM = NOPE_DIM + ROPE_DIM
ROPE_BASE = 10000.0
MLSTM_HEADS = 4
N_DIR = 2
CHUNK = 128
POOL_WINDOWS = (2, 4, 8, 16)
POOL_GROUPS = 4
N_BRANCH = 3

LANES = 128
VMEM_LIMIT_MB = 56
COND_ROWS = 8
PACK_ROWS = 16
V_PAD = 2 * V_DIM

F32 = jnp.float32
BF16 = jnp.bfloat16


def _params(sem):
    return pltpu.CompilerParams(dimension_semantics=sem, vmem_limit_bytes=VMEM_LIMIT_MB << 20)


def _call(kernel, name, grid, in_specs, args, out_specs, out_shape, sem, bases=(), scratch=()):
    n_in = len(args)
    extra = [b for b in bases if b is not None]
    aliases = {}
    for k, b in enumerate(bases):
        if b is not None:
            aliases[n_in + len(aliases)] = k

    def body(*refs):
        kernel(*refs[:n_in], *refs[n_in + len(extra):])

    return pl.pallas_call(
        body if extra else kernel, grid=grid,
        in_specs=list(in_specs) + [pl.BlockSpec(memory_space=pl.ANY)] * len(extra),
        out_specs=out_specs, out_shape=out_shape, input_output_aliases=aliases,
        scratch_shapes=list(scratch), compiler_params=_params(sem), name=name)(*args, *extra)


def _round_up(n, m):
    return (n + m - 1) // m * m


def _lane_tile(n, cap):
    t = cap - cap % LANES
    while n % t:
        t -= LANES
    return t


def _sigmoid(x):
    return 1.0 / (1.0 + jnp.exp(-x))


def _log_sigmoid(x):
    return -(jnp.maximum(-x, 0.0) + jnp.log1p(jnp.exp(-jnp.abs(x))))


def _rms(x, w):
    return x * lax.rsqrt(jnp.mean(x * x, axis=-1, keepdims=True) + EPS) * w


def _dot(a, b):
    return jnp.dot(a, b, preferred_element_type=F32)


def _dot_nt(a, b):
    return lax.dot_general(a, b, (((1,), (1,)), ((), ())), preferred_element_type=F32)


def _mod_kernel(c_ref, w_ref, b_ref, o_ref):
    c = c_ref[...]
    a = (c * _sigmoid(c)).astype(BF16)
    o_ref[...] = _dot(a, w_ref[...].astype(BF16)) + b_ref[...]


def _mod_all(cond, w_mod, b_mod):
    depth, d, nd = w_mod.shape
    tn = _lane_tile(nd, 1024)
    return _call(
        _mod_kernel, "adaln_mod", (depth, nd // tn),
        [pl.BlockSpec((COND_ROWS, d), lambda l, j: (0, 0)),
         pl.BlockSpec((None, d, tn), lambda l, j: (l, 0, j)),
         pl.BlockSpec((None, 1, tn), lambda l, j: (l, 0, j))],
        [cond, w_mod, b_mod.reshape(depth, 1, nd)],
        pl.BlockSpec((None, COND_ROWS, tn), lambda l, j: (l, 0, j)),
        jax.ShapeDtypeStruct((depth, COND_ROWS, nd), F32),
        ("parallel", "parallel"))


class _Rows:
    def __init__(self, n_ctx, t_ctx, n_lat, t_lat):
        self.n_ctx, self.t_ctx, self.n_lat, self.t_lat = n_ctx, t_ctx, n_lat, t_lat
        self.n = n_ctx + n_lat

    def mod_row(self, row):
        return jnp.where(row < self.n_ctx, 0, 1 + (row - self.n_ctx) // self.t_lat)

    def tile(self, cap):
        t = min(cap, self.n_ctx, self.t_lat)
        assert self.n_ctx % t == 0 and self.t_lat % t == 0
        return t


def _mod_spec(rows, tm, layer, k, d):
    return pl.BlockSpec((None, None, None, 1, d),
                        lambda i, j: (layer, rows.mod_row(i * tm), k, 0, 0))


def _norm_mod_to(h_ref, x_ref, nw_ref, sh_ref, sc_ref):
    y = _rms(x_ref[...], nw_ref[...])
    h_ref[...] = (y * (1.0 + sc_ref[...]) + sh_ref[...]).astype(h_ref.dtype)


def _ffn_kernel(emit_next, ctx_tiles, *refs):
    x_refs, refs = (refs[:1], refs[1:]) if ctx_tiles is None else (refs[:2], refs[2:])
    sh_ref, sc_ref, g_ref, nw_ref, wg_ref, wu_ref, wd_ref = refs[:7]
    if emit_next:
        sh2_ref, sc2_ref, nw2_ref, o_ref, h2_ref, h_ref = refs[7:]
    else:
        o_ref, h_ref = refs[7:]
    i = pl.program_id(0)
    j = pl.program_id(1)

    def per_source(fn):
        if ctx_tiles is None:
            fn(x_refs[0])
        else:
            pl.when(i < ctx_tiles)(functools.partial(fn, x_refs[0]))
            pl.when(i >= ctx_tiles)(functools.partial(fn, x_refs[1]))

    def prologue(x_ref):
        _norm_mod_to(h_ref, x_ref, nw_ref, sh_ref, sc_ref)

    def epilogue(x_ref):
        o_ref[...] = x_ref[...] + 0.5 * g_ref[...] * o_ref[...]
        if emit_next:
            _norm_mod_to(h2_ref, o_ref, nw2_ref, sh2_ref, sc2_ref)

    @pl.when(j == 0)
    def _():
        per_source(prologue)
        o_ref[...] = jnp.zeros_like(o_ref)

    h = h_ref[...]
    g = _dot(h, wg_ref[...])
    u = _dot(h, wu_ref[...])
    a = (g * _sigmoid(g) * u).astype(BF16)
    o_ref[...] += _dot(a, wd_ref[...])

    @pl.when(j == pl.num_programs(1) - 1)
    def _():
        per_source(epilogue)


def _ffn(x, mod, norm_w, w_g, w_u, w_down, rows, layer, which):
    tm = rows.tile(512)
    if isinstance(x, tuple):
        ctx_tiles = rows.n_ctx // tm
        x_args = list(x)
        x_specs = [pl.BlockSpec((tm, x[0].shape[1]), lambda i, j: (jnp.minimum(i, ctx_tiles - 1), 0)),
                   pl.BlockSpec((tm, x[0].shape[1]), lambda i, j: (jnp.maximum(i - ctx_tiles, 0), 0))]
    else:
        ctx_tiles = None
        x_args = [x]
        x_specs = [pl.BlockSpec((tm, x.shape[1]), lambda i, j: (i, 0))]
    n, d = rows.n, x_args[0].shape[1]
    hp = w_down.shape[2]
    th = _lane_tile(hp, 512)
    k0 = 0 if which == 0 else 6
    emit_next = which == 0
    w_spec = pl.BlockSpec((None, None, d, th), lambda i, j: (layer, which, 0, j))
    row_spec = pl.BlockSpec((tm, d), lambda i, j: (i, 0))
    norm_spec = lambda k: pl.BlockSpec((None, None, 1, d), lambda i, j: (layer, k, 0, 0))
    in_specs = x_specs + [
        _mod_spec(rows, tm, layer, k0, d),
        _mod_spec(rows, tm, layer, k0 + 1, d),
        _mod_spec(rows, tm, layer, k0 + 2, d),
        norm_spec(2 * which), w_spec, w_spec,
        pl.BlockSpec((None, None, th, d), lambda i, j: (layer, which, j, 0))]
    args = x_args + [mod, mod, mod, norm_w, w_g, w_u, w_down]
    out_specs, out_shape = row_spec, jax.ShapeDtypeStruct((n, d), F32)
    if emit_next:
        in_specs += [_mod_spec(rows, tm, layer, 3, d), _mod_spec(rows, tm, layer, 4, d), norm_spec(1)]
        args += [mod, mod, norm_w]
        out_specs, out_shape = [row_spec, row_spec], [out_shape, jax.ShapeDtypeStruct((n, d), BF16)]
    return _call(
        functools.partial(_ffn_kernel, emit_next, ctx_tiles), "ffn_half_step", (n // tm, hp // th),
        in_specs, args, out_specs, out_shape, ("parallel", "arbitrary"),
        scratch=[pltpu.VMEM((tm, d), BF16)])


def _inproj_kernel(h_ref, w_ref, o_ref):
    o_ref[...] = _dot(h_ref[...], w_ref[...]).astype(o_ref.dtype)


def _inproj(h, w, rows, layer, out_dtype, tn):
    n, d = h.shape
    cols = w.shape[2]
    tm = rows.tile(1024)
    return _call(
        _inproj_kernel, "mixer_in_proj", (n // tm, cols // tn),
        [pl.BlockSpec((tm, d), lambda i, j: (i, 0)),
         pl.BlockSpec((None, d, tn), lambda i, j: (layer, 0, j))],
        [h, w],
        pl.BlockSpec((tm, tn), lambda i, j: (i, j)),
        jax.ShapeDtypeStruct((n, cols), out_dtype),
        ("parallel", "arbitrary"))


def _inproj_t_kernel(h_ref, wt_ref, o_ref):
    o_ref[...] = _dot_nt(wt_ref[...], h_ref[...]).astype(o_ref.dtype)


def _inproj_t(h, w_t, rows, layer):
    n, d = h.shape
    cols = w_t.shape[1]
    tm = rows.tile(1024)
    tn = _lane_tile(cols, 1024)
    return _call(
        _inproj_t_kernel, "mixer_in_proj_t", (n // tm, cols // tn),
        [pl.BlockSpec((tm, d), lambda i, j: (i, 0)),
         pl.BlockSpec((None, tn, d), lambda i, j: (layer, j, 0))],
        [h, w_t],
        pl.BlockSpec((tn, tm), lambda i, j: (j, i)),
        jax.ShapeDtypeStruct((cols, n), BF16),
        ("parallel", "arbitrary"))


def _ones_column(rows):
    lane = lax.broadcasted_iota(jnp.int32, (rows, V_PAD - V_DIM), 1)
    return (lane == 0).astype(BF16)


def _mla_prep_kernel(rope, q_rank, kv_rank, p_ref, qnw_ref, kvnw_ref, wq_ref, wkv_ref, *rest):
    if rope:
        cos_ref, sin_ref, q_ref, k_ref, v_ref = rest
    else:
        q_ref, k_ref, v_ref, ckv_ref, kpe_ref = rest
    heads = MLA_HEADS
    p = p_ref[...]
    c_q = p[:, :q_rank]
    c_kv = p[:, q_rank:q_rank + kv_rank]
    o = q_rank + kv_rank
    k_pe = p[:, o:o + ROPE_DIM]
    k_pe_sw = p[:, o + ROPE_DIM:o + 2 * ROPE_DIM]

    qa = _dot(_rms(c_q, qnw_ref[...]).astype(BF16), wq_ref[...])
    ckv_n = _rms(c_kv, kvnw_ref[...])
    kv = _dot(ckv_n.astype(BF16), wkv_ref[...])
    scale = QK_DIM ** -0.5 * math.log2(math.e)
    if rope:
        cos = cos_ref[...]
        sin = sin_ref[...]
        k_pe = k_pe * cos + k_pe_sw * sin
    else:
        ckv_ref[...] = ckv_n
        kpe_ref[...] = k_pe
    ones = _ones_column(p.shape[0])
    pe0 = heads * NOPE_DIM
    sw0 = pe0 + heads * ROPE_DIM
    for h in range(heads):
        q_pe = qa[:, pe0 + h * ROPE_DIM:pe0 + (h + 1) * ROPE_DIM]
        if rope:
            q_pe = q_pe * cos + qa[:, sw0 + h * ROPE_DIM:sw0 + (h + 1) * ROPE_DIM] * sin
        q_ref[h, :, :NOPE_DIM] = (qa[:, h * NOPE_DIM:(h + 1) * NOPE_DIM] * scale).astype(BF16)
        q_ref[h, :, NOPE_DIM:] = (q_pe * scale).astype(BF16)
        k_ref[h, :, :NOPE_DIM] = kv[:, h * NOPE_DIM:(h + 1) * NOPE_DIM].astype(BF16)
        k_ref[h, :, NOPE_DIM:] = k_pe.astype(BF16)
        v0 = heads * NOPE_DIM + h * V_DIM
        v_ref[h, :, :V_DIM] = kv[:, v0:v0 + V_DIM].astype(BF16)
        v_ref[h, :, V_DIM:] = ones


def _mla_prep(p32, q_norm_w, kv_norm_w, wq, wkv, layer, row0, batch, t, rope_tabs, bases):
    ws = p32.shape[1]
    q_rank, kv_rank = q_norm_w.shape[-1], kv_norm_w.shape[-1]
    depth = q_norm_w.shape[0]
    tm = min(256, t)
    nt = t // tm
    heads = MLA_HEADS
    rope = rope_tabs is not None
    in_specs = [pl.BlockSpec((tm, ws), lambda b, i: (row0 // tm + b * nt + i, 0)),
                pl.BlockSpec((None, 1, q_rank), lambda b, i: (layer, 0, 0)),
                pl.BlockSpec((None, 1, kv_rank), lambda b, i: (layer, 0, 0)),
                pl.BlockSpec((None,) + wq.shape[1:], lambda b, i: (layer, 0, 0)),
                pl.BlockSpec((None,) + wkv.shape[1:], lambda b, i: (layer, 0, 0))]
    args = [p32, q_norm_w, kv_norm_w, wq, wkv]
    head_spec = lambda width: pl.BlockSpec((None, heads, tm, width), lambda b, i: (b, 0, i, 0))
    out_specs = [head_spec(QK_DIM), head_spec(QK_DIM), head_spec(V_PAD)]
    out_shape = [jax.ShapeDtypeStruct((batch, heads, t, QK_DIM), BF16),
                 jax.ShapeDtypeStruct((batch, heads, t, QK_DIM), BF16),
                 jax.ShapeDtypeStruct((batch, heads, t, V_PAD), BF16)]
    if rope:
        in_specs += [pl.BlockSpec((tm, ROPE_DIM), lambda b, i: (i, 0))] * 2
        args += list(rope_tabs)
        all_bases = ()
    else:
        out_specs += [pl.BlockSpec((None, None, tm, kv_rank), lambda b, i: (b, layer, i, 0)),
                      pl.BlockSpec((None, None, tm, ROPE_DIM), lambda b, i: (b, layer, i, 0))]
        out_shape += [jax.ShapeDtypeStruct((batch, depth, t, kv_rank), F32),
                      jax.ShapeDtypeStruct((batch, depth, t, ROPE_DIM), F32)]
        all_bases = (None, None, None) + tuple(bases)
    return _call(functools.partial(_mla_prep_kernel, rope, q_rank, kv_rank), "mla_prep",
                 (batch, nt), in_specs, args, out_specs, out_shape, ("parallel", "parallel"),
                 bases=all_bases)


def _cache_kv_kernel(ckv_ref, kpe_ref, wkv_ref, k_ref, v_ref):
    heads = MLA_HEADS
    kv = _dot(ckv_ref[...].astype(BF16), wkv_ref[...])
    k_pe = kpe_ref[...].astype(BF16)
    ones = _ones_column(kv.shape[0])
    for h in range(heads):
        k_ref[h, :, :NOPE_DIM] = kv[:, h * NOPE_DIM:(h + 1) * NOPE_DIM].astype(BF16)
        k_ref[h, :, NOPE_DIM:] = k_pe
        v0 = heads * NOPE_DIM + h * V_DIM
        v_ref[h, :, :V_DIM] = kv[:, v0:v0 + V_DIM].astype(BF16)
        v_ref[h, :, V_DIM:] = ones


def _cache_kv(cache_ckv, cache_kpe, wkv):
    batch, depth, past, kv_rank = cache_ckv.shape
    heads = MLA_HEADS
    return _call(
        _cache_kv_kernel, "mla_cache_kv", (batch, depth),
        [pl.BlockSpec((None, None, past, kv_rank), lambda b, l: (b, l, 0, 0)),
         pl.BlockSpec((None, None, past, ROPE_DIM), lambda b, l: (b, l, 0, 0)),
         pl.BlockSpec((None,) + wkv.shape[1:], lambda b, l: (l, 0, 0))],
        [cache_ckv, cache_kpe, wkv],
        [pl.BlockSpec((None, None, heads, past, QK_DIM), lambda b, l: (b, l, 0, 0, 0)),
         pl.BlockSpec((None, None, heads, past, V_PAD), lambda b, l: (b, l, 0, 0, 0))],
        [jax.ShapeDtypeStruct((batch, depth, heads, past, QK_DIM), BF16),
         jax.ShapeDtypeStruct((batch, depth, heads, past, V_PAD), BF16)],
        ("parallel", "parallel"))


ATTN_KEY_CHUNK = 512


def _attn_kernel(past, q_ref, qn_ref, k_ref, kn_ref, v_ref, *rest):
    if past:
        kc_ref, kcn_ref, vc_ref = rest[:3]
        rest = rest[3:]
    else:
        kc_ref = kcn_ref = vc_ref = None
    o_ref, s0_ref, s1_ref, m0_ref, m1_ref = rest
    tq = qn_ref.shape[0]
    t = k_ref.shape[0]
    chunk = min(ATTN_KEY_CHUNK, t)

    def scores(q, keys_ref, cache_keys_ref, s_ref, m_ref):
        s = _dot_nt(q, keys_ref[...])
        m = jnp.max(s, axis=-1, keepdims=True)
        if past:
            sc = _dot_nt(q, cache_keys_ref[...])
            m = jnp.maximum(m, jnp.max(sc, axis=-1, keepdims=True))
            s_ref[:, :past] = sc
        s_ref[:, past:] = s
        m_ref[...] = m

    def values(r, s_ref, m_ref):
        m = m_ref[...]
        acc = None
        if past:
            acc = _dot(jnp.exp2(s_ref[:, :past] - m).astype(BF16), vc_ref[...])
        for c in range(0, t, chunk):
            p = jnp.exp2(s_ref[:, past + c:past + c + chunk] - m).astype(BF16)
            d = _dot(p, v_ref[c:c + chunk, :])
            acc = d if acc is None else acc + d
        o_ref[r * tq:(r + 1) * tq, :] = (acc[:, :V_DIM] / acc[:, V_DIM:V_DIM + 1]).astype(o_ref.dtype)

    @pl.when(pl.program_id(0) == 0)
    def _():
        scores(q_ref[:tq, :], k_ref, kc_ref, s0_ref, m0_ref)

    values(0, s0_ref, m0_ref)
    scores(q_ref[tq:, :], k_ref, kc_ref, s1_ref, m1_ref)
    values(1, s1_ref, m1_ref)
    scores(qn_ref[...], kn_ref, kcn_ref, s0_ref, m0_ref)


def _attn_short_kernel(q_ref, k_ref, v_ref, o_ref):
    heads = range(q_ref.shape[0])
    s = [_dot_nt(q_ref[h], k_ref[h]) for h in heads]
    p = [jnp.exp2(x - jnp.max(x, axis=-1, keepdims=True)).astype(BF16) for x in s]
    o = [_dot(p[h], v_ref[h]) for h in heads]
    for h in heads:
        o_ref[:, h * V_DIM:(h + 1) * V_DIM] = (o[h][:, :V_DIM] / o[h][:, V_DIM:V_DIM + 1]).astype(o_ref.dtype)


ATTN_SHORT_SEQ = 256


def _attention(q, k, v, cache, layer, n_rows, row0, base):
    batch, heads, t, _ = q.shape
    if cache is None and t <= ATTN_SHORT_SEQ:
        assert row0 % t == 0
        head_block = lambda width: pl.BlockSpec((None, heads, t, width), lambda b: (b, 0, 0, 0))
        return _call(
            _attn_short_kernel, "mla_attention_short", (batch,),
            [head_block(QK_DIM), head_block(QK_DIM), head_block(V_PAD)], [q, k, v],
            pl.BlockSpec((t, heads * V_DIM), lambda b: (row0 // t + b, 0)),
            jax.ShapeDtypeStruct((n_rows, heads * V_DIM), BF16), ("parallel",), bases=(base,))
    tq = min(512, t // 2)
    pair = 2 * tq
    npair = t // pair
    n_steps = batch * heads * npair
    past = cache[0].shape[3] if cache is not None else 0
    assert t % pair == 0 and row0 % pair == 0

    def where(tile):
        tile = jnp.minimum(tile, 2 * n_steps - 1)
        p = tile // 2
        return p // (heads * npair), (p // npair) % heads, 2 * (p % npair) + tile % 2

    def pair_map(g):
        b, h, i = where(2 * g)
        return b, h, i // 2, 0

    def next_map(g):
        b, h, i = where(2 * g + 2)
        return b, h, i, 0

    def kv_map(shift):
        def index(g):
            b, h, _ = where(2 * g + shift)
            return b, h, 0, 0
        return index

    def cache_map(shift):
        def index(g):
            b, h, _ = where(2 * g + shift)
            return b, layer, h, 0, 0
        return index

    def o_map(g):
        b, h, i = where(2 * g)
        return row0 // pair + b * npair + i // 2, h

    in_specs = [pl.BlockSpec((None, None, pair, QK_DIM), pair_map),
                pl.BlockSpec((None, None, tq, QK_DIM), next_map),
                pl.BlockSpec((None, None, t, QK_DIM), kv_map(0)),
                pl.BlockSpec((None, None, t, QK_DIM), kv_map(2)),
                pl.BlockSpec((None, None, t, V_PAD), kv_map(0))]
    args = [q, q, k, k, v]
    if past:
        in_specs += [pl.BlockSpec((None, None, None, past, QK_DIM), cache_map(0)),
                     pl.BlockSpec((None, None, None, past, QK_DIM), cache_map(2)),
                     pl.BlockSpec((None, None, None, past, V_PAD), cache_map(0))]
        args += [cache[0], cache[0], cache[1]]
    s_total = past + t
    return _call(
        functools.partial(_attn_kernel, past), "mla_attention", (n_steps,),
        in_specs, args,
        pl.BlockSpec((pair, V_DIM), o_map),
        jax.ShapeDtypeStruct((n_rows, heads * V_DIM), BF16),
        ("arbitrary",), bases=(base,),
        scratch=[pltpu.VMEM((tq, s_total), F32), pltpu.VMEM((tq, s_total), F32),
                 pltpu.VMEM((tq, 1), F32), pltpu.VMEM((tq, 1), F32)])


MLSTM_CHUNKS_PER_STEP = 2


def _split3(x):
    hi = x.astype(BF16)
    r = x - hi.astype(F32)
    mid = r.astype(BF16)
    return hi, mid, (r - mid.astype(F32)).astype(BF16)


def _mlstm_kernel(has_init, dh, n_sub, *refs):
    (qf_ref, kf_ref, vf_ref, qb_ref, kb_ref, vb_ref, gf_ref, gb_ref, gtf_ref, gtb_ref,
     brow_ref, bcol_ref) = refs[:12]
    refs = refs[12:]
    if has_init:
        c0_ref, n0_ref, m0_ref = refs[:3]
        refs = refs[3:]
    hf_ref, hb_ref, c_ref, n_ref, m_ref = refs
    heads = MLSTM_HEADS
    n_gate = N_DIR * 2 * heads
    step = pl.program_id(1)

    @pl.when(step == 0)
    def _():
        if has_init:
            c_ref[...] = c0_ref[...]
            n_ref[...] = n0_ref[...]
            m_ref[...] = m0_ref[...]
        else:
            c_ref[...] = jnp.zeros_like(c_ref)
            n_ref[...] = jnp.zeros_like(n_ref)
            m_ref[...] = jnp.zeros_like(m_ref)

    tok0 = lax.broadcasted_iota(jnp.int32, (CHUNK, CHUNK), 0)
    tok1 = lax.broadcasted_iota(jnp.int32, (CHUNK, CHUNK), 1)
    k_scale = dh ** -0.5
    m_all = m_ref[...]
    m_out = m_all
    unit_lane = lax.broadcasted_iota(jnp.int32, m_all.shape, 1)
    units = [(d, h) for d in range(N_DIR) for h in range(heads)]
    for sub in range(n_sub):
        m_all = m_out
        gate = {}
        for d in range(N_DIR):
            g_ref, gt_ref = (gf_ref, gtf_ref) if d == 0 else (gb_ref, gtb_ref)
            chunk_idx = sub if d == 0 else n_sub - 1 - sub
            tok = slice(chunk_idx * CHUNK, (chunk_idx + 1) * CHUNK)
            seen_t = (tok0 <= tok1) if d == 0 else (tok0 >= tok1)
            seen_t_bf = seen_t.astype(BF16)
            seen_bf = ((tok1 <= tok0) if d == 0 else (tok1 >= tok0)).astype(BF16)
            pre_col = g_ref[tok, :n_gate] + brow_ref[...]
            pre_row = gt_ref[:, tok] + bcol_ref[...]
            cum_col = sum(_dot(seen_bf, part) for part in _split3(_log_sigmoid(pre_col)))
            cum_row = sum(_dot(part, seen_t_bf) for part in _split3(_log_sigmoid(pre_row)))
            gate[d] = (tok, seen_t, pre_col, pre_row, cum_col, cum_row)

        st = {}
        for d, h in units:
            tok, seen_t, pre_col, pre_row, cum_col, cum_row = gate[d]
            q_ref, k_ref, vt_ref = (qf_ref, kf_ref, vf_ref) if d == 0 else (qb_ref, kb_ref, vb_ref)
            ci = d * 2 * heads + h
            cf = ci + heads
            sid = d * heads + h
            sl = slice(h * dh, (h + 1) * dh)
            last = CHUNK - 1 if d == 0 else 0
            c_col = pre_col[:, ci:ci + 1] - cum_col[:, cf:cf + 1]
            i_row = pre_row[ci:ci + 1, :]
            b_row = cum_row[cf:cf + 1, :]
            b_end = b_row[:, last:last + 1]
            m_prev = m_all[:, sid:sid + 1]
            a_row = b_row + m_prev
            dmat = jnp.where(seen_t, b_row + c_col, -jnp.inf)
            m_t = jnp.maximum(a_row, jnp.max(dmat, axis=0, keepdims=True))
            q = q_ref[tok, sl]
            k_bf = (k_ref[tok, sl].astype(F32) * k_scale).astype(BF16)
            st[d, h] = dict(tok=tok, sl=sl, sid=sid, i_row=i_row, b_row=b_row, b_end=b_end, m_prev=m_prev,
                            m_t=m_t, w_intra=jnp.exp(dmat - m_t), w_inter=jnp.exp(a_row - m_t),
                            q=q, k_bf=k_bf, v_t=vt_ref[sl, tok], c_prev=c_ref[d, h], n_prev=n_ref[d, h])

        for u in units:
            x = st[u]
            x["s_t"] = _dot_nt(x["k_bf"], x["q"]) * x["w_intra"]
            n_rows = jnp.broadcast_to(x["n_prev"], (PACK_ROWS, dh)).astype(BF16)
            x["cq"] = _dot_nt(jnp.concatenate([x["c_prev"].astype(BF16), n_rows], axis=0), x["q"])

        for d, h in units:
            x = st[d, h]
            h_ref = hf_ref if d == 0 else hb_ref
            num = x["w_inter"] * x["cq"][:dh, :] + _dot(x["v_t"], x["s_t"].astype(BF16))
            den = x["w_inter"] * x["cq"][dh:dh + 1, :] + jnp.sum(x["s_t"], axis=0, keepdims=True)
            h_ref[x["sl"], x["tok"]] = num / jnp.maximum(jnp.abs(den), jnp.exp(-x["m_t"]))

        for d, h in units:
            x = st[d, h]
            g_row = x["b_end"] - x["b_row"] + x["i_row"]
            m_new = jnp.maximum(x["b_end"] + x["m_prev"], jnp.max(g_row, axis=1, keepdims=True))
            w_pos = jnp.exp(g_row - m_new)
            w_carry = jnp.exp(x["b_end"] + x["m_prev"] - m_new)
            w_rows = jnp.broadcast_to(w_pos, (PACK_ROWS, CHUNK)).astype(BF16)
            upd = _dot(jnp.concatenate([(x["v_t"].astype(F32) * w_pos).astype(BF16), w_rows], axis=0), x["k_bf"])
            c_ref[d, h] = w_carry * x["c_prev"] + upd[:dh, :]
            n_ref[d, h] = w_carry * x["n_prev"] + upd[dh:dh + 1, :]
            m_out = jnp.where(unit_lane == x["sid"], m_new, m_out)
    m_ref[...] = m_out


def _mlstm(pb, pb_t, p32, gates_t, gate_b, state, layer, row0, batch, t, gate_blk, dh, bases):
    heads = MLSTM_HEADS
    n_gate = N_DIR * 2 * heads
    depth = gate_b.shape[0]
    n_sub = MLSTM_CHUNKS_PER_STEP if (t // CHUNK) % MLSTM_CHUNKS_PER_STEP == 0 else 1
    span = n_sub * CHUNK
    nc = t // span
    assert row0 % span == 0
    blk0 = row0 // span
    w = heads * dh
    fwd = lambda b, c: blk0 + b * nc + c
    bwd = lambda b, c: blk0 + b * nc + nc - 1 - c

    def tok(col, blk):
        return pl.BlockSpec((span, w), lambda b, c: (blk(b, c), col))

    def feat(blk):
        return pl.BlockSpec((w, span), lambda b, c: (0, blk(b, c)))

    in_specs = [tok(0, fwd), tok(1, fwd), feat(fwd), tok(0, bwd), tok(1, bwd), feat(bwd),
                pl.BlockSpec((span, LANES), lambda b, c: (fwd(b, c), gate_blk)),
                pl.BlockSpec((span, LANES), lambda b, c: (bwd(b, c), gate_blk)),
                pl.BlockSpec((n_gate, span), lambda b, c: (0, fwd(b, c))),
                pl.BlockSpec((n_gate, span), lambda b, c: (0, bwd(b, c))),
                pl.BlockSpec((None, 1, n_gate), lambda b, c: (layer, 0, 0)),
                pl.BlockSpec((None, n_gate, 1), lambda b, c: (layer, 0, 0))]
    args = [pb, pb, pb_t, pb, pb, pb_t, p32, p32, gates_t, gates_t,
            gate_b.reshape(-1, 1, n_gate), gate_b.reshape(-1, n_gate, 1)]
    has_init = state is not None
    state_shapes = [(N_DIR, heads, dh, dh), (N_DIR, heads, 1, dh), (1, N_DIR * heads)]
    if has_init:
        c0, n0, m0 = state
        in_specs += [pl.BlockSpec((None, None) + shp, lambda b, c, z=(0,) * len(shp): (b, layer) + z)
                     for shp in state_shapes]
        args += [c0, n0.reshape((batch, depth) + state_shapes[1]), m0.reshape((batch, depth) + state_shapes[2])]
        st_specs = [pl.BlockSpec((None,) + shp, lambda b, c, z=(0,) * len(shp): (b,) + z) for shp in state_shapes]
        st_shapes = [jax.ShapeDtypeStruct((batch,) + shp, F32) for shp in state_shapes]
        all_bases = ()
    else:
        st_specs = [pl.BlockSpec((None, None) + shp, lambda b, c, z=(0,) * len(shp): (b, layer) + z)
                    for shp in state_shapes]
        st_shapes = [jax.ShapeDtypeStruct((batch, depth) + shp, F32) for shp in state_shapes]
        all_bases = (None, None) + tuple(bases)
    return _call(
        functools.partial(_mlstm_kernel, has_init, dh, n_sub), "mlstm_scan", (batch, nc), in_specs, args,
        [pl.BlockSpec((w, span), lambda b, c: (0, b * nc + c)),
         pl.BlockSpec((w, span), lambda b, c: (0, b * nc + nc - 1 - c))] + st_specs,
        [jax.ShapeDtypeStruct((w, batch * t), F32),
         jax.ShapeDtypeStruct((w, batch * t), F32)] + st_shapes,
        ("parallel", "arbitrary"), bases=all_bases)


def _mlstm_post_kernel(dh, hf_ref, hb_ref, o_ref, w_ref, y_ref):
    tm = hf_ref.shape[1]
    hm = hf_ref[...] + hb_ref[...]
    gate = _sigmoid(o_ref[...].astype(F32))
    w = w_ref[...]
    eye = (lax.broadcasted_iota(jnp.int32, (tm, tm), 0)
           == lax.broadcasted_iota(jnp.int32, (tm, tm), 1)).astype(BF16)
    for h in range(MLSTM_HEADS):
        sl = slice(h * dh, (h + 1) * dh)
        x = hm[sl, :]
        y = x * lax.rsqrt(jnp.mean(x * x, axis=0, keepdims=True) + EPS) * w[sl, :]
        y_t = (gate[sl, :] * y).astype(BF16)
        y_ref[:, sl] = _dot_nt(eye, y_t).astype(y_ref.dtype)


def _mlstm_post(h_f, h_b, pb_t, m_norm_w, layer, n_rows, row0, dh, base):
    w, n = h_f.shape
    tm = min(256, n)
    return _call(
        functools.partial(_mlstm_post_kernel, dh), "mlstm_post", (n // tm,),
        [pl.BlockSpec((w, tm), lambda i: (0, i)),
         pl.BlockSpec((w, tm), lambda i: (0, i)),
         pl.BlockSpec((w, tm), lambda i: (1, row0 // tm + i)),
         pl.BlockSpec((None, w, 1), lambda i: (layer, 0, 0))],
        [h_f, h_b, pb_t, m_norm_w],
        pl.BlockSpec((tm, w), lambda i: (row0 // tm + i, 0)),
        jax.ShapeDtypeStruct((n_rows, w), BF16),
        ("parallel",), bases=(base,))


POOL_TILE = 256


def _pool_bands():
    t = np.arange(POOL_TILE)[:, None]
    bands = np.zeros((POOL_GROUPS, 3, POOL_TILE, POOL_TILE), np.float32)
    for g, win in enumerate(POOL_WINDOWS):
        for part in range(3):
            s = np.arange(POOL_TILE)[None, :] + (part - 1) * POOL_TILE
            bands[g, part] = (s >= t - win // 2) & (s < t - win // 2 + win)
    return jnp.asarray(bands, BF16)


def _pool_kernel(t_seq, gd, up_ref, um_ref, un_ref, band_ref, pw_ref, ps_ref, y_ref):
    j = pl.program_id(1)
    has_prev = (j > 0).astype(F32)
    has_next = (j < pl.num_programs(1) - 1).astype(F32)
    tile = um_ref.shape[0]
    pos = j * tile + lax.broadcasted_iota(jnp.int32, (tile, 1), 0)
    groups = range(len(POOL_WINDOWS))
    cols = [slice(g * gd, (g + 1) * gd) for g in groups]
    acc = [_dot(band_ref[g, 1], um_ref[:, cols[g]])
           + has_prev * _dot(band_ref[g, 0], up_ref[:, cols[g]])
           + has_next * _dot(band_ref[g, 2], un_ref[:, cols[g]]) for g in groups]
    pooled = []
    for g, win in enumerate(POOL_WINDOWS):
        lo = jnp.clip(pos - win // 2, 0, t_seq)
        hi = jnp.clip(pos - win // 2 + win, 0, t_seq)
        pooled.append((acc[g] / (hi - lo).astype(F32) - um_ref[:, cols[g]].astype(F32)).astype(BF16))
    y = [_dot(pooled[g], pw_ref[g]) * ps_ref[:, cols[g]] for g in groups]
    for g in groups:
        y_ref[:, cols[g]] = y[g].astype(y_ref.dtype)


def _pool(pb, bands, pool_w, pool_scale, layer, n_rows, row0, batch, t, base):
    gd = pool_w.shape[-1]
    w = POOL_GROUPS * gd
    tile = POOL_TILE
    assert t % tile == 0
    nt = t // tile
    blk0 = row0 // tile

    def u_spec(shift):
        return pl.BlockSpec((tile, w), lambda b, j: (blk0 + b * nt + jnp.clip(j + shift, 0, nt - 1), 2))

    return _call(
        functools.partial(_pool_kernel, t, gd), "multiscale_pool", (batch, nt),
        [u_spec(-1), u_spec(0), u_spec(1),
         pl.BlockSpec(bands.shape, lambda b, j: (0, 0, 0, 0)),
         pl.BlockSpec((None, POOL_GROUPS, gd, gd), lambda b, j: (layer, 0, 0, 0)),
         pl.BlockSpec((None, 1, w), lambda b, j: (layer, 0, 0))],
        [pb, pb, pb, bands, pool_w, pool_scale],
        pl.BlockSpec((tile, w), lambda b, j: (blk0 + b * nt + j, 0)),
        jax.ShapeDtypeStruct((n_rows, w), BF16),
        ("parallel", "parallel"), bases=(base,))


def _merge_kernel(ya_ref, yb_ref, yc_ref, ga_ref, gb_ref, gc_ref, w_ref, o_ref):
    branch = [_dot(y_ref[...], w_ref[k]) for k, y_ref in enumerate((ya_ref, yb_ref, yc_ref))]
    gates = [_sigmoid(g_ref[...].astype(F32)) for g_ref in (ga_ref, gb_ref, gc_ref)]
    o_ref[...] = (gates[0] * branch[0] + gates[1] * branch[1] + gates[2] * branch[2]).astype(o_ref.dtype)


def _merge(y_a, y_b, y_c, pb, w_branch, rows, layer, gate_col0):
    n, bw = y_a.shape
    d = w_branch.shape[-1]
    tm = rows.tile(1024)
    tn = min(1024, d)
    g0 = gate_col0 // tn
    nd = d // tn

    def gate_spec(k):
        return pl.BlockSpec((tm, tn), lambda i, j: (i, g0 + k * nd + j))

    y_spec = pl.BlockSpec((tm, bw), lambda i, j: (i, 0))
    return _call(
        _merge_kernel, "branch_merge", (n // tm, nd),
        [y_spec, y_spec, y_spec, gate_spec(0), gate_spec(1), gate_spec(2),
         pl.BlockSpec((None, N_BRANCH, bw, tn), lambda i, j: (layer, 0, 0, j))],
        [y_a, y_b, y_c, pb, pb, pb, w_branch],
        pl.BlockSpec((tm, tn), lambda i, j: (i, j)),
        jax.ShapeDtypeStruct((n, d), BF16),
        ("parallel", "arbitrary"))


def _outproj_kernel(m_ref, w_ref, x_ref, g_ref, o_ref):
    o_ref[...] = x_ref[...] + g_ref[...] * _dot(m_ref[...], w_ref[...])


def _outproj(merged, w_out, x, mod, rows, layer):
    n, d = x.shape
    tm = rows.tile(512)
    tn = d
    return _call(
        _outproj_kernel, "mixer_out_proj", (n // tm, d // tn),
        [pl.BlockSpec((tm, d), lambda i, j: (i, 0)),
         pl.BlockSpec((None, d, tn), lambda i, j: (layer, 0, j)),
         pl.BlockSpec((tm, tn), lambda i, j: (i, j)),
         pl.BlockSpec((None, None, None, 1, tn),
                      lambda i, j: (layer, rows.mod_row(i * tm), 5, 0, j))],
        [merged, w_out, x, mod],
        pl.BlockSpec((tm, tn), lambda i, j: (i, j)),
        jax.ShapeDtypeStruct((n, d), F32),
        ("parallel", "arbitrary"))


def _final_norm_kernel(x_ref, w_ref, o_ref):
    o_ref[...] = _rms(x_ref[...], w_ref[...])


def _final_norm(x, w, row0, n_rows):
    d = x.shape[1]
    tm = min(512, n_rows)
    return _call(
        _final_norm_kernel, "final_norm", (n_rows // tm,),
        [pl.BlockSpec((tm, d), lambda i: (row0 // tm + i, 0)),
         pl.BlockSpec((1, d), lambda i: (0, 0))],
        [x, w.reshape(1, d)],
        pl.BlockSpec((tm, d), lambda i: (i, 0)),
        jax.ShapeDtypeStruct((n_rows, d), F32),
        ("parallel",))


def _gate_up_prep_kernel(valid, g_ref, u_ref, og_ref, ou_ref):
    for src, dst in ((g_ref, og_ref), (u_ref, ou_ref)):
        dst[:, :valid] = src[...].astype(BF16)
        if dst.shape[1] > valid:
            dst[:, valid:] = jnp.zeros((dst.shape[0], dst.shape[1] - valid), BF16)


def _gate_up_prep(w_gu, hp):
    depth, n_ffn, d, h2 = w_gu.shape
    h = h2 // 2
    assert h % LANES == 0
    n_rows = depth * n_ffn * d
    tr = 256
    flat = w_gu.reshape(n_rows, h2)
    out = jax.ShapeDtypeStruct((n_rows, hp), BF16)
    w_g, w_u = _call(
        functools.partial(_gate_up_prep_kernel, h), "ffn_gate_up_prep", (n_rows // tr,),
        [pl.BlockSpec((tr, h), lambda r: (r, 0)), pl.BlockSpec((tr, h), lambda r: (r, 1))],
        [flat, flat],
        [pl.BlockSpec((tr, hp), lambda r: (r, 0))] * 2, [out, out], ("parallel",))
    return w_g.reshape(depth, n_ffn, d, hp), w_u.reshape(depth, n_ffn, d, hp)


def _down_prep_kernel(valid, w_ref, o_ref):
    o_ref[:valid, :] = w_ref[...].astype(BF16)
    if o_ref.shape[0] > valid:
        o_ref[valid:, :] = jnp.zeros((o_ref.shape[0] - valid, o_ref.shape[1]), BF16)


def _down_prep(w_down, hp):
    depth, n_ffn, h, d = w_down.shape
    assert h % PACK_ROWS == 0
    td = _lane_tile(d, 256)
    flat = w_down.reshape(depth * n_ffn, h, d)
    out = _call(
        functools.partial(_down_prep_kernel, h), "ffn_down_prep", (depth * n_ffn, d // td),
        [pl.BlockSpec((None, h, td), lambda a, j: (a, 0, j))], [flat],
        pl.BlockSpec((None, hp, td), lambda a, j: (a, 0, j)),
        jax.ShapeDtypeStruct((depth * n_ffn, hp, d), BF16), ("parallel", "parallel"))
    return out.reshape(depth, n_ffn, hp, d)


def _window(ref, start, stop):
    lo = start - start % LANES
    hi = min(_round_up(stop, LANES), ref.shape[1])
    return ref[:, lo:hi][:, start - lo:stop - lo]


def _in_proj_prep_kernel(offs, n_gate, w_ref, small_ref, big_ref, feat_ref):
    tr = w_ref.shape[0]
    k_pe = _window(w_ref, offs[2], offs[3]).astype(BF16)
    quarter = ROPE_DIM // 4
    src = lax.broadcasted_iota(jnp.int32, (ROPE_DIM, ROPE_DIM), 0)
    dst = lax.broadcasted_iota(jnp.int32, (ROPE_DIM, ROPE_DIM), 1)
    swapped_dst = jnp.where((dst // quarter) % 2 == 0, dst + quarter, dst - quarter)
    select = (src == swapped_dst).astype(BF16)
    small_ref[:, :offs[2]] = w_ref[:, :offs[2]].astype(BF16)
    small_ref[:, offs[2]:offs[2] + 2 * ROPE_DIM] = jnp.concatenate(
        [k_pe, _dot(k_pe, select).astype(BF16)], axis=1)
    gates = _window(w_ref, offs[7], offs[8]).astype(BF16)
    small_ref[:, offs[2] + 2 * ROPE_DIM:] = jnp.concatenate(
        [gates, jnp.zeros((tr, LANES - n_gate), BF16)], axis=1)
    qk = offs[5] - offs[3]
    big_ref[:, :qk] = _window(w_ref, offs[3], offs[5]).astype(BF16)
    big_ref[:, qk:] = _window(w_ref, offs[8], offs[10]).astype(BF16)
    vo = _window(w_ref, offs[5], offs[7]).astype(BF16)
    cw = 2 * LANES
    eye = (lax.broadcasted_iota(jnp.int32, (cw, cw), 0)
           == lax.broadcasted_iota(jnp.int32, (cw, cw), 1)).astype(BF16)
    for c in range(0, vo.shape[1], cw):
        feat_ref[c:c + cw, :] = _dot_nt(eye, vo[:, c:c + cw]).astype(BF16)


def _in_proj_prep(w_in, offs, n_gate):
    depth, d, cols = w_in.shape
    offs = tuple(int(o) for o in offs)
    small_cols = offs[2] + 2 * ROPE_DIM + LANES
    big_cols = (offs[5] - offs[3]) + (offs[10] - offs[8])
    feat_rows = offs[7] - offs[5]
    assert offs[2] % LANES == 0 and feat_rows % (2 * LANES) == 0 and big_cols % LANES == 0
    tr = 256
    return _call(
        functools.partial(_in_proj_prep_kernel, offs, n_gate), "in_proj_weight_prep", (depth, d // tr),
        [pl.BlockSpec((None, tr, cols), lambda l, i: (l, i, 0))], [w_in],
        [pl.BlockSpec((None, tr, small_cols), lambda l, i: (l, i, 0)),
         pl.BlockSpec((None, tr, big_cols), lambda l, i: (l, i, 0)),
         pl.BlockSpec((None, feat_rows, tr), lambda l, i: (l, 0, i))],
        [jax.ShapeDtypeStruct((depth, d, small_cols), BF16),
         jax.ShapeDtypeStruct((depth, d, big_cols), BF16),
         jax.ShapeDtypeStruct((depth, feat_rows, d), BF16)],
        ("parallel", "parallel"))


def _rope_swap_index():
    quarter = ROPE_DIM // 4
    idx = np.arange(ROPE_DIM).reshape(2, 2, quarter)
    return idx[:, ::-1, :].reshape(-1)


def _rope_tables(t):
    pos = jnp.arange(t)
    row = (pos // GRID_W).astype(F32)
    col = (pos % GRID_W).astype(F32)
    n_freq = ROPE_DIM // 4
    inv_freq = jnp.power(ROPE_BASE, -jnp.arange(n_freq, dtype=F32) / n_freq)
    ang_r = row[:, None] * inv_freq
    ang_c = col[:, None] * inv_freq
    cos = jnp.concatenate([jnp.cos(ang_r), jnp.cos(ang_r), jnp.cos(ang_c), jnp.cos(ang_c)], axis=-1)
    sin = jnp.concatenate([-jnp.sin(ang_r), jnp.sin(ang_r), -jnp.sin(ang_c), jnp.sin(ang_c)], axis=-1)
    return cos, sin


def kernel(x_prompt, x_sample, cache_ckv, cache_kpe, state_C, state_n, state_m, c, c_ctx, w_mod, b_mod, norm_w, ffn_w_gu, ffn_w_down, w_in, q_norm_w, kv_norm_w, w_uq, w_ukv, mlstm_gate_b, mlstm_norm_w, pool_w, pool_scale, w_branch, w_out, final_norm_w):
    batch, seq, d = x_prompt.shape
    dec_batch, dec_seq, _ = x_sample.shape
    depth = w_mod.shape[0]
    q_rank, kv_rank = q_norm_w.shape[1], kv_norm_w.shape[1]
    heads = MLA_HEADS
    mw = mlstm_norm_w.shape[1]
    dh = mw // MLSTM_HEADS
    pw = pool_scale.shape[1]
    ffn_h = ffn_w_down.shape[2]
    n_gate = N_DIR * 2 * MLSTM_HEADS
    assert mw == pw == w_branch.shape[2] == heads * V_DIM
    rows = _Rows(batch * seq, seq, dec_batch * dec_seq, dec_seq)
    n = rows.n

    hp = _round_up(ffn_h, 512)
    w_g, w_u = _gate_up_prep(ffn_w_gu, hp)
    wdn = _down_prep(ffn_w_down, hp)

    sizes = (q_rank, kv_rank, ROPE_DIM, mw, mw, mw, mw, n_gate, pw, N_BRANCH * d)
    offs = np.concatenate([[0], np.cumsum(sizes)])
    swap = _rope_swap_index()
    small_cols = q_rank + kv_rank + 2 * ROPE_DIM + LANES
    w_in16 = jnp.pad(w_in.astype(BF16), ((0, 0), (0, 0), (0, _round_up(w_in.shape[2], LANES) - w_in.shape[2])))
    w_small, w_big, w_feat = _in_proj_prep(w_in16, offs, n_gate)
    gate_blk = (q_rank + kv_rank + 2 * ROPE_DIM) // LANES
    gate_col0 = 3 * mw

    wq4 = w_uq.reshape(depth, q_rank, heads, QK_DIM)
    wq_pe = wq4[..., NOPE_DIM:]
    wq = jnp.concatenate([wq4[..., :NOPE_DIM].reshape(depth, q_rank, -1),
                          wq_pe.reshape(depth, q_rank, -1),
                          wq_pe[..., swap].reshape(depth, q_rank, -1)], axis=-1).astype(BF16)
    wkv4 = w_ukv.reshape(depth, kv_rank, heads, NOPE_DIM + V_DIM)
    wkv = jnp.concatenate([wkv4[..., :NOPE_DIM].reshape(depth, kv_rank, -1),
                           wkv4[..., NOPE_DIM:].reshape(depth, kv_rank, -1)], axis=-1).astype(BF16)
    wbr = w_branch.astype(BF16)
    wout = w_out.astype(BF16)
    pwb = pool_w.astype(BF16)
    norm_w4 = norm_w.reshape(depth, 3, 1, d)
    qnw = q_norm_w.reshape(depth, 1, q_rank)
    kvnw = kv_norm_w.reshape(depth, 1, kv_rank)
    mnw = mlstm_norm_w.reshape(depth, mw, 1)
    psc = pool_scale.reshape(depth, 1, pw)
    bands = _pool_bands()
    rope_tabs = _rope_tables(dec_seq)

    cond = jnp.concatenate([c_ctx[None, :], c, jnp.zeros((COND_ROWS - 1 - dec_batch, d), F32)], axis=0)
    mod = _mod_all(cond, w_mod, b_mod).reshape(depth, COND_ROWS, N_MOD, 1, d)

    cache_kv = _cache_kv(cache_ckv, cache_kpe, wkv)

    x = (x_prompt.reshape(rows.n_ctx, d), x_sample.reshape(rows.n_lat, d))
    new_cache = (None, None)
    new_state = (None, None, None)
    for l in range(depth):
        x, h_mix = _ffn(x, mod, norm_w4, w_g, w_u, wdn, rows, l, 0)

        p32 = _inproj(h_mix, w_small, rows, l, F32, small_cols)
        pb = _inproj(h_mix, w_big, rows, l, BF16, _lane_tile(w_big.shape[2], 1024))
        pb_t = _inproj_t(h_mix, w_feat, rows, l)
        gates_t = p32[:, gate_blk * LANES:gate_blk * LANES + n_gate].T

        q_c, k_c, v_c, *new_cache = _mla_prep(p32, qnw, kvnw, wq, wkv, l, 0, batch, seq, None, new_cache)
        q_s, k_s, v_s = _mla_prep(p32, qnw, kvnw, wq, wkv, l, rows.n_ctx, dec_batch, dec_seq, rope_tabs, None)
        y_a = _attention(q_c, k_c, v_c, None, l, n, 0, None)
        y_a = _attention(q_s, k_s, v_s, cache_kv, l, n, rows.n_ctx, y_a)

        hf_c, hb_c, *new_state = _mlstm(pb, pb_t, p32, gates_t, mlstm_gate_b, None, l, 0, batch, seq,
                                        gate_blk, dh, new_state)
        hf_s, hb_s, _, _, _ = _mlstm(pb, pb_t, p32, gates_t, mlstm_gate_b, (state_C, state_n, state_m), l,
                                     rows.n_ctx, dec_batch, dec_seq, gate_blk, dh, None)
        y_b = _mlstm_post(hf_c, hb_c, pb_t, mnw, l, n, 0, dh, None)
        y_b = _mlstm_post(hf_s, hb_s, pb_t, mnw, l, n, rows.n_ctx, dh, y_b)

        y_c = _pool(pb, bands, pwb, psc, l, n, 0, batch, seq, None)
        y_c = _pool(pb, bands, pwb, psc, l, n, rows.n_ctx, dec_batch, dec_seq, y_c)

        merged = _merge(y_a, y_b, y_c, pb, wbr, rows, l, gate_col0)
        x = _outproj(merged, wout, x, mod, rows, l)
        x = _ffn(x, mod, norm_w4, w_g, w_u, wdn, rows, l, 1)

    y_prompt = _final_norm(x, final_norm_w, 0, rows.n_ctx).reshape(batch, seq, d)
    y_sample = _final_norm(x, final_norm_w, rows.n_ctx, rows.n_lat).reshape(dec_batch, dec_seq, d)
    new_c, new_n, new_m = new_state
    return (y_prompt, y_sample, new_cache[0], new_cache[1], new_c,
            new_n.reshape(batch, depth, N_DIR, MLSTM_HEADS, dh),
            new_m.reshape(batch, depth, N_DIR, MLSTM_HEADS))
```

```python
import functools
import math

import numpy as np
import jax
import jax.numpy as jnp
from jax import lax
from jax.experimental import pallas as pl
from jax.experimental.pallas import tpu as pltpu

GRID_W = 64
EPS = 1e-6
N_MOD = 9
MLA_HEADS = 8
NOPE_DIM = 128
ROPE_DIM = 64
V_DIM = 128
QK_DIM = NOPE_DIM + ROPE_DIM
ROPE_BASE = 10000.0
MLSTM_HEADS = 4
N_DIR = 2
CHUNK = 128
POOL_WINDOWS = (2, 4, 8, 16)
POOL_GROUPS = 4
N_BRANCH = 3

LANES = 128
VMEM_LIMIT_MB = 56
COND_ROWS = 8
PACK_ROWS = 16
V_PAD = 2 * V_DIM

F32 = jnp.float32
BF16 = jnp.bfloat16


def _params(sem):
    return pltpu.CompilerParams(dimension_semantics=sem, vmem_limit_bytes=VMEM_LIMIT_MB << 20)


def _call(kernel, name, grid, in_specs, args, out_specs, out_shape, sem, bases=(), scratch=()):
    n_in = len(args)
    extra = [b for b in bases if b is not None]
    aliases = {}
    for k, b in enumerate(bases):
        if b is not None:
            aliases[n_in + len(aliases)] = k

    def body(*refs):
        kernel(*refs[:n_in], *refs[n_in + len(extra):])

    return pl.pallas_call(
        body if extra else kernel, grid=grid,
        in_specs=list(in_specs) + [pl.BlockSpec(memory_space=pl.ANY)] * len(extra),
        out_specs=out_specs, out_shape=out_shape, input_output_aliases=aliases,
        scratch_shapes=list(scratch), compiler_params=_params(sem), name=name)(*args, *extra)


def _round_up(n, m):
    return (n + m - 1) // m * m


def _lane_tile(n, cap):
    t = cap - cap % LANES
    while n % t:
        t -= LANES
    return t


def _sigmoid(x):
    return 1.0 / (1.0 + jnp.exp(-x))


def _log_sigmoid(x):
    return -(jnp.maximum(-x, 0.0) + jnp.log1p(jnp.exp(-jnp.abs(x))))


def _rms(x, w):
    return x * lax.rsqrt(jnp.mean(x * x, axis=-1, keepdims=True) + EPS) * w


def _dot(a, b):
    return jnp.dot(a, b, preferred_element_type=F32)


def _dot_nt(a, b):
    return lax.dot_general(a, b, (((1,), (1,)), ((), ())), preferred_element_type=F32)


def _mod_kernel(c_ref, w_ref, b_ref, o_ref):
    c = c_ref[...]
    a = (c * _sigmoid(c)).astype(BF16)
    o_ref[...] = _dot(a, w_ref[...].astype(BF16)) + b_ref[...]


def _mod_all(cond, w_mod, b_mod):
    depth, d, nd = w_mod.shape
    tn = _lane_tile(nd, 1024)
    return _call(
        _mod_kernel, "adaln_mod", (depth, nd // tn),
        [pl.BlockSpec((COND_ROWS, d), lambda l, j: (0, 0)),
         pl.BlockSpec((None, d, tn), lambda l, j: (l, 0, j)),
         pl.BlockSpec((None, 1, tn), lambda l, j: (l, 0, j))],
        [cond, w_mod, b_mod.reshape(depth, 1, nd)],
        pl.BlockSpec((None, COND_ROWS, tn), lambda l, j: (l, 0, j)),
        jax.ShapeDtypeStruct((depth, COND_ROWS, nd), F32),
        ("parallel", "parallel"))


class _Rows:
    def __init__(self, n_ctx, t_ctx, n_lat, t_lat):
        self.n_ctx, self.t_ctx, self.n_lat, self.t_lat = n_ctx, t_ctx, n_lat, t_lat
        self.n = n_ctx + n_lat

    def mod_row(self, row):
        return jnp.where(row < self.n_ctx, 0, 1 + (row - self.n_ctx) // self.t_lat)

    def tile(self, cap):
        t = min(cap, self.n_ctx, self.t_lat)
        assert self.n_ctx % t == 0 and self.t_lat % t == 0
        return t


def _mod_spec(rows, tm, layer, k, d):
    return pl.BlockSpec((None, None, None, 1, d),
                        lambda i, j: (layer, rows.mod_row(i * tm), k, 0, 0))


def _norm_mod_to(h_ref, x_ref, nw_ref, sh_ref, sc_ref):
    y = _rms(x_ref[...], nw_ref[...])
    h_ref[...] = (y * (1.0 + sc_ref[...]) + sh_ref[...]).astype(h_ref.dtype)


def _ffn_kernel(emit_next, ctx_tiles, *refs):
    x_refs, refs = (refs[:1], refs[1:]) if ctx_tiles is None else (refs[:2], refs[2:])
    sh_ref, sc_ref, g_ref, nw_ref, wg_ref, wu_ref, wd_ref = refs[:7]
    if emit_next:
        sh2_ref, sc2_ref, nw2_ref, o_ref, h2_ref, h_ref = refs[7:]
    else:
        o_ref, h_ref = refs[7:]
    i = pl.program_id(0)
    j = pl.program_id(1)

    def per_source(fn):
        if ctx_tiles is None:
            fn(x_refs[0])
        else:
            pl.when(i < ctx_tiles)(functools.partial(fn, x_refs[0]))
            pl.when(i >= ctx_tiles)(functools.partial(fn, x_refs[1]))

    def prologue(x_ref):
        _norm_mod_to(h_ref, x_ref, nw_ref, sh_ref, sc_ref)

    def epilogue(x_ref):
        o_ref[...] = x_ref[...] + 0.5 * g_ref[...] * o_ref[...]
        if emit_next:
            _norm_mod_to(h2_ref, o_ref, nw2_ref, sh2_ref, sc2_ref)

    @pl.when(j == 0)
    def _():
        per_source(prologue)
        o_ref[...] = jnp.zeros_like(o_ref)

    h = h_ref[...]
    g = _dot(h, wg_ref[...])
    u = _dot(h, wu_ref[...])
    a = (g * _sigmoid(g) * u).astype(BF16)
    o_ref[...] += _dot(a, wd_ref[...])

    @pl.when(j == pl.num_programs(1) - 1)
    def _():
        per_source(epilogue)


def _ffn(x, mod, norm_w, w_g, w_u, w_down, rows, layer, which):
    tm = rows.tile(512)
    if isinstance(x, tuple):
        ctx_tiles = rows.n_ctx // tm
        x_args = list(x)
        x_specs = [pl.BlockSpec((tm, x[0].shape[1]), lambda i, j: (jnp.minimum(i, ctx_tiles - 1), 0)),
                   pl.BlockSpec((tm, x[0].shape[1]), lambda i, j: (jnp.maximum(i - ctx_tiles, 0), 0))]
    else:
        ctx_tiles = None
        x_args = [x]
        x_specs = [pl.BlockSpec((tm, x.shape[1]), lambda i, j: (i, 0))]
    n, d = rows.n, x_args[0].shape[1]
    hp = w_down.shape[2]
    th = _lane_tile(hp, 512)
    k0 = 0 if which == 0 else 6
    emit_next = which == 0
    w_spec = pl.BlockSpec((None, None, d, th), lambda i, j: (layer, which, 0, j))
    row_spec = pl.BlockSpec((tm, d), lambda i, j: (i, 0))
    norm_spec = lambda k: pl.BlockSpec((None, None, 1, d), lambda i, j: (layer, k, 0, 0))
    in_specs = x_specs + [
        _mod_spec(rows, tm, layer, k0, d),
        _mod_spec(rows, tm, layer, k0 + 1, d),
        _mod_spec(rows, tm, layer, k0 + 2, d),
        norm_spec(2 * which), w_spec, w_spec,
        pl.BlockSpec((None, None, th, d), lambda i, j: (layer, which, j, 0))]
    args = x_args + [mod, mod, mod, norm_w, w_g, w_u, w_down]
    out_specs, out_shape = row_spec, jax.ShapeDtypeStruct((n, d), F32)
    if emit_next:
        in_specs += [_mod_spec(rows, tm, layer, 3, d), _mod_spec(rows, tm, layer, 4, d), norm_spec(1)]
        args += [mod, mod, norm_w]
        out_specs, out_shape = [row_spec, row_spec], [out_shape, jax.ShapeDtypeStruct((n, d), BF16)]
    return _call(
        functools.partial(_ffn_kernel, emit_next, ctx_tiles), "ffn_half_step", (n // tm, hp // th),
        in_specs, args, out_specs, out_shape, ("parallel", "arbitrary"),
        scratch=[pltpu.VMEM((tm, d), BF16)])


def _inproj_kernel(h_ref, w_ref, o_ref):
    o_ref[...] = _dot(h_ref[...], w_ref[...]).astype(o_ref.dtype)


def _inproj(h, w, rows, layer, out_dtype, tn):
    n, d = h.shape
    cols = w.shape[2]
    tm = rows.tile(1024)
    return _call(
        _inproj_kernel, "mixer_in_proj", (n // tm, cols // tn),
        [pl.BlockSpec((tm, d), lambda i, j: (i, 0)),
         pl.BlockSpec((None, d, tn), lambda i, j: (layer, 0, j))],
        [h, w],
        pl.BlockSpec((tm, tn), lambda i, j: (i, j)),
        jax.ShapeDtypeStruct((n, cols), out_dtype),
        ("parallel", "arbitrary"))


def _inproj_t_kernel(h_ref, wt_ref, o_ref):
    o_ref[...] = _dot_nt(wt_ref[...], h_ref[...]).astype(o_ref.dtype)


def _inproj_t(h, w_t, rows, layer):
    n, d = h.shape
    cols = w_t.shape[1]
    tm = rows.tile(1024)
    tn = _lane_tile(cols, 2048)
    return _call(
        _inproj_t_kernel, "mixer_in_proj_t", (n // tm, cols // tn),
        [pl.BlockSpec((tm, d), lambda i, j: (i, 0)),
         pl.BlockSpec((None, tn, d), lambda i, j: (layer, j, 0))],
        [h, w_t],
        pl.BlockSpec((tn, tm), lambda i, j: (j, i)),
        jax.ShapeDtypeStruct((cols, n), BF16),
        ("parallel", "arbitrary"))


def _ones_column(rows):
    lane = lax.broadcasted_iota(jnp.int32, (rows, V_PAD - V_DIM), 1)
    return (lane == 0).astype(BF16)


def _mla_prep_kernel(rope, q_rank, kv_rank, p_ref, qnw_ref, kvnw_ref, wq_ref, wkv_ref, *rest):
    if rope:
        cos_ref, sin_ref, q_ref, k_ref, v_ref = rest
    else:
        q_ref, k_ref, v_ref, ckv_ref, kpe_ref = rest
    heads = MLA_HEADS
    p = p_ref[...]
    c_q = p[:, :q_rank]
    c_kv = p[:, q_rank:q_rank + kv_rank]
    o = q_rank + kv_rank
    k_pe = p[:, o:o + ROPE_DIM]
    k_pe_sw = p[:, o + ROPE_DIM:o + 2 * ROPE_DIM]

    qa = _dot(_rms(c_q, qnw_ref[...]).astype(BF16), wq_ref[...])
    ckv_n = _rms(c_kv, kvnw_ref[...])
    kv = _dot(ckv_n.astype(BF16), wkv_ref[...])
    scale = QK_DIM ** -0.5 * math.log2(math.e)
    if rope:
        cos = cos_ref[...]
        sin = sin_ref[...]
        k_pe = k_pe * cos + k_pe_sw * sin
    else:
        ckv_ref[...] = ckv_n
        kpe_ref[...] = k_pe
    ones = _ones_column(p.shape[0])
    pe0 = heads * NOPE_DIM
    sw0 = pe0 + heads * ROPE_DIM
    for h in range(heads):
        q_pe = qa[:, pe0 + h * ROPE_DIM:pe0 + (h + 1) * ROPE_DIM]
        if rope:
            q_pe = q_pe * cos + qa[:, sw0 + h * ROPE_DIM:sw0 + (h + 1) * ROPE_DIM] * sin
        q_ref[h, :, :NOPE_DIM] = (qa[:, h * NOPE_DIM:(h + 1) * NOPE_DIM] * scale).astype(BF16)
        q_ref[h, :, NOPE_DIM:] = (q_pe * scale).astype(BF16)
        k_ref[h, :, :NOPE_DIM] = kv[:, h * NOPE_DIM:(h + 1) * NOPE_DIM].astype(BF16)
        k_ref[h, :, NOPE_DIM:] = k_pe.astype(BF16)
        v0 = heads * NOPE_DIM + h * V_DIM
        v_ref[h, :, :V_DIM] = kv[:, v0:v0 + V_DIM].astype(BF16)
        v_ref[h, :, V_DIM:] = ones


def _mla_prep(p32, q_norm_w, kv_norm_w, wq, wkv, layer, row0, batch, t, rope_tabs, bases):
    ws = p32.shape[1]
    q_rank, kv_rank = q_norm_w.shape[-1], kv_norm_w.shape[-1]
    depth = q_norm_w.shape[0]
    tm = min(256, t)
    nt = t // tm
    heads = MLA_HEADS
    rope = rope_tabs is not None
    in_specs = [pl.BlockSpec((tm, ws), lambda b, i: (row0 // tm + b * nt + i, 0)),
                pl.BlockSpec((None, 1, q_rank), lambda b, i: (layer, 0, 0)),
                pl.BlockSpec((None, 1, kv_rank), lambda b, i: (layer, 0, 0)),
                pl.BlockSpec((None,) + wq.shape[1:], lambda b, i: (layer, 0, 0)),
                pl.BlockSpec((None,) + wkv.shape[1:], lambda b, i: (layer, 0, 0))]
    args = [p32, q_norm_w, kv_norm_w, wq, wkv]
    head_spec = lambda width: pl.BlockSpec((None, heads, tm, width), lambda b, i: (b, 0, i, 0))
    out_specs = [head_spec(QK_DIM), head_spec(QK_DIM), head_spec(V_PAD)]
    out_shape = [jax.ShapeDtypeStruct((batch, heads, t, QK_DIM), BF16),
                 jax.ShapeDtypeStruct((batch, heads, t, QK_DIM), BF16),
                 jax.ShapeDtypeStruct((batch, heads, t, V_PAD), BF16)]
    if rope:
        in_specs += [pl.BlockSpec((tm, ROPE_DIM), lambda b, i: (i, 0))] * 2
        args += list(rope_tabs)
        all_bases = ()
    else:
        out_specs += [pl.BlockSpec((None, None, tm, kv_rank), lambda b, i: (b, layer, i, 0)),
                      pl.BlockSpec((None, None, tm, ROPE_DIM), lambda b, i: (b, layer, i, 0))]
        out_shape += [jax.ShapeDtypeStruct((batch, depth, t, kv_rank), F32),
                      jax.ShapeDtypeStruct((batch, depth, t, ROPE_DIM), F32)]
        all_bases = (None, None, None) + tuple(bases)
    return _call(functools.partial(_mla_prep_kernel, rope, q_rank, kv_rank), "mla_prep",
                 (batch, nt), in_specs, args, out_specs, out_shape, ("parallel", "parallel"),
                 bases=all_bases)


def _cache_kv_kernel(ckv_ref, kpe_ref, wkv_ref, k_ref, v_ref):
    heads = MLA_HEADS
    kv = _dot(ckv_ref[...].astype(BF16), wkv_ref[...])
    k_pe = kpe_ref[...].astype(BF16)
    ones = _ones_column(kv.shape[0])
    for h in range(heads):
        k_ref[h, :, :NOPE_DIM] = kv[:, h * NOPE_DIM:(h + 1) * NOPE_DIM].astype(BF16)
        k_ref[h, :, NOPE_DIM:] = k_pe
        v0 = heads * NOPE_DIM + h * V_DIM
        v_ref[h, :, :V_DIM] = kv[:, v0:v0 + V_DIM].astype(BF16)
        v_ref[h, :, V_DIM:] = ones


def _cache_kv(cache_ckv, cache_kpe, wkv):
    batch, depth, past, kv_rank = cache_ckv.shape
    heads = MLA_HEADS
    return _call(
        _cache_kv_kernel, "mla_cache_kv", (batch, depth),
        [pl.BlockSpec((None, None, past, kv_rank), lambda b, l: (b, l, 0, 0)),
         pl.BlockSpec((None, None, past, ROPE_DIM), lambda b, l: (b, l, 0, 0)),
         pl.BlockSpec((None,) + wkv.shape[1:], lambda b, l: (l, 0, 0))],
        [cache_ckv, cache_kpe, wkv],
        [pl.BlockSpec((None, None, heads, past, QK_DIM), lambda b, l: (b, l, 0, 0, 0)),
         pl.BlockSpec((None, None, heads, past, V_PAD), lambda b, l: (b, l, 0, 0, 0))],
        [jax.ShapeDtypeStruct((batch, depth, heads, past, QK_DIM), BF16),
         jax.ShapeDtypeStruct((batch, depth, heads, past, V_PAD), BF16)],
        ("parallel", "parallel"))


ATTN_KEY_CHUNK = 512


def _attn_kernel(past, q_ref, qn_ref, k_ref, kn_ref, v_ref, *rest):
    if past:
        kc_ref, kcn_ref, vc_ref = rest[:3]
        rest = rest[3:]
    else:
        kc_ref = kcn_ref = vc_ref = None
    o_ref, s0_ref, s1_ref, m0_ref, m1_ref = rest
    tq = qn_ref.shape[0]
    t = k_ref.shape[0]
    chunk = min(ATTN_KEY_CHUNK, t)

    def scores(q, keys_ref, cache_keys_ref, s_ref, m_ref):
        s = _dot_nt(q, keys_ref[...])
        m = jnp.max(s, axis=-1, keepdims=True)
        if past:
            sc = _dot_nt(q, cache_keys_ref[...])
            m = jnp.maximum(m, jnp.max(sc, axis=-1, keepdims=True))
            s_ref[:, :past] = sc
        s_ref[:, past:] = s
        m_ref[...] = m

    def values(r, s_ref, m_ref):
        m = m_ref[...]
        acc = None
        if past:
            acc = _dot(jnp.exp2(s_ref[:, :past] - m).astype(BF16), vc_ref[...])
        for c in range(0, t, chunk):
            p = jnp.exp2(s_ref[:, past + c:past + c + chunk] - m).astype(BF16)
            d = _dot(p, v_ref[c:c + chunk, :])
            acc = d if acc is None else acc + d
        o_ref[r * tq:(r + 1) * tq, :] = (acc[:, :V_DIM] / acc[:, V_DIM:V_DIM + 1]).astype(o_ref.dtype)

    @pl.when(pl.program_id(0) == 0)
    def _():
        scores(q_ref[:tq, :], k_ref, kc_ref, s0_ref, m0_ref)

    values(0, s0_ref, m0_ref)
    scores(q_ref[tq:, :], k_ref, kc_ref, s1_ref, m1_ref)
    values(1, s1_ref, m1_ref)
    scores(qn_ref[...], kn_ref, kcn_ref, s0_ref, m0_ref)


def _attn_short_kernel(q_ref, k_ref, v_ref, o_ref):
    heads = range(q_ref.shape[0])
    s = [_dot_nt(q_ref[h], k_ref[h]) for h in heads]
    p = [jnp.exp2(x - jnp.max(x, axis=-1, keepdims=True)).astype(BF16) for x in s]
    o = [_dot(p[h], v_ref[h]) for h in heads]
    for h in heads:
        o_ref[:, h * V_DIM:(h + 1) * V_DIM] = (o[h][:, :V_DIM] / o[h][:, V_DIM:V_DIM + 1]).astype(o_ref.dtype)


ATTN_SHORT_SEQ = 256


def _attention(q, k, v, cache, layer, n_rows, row0, base):
    batch, heads, t, _ = q.shape
    if cache is None and t <= ATTN_SHORT_SEQ:
        assert row0 % t == 0
        head_block = lambda width: pl.BlockSpec((None, heads, t, width), lambda b: (b, 0, 0, 0))
        return _call(
            _attn_short_kernel, "mla_attention_short", (batch,),
            [head_block(QK_DIM), head_block(QK_DIM), head_block(V_PAD)], [q, k, v],
            pl.BlockSpec((t, heads * V_DIM), lambda b: (row0 // t + b, 0)),
            jax.ShapeDtypeStruct((n_rows, heads * V_DIM), BF16), ("parallel",), bases=(base,))
    tq = min(512, t // 2)
    pair = 2 * tq
    npair = t // pair
    n_steps = batch * heads * npair
    past = cache[0].shape[3] if cache is not None else 0
    assert t % pair == 0 and row0 % pair == 0

    def where(tile):
        tile = jnp.minimum(tile, 2 * n_steps - 1)
        p = tile // 2
        return p // (heads * npair), (p // npair) % heads, 2 * (p % npair) + tile % 2

    def pair_map(g):
        b, h, i = where(2 * g)
        return b, h, i // 2, 0

    def next_map(g):
        b, h, i = where(2 * g + 2)
        return b, h, i, 0

    def kv_map(shift):
        def index(g):
            b, h, _ = where(2 * g + shift)
            return b, h, 0, 0
        return index

    def cache_map(shift):
        def index(g):
            b, h, _ = where(2 * g + shift)
            return b, layer, h, 0, 0
        return index

    def o_map(g):
        b, h, i = where(2 * g)
        return row0 // pair + b * npair + i // 2, h

    in_specs = [pl.BlockSpec((None, None, pair, QK_DIM), pair_map),
                pl.BlockSpec((None, None, tq, QK_DIM), next_map),
                pl.BlockSpec((None, None, t, QK_DIM), kv_map(0)),
                pl.BlockSpec((None, None, t, QK_DIM), kv_map(2)),
                pl.BlockSpec((None, None, t, V_PAD), kv_map(0))]
    args = [q, q, k, k, v]
    if past:
        in_specs += [pl.BlockSpec((None, None, None, past, QK_DIM), cache_map(0)),
                     pl.BlockSpec((None, None, None, past, QK_DIM), cache_map(2)),
                     pl.BlockSpec((None, None, None, past, V_PAD), cache_map(0))]
        args += [cache[0], cache[0], cache[1]]
    s_total = past + t
    return _call(
        functools.partial(_attn_kernel, past), "mla_attention", (n_steps,),
        in_specs, args,
        pl.BlockSpec((pair, V_DIM), o_map),
        jax.ShapeDtypeStruct((n_rows, heads * V_DIM), BF16),
        ("arbitrary",), bases=(base,),
        scratch=[pltpu.VMEM((tq, s_total), F32), pltpu.VMEM((tq, s_total), F32),
                 pltpu.VMEM((tq, 1), F32), pltpu.VMEM((tq, 1), F32)])


MLSTM_CHUNKS_PER_STEP = 2


def _split3(x):
    hi = x.astype(BF16)
    r = x - hi.astype(F32)
    mid = r.astype(BF16)
    return hi, mid, (r - mid.astype(F32)).astype(BF16)


def _mlstm_kernel(has_init, dh, n_sub, *refs):
    (qf_ref, kf_ref, vf_ref, qb_ref, kb_ref, vb_ref, gf_ref, gb_ref, gtf_ref, gtb_ref,
     brow_ref, bcol_ref) = refs[:12]
    refs = refs[12:]
    if has_init:
        c0_ref, n0_ref, m0_ref = refs[:3]
        refs = refs[3:]
    hf_ref, hb_ref, c_ref, n_ref, m_ref = refs
    heads = MLSTM_HEADS
    n_gate = N_DIR * 2 * heads
    step = pl.program_id(1)

    @pl.when(step == 0)
    def _():
        if has_init:
            c_ref[...] = c0_ref[...]
            n_ref[...] = n0_ref[...]
            m_ref[...] = m0_ref[...]
        else:
            c_ref[...] = jnp.zeros_like(c_ref)
            n_ref[...] = jnp.zeros_like(n_ref)
            m_ref[...] = jnp.zeros_like(m_ref)

    tok0 = lax.broadcasted_iota(jnp.int32, (CHUNK, CHUNK), 0)
    tok1 = lax.broadcasted_iota(jnp.int32, (CHUNK, CHUNK), 1)
    k_scale = dh ** -0.5
    m_all = m_ref[...]
    m_out = m_all
    unit_lane = lax.broadcasted_iota(jnp.int32, m_all.shape, 1)
    units = [(d, h) for d in range(N_DIR) for h in range(heads)]
    for sub in range(n_sub):
        m_all = m_out
        gate = {}
        for d in range(N_DIR):
            g_ref, gt_ref = (gf_ref, gtf_ref) if d == 0 else (gb_ref, gtb_ref)
            chunk_idx = sub if d == 0 else n_sub - 1 - sub
            tok = slice(chunk_idx * CHUNK, (chunk_idx + 1) * CHUNK)
            seen_t = (tok0 <= tok1) if d == 0 else (tok0 >= tok1)
            seen_t_bf = seen_t.astype(BF16)
            seen_bf = ((tok1 <= tok0) if d == 0 else (tok1 >= tok0)).astype(BF16)
            pre_col = g_ref[tok, :n_gate] + brow_ref[...]
            pre_row = gt_ref[:, tok] + bcol_ref[...]
            cum_col = sum(_dot(seen_bf, part) for part in _split3(_log_sigmoid(pre_col)))
            cum_row = sum(_dot(part, seen_t_bf) for part in _split3(_log_sigmoid(pre_row)))
            gate[d] = (tok, seen_t, pre_col, pre_row, cum_col, cum_row)

        st = {}
        for d, h in units:
            tok, seen_t, pre_col, pre_row, cum_col, cum_row = gate[d]
            q_ref, k_ref, vt_ref = (qf_ref, kf_ref, vf_ref) if d == 0 else (qb_ref, kb_ref, vb_ref)
            ci = d * 2 * heads + h
            cf = ci + heads
            sid = d * heads + h
            sl = slice(h * dh, (h + 1) * dh)
            last = CHUNK - 1 if d == 0 else 0
            c_col = pre_col[:, ci:ci + 1] - cum_col[:, cf:cf + 1]
            i_row = pre_row[ci:ci + 1, :]
            b_row = cum_row[cf:cf + 1, :]
            b_end = b_row[:, last:last + 1]
            m_prev = m_all[:, sid:sid + 1]
            a_row = b_row + m_prev
            dmat = jnp.where(seen_t, b_row + c_col, -jnp.inf)
            m_t = jnp.maximum(a_row, jnp.max(dmat, axis=0, keepdims=True))
            q = q_ref[tok, sl]
            k_bf = (k_ref[tok, sl].astype(F32) * k_scale).astype(BF16)
            st[d, h] = dict(tok=tok, sl=sl, sid=sid, i_row=i_row, b_row=b_row, b_end=b_end, m_prev=m_prev,
                            m_t=m_t, w_intra=jnp.exp(dmat - m_t), w_inter=jnp.exp(a_row - m_t),
                            q=q, k_bf=k_bf, v_t=vt_ref[sl, tok], c_prev=c_ref[d, h], n_prev=n_ref[d, h])

        for u in units:
            x = st[u]
            x["s_t"] = _dot_nt(x["k_bf"], x["q"]) * x["w_intra"]
            n_rows = jnp.broadcast_to(x["n_prev"], (PACK_ROWS, dh)).astype(BF16)
            x["cq"] = _dot_nt(jnp.concatenate([x["c_prev"].astype(BF16), n_rows], axis=0), x["q"])

        for d, h in units:
            x = st[d, h]
            h_ref = hf_ref if d == 0 else hb_ref
            num = x["w_inter"] * x["cq"][:dh, :] + _dot(x["v_t"], x["s_t"].astype(BF16))
            den = x["w_inter"] * x["cq"][dh:dh + 1, :] + jnp.sum(x["s_t"], axis=0, keepdims=True)
            h_ref[x["sl"], x["tok"]] = num / jnp.maximum(jnp.abs(den), jnp.exp(-x["m_t"]))

        for d, h in units:
            x = st[d, h]
            g_row = x["b_end"] - x["b_row"] + x["i_row"]
            m_new = jnp.maximum(x["b_end"] + x["m_prev"], jnp.max(g_row, axis=1, keepdims=True))
            w_pos = jnp.exp(g_row - m_new)
            w_carry = jnp.exp(x["b_end"] + x["m_prev"] - m_new)
            w_rows = jnp.broadcast_to(w_pos, (PACK_ROWS, CHUNK)).astype(BF16)
            upd = _dot(jnp.concatenate([(x["v_t"].astype(F32) * w_pos).astype(BF16), w_rows], axis=0), x["k_bf"])
            c_ref[d, h] = w_carry * x["c_prev"] + upd[:dh, :]
            n_ref[d, h] = w_carry * x["n_prev"] + upd[dh:dh + 1, :]
            m_out = jnp.where(unit_lane == x["sid"], m_new, m_out)
    m_ref[...] = m_out


def _mlstm(pb, pb_t, p32, gates_t, gate_b, state, layer, row0, batch, t, gate_blk, dh, bases):
    heads = MLSTM_HEADS
    n_gate = N_DIR * 2 * heads
    depth = gate_b.shape[0]
    n_sub = MLSTM_CHUNKS_PER_STEP if (t // CHUNK) % MLSTM_CHUNKS_PER_STEP == 0 else 1
    span = n_sub * CHUNK
    nc = t // span
    assert row0 % span == 0
    blk0 = row0 // span
    w = heads * dh
    fwd = lambda b, c: blk0 + b * nc + c
    bwd = lambda b, c: blk0 + b * nc + nc - 1 - c

    def tok(col, blk):
        return pl.BlockSpec((span, w), lambda b, c: (blk(b, c), col))

    def feat(blk):
        return pl.BlockSpec((w, span), lambda b, c: (0, blk(b, c)))

    in_specs = [tok(0, fwd), tok(1, fwd), feat(fwd), tok(0, bwd), tok(1, bwd), feat(bwd),
                pl.BlockSpec((span, LANES), lambda b, c: (fwd(b, c), gate_blk)),
                pl.BlockSpec((span, LANES), lambda b, c: (bwd(b, c), gate_blk)),
                pl.BlockSpec((n_gate, span), lambda b, c: (0, fwd(b, c))),
                pl.BlockSpec((n_gate, span), lambda b, c: (0, bwd(b, c))),
                pl.BlockSpec((None, 1, n_gate), lambda b, c: (layer, 0, 0)),
                pl.BlockSpec((None, n_gate, 1), lambda b, c: (layer, 0, 0))]
    args = [pb, pb, pb_t, pb, pb, pb_t, p32, p32, gates_t, gates_t,
            gate_b.reshape(-1, 1, n_gate), gate_b.reshape(-1, n_gate, 1)]
    has_init = state is not None
    state_shapes = [(N_DIR, heads, dh, dh), (N_DIR, heads, 1, dh), (1, N_DIR * heads)]
    if has_init:
        c0, n0, m0 = state
        in_specs += [pl.BlockSpec((None, None) + shp, lambda b, c, z=(0,) * len(shp): (b, layer) + z)
                     for shp in state_shapes]
        args += [c0, n0.reshape((batch, depth) + state_shapes[1]), m0.reshape((batch, depth) + state_shapes[2])]
        st_specs = [pl.BlockSpec((None,) + shp, lambda b, c, z=(0,) * len(shp): (b,) + z) for shp in state_shapes]
        st_shapes = [jax.ShapeDtypeStruct((batch,) + shp, F32) for shp in state_shapes]
        all_bases = ()
    else:
        st_specs = [pl.BlockSpec((None, None) + shp, lambda b, c, z=(0,) * len(shp): (b, layer) + z)
                    for shp in state_shapes]
        st_shapes = [jax.ShapeDtypeStruct((batch, depth) + shp, F32) for shp in state_shapes]
        all_bases = (None, None) + tuple(bases)
    return _call(
        functools.partial(_mlstm_kernel, has_init, dh, n_sub), "mlstm_scan", (batch, nc), in_specs, args,
        [pl.BlockSpec((w, span), lambda b, c: (0, b * nc + c)),
         pl.BlockSpec((w, span), lambda b, c: (0, b * nc + nc - 1 - c))] + st_specs,
        [jax.ShapeDtypeStruct((w, batch * t), F32),
         jax.ShapeDtypeStruct((w, batch * t), F32)] + st_shapes,
        ("parallel", "arbitrary"), bases=all_bases)


def _mlstm_post_kernel(dh, hf_ref, hb_ref, o_ref, w_ref, y_ref):
    tm = hf_ref.shape[1]
    hm = hf_ref[...] + hb_ref[...]
    gate = _sigmoid(o_ref[...].astype(F32))
    w = w_ref[...]
    eye = (lax.broadcasted_iota(jnp.int32, (tm, tm), 0)
           == lax.broadcasted_iota(jnp.int32, (tm, tm), 1)).astype(BF16)
    for h in range(MLSTM_HEADS):
        sl = slice(h * dh, (h + 1) * dh)
        x = hm[sl, :]
        y = x * lax.rsqrt(jnp.mean(x * x, axis=0, keepdims=True) + EPS) * w[sl, :]
        y_t = (gate[sl, :] * y).astype(BF16)
        y_ref[:, sl] = _dot_nt(eye, y_t).astype(y_ref.dtype)


def _mlstm_post(h_f, h_b, pb_t, m_norm_w, layer, n_rows, row0, dh, base):
    w, n = h_f.shape
    tm = min(256, n)
    return _call(
        functools.partial(_mlstm_post_kernel, dh), "mlstm_post", (n // tm,),
        [pl.BlockSpec((w, tm), lambda i: (0, i)),
         pl.BlockSpec((w, tm), lambda i: (0, i)),
         pl.BlockSpec((w, tm), lambda i: (1, row0 // tm + i)),
         pl.BlockSpec((None, w, 1), lambda i: (layer, 0, 0))],
        [h_f, h_b, pb_t, m_norm_w],
        pl.BlockSpec((tm, w), lambda i: (row0 // tm + i, 0)),
        jax.ShapeDtypeStruct((n_rows, w), BF16),
        ("parallel",), bases=(base,))


POOL_TILE = 256


def _pool_bands():
    t = np.arange(POOL_TILE)[:, None]
    bands = np.zeros((POOL_GROUPS, 3, POOL_TILE, POOL_TILE), np.float32)
    for g, win in enumerate(POOL_WINDOWS):
        for part in range(3):
            s = np.arange(POOL_TILE)[None, :] + (part - 1) * POOL_TILE
            bands[g, part] = (s >= t - win // 2) & (s < t - win // 2 + win)
    return jnp.asarray(bands, BF16)


def _pool_kernel(t_seq, gd, up_ref, um_ref, un_ref, band_ref, pw_ref, ps_ref, y_ref):
    j = pl.program_id(1)
    has_prev = (j > 0).astype(F32)
    has_next = (j < pl.num_programs(1) - 1).astype(F32)
    tile = um_ref.shape[0]
    pos = j * tile + lax.broadcasted_iota(jnp.int32, (tile, 1), 0)
    groups = range(len(POOL_WINDOWS))
    cols = [slice(g * gd, (g + 1) * gd) for g in groups]
    acc = [_dot(band_ref[g, 1], um_ref[:, cols[g]])
           + has_prev * _dot(band_ref[g, 0], up_ref[:, cols[g]])
           + has_next * _dot(band_ref[g, 2], un_ref[:, cols[g]]) for g in groups]
    pooled = []
    for g, win in enumerate(POOL_WINDOWS):
        lo = jnp.clip(pos - win // 2, 0, t_seq)
        hi = jnp.clip(pos - win // 2 + win, 0, t_seq)
        pooled.append((acc[g] / (hi - lo).astype(F32) - um_ref[:, cols[g]].astype(F32)).astype(BF16))
    y = [_dot(pooled[g], pw_ref[g]) * ps_ref[:, cols[g]] for g in groups]
    for g in groups:
        y_ref[:, cols[g]] = y[g].astype(y_ref.dtype)


def _pool(pb, bands, pool_w, pool_scale, layer, n_rows, row0, batch, t, base):
    gd = pool_w.shape[-1]
    w = POOL_GROUPS * gd
    tile = POOL_TILE
    assert t % tile == 0
    nt = t // tile
    blk0 = row0 // tile

    def u_spec(shift):
        return pl.BlockSpec((tile, w), lambda b, j: (blk0 + b * nt + jnp.clip(j + shift, 0, nt - 1), 2))

    return _call(
        functools.partial(_pool_kernel, t, gd), "multiscale_pool", (batch, nt),
        [u_spec(-1), u_spec(0), u_spec(1),
         pl.BlockSpec(bands.shape, lambda b, j: (0, 0, 0, 0)),
         pl.BlockSpec((None, POOL_GROUPS, gd, gd), lambda b, j: (layer, 0, 0, 0)),
         pl.BlockSpec((None, 1, w), lambda b, j: (layer, 0, 0))],
        [pb, pb, pb, bands, pool_w, pool_scale],
        pl.BlockSpec((tile, w), lambda b, j: (blk0 + b * nt + j, 0)),
        jax.ShapeDtypeStruct((n_rows, w), BF16),
        ("parallel", "parallel"), bases=(base,))


def _merge_kernel(ya_ref, yb_ref, yc_ref, ga_ref, gb_ref, gc_ref, w_ref, o_ref):
    branch = [_dot(y_ref[...], w_ref[k]) for k, y_ref in enumerate((ya_ref, yb_ref, yc_ref))]
    gates = [_sigmoid(g_ref[...].astype(F32)) for g_ref in (ga_ref, gb_ref, gc_ref)]
    o_ref[...] = (gates[0] * branch[0] + gates[1] * branch[1] + gates[2] * branch[2]).astype(o_ref.dtype)


def _merge(y_a, y_b, y_c, pb, w_branch, rows, layer, gate_col0):
    n, bw = y_a.shape
    d = w_branch.shape[-1]
    tm = rows.tile(1024)
    tn = min(1024, d)
    g0 = gate_col0 // tn
    nd = d // tn

    def gate_spec(k):
        return pl.BlockSpec((tm, tn), lambda i, j: (i, g0 + k * nd + j))

    y_spec = pl.BlockSpec((tm, bw), lambda i, j: (i, 0))
    return _call(
        _merge_kernel, "branch_merge", (n // tm, nd),
        [y_spec, y_spec, y_spec, gate_spec(0), gate_spec(1), gate_spec(2),
         pl.BlockSpec((None, N_BRANCH, bw, tn), lambda i, j: (layer, 0, 0, j))],
        [y_a, y_b, y_c, pb, pb, pb, w_branch],
        pl.BlockSpec((tm, tn), lambda i, j: (i, j)),
        jax.ShapeDtypeStruct((n, d), BF16),
        ("parallel", "arbitrary"))


def _outproj_kernel(m_ref, w_ref, x_ref, g_ref, o_ref):
    o_ref[...] = x_ref[...] + g_ref[...] * _dot(m_ref[...], w_ref[...])


def _outproj(merged, w_out, x, mod, rows, layer):
    n, d = x.shape
    tm = rows.tile(512)
    tn = d
    return _call(
        _outproj_kernel, "mixer_out_proj", (n // tm, d // tn),
        [pl.BlockSpec((tm, d), lambda i, j: (i, 0)),
         pl.BlockSpec((None, d, tn), lambda i, j: (layer, 0, j)),
         pl.BlockSpec((tm, tn), lambda i, j: (i, j)),
         pl.BlockSpec((None, None, None, 1, tn),
                      lambda i, j: (layer, rows.mod_row(i * tm), 5, 0, j))],
        [merged, w_out, x, mod],
        pl.BlockSpec((tm, tn), lambda i, j: (i, j)),
        jax.ShapeDtypeStruct((n, d), F32),
        ("parallel", "arbitrary"))


def _final_norm_kernel(x_ref, w_ref, o_ref):
    o_ref[...] = _rms(x_ref[...], w_ref[...])


def _final_norm(x, w, row0, n_rows):
    d = x.shape[1]
    tm = min(512, n_rows)
    return _call(
        _final_norm_kernel, "final_norm", (n_rows // tm,),
        [pl.BlockSpec((tm, d), lambda i: (row0 // tm + i, 0)),
         pl.BlockSpec((1, d), lambda i: (0, 0))],
        [x, w.reshape(1, d)],
        pl.BlockSpec((tm, d), lambda i: (i, 0)),
        jax.ShapeDtypeStruct((n_rows, d), F32),
        ("parallel",))


def _gate_up_prep_kernel(valid, g_ref, u_ref, og_ref, ou_ref):
    for src, dst in ((g_ref, og_ref), (u_ref, ou_ref)):
        dst[:, :valid] = src[...].astype(BF16)
        if dst.shape[1] > valid:
            dst[:, valid:] = jnp.zeros((dst.shape[0], dst.shape[1] - valid), BF16)


def _gate_up_prep(w_gu, hp):
    depth, n_ffn, d, h2 = w_gu.shape
    h = h2 // 2
    assert h % LANES == 0
    n_rows = depth * n_ffn * d
    tr = 256
    flat = w_gu.reshape(n_rows, h2)
    out = jax.ShapeDtypeStruct((n_rows, hp), BF16)
    w_g, w_u = _call(
        functools.partial(_gate_up_prep_kernel, h), "ffn_gate_up_prep", (n_rows // tr,),
        [pl.BlockSpec((tr, h), lambda r: (r, 0)), pl.BlockSpec((tr, h), lambda r: (r, 1))],
        [flat, flat],
        [pl.BlockSpec((tr, hp), lambda r: (r, 0))] * 2, [out, out], ("parallel",))
    return w_g.reshape(depth, n_ffn, d, hp), w_u.reshape(depth, n_ffn, d, hp)


def _down_prep_kernel(valid, w_ref, o_ref):
    o_ref[:valid, :] = w_ref[...].astype(BF16)
    if o_ref.shape[0] > valid:
        o_ref[valid:, :] = jnp.zeros((o_ref.shape[0] - valid, o_ref.shape[1]), BF16)


def _down_prep(w_down, hp):
    depth, n_ffn, h, d = w_down.shape
    assert h % PACK_ROWS == 0
    td = _lane_tile(d, 256)
    flat = w_down.reshape(depth * n_ffn, h, d)
    out = _call(
        functools.partial(_down_prep_kernel, h), "ffn_down_prep", (depth * n_ffn, d // td),
        [pl.BlockSpec((None, h, td), lambda a, j: (a, 0, j))], [flat],
        pl.BlockSpec((None, hp, td), lambda a, j: (a, 0, j)),
        jax.ShapeDtypeStruct((depth * n_ffn, hp, d), BF16), ("parallel", "parallel"))
    return out.reshape(depth, n_ffn, hp, d)


def _window(ref, start, stop):
    lo = start - start % LANES
    hi = min(_round_up(stop, LANES), ref.shape[1])
    return ref[:, lo:hi][:, start - lo:stop - lo]


def _in_proj_prep_kernel(offs, n_gate, w_ref, small_ref, big_ref, feat_ref):
    tr = w_ref.shape[0]
    k_pe = _window(w_ref, offs[2], offs[3]).astype(BF16)
    quarter = ROPE_DIM // 4
    src = lax.broadcasted_iota(jnp.int32, (ROPE_DIM, ROPE_DIM), 0)
    dst = lax.broadcasted_iota(jnp.int32, (ROPE_DIM, ROPE_DIM), 1)
    swapped_dst = jnp.where((dst // quarter) % 2 == 0, dst + quarter, dst - quarter)
    select = (src == swapped_dst).astype(BF16)
    small_ref[:, :offs[2]] = w_ref[:, :offs[2]].astype(BF16)
    small_ref[:, offs[2]:offs[2] + 2 * ROPE_DIM] = jnp.concatenate(
        [k_pe, _dot(k_pe, select).astype(BF16)], axis=1)
    gates = _window(w_ref, offs[7], offs[8]).astype(BF16)
    small_ref[:, offs[2] + 2 * ROPE_DIM:] = jnp.concatenate(
        [gates, jnp.zeros((tr, LANES - n_gate), BF16)], axis=1)
    qk = offs[5] - offs[3]
    big_ref[:, :qk] = _window(w_ref, offs[3], offs[5]).astype(BF16)
    big_ref[:, qk:] = _window(w_ref, offs[8], offs[10]).astype(BF16)
    vo = _window(w_ref, offs[5], offs[7]).astype(BF16)
    cw = 2 * LANES
    eye = (lax.broadcasted_iota(jnp.int32, (cw, cw), 0)
           == lax.broadcasted_iota(jnp.int32, (cw, cw), 1)).astype(BF16)
    for c in range(0, vo.shape[1], cw):
        feat_ref[c:c + cw, :] = _dot_nt(eye, vo[:, c:c + cw]).astype(BF16)


def _in_proj_prep(w_in, offs, n_gate):
    depth, d, cols = w_in.shape
    offs = tuple(int(o) for o in offs)
    small_cols = offs[2] + 2 * ROPE_DIM + LANES
    big_cols = (offs[5] - offs[3]) + (offs[10] - offs[8])
    feat_rows = offs[7] - offs[5]
    assert offs[2] % LANES == 0 and feat_rows % (2 * LANES) == 0 and big_cols % LANES == 0
    tr = 256
    return _call(
        functools.partial(_in_proj_prep_kernel, offs, n_gate), "in_proj_weight_prep", (depth, d // tr),
        [pl.BlockSpec((None, tr, cols), lambda l, i: (l, i, 0))], [w_in],
        [pl.BlockSpec((None, tr, small_cols), lambda l, i: (l, i, 0)),
         pl.BlockSpec((None, tr, big_cols), lambda l, i: (l, i, 0)),
         pl.BlockSpec((None, feat_rows, tr), lambda l, i: (l, 0, i))],
        [jax.ShapeDtypeStruct((depth, d, small_cols), BF16),
         jax.ShapeDtypeStruct((depth, d, big_cols), BF16),
         jax.ShapeDtypeStruct((depth, feat_rows, d), BF16)],
        ("parallel", "parallel"))


def _rope_swap_index():
    quarter = ROPE_DIM // 4
    idx = np.arange(ROPE_DIM).reshape(2, 2, quarter)
    return idx[:, ::-1, :].reshape(-1)


def _rope_tables(t):
    pos = jnp.arange(t)
    row = (pos // GRID_W).astype(F32)
    col = (pos % GRID_W).astype(F32)
    n_freq = ROPE_DIM // 4
    inv_freq = jnp.power(ROPE_BASE, -jnp.arange(n_freq, dtype=F32) / n_freq)
    ang_r = row[:, None] * inv_freq
    ang_c = col[:, None] * inv_freq
    cos = jnp.concatenate([jnp.cos(ang_r), jnp.cos(ang_r), jnp.cos(ang_c), jnp.cos(ang_c)], axis=-1)
    sin = jnp.concatenate([-jnp.sin(ang_r), jnp.sin(ang_r), -jnp.sin(ang_c), jnp.sin(ang_c)], axis=-1)
    return cos, sin


def kernel(x_prompt, x_sample, cache_ckv, cache_kpe, state_C, state_n, state_m, c, c_ctx, w_mod, b_mod, norm_w, ffn_w_gu, ffn_w_down, w_in, q_norm_w, kv_norm_w, w_uq, w_ukv, mlstm_gate_b, mlstm_norm_w, pool_w, pool_scale, w_branch, w_out, final_norm_w):
    batch, seq, d = x_prompt.shape
    dec_batch, dec_seq, _ = x_sample.shape
    depth = w_mod.shape[0]
    q_rank, kv_rank = q_norm_w.shape[1], kv_norm_w.shape[1]
    heads = MLA_HEADS
    mw = mlstm_norm_w.shape[1]
    dh = mw // MLSTM_HEADS
    pw = pool_scale.shape[1]
    ffn_h = ffn_w_down.shape[2]
    n_gate = N_DIR * 2 * MLSTM_HEADS
    assert mw == pw == w_branch.shape[2] == heads * V_DIM
    rows = _Rows(batch * seq, seq, dec_batch * dec_seq, dec_seq)
    n = rows.n

    hp = _round_up(ffn_h, 512)
    w_g, w_u = _gate_up_prep(ffn_w_gu, hp)
    wdn = _down_prep(ffn_w_down, hp)

    sizes = (q_rank, kv_rank, ROPE_DIM, mw, mw, mw, mw, n_gate, pw, N_BRANCH * d)
    offs = np.concatenate([[0], np.cumsum(sizes)])
    swap = _rope_swap_index()
    small_cols = q_rank + kv_rank + 2 * ROPE_DIM + LANES
    w_in16 = jnp.pad(w_in.astype(BF16), ((0, 0), (0, 0), (0, _round_up(w_in.shape[2], LANES) - w_in.shape[2])))
    w_small, w_big, w_feat = _in_proj_prep(w_in16, offs, n_gate)
    gate_blk = (q_rank + kv_rank + 2 * ROPE_DIM) // LANES
    gate_col0 = 3 * mw

    wq4 = w_uq.reshape(depth, q_rank, heads, QK_DIM)
    wq_pe = wq4[..., NOPE_DIM:]
    wq = jnp.concatenate([wq4[..., :NOPE_DIM].reshape(depth, q_rank, -1),
                          wq_pe.reshape(depth, q_rank, -1),
                          wq_pe[..., swap].reshape(depth, q_rank, -1)], axis=-1).astype(BF16)
    wkv4 = w_ukv.reshape(depth, kv_rank, heads, NOPE_DIM + V_DIM)
    wkv = jnp.concatenate([wkv4[..., :NOPE_DIM].reshape(depth, kv_rank, -1),
                           wkv4[..., NOPE_DIM:].reshape(depth, kv_rank, -1)], axis=-1).astype(BF16)
    wbr = w_branch.astype(BF16)
    wout = w_out.astype(BF16)
    pwb = pool_w.astype(BF16)
    norm_w4 = norm_w.reshape(depth, 3, 1, d)
    qnw = q_norm_w.reshape(depth, 1, q_rank)
    kvnw = kv_norm_w.reshape(depth, 1, kv_rank)
    mnw = mlstm_norm_w.reshape(depth, mw, 1)
    psc = pool_scale.reshape(depth, 1, pw)
    bands = _pool_bands()
    rope_tabs = _rope_tables(dec_seq)

    cond = jnp.concatenate([c_ctx[None, :], c, jnp.zeros((COND_ROWS - 1 - dec_batch, d), F32)], axis=0)
    mod = _mod_all(cond, w_mod, b_mod).reshape(depth, COND_ROWS, N_MOD, 1, d)

    cache_kv = _cache_kv(cache_ckv, cache_kpe, wkv)

    x = (x_prompt.reshape(rows.n_ctx, d), x_sample.reshape(rows.n_lat, d))
    new_cache = (None, None)
    new_state = (None, None, None)
    for l in range(depth):
        x, h_mix = _ffn(x, mod, norm_w4, w_g, w_u, wdn, rows, l, 0)

        p32 = _inproj(h_mix, w_small, rows, l, F32, small_cols)
        pb = _inproj(h_mix, w_big, rows, l, BF16, _lane_tile(w_big.shape[2], 1024))
        pb_t = _inproj_t(h_mix, w_feat, rows, l)
        gates_t = p32[:, gate_blk * LANES:gate_blk * LANES + n_gate].T

        q_c, k_c, v_c, *new_cache = _mla_prep(p32, qnw, kvnw, wq, wkv, l, 0, batch, seq, None, new_cache)
        q_s, k_s, v_s = _mla_prep(p32, qnw, kvnw, wq, wkv, l, rows.n_ctx, dec_batch, dec_seq, rope_tabs, None)
        y_a = _attention(q_c, k_c, v_c, None, l, n, 0, None)
        y_a = _attention(q_s, k_s, v_s, cache_kv, l, n, rows.n_ctx, y_a)

        hf_c, hb_c, *new_state = _mlstm(pb, pb_t, p32, gates_t, mlstm_gate_b, None, l, 0, batch, seq,
                                        gate_blk, dh, new_state)
        hf_s, hb_s, _, _, _ = _mlstm(pb, pb_t, p32, gates_t, mlstm_gate_b, (state_C, state_n, state_m), l,
                                     rows.n_ctx, dec_batch, dec_seq, gate_blk, dh, None)
        y_b = _mlstm_post(hf_c, hb_c, pb_t, mnw, l, n, 0, dh, None)
        y_b = _mlstm_post(hf_s, hb_s, pb_t, mnw, l, n, rows.n_ctx, dh, y_b)

        y_c = _pool(pb, bands, pwb, psc, l, n, 0, batch, seq, None)
        y_c = _pool(pb, bands, pwb, psc, l, n, rows.n_ctx, dec_batch, dec_seq, y_c)

        merged = _merge(y_a, y_b, y_c, pb, wbr, rows, l, gate_col0)
        x = _outproj(merged, wout, x, mod, rows, l)
        x = _ffn(x, mod, norm_w4, w_g, w_u, wdn, rows, l, 1)

    y_prompt = _final_norm(x, final_norm_w, 0, rows.n_ctx).reshape(batch, seq, d)
    y_sample = _final_norm(x, final_norm_w, rows.n_ctx, rows.n_lat).reshape(dec_batch, dec_seq, d)
    new_c, new_n, new_m = new_state
    return (y_prompt, y_sample, new_cache[0], new_cache[1], new_c,
            new_n.reshape(batch, depth, N_DIR, MLSTM_HEADS, dh),
            new_m.reshape(batch, depth, N_DIR, MLSTM_HEADS))
```

```python
import functools
import math

import numpy as np
import jax
import jax.numpy as jnp
from jax import lax
from jax.experimental import pallas as pl
from jax.experimental.pallas import tpu as pltpu

GRID_W = 64
EPS = 1e-6
N_MOD = 9
MLA_HEADS = 8
NOPE_DIM = 128
ROPE_DIM = 64
V_DIM = 128
QK_DIM = NOPE_DIM + ROPE_DIM
ROPE_BASE = 10000.0
MLSTM_HEADS = 4
N_DIR = 2
CHUNK = 128
POOL_WINDOWS = (2, 4, 8, 16)
POOL_GROUPS = 4
N_BRANCH = 3

LANES = 128
VMEM_LIMIT_MB = 56
COND_ROWS = 8
PACK_ROWS = 16
V_PAD = 2 * V_DIM

F32 = jnp.float32
BF16 = jnp.bfloat16


def _params(sem):
    return pltpu.CompilerParams(dimension_semantics=sem, vmem_limit_bytes=VMEM_LIMIT_MB << 20)


def _call(kernel, name, grid, in_specs, args, out_specs, out_shape, sem, bases=(), scratch=()):
    n_in = len(args)
    extra = [b for b in bases if b is not None]
    aliases = {}
    for k, b in enumerate(bases):
        if b is not None:
            aliases[n_in + len(aliases)] = k

    def body(*refs):
        kernel(*refs[:n_in], *refs[n_in + len(extra):])

    return pl.pallas_call(
        body if extra else kernel, grid=grid,
        in_specs=list(in_specs) + [pl.BlockSpec(memory_space=pl.ANY)] * len(extra),
        out_specs=out_specs, out_shape=out_shape, input_output_aliases=aliases,
        scratch_shapes=list(scratch), compiler_params=_params(sem), name=name)(*args, *extra)


def _round_up(n, m):
    return (n + m - 1) // m * m


def _lane_tile(n, cap):
    t = cap - cap % LANES
    while n % t:
        t -= LANES
    return t


def _sigmoid(x):
    return 1.0 / (1.0 + jnp.exp(-x))


def _log_sigmoid(x):
    return -(jnp.maximum(-x, 0.0) + jnp.log1p(jnp.exp(-jnp.abs(x))))


def _rms(x, w):
    return x * lax.rsqrt(jnp.mean(x * x, axis=-1, keepdims=True) + EPS) * w


def _dot(a, b):
    return jnp.dot(a, b, preferred_element_type=F32)


def _dot_nt(a, b):
    return lax.dot_general(a, b, (((1,), (1,)), ((), ())), preferred_element_type=F32)


def _mod_kernel(c_ref, w_ref, b_ref, o_ref):
    c = c_ref[...]
    a = (c * _sigmoid(c)).astype(BF16)
    o_ref[...] = _dot(a, w_ref[...].astype(BF16)) + b_ref[...]


def _mod_all(cond, w_mod, b_mod):
    depth, d, nd = w_mod.shape
    tn = _lane_tile(nd, 1024)
    return _call(
        _mod_kernel, "adaln_mod", (depth, nd // tn),
        [pl.BlockSpec((COND_ROWS, d), lambda l, j: (0, 0)),
         pl.BlockSpec((None, d, tn), lambda l, j: (l, 0, j)),
         pl.BlockSpec((None, 1, tn), lambda l, j: (l, 0, j))],
        [cond, w_mod, b_mod.reshape(depth, 1, nd)],
        pl.BlockSpec((None, COND_ROWS, tn), lambda l, j: (l, 0, j)),
        jax.ShapeDtypeStruct((depth, COND_ROWS, nd), F32),
        ("parallel", "parallel"))


class _Rows:
    def __init__(self, n_ctx, t_ctx, n_lat, t_lat):
        self.n_ctx, self.t_ctx, self.n_lat, self.t_lat = n_ctx, t_ctx, n_lat, t_lat
        self.n = n_ctx + n_lat

    def mod_row(self, row):
        return jnp.where(row < self.n_ctx, 0, 1 + (row - self.n_ctx) // self.t_lat)

    def tile(self, cap):
        t = min(cap, self.n_ctx, self.t_lat)
        assert self.n_ctx % t == 0 and self.t_lat % t == 0
        return t


def _mod_spec(rows, tm, layer, k, d):
    return pl.BlockSpec((None, None, None, 1, d),
                        lambda i, j: (layer, rows.mod_row(i * tm), k, 0, 0))


def _norm_mod_to(h_ref, x_ref, nw_ref, sh_ref, sc_ref):
    y = _rms(x_ref[...], nw_ref[...])
    h_ref[...] = (y * (1.0 + sc_ref[...]) + sh_ref[...]).astype(h_ref.dtype)


def _ffn_kernel(emit_next, ctx_tiles, *refs):
    x_refs, refs = (refs[:1], refs[1:]) if ctx_tiles is None else (refs[:2], refs[2:])
    sh_ref, sc_ref, g_ref, nw_ref, wg_ref, wu_ref, wd_ref = refs[:7]
    if emit_next:
        sh2_ref, sc2_ref, nw2_ref, o_ref, h2_ref, h_ref = refs[7:]
    else:
        o_ref, h_ref = refs[7:]
    i = pl.program_id(0)
    j = pl.program_id(1)

    def per_source(fn):
        if ctx_tiles is None:
            fn(x_refs[0])
        else:
            pl.when(i < ctx_tiles)(functools.partial(fn, x_refs[0]))
            pl.when(i >= ctx_tiles)(functools.partial(fn, x_refs[1]))

    def prologue(x_ref):
        _norm_mod_to(h_ref, x_ref, nw_ref, sh_ref, sc_ref)

    def epilogue(x_ref):
        o_ref[...] = x_ref[...] + 0.5 * g_ref[...] * o_ref[...]
        if emit_next:
            _norm_mod_to(h2_ref, o_ref, nw2_ref, sh2_ref, sc2_ref)

    @pl.when(j == 0)
    def _():
        per_source(prologue)
        o_ref[...] = jnp.zeros_like(o_ref)

    h = h_ref[...]
    g = _dot(h, wg_ref[...])
    u = _dot(h, wu_ref[...])
    a = (g * _sigmoid(g) * u).astype(BF16)
    o_ref[...] += _dot(a, wd_ref[...])

    @pl.when(j == pl.num_programs(1) - 1)
    def _():
        per_source(epilogue)


def _ffn(x, mod, norm_w, w_g, w_u, w_down, rows, layer, which):
    tm = rows.tile(512)
    if isinstance(x, tuple):
        ctx_tiles = rows.n_ctx // tm
        x_args = list(x)
        x_specs = [pl.BlockSpec((tm, x[0].shape[1]), lambda i, j: (jnp.minimum(i, ctx_tiles - 1), 0)),
                   pl.BlockSpec((tm, x[0].shape[1]), lambda i, j: (jnp.maximum(i - ctx_tiles, 0), 0))]
    else:
        ctx_tiles = None
        x_args = [x]
        x_specs = [pl.BlockSpec((tm, x.shape[1]), lambda i, j: (i, 0))]
    n, d = rows.n, x_args[0].shape[1]
    hp = w_down.shape[2]
    th = _lane_tile(hp, 512)
    k0 = 0 if which == 0 else 6
    emit_next = which == 0
    w_spec = pl.BlockSpec((None, None, d, th), lambda i, j: (layer, which, 0, j))
    row_spec = pl.BlockSpec((tm, d), lambda i, j: (i, 0))
    norm_spec = lambda k: pl.BlockSpec((None, None, 1, d), lambda i, j: (layer, k, 0, 0))
    in_specs = x_specs + [
        _mod_spec(rows, tm, layer, k0, d),
        _mod_spec(rows, tm, layer, k0 + 1, d),
        _mod_spec(rows, tm, layer, k0 + 2, d),
        norm_spec(2 * which), w_spec, w_spec,
        pl.BlockSpec((None, None, th, d), lambda i, j: (layer, which, j, 0))]
    args = x_args + [mod, mod, mod, norm_w, w_g, w_u, w_down]
    out_specs, out_shape = row_spec, jax.ShapeDtypeStruct((n, d), F32)
    if emit_next:
        in_specs += [_mod_spec(rows, tm, layer, 3, d), _mod_spec(rows, tm, layer, 4, d), norm_spec(1)]
        args += [mod, mod, norm_w]
        out_specs, out_shape = [row_spec, row_spec], [out_shape, jax.ShapeDtypeStruct((n, d), BF16)]
    return _call(
        functools.partial(_ffn_kernel, emit_next, ctx_tiles), "ffn_half_step", (n // tm, hp // th),
        in_specs, args, out_specs, out_shape, ("parallel", "arbitrary"),
        scratch=[pltpu.VMEM((tm, d), BF16)])


def _inproj_kernel(h_ref, w_ref, o_ref):
    o_ref[...] = _dot(h_ref[...], w_ref[...]).astype(o_ref.dtype)


def _inproj(h, w, rows, layer, out_dtype, tn):
    n, d = h.shape
    cols = w.shape[2]
    tm = rows.tile(1024)
    return _call(
        _inproj_kernel, "mixer_in_proj", (n // tm, cols // tn),
        [pl.BlockSpec((tm, d), lambda i, j: (i, 0)),
         pl.BlockSpec((None, d, tn), lambda i, j: (layer, 0, j))],
        [h, w],
        pl.BlockSpec((tm, tn), lambda i, j: (i, j)),
        jax.ShapeDtypeStruct((n, cols), out_dtype),
        ("parallel", "arbitrary"))


def _inproj_t_kernel(h_ref, wt_ref, o_ref):
    o_ref[...] = _dot_nt(wt_ref[...], h_ref[...]).astype(o_ref.dtype)


def _inproj_t(h, w_t, rows, layer):
    n, d = h.shape
    cols = w_t.shape[1]
    tm = rows.tile(1024)
    tn = _lane_tile(cols, 2048)
    return _call(
        _inproj_t_kernel, "mixer_in_proj_t", (n // tm, cols // tn),
        [pl.BlockSpec((tm, d), lambda i, j: (i, 0)),
         pl.BlockSpec((None, tn, d), lambda i, j: (layer, j, 0))],
        [h, w_t],
        pl.BlockSpec((tn, tm), lambda i, j: (j, i)),
        jax.ShapeDtypeStruct((cols, n), BF16),
        ("parallel", "arbitrary"))


def _ones_column(rows):
    lane = lax.broadcasted_iota(jnp.int32, (rows, V_PAD - V_DIM), 1)
    return (lane == 0).astype(BF16)


def _mla_prep_kernel(rope, q_rank, kv_rank, p_ref, qnw_ref, kvnw_ref, wq_ref, wkv_ref, *rest):
    if rope:
        cos_ref, sin_ref, q_ref, k_ref, v_ref = rest
    else:
        q_ref, k_ref, v_ref, ckv_ref, kpe_ref = rest
    heads = MLA_HEADS
    p = p_ref[...]
    c_q = p[:, :q_rank]
    c_kv = p[:, q_rank:q_rank + kv_rank]
    o = q_rank + kv_rank
    k_pe = p[:, o:o + ROPE_DIM]
    k_pe_sw = p[:, o + ROPE_DIM:o + 2 * ROPE_DIM]

    qa = _dot(_rms(c_q, qnw_ref[...]).astype(BF16), wq_ref[...])
    ckv_n = _rms(c_kv, kvnw_ref[...])
    kv = _dot(ckv_n.astype(BF16), wkv_ref[...])
    scale = QK_DIM ** -0.5 * math.log2(math.e)
    if rope:
        cos = cos_ref[...]
        sin = sin_ref[...]
        k_pe = k_pe * cos + k_pe_sw * sin
    else:
        ckv_ref[...] = ckv_n
        kpe_ref[...] = k_pe
    ones = _ones_column(p.shape[0])
    pe0 = heads * NOPE_DIM
    sw0 = pe0 + heads * ROPE_DIM
    for h in range(heads):
        q_pe = qa[:, pe0 + h * ROPE_DIM:pe0 + (h + 1) * ROPE_DIM]
        if rope:
            q_pe = q_pe * cos + qa[:, sw0 + h * ROPE_DIM:sw0 + (h + 1) * ROPE_DIM] * sin
        q_ref[h, :, :NOPE_DIM] = (qa[:, h * NOPE_DIM:(h + 1) * NOPE_DIM] * scale).astype(BF16)
        q_ref[h, :, NOPE_DIM:] = (q_pe * scale).astype(BF16)
        k_ref[h, :, :NOPE_DIM] = kv[:, h * NOPE_DIM:(h + 1) * NOPE_DIM].astype(BF16)
        k_ref[h, :, NOPE_DIM:] = k_pe.astype(BF16)
        v0 = heads * NOPE_DIM + h * V_DIM
        v_ref[h, :, :V_DIM] = kv[:, v0:v0 + V_DIM].astype(BF16)
        v_ref[h, :, V_DIM:] = ones


def _mla_prep(p32, q_norm_w, kv_norm_w, wq, wkv, layer, row0, batch, t, rope_tabs, bases):
    ws = p32.shape[1]
    q_rank, kv_rank = q_norm_w.shape[-1], kv_norm_w.shape[-1]
    depth = q_norm_w.shape[0]
    tm = min(512, t)
    nt = t // tm
    heads = MLA_HEADS
    rope = rope_tabs is not None
    in_specs = [pl.BlockSpec((tm, ws), lambda b, i: (row0 // tm + b * nt + i, 0)),
                pl.BlockSpec((None, 1, q_rank), lambda b, i: (layer, 0, 0)),
                pl.BlockSpec((None, 1, kv_rank), lambda b, i: (layer, 0, 0)),
                pl.BlockSpec((None,) + wq.shape[1:], lambda b, i: (layer, 0, 0)),
                pl.BlockSpec((None,) + wkv.shape[1:], lambda b, i: (layer, 0, 0))]
    args = [p32, q_norm_w, kv_norm_w, wq, wkv]
    head_spec = lambda width: pl.BlockSpec((None, heads, tm, width), lambda b, i: (b, 0, i, 0))
    out_specs = [head_spec(QK_DIM), head_spec(QK_DIM), head_spec(V_PAD)]
    out_shape = [jax.ShapeDtypeStruct((batch, heads, t, QK_DIM), BF16),
                 jax.ShapeDtypeStruct((batch, heads, t, QK_DIM), BF16),
                 jax.ShapeDtypeStruct((batch, heads, t, V_PAD), BF16)]
    if rope:
        in_specs += [pl.BlockSpec((tm, ROPE_DIM), lambda b, i: (i, 0))] * 2
        args += list(rope_tabs)
        all_bases = ()
    else:
        out_specs += [pl.BlockSpec((None, None, tm, kv_rank), lambda b, i: (b, layer, i, 0)),
                      pl.BlockSpec((None, None, tm, ROPE_DIM), lambda b, i: (b, layer, i, 0))]
        out_shape += [jax.ShapeDtypeStruct((batch, depth, t, kv_rank), F32),
                      jax.ShapeDtypeStruct((batch, depth, t, ROPE_DIM), F32)]
        all_bases = (None, None, None) + tuple(bases)
    return _call(functools.partial(_mla_prep_kernel, rope, q_rank, kv_rank), "mla_prep",
                 (batch, nt), in_specs, args, out_specs, out_shape, ("parallel", "parallel"),
                 bases=all_bases)


def _cache_kv_kernel(ckv_ref, kpe_ref, wkv_ref, k_ref, v_ref):
    heads = MLA_HEADS
    kv = _dot(ckv_ref[...].astype(BF16), wkv_ref[...])
    k_pe = kpe_ref[...].astype(BF16)
    ones = _ones_column(kv.shape[0])
    for h in range(heads):
        k_ref[h, :, :NOPE_DIM] = kv[:, h * NOPE_DIM:(h + 1) * NOPE_DIM].astype(BF16)
        k_ref[h, :, NOPE_DIM:] = k_pe
        v0 = heads * NOPE_DIM + h * V_DIM
        v_ref[h, :, :V_DIM] = kv[:, v0:v0 + V_DIM].astype(BF16)
        v_ref[h, :, V_DIM:] = ones


def _cache_kv(cache_ckv, cache_kpe, wkv):
    batch, depth, past, kv_rank = cache_ckv.shape
    heads = MLA_HEADS
    return _call(
        _cache_kv_kernel, "mla_cache_kv", (batch, depth),
        [pl.BlockSpec((None, None, past, kv_rank), lambda b, l: (b, l, 0, 0)),
         pl.BlockSpec((None, None, past, ROPE_DIM), lambda b, l: (b, l, 0, 0)),
         pl.BlockSpec((None,) + wkv.shape[1:], lambda b, l: (l, 0, 0))],
        [cache_ckv, cache_kpe, wkv],
        [pl.BlockSpec((None, None, heads, past, QK_DIM), lambda b, l: (b, l, 0, 0, 0)),
         pl.BlockSpec((None, None, heads, past, V_PAD), lambda b, l: (b, l, 0, 0, 0))],
        [jax.ShapeDtypeStruct((batch, depth, heads, past, QK_DIM), BF16),
         jax.ShapeDtypeStruct((batch, depth, heads, past, V_PAD), BF16)],
        ("parallel", "parallel"))


ATTN_KEY_CHUNK = 512


def _attn_kernel(past, q_ref, qn_ref, k_ref, kn_ref, v_ref, *rest):
    if past:
        kc_ref, kcn_ref, vc_ref = rest[:3]
        rest = rest[3:]
    else:
        kc_ref = kcn_ref = vc_ref = None
    o_ref, s0_ref, s1_ref, m0_ref, m1_ref = rest
    tq = qn_ref.shape[0]
    t = k_ref.shape[0]
    chunk = min(ATTN_KEY_CHUNK, t)

    def scores(q, keys_ref, cache_keys_ref, s_ref, m_ref):
        s = _dot_nt(q, keys_ref[...])
        m = jnp.max(s, axis=-1, keepdims=True)
        if past:
            sc = _dot_nt(q, cache_keys_ref[...])
            m = jnp.maximum(m, jnp.max(sc, axis=-1, keepdims=True))
            s_ref[:, :past] = sc
        s_ref[:, past:] = s
        m_ref[...] = m

    def values(r, s_ref, m_ref):
        m = m_ref[...]
        acc = None
        if past:
            acc = _dot(jnp.exp2(s_ref[:, :past] - m).astype(BF16), vc_ref[...])
        for c in range(0, t, chunk):
            p = jnp.exp2(s_ref[:, past + c:past + c + chunk] - m).astype(BF16)
            d = _dot(p, v_ref[c:c + chunk, :])
            acc = d if acc is None else acc + d
        o_ref[r * tq:(r + 1) * tq, :] = (acc[:, :V_DIM] / acc[:, V_DIM:V_DIM + 1]).astype(o_ref.dtype)

    @pl.when(pl.program_id(0) == 0)
    def _():
        scores(q_ref[:tq, :], k_ref, kc_ref, s0_ref, m0_ref)

    values(0, s0_ref, m0_ref)
    scores(q_ref[tq:, :], k_ref, kc_ref, s1_ref, m1_ref)
    values(1, s1_ref, m1_ref)
    scores(qn_ref[...], kn_ref, kcn_ref, s0_ref, m0_ref)


def _attn_short_kernel(q_ref, k_ref, v_ref, o_ref):
    heads = range(q_ref.shape[0])
    s = [_dot_nt(q_ref[h], k_ref[h]) for h in heads]
    p = [jnp.exp2(x - jnp.max(x, axis=-1, keepdims=True)).astype(BF16) for x in s]
    o = [_dot(p[h], v_ref[h]) for h in heads]
    for h in heads:
        o_ref[:, h * V_DIM:(h + 1) * V_DIM] = (o[h][:, :V_DIM] / o[h][:, V_DIM:V_DIM + 1]).astype(o_ref.dtype)


ATTN_SHORT_SEQ = 256


def _attention(q, k, v, cache, layer, n_rows, row0, base):
    batch, heads, t, _ = q.shape
    if cache is None and t <= ATTN_SHORT_SEQ:
        assert row0 % t == 0
        head_block = lambda width: pl.BlockSpec((None, heads, t, width), lambda b: (b, 0, 0, 0))
        return _call(
            _attn_short_kernel, "mla_attention_short", (batch,),
            [head_block(QK_DIM), head_block(QK_DIM), head_block(V_PAD)], [q, k, v],
            pl.BlockSpec((t, heads * V_DIM), lambda b: (row0 // t + b, 0)),
            jax.ShapeDtypeStruct((n_rows, heads * V_DIM), BF16), ("parallel",), bases=(base,))
    tq = min(512, t // 2)
    pair = 2 * tq
    npair = t // pair
    n_steps = batch * heads * npair
    past = cache[0].shape[3] if cache is not None else 0
    assert t % pair == 0 and row0 % pair == 0

    def where(tile):
        tile = jnp.minimum(tile, 2 * n_steps - 1)
        p = tile // 2
        return p // (heads * npair), (p // npair) % heads, 2 * (p % npair) + tile % 2

    def pair_map(g):
        b, h, i = where(2 * g)
        return b, h, i // 2, 0

    def next_map(g):
        b, h, i = where(2 * g + 2)
        return b, h, i, 0

    def kv_map(shift):
        def index(g):
            b, h, _ = where(2 * g + shift)
            return b, h, 0, 0
        return index

    def cache_map(shift):
        def index(g):
            b, h, _ = where(2 * g + shift)
            return b, layer, h, 0, 0
        return index

    def o_map(g):
        b, h, i = where(2 * g)
        return row0 // pair + b * npair + i // 2, h

    in_specs = [pl.BlockSpec((None, None, pair, QK_DIM), pair_map),
                pl.BlockSpec((None, None, tq, QK_DIM), next_map),
                pl.BlockSpec((None, None, t, QK_DIM), kv_map(0)),
                pl.BlockSpec((None, None, t, QK_DIM), kv_map(2)),
                pl.BlockSpec((None, None, t, V_PAD), kv_map(0))]
    args = [q, q, k, k, v]
    if past:
        in_specs += [pl.BlockSpec((None, None, None, past, QK_DIM), cache_map(0)),
                     pl.BlockSpec((None, None, None, past, QK_DIM), cache_map(2)),
                     pl.BlockSpec((None, None, None, past, V_PAD), cache_map(0))]
        args += [cache[0], cache[0], cache[1]]
    s_total = past + t
    return _call(
        functools.partial(_attn_kernel, past), "mla_attention", (n_steps,),
        in_specs, args,
        pl.BlockSpec((pair, V_DIM), o_map),
        jax.ShapeDtypeStruct((n_rows, heads * V_DIM), BF16),
        ("arbitrary",), bases=(base,),
        scratch=[pltpu.VMEM((tq, s_total), F32), pltpu.VMEM((tq, s_total), F32),
                 pltpu.VMEM((tq, 1), F32), pltpu.VMEM((tq, 1), F32)])


MLSTM_CHUNKS_PER_STEP = 2


def _split3(x):
    hi = x.astype(BF16)
    r = x - hi.astype(F32)
    mid = r.astype(BF16)
    return hi, mid, (r - mid.astype(F32)).astype(BF16)


def _mlstm_kernel(has_init, dh, n_sub, *refs):
    (qf_ref, kf_ref, vf_ref, qb_ref, kb_ref, vb_ref, gf_ref, gb_ref, gtf_ref, gtb_ref,
     brow_ref, bcol_ref) = refs[:12]
    refs = refs[12:]
    if has_init:
        c0_ref, n0_ref, m0_ref = refs[:3]
        refs = refs[3:]
    hf_ref, hb_ref, c_ref, n_ref, m_ref = refs
    heads = MLSTM_HEADS
    n_gate = N_DIR * 2 * heads
    step = pl.program_id(1)

    @pl.when(step == 0)
    def _():
        if has_init:
            c_ref[...] = c0_ref[...]
            n_ref[...] = n0_ref[...]
            m_ref[...] = m0_ref[...]
        else:
            c_ref[...] = jnp.zeros_like(c_ref)
            n_ref[...] = jnp.zeros_like(n_ref)
            m_ref[...] = jnp.zeros_like(m_ref)

    tok0 = lax.broadcasted_iota(jnp.int32, (CHUNK, CHUNK), 0)
    tok1 = lax.broadcasted_iota(jnp.int32, (CHUNK, CHUNK), 1)
    k_scale = dh ** -0.5
    m_all = m_ref[...]
    m_out = m_all
    unit_lane = lax.broadcasted_iota(jnp.int32, m_all.shape, 1)
    units = [(d, h) for d in range(N_DIR) for h in range(heads)]
    for sub in range(n_sub):
        m_all = m_out
        gate = {}
        for d in range(N_DIR):
            g_ref, gt_ref = (gf_ref, gtf_ref) if d == 0 else (gb_ref, gtb_ref)
            chunk_idx = sub if d == 0 else n_sub - 1 - sub
            tok = slice(chunk_idx * CHUNK, (chunk_idx + 1) * CHUNK)
            seen_t = (tok0 <= tok1) if d == 0 else (tok0 >= tok1)
            seen_t_bf = seen_t.astype(BF16)
            seen_bf = ((tok1 <= tok0) if d == 0 else (tok1 >= tok0)).astype(BF16)
            pre_col = g_ref[tok, :n_gate] + brow_ref[...]
            pre_row = gt_ref[:, tok] + bcol_ref[...]
            cum_col = sum(_dot(seen_bf, part) for part in _split3(_log_sigmoid(pre_col)))
            cum_row = sum(_dot(part, seen_t_bf) for part in _split3(_log_sigmoid(pre_row)))
            gate[d] = (tok, seen_t, pre_col, pre_row, cum_col, cum_row)

        st = {}
        for d, h in units:
            tok, seen_t, pre_col, pre_row, cum_col, cum_row = gate[d]
            q_ref, k_ref, vt_ref = (qf_ref, kf_ref, vf_ref) if d == 0 else (qb_ref, kb_ref, vb_ref)
            ci = d * 2 * heads + h
            cf = ci + heads
            sid = d * heads + h
            sl = slice(h * dh, (h + 1) * dh)
            last = CHUNK - 1 if d == 0 else 0
            c_col = pre_col[:, ci:ci + 1] - cum_col[:, cf:cf + 1]
            i_row = pre_row[ci:ci + 1, :]
            b_row = cum_row[cf:cf + 1, :]
            b_end = b_row[:, last:last + 1]
            m_prev = m_all[:, sid:sid + 1]
            a_row = b_row + m_prev
            dmat = jnp.where(seen_t, b_row + c_col, -jnp.inf)
            m_t = jnp.maximum(a_row, jnp.max(dmat, axis=0, keepdims=True))
            q = q_ref[tok, sl]
            k_bf = (k_ref[tok, sl].astype(F32) * k_scale).astype(BF16)
            st[d, h] = dict(tok=tok, sl=sl, sid=sid, i_row=i_row, b_row=b_row, b_end=b_end, m_prev=m_prev,
                            m_t=m_t, w_intra=jnp.exp(dmat - m_t), w_inter=jnp.exp(a_row - m_t),
                            q=q, k_bf=k_bf, v_t=vt_ref[sl, tok], c_prev=c_ref[d, h], n_prev=n_ref[d, h])

        for u in units:
            x = st[u]
            x["s_t"] = _dot_nt(x["k_bf"], x["q"]) * x["w_intra"]
            n_rows = jnp.broadcast_to(x["n_prev"], (PACK_ROWS, dh)).astype(BF16)
            x["cq"] = _dot_nt(jnp.concatenate([x["c_prev"].astype(BF16), n_rows], axis=0), x["q"])

        for d, h in units:
            x = st[d, h]
            h_ref = hf_ref if d == 0 else hb_ref
            num = x["w_inter"] * x["cq"][:dh, :] + _dot(x["v_t"], x["s_t"].astype(BF16))
            den = x["w_inter"] * x["cq"][dh:dh + 1, :] + jnp.sum(x["s_t"], axis=0, keepdims=True)
            h_ref[x["sl"], x["tok"]] = num / jnp.maximum(jnp.abs(den), jnp.exp(-x["m_t"]))

        for d, h in units:
            x = st[d, h]
            g_row = x["b_end"] - x["b_row"] + x["i_row"]
            m_new = jnp.maximum(x["b_end"] + x["m_prev"], jnp.max(g_row, axis=1, keepdims=True))
            w_pos = jnp.exp(g_row - m_new)
            w_carry = jnp.exp(x["b_end"] + x["m_prev"] - m_new)
            w_rows = jnp.broadcast_to(w_pos, (PACK_ROWS, CHUNK)).astype(BF16)
            upd = _dot(jnp.concatenate([(x["v_t"].astype(F32) * w_pos).astype(BF16), w_rows], axis=0), x["k_bf"])
            c_ref[d, h] = w_carry * x["c_prev"] + upd[:dh, :]
            n_ref[d, h] = w_carry * x["n_prev"] + upd[dh:dh + 1, :]
            m_out = jnp.where(unit_lane == x["sid"], m_new, m_out)
    m_ref[...] = m_out


def _mlstm(pb, pb_t, p32, gates_t, gate_b, state, layer, row0, batch, t, gate_blk, dh, bases):
    heads = MLSTM_HEADS
    n_gate = N_DIR * 2 * heads
    depth = gate_b.shape[0]
    n_sub = MLSTM_CHUNKS_PER_STEP if (t // CHUNK) % MLSTM_CHUNKS_PER_STEP == 0 else 1
    span = n_sub * CHUNK
    nc = t // span
    assert row0 % span == 0
    blk0 = row0 // span
    w = heads * dh
    fwd = lambda b, c: blk0 + b * nc + c
    bwd = lambda b, c: blk0 + b * nc + nc - 1 - c

    def tok(col, blk):
        return pl.BlockSpec((span, w), lambda b, c: (blk(b, c), col))

    def feat(blk):
        return pl.BlockSpec((w, span), lambda b, c: (0, blk(b, c)))

    in_specs = [tok(0, fwd), tok(1, fwd), feat(fwd), tok(0, bwd), tok(1, bwd), feat(bwd),
                pl.BlockSpec((span, LANES), lambda b, c: (fwd(b, c), gate_blk)),
                pl.BlockSpec((span, LANES), lambda b, c: (bwd(b, c), gate_blk)),
                pl.BlockSpec((n_gate, span), lambda b, c: (0, fwd(b, c))),
                pl.BlockSpec((n_gate, span), lambda b, c: (0, bwd(b, c))),
                pl.BlockSpec((None, 1, n_gate), lambda b, c: (layer, 0, 0)),
                pl.BlockSpec((None, n_gate, 1), lambda b, c: (layer, 0, 0))]
    args = [pb, pb, pb_t, pb, pb, pb_t, p32, p32, gates_t, gates_t,
            gate_b.reshape(-1, 1, n_gate), gate_b.reshape(-1, n_gate, 1)]
    has_init = state is not None
    state_shapes = [(N_DIR, heads, dh, dh), (N_DIR, heads, 1, dh), (1, N_DIR * heads)]
    if has_init:
        c0, n0, m0 = state
        in_specs += [pl.BlockSpec((None, None) + shp, lambda b, c, z=(0,) * len(shp): (b, layer) + z)
                     for shp in state_shapes]
        args += [c0, n0.reshape((batch, depth) + state_shapes[1]), m0.reshape((batch, depth) + state_shapes[2])]
        st_specs = [pl.BlockSpec((None,) + shp, lambda b, c, z=(0,) * len(shp): (b,) + z) for shp in state_shapes]
        st_shapes = [jax.ShapeDtypeStruct((batch,) + shp, F32) for shp in state_shapes]
        all_bases = ()
    else:
        st_specs = [pl.BlockSpec((None, None) + shp, lambda b, c, z=(0,) * len(shp): (b, layer) + z)
                    for shp in state_shapes]
        st_shapes = [jax.ShapeDtypeStruct((batch, depth) + shp, F32) for shp in state_shapes]
        all_bases = (None, None) + tuple(bases)
    return _call(
        functools.partial(_mlstm_kernel, has_init, dh, n_sub), "mlstm_scan", (batch, nc), in_specs, args,
        [pl.BlockSpec((w, span), lambda b, c: (0, b * nc + c)),
         pl.BlockSpec((w, span), lambda b, c: (0, b * nc + nc - 1 - c))] + st_specs,
        [jax.ShapeDtypeStruct((w, batch * t), F32),
         jax.ShapeDtypeStruct((w, batch * t), F32)] + st_shapes,
        ("parallel", "arbitrary"), bases=all_bases)


def _mlstm_post_kernel(dh, hf_ref, hb_ref, o_ref, w_ref, y_ref):
    tm = hf_ref.shape[1]
    hm = hf_ref[...] + hb_ref[...]
    gate = _sigmoid(o_ref[...].astype(F32))
    w = w_ref[...]
    eye = (lax.broadcasted_iota(jnp.int32, (tm, tm), 0)
           == lax.broadcasted_iota(jnp.int32, (tm, tm), 1)).astype(BF16)
    for h in range(MLSTM_HEADS):
        sl = slice(h * dh, (h + 1) * dh)
        x = hm[sl, :]
        y = x * lax.rsqrt(jnp.mean(x * x, axis=0, keepdims=True) + EPS) * w[sl, :]
        y_t = (gate[sl, :] * y).astype(BF16)
        y_ref[:, sl] = _dot_nt(eye, y_t).astype(y_ref.dtype)


def _mlstm_post(h_f, h_b, pb_t, m_norm_w, layer, n_rows, row0, dh, base):
    w, n = h_f.shape
    tm = min(512, n)
    return _call(
        functools.partial(_mlstm_post_kernel, dh), "mlstm_post", (n // tm,),
        [pl.BlockSpec((w, tm), lambda i: (0, i)),
         pl.BlockSpec((w, tm), lambda i: (0, i)),
         pl.BlockSpec((w, tm), lambda i: (1, row0 // tm + i)),
         pl.BlockSpec((None, w, 1), lambda i: (layer, 0, 0))],
        [h_f, h_b, pb_t, m_norm_w],
        pl.BlockSpec((tm, w), lambda i: (row0 // tm + i, 0)),
        jax.ShapeDtypeStruct((n_rows, w), BF16),
        ("parallel",), bases=(base,))


POOL_TILE = 256


def _pool_bands():
    t = np.arange(POOL_TILE)[:, None]
    bands = np.zeros((POOL_GROUPS, 3, POOL_TILE, POOL_TILE), np.float32)
    for g, win in enumerate(POOL_WINDOWS):
        for part in range(3):
            s = np.arange(POOL_TILE)[None, :] + (part - 1) * POOL_TILE
            bands[g, part] = (s >= t - win // 2) & (s < t - win // 2 + win)
    return jnp.asarray(bands, BF16)


def _pool_kernel(t_seq, gd, up_ref, um_ref, un_ref, band_ref, pw_ref, ps_ref, y_ref):
    j = pl.program_id(1)
    has_prev = (j > 0).astype(F32)
    has_next = (j < pl.num_programs(1) - 1).astype(F32)
    tile = um_ref.shape[0]
    pos = j * tile + lax.broadcasted_iota(jnp.int32, (tile, 1), 0)
    groups = range(len(POOL_WINDOWS))
    cols = [slice(g * gd, (g + 1) * gd) for g in groups]
    acc = [_dot(band_ref[g, 1], um_ref[:, cols[g]])
           + has_prev * _dot(band_ref[g, 0], up_ref[:, cols[g]])
           + has_next * _dot(band_ref[g, 2], un_ref[:, cols[g]]) for g in groups]
    pooled = []
    for g, win in enumerate(POOL_WINDOWS):
        lo = jnp.clip(pos - win // 2, 0, t_seq)
        hi = jnp.clip(pos - win // 2 + win, 0, t_seq)
        pooled.append((acc[g] / (hi - lo).astype(F32) - um_ref[:, cols[g]].astype(F32)).astype(BF16))
    y = [_dot(pooled[g], pw_ref[g]) * ps_ref[:, cols[g]] for g in groups]
    for g in groups:
        y_ref[:, cols[g]] = y[g].astype(y_ref.dtype)


def _pool(pb, bands, pool_w, pool_scale, layer, n_rows, row0, batch, t, base):
    gd = pool_w.shape[-1]
    w = POOL_GROUPS * gd
    tile = POOL_TILE
    assert t % tile == 0
    nt = t // tile
    blk0 = row0 // tile

    def u_spec(shift):
        return pl.BlockSpec((tile, w), lambda b, j: (blk0 + b * nt + jnp.clip(j + shift, 0, nt - 1), 2))

    return _call(
        functools.partial(_pool_kernel, t, gd), "multiscale_pool", (batch, nt),
        [u_spec(-1), u_spec(0), u_spec(1),
         pl.BlockSpec(bands.shape, lambda b, j: (0, 0, 0, 0)),
         pl.BlockSpec((None, POOL_GROUPS, gd, gd), lambda b, j: (layer, 0, 0, 0)),
         pl.BlockSpec((None, 1, w), lambda b, j: (layer, 0, 0))],
        [pb, pb, pb, bands, pool_w, pool_scale],
        pl.BlockSpec((tile, w), lambda b, j: (blk0 + b * nt + j, 0)),
        jax.ShapeDtypeStruct((n_rows, w), BF16),
        ("parallel", "parallel"), bases=(base,))


def _merge_kernel(ya_ref, yb_ref, yc_ref, ga_ref, gb_ref, gc_ref, w_ref, o_ref):
    branch = [_dot(y_ref[...], w_ref[k]) for k, y_ref in enumerate((ya_ref, yb_ref, yc_ref))]
    gates = [_sigmoid(g_ref[...].astype(F32)) for g_ref in (ga_ref, gb_ref, gc_ref)]
    o_ref[...] = (gates[0] * branch[0] + gates[1] * branch[1] + gates[2] * branch[2]).astype(o_ref.dtype)


def _merge(y_a, y_b, y_c, pb, w_branch, rows, layer, gate_col0):
    n, bw = y_a.shape
    d = w_branch.shape[-1]
    tm = rows.tile(1024)
    tn = min(1024, d)
    g0 = gate_col0 // tn
    nd = d // tn

    def gate_spec(k):
        return pl.BlockSpec((tm, tn), lambda i, j: (i, g0 + k * nd + j))

    y_spec = pl.BlockSpec((tm, bw), lambda i, j: (i, 0))
    return _call(
        _merge_kernel, "branch_merge", (n // tm, nd),
        [y_spec, y_spec, y_spec, gate_spec(0), gate_spec(1), gate_spec(2),
         pl.BlockSpec((None, N_BRANCH, bw, tn), lambda i, j: (layer, 0, 0, j))],
        [y_a, y_b, y_c, pb, pb, pb, w_branch],
        pl.BlockSpec((tm, tn), lambda i, j: (i, j)),
        jax.ShapeDtypeStruct((n, d), BF16),
        ("parallel", "arbitrary"))


def _outproj_kernel(m_ref, w_ref, x_ref, g_ref, o_ref):
    o_ref[...] = x_ref[...] + g_ref[...] * _dot(m_ref[...], w_ref[...])


def _outproj(merged, w_out, x, mod, rows, layer):
    n, d = x.shape
    tm = rows.tile(512)
    tn = d
    return _call(
        _outproj_kernel, "mixer_out_proj", (n // tm, d // tn),
        [pl.BlockSpec((tm, d), lambda i, j: (i, 0)),
         pl.BlockSpec((None, d, tn), lambda i, j: (layer, 0, j)),
         pl.BlockSpec((tm, tn), lambda i, j: (i, j)),
         pl.BlockSpec((None, None, None, 1, tn),
                      lambda i, j: (layer, rows.mod_row(i * tm), 5, 0, j))],
        [merged, w_out, x, mod],
        pl.BlockSpec((tm, tn), lambda i, j: (i, j)),
        jax.ShapeDtypeStruct((n, d), F32),
        ("parallel", "arbitrary"))


def _final_norm_kernel(x_ref, w_ref, o_ref):
    o_ref[...] = _rms(x_ref[...], w_ref[...])


def _final_norm(x, w, row0, n_rows):
    d = x.shape[1]
    tm = min(512, n_rows)
    return _call(
        _final_norm_kernel, "final_norm", (n_rows // tm,),
        [pl.BlockSpec((tm, d), lambda i: (row0 // tm + i, 0)),
         pl.BlockSpec((1, d), lambda i: (0, 0))],
        [x, w.reshape(1, d)],
        pl.BlockSpec((tm, d), lambda i: (i, 0)),
        jax.ShapeDtypeStruct((n_rows, d), F32),
        ("parallel",))


def _gate_up_prep_kernel(valid, g_ref, u_ref, og_ref, ou_ref):
    for src, dst in ((g_ref, og_ref), (u_ref, ou_ref)):
        dst[:, :valid] = src[...].astype(BF16)
        if dst.shape[1] > valid:
            dst[:, valid:] = jnp.zeros((dst.shape[0], dst.shape[1] - valid), BF16)


def _gate_up_prep(w_gu, hp):
    depth, n_ffn, d, h2 = w_gu.shape
    h = h2 // 2
    assert h % LANES == 0
    n_rows = depth * n_ffn * d
    tr = 256
    flat = w_gu.reshape(n_rows, h2)
    out = jax.ShapeDtypeStruct((n_rows, hp), BF16)
    w_g, w_u = _call(
        functools.partial(_gate_up_prep_kernel, h), "ffn_gate_up_prep", (n_rows // tr,),
        [pl.BlockSpec((tr, h), lambda r: (r, 0)), pl.BlockSpec((tr, h), lambda r: (r, 1))],
        [flat, flat],
        [pl.BlockSpec((tr, hp), lambda r: (r, 0))] * 2, [out, out], ("parallel",))
    return w_g.reshape(depth, n_ffn, d, hp), w_u.reshape(depth, n_ffn, d, hp)


def _down_prep_kernel(valid, w_ref, o_ref):
    o_ref[:valid, :] = w_ref[...].astype(BF16)
    if o_ref.shape[0] > valid:
        o_ref[valid:, :] = jnp.zeros((o_ref.shape[0] - valid, o_ref.shape[1]), BF16)


def _down_prep(w_down, hp):
    depth, n_ffn, h, d = w_down.shape
    assert h % PACK_ROWS == 0
    td = _lane_tile(d, 256)
    flat = w_down.reshape(depth * n_ffn, h, d)
    out = _call(
        functools.partial(_down_prep_kernel, h), "ffn_down_prep", (depth * n_ffn, d // td),
        [pl.BlockSpec((None, h, td), lambda a, j: (a, 0, j))], [flat],
        pl.BlockSpec((None, hp, td), lambda a, j: (a, 0, j)),
        jax.ShapeDtypeStruct((depth * n_ffn, hp, d), BF16), ("parallel", "parallel"))
    return out.reshape(depth, n_ffn, hp, d)


def _window(ref, start, stop):
    lo = start - start % LANES
    hi = min(_round_up(stop, LANES), ref.shape[1])
    return ref[:, lo:hi][:, start - lo:stop - lo]


def _in_proj_prep_kernel(offs, n_gate, w_ref, small_ref, big_ref, feat_ref):
    tr = w_ref.shape[0]
    k_pe = _window(w_ref, offs[2], offs[3]).astype(BF16)
    quarter = ROPE_DIM // 4
    src = lax.broadcasted_iota(jnp.int32, (ROPE_DIM, ROPE_DIM), 0)
    dst = lax.broadcasted_iota(jnp.int32, (ROPE_DIM, ROPE_DIM), 1)
    swapped_dst = jnp.where((dst // quarter) % 2 == 0, dst + quarter, dst - quarter)
    select = (src == swapped_dst).astype(BF16)
    small_ref[:, :offs[2]] = w_ref[:, :offs[2]].astype(BF16)
    small_ref[:, offs[2]:offs[2] + 2 * ROPE_DIM] = jnp.concatenate(
        [k_pe, _dot(k_pe, select).astype(BF16)], axis=1)
    gates = _window(w_ref, offs[7], offs[8]).astype(BF16)
    small_ref[:, offs[2] + 2 * ROPE_DIM:] = jnp.concatenate(
        [gates, jnp.zeros((tr, LANES - n_gate), BF16)], axis=1)
    qk = offs[5] - offs[3]
    big_ref[:, :qk] = _window(w_ref, offs[3], offs[5]).astype(BF16)
    big_ref[:, qk:] = _window(w_ref, offs[8], offs[10]).astype(BF16)
    vo = _window(w_ref, offs[5], offs[7]).astype(BF16)
    cw = 2 * LANES
    eye = (lax.broadcasted_iota(jnp.int32, (cw, cw), 0)
           == lax.broadcasted_iota(jnp.int32, (cw, cw), 1)).astype(BF16)
    for c in range(0, vo.shape[1], cw):
        feat_ref[c:c + cw, :] = _dot_nt(eye, vo[:, c:c + cw]).astype(BF16)


def _in_proj_prep(w_in, offs, n_gate):
    depth, d, cols = w_in.shape
    offs = tuple(int(o) for o in offs)
    small_cols = offs[2] + 2 * ROPE_DIM + LANES
    big_cols = (offs[5] - offs[3]) + (offs[10] - offs[8])
    feat_rows = offs[7] - offs[5]
    assert offs[2] % LANES == 0 and feat_rows % (2 * LANES) == 0 and big_cols % LANES == 0
    tr = 256
    return _call(
        functools.partial(_in_proj_prep_kernel, offs, n_gate), "in_proj_weight_prep", (depth, d // tr),
        [pl.BlockSpec((None, tr, cols), lambda l, i: (l, i, 0))], [w_in],
        [pl.BlockSpec((None, tr, small_cols), lambda l, i: (l, i, 0)),
         pl.BlockSpec((None, tr, big_cols), lambda l, i: (l, i, 0)),
         pl.BlockSpec((None, feat_rows, tr), lambda l, i: (l, 0, i))],
        [jax.ShapeDtypeStruct((depth, d, small_cols), BF16),
         jax.ShapeDtypeStruct((depth, d, big_cols), BF16),
         jax.ShapeDtypeStruct((depth, feat_rows, d), BF16)],
        ("parallel", "parallel"))


def _rope_swap_index():
    quarter = ROPE_DIM // 4
    idx = np.arange(ROPE_DIM).reshape(2, 2, quarter)
    return idx[:, ::-1, :].reshape(-1)


def _rope_tables(t):
    pos = jnp.arange(t)
    row = (pos // GRID_W).astype(F32)
    col = (pos % GRID_W).astype(F32)
    n_freq = ROPE_DIM // 4
    inv_freq = jnp.power(ROPE_BASE, -jnp.arange(n_freq, dtype=F32) / n_freq)
    ang_r = row[:, None] * inv_freq
    ang_c = col[:, None] * inv_freq
    cos = jnp.concatenate([jnp.cos(ang_r), jnp.cos(ang_r), jnp.cos(ang_c), jnp.cos(ang_c)], axis=-1)
    sin = jnp.concatenate([-jnp.sin(ang_r), jnp.sin(ang_r), -jnp.sin(ang_c), jnp.sin(ang_c)], axis=-1)
    return cos, sin


def kernel(x_prompt, x_sample, cache_ckv, cache_kpe, state_C, state_n, state_m, c, c_ctx, w_mod, b_mod, norm_w, ffn_w_gu, ffn_w_down, w_in, q_norm_w, kv_norm_w, w_uq, w_ukv, mlstm_gate_b, mlstm_norm_w, pool_w, pool_scale, w_branch, w_out, final_norm_w):
    batch, seq, d = x_prompt.shape
    dec_batch, dec_seq, _ = x_sample.shape
    depth = w_mod.shape[0]
    q_rank, kv_rank = q_norm_w.shape[1], kv_norm_w.shape[1]
    heads = MLA_HEADS
    mw = mlstm_norm_w.shape[1]
    dh = mw // MLSTM_HEADS
    pw = pool_scale.shape[1]
    ffn_h = ffn_w_down.shape[2]
    n_gate = N_DIR * 2 * MLSTM_HEADS
    assert mw == pw == w_branch.shape[2] == heads * V_DIM
    rows = _Rows(batch * seq, seq, dec_batch * dec_seq, dec_seq)
    n = rows.n

    hp = _round_up(ffn_h, 512)
    w_g, w_u = _gate_up_prep(ffn_w_gu, hp)
    wdn = _down_prep(ffn_w_down, hp)

    sizes = (q_rank, kv_rank, ROPE_DIM, mw, mw, mw, mw, n_gate, pw, N_BRANCH * d)
    offs = np.concatenate([[0], np.cumsum(sizes)])
    swap = _rope_swap_index()
    small_cols = q_rank + kv_rank + 2 * ROPE_DIM + LANES
    w_in16 = jnp.pad(w_in.astype(BF16), ((0, 0), (0, 0), (0, _round_up(w_in.shape[2], LANES) - w_in.shape[2])))
    w_small, w_big, w_feat = _in_proj_prep(w_in16, offs, n_gate)
    gate_blk = (q_rank + kv_rank + 2 * ROPE_DIM) // LANES
    gate_col0 = 3 * mw

    wq4 = w_uq.reshape(depth, q_rank, heads, QK_DIM)
    wq_pe = wq4[..., NOPE_DIM:]
    wq = jnp.concatenate([wq4[..., :NOPE_DIM].reshape(depth, q_rank, -1),
                          wq_pe.reshape(depth, q_rank, -1),
                          wq_pe[..., swap].reshape(depth, q_rank, -1)], axis=-1).astype(BF16)
    wkv4 = w_ukv.reshape(depth, kv_rank, heads, NOPE_DIM + V_DIM)
    wkv = jnp.concatenate([wkv4[..., :NOPE_DIM].reshape(depth, kv_rank, -1),
                           wkv4[..., NOPE_DIM:].reshape(depth, kv_rank, -1)], axis=-1).astype(BF16)
    wbr = w_branch.astype(BF16)
    wout = w_out.astype(BF16)
    pwb = pool_w.astype(BF16)
    norm_w4 = norm_w.reshape(depth, 3, 1, d)
    qnw = q_norm_w.reshape(depth, 1, q_rank)
    kvnw = kv_norm_w.reshape(depth, 1, kv_rank)
    mnw = mlstm_norm_w.reshape(depth, mw, 1)
    psc = pool_scale.reshape(depth, 1, pw)
    bands = _pool_bands()
    rope_tabs = _rope_tables(dec_seq)

    cond = jnp.concatenate([c_ctx[None, :], c, jnp.zeros((COND_ROWS - 1 - dec_batch, d), F32)], axis=0)
    mod = _mod_all(cond, w_mod, b_mod).reshape(depth, COND_ROWS, N_MOD, 1, d)

    cache_kv = _cache_kv(cache_ckv, cache_kpe, wkv)

    x = (x_prompt.reshape(rows.n_ctx, d), x_sample.reshape(rows.n_lat, d))
    new_cache = (None, None)
    new_state = (None, None, None)
    for l in range(depth):
        x, h_mix = _ffn(x, mod, norm_w4, w_g, w_u, wdn, rows, l, 0)

        p32 = _inproj(h_mix, w_small, rows, l, F32, small_cols)
        pb = _inproj(h_mix, w_big, rows, l, BF16, _lane_tile(w_big.shape[2], 1024))
        pb_t = _inproj_t(h_mix, w_feat, rows, l)
        gates_t = p32[:, gate_blk * LANES:gate_blk * LANES + n_gate].T

        q_c, k_c, v_c, *new_cache = _mla_prep(p32, qnw, kvnw, wq, wkv, l, 0, batch, seq, None, new_cache)
        q_s, k_s, v_s = _mla_prep(p32, qnw, kvnw, wq, wkv, l, rows.n_ctx, dec_batch, dec_seq, rope_tabs, None)
        y_a = _attention(q_c, k_c, v_c, None, l, n, 0, None)
        y_a = _attention(q_s, k_s, v_s, cache_kv, l, n, rows.n_ctx, y_a)

        hf_c, hb_c, *new_state = _mlstm(pb, pb_t, p32, gates_t, mlstm_gate_b, None, l, 0, batch, seq,
                                        gate_blk, dh, new_state)
        hf_s, hb_s, _, _, _ = _mlstm(pb, pb_t, p32, gates_t, mlstm_gate_b, (state_C, state_n, state_m), l,
                                     rows.n_ctx, dec_batch, dec_seq, gate_blk, dh, None)
        y_b = _mlstm_post(hf_c, hb_c, pb_t, mnw, l, n, 0, dh, None)
        y_b = _mlstm_post(hf_s, hb_s, pb_t, mnw, l, n, rows.n_ctx, dh, y_b)

        y_c = _pool(pb, bands, pwb, psc, l, n, 0, batch, seq, None)
        y_c = _pool(pb, bands, pwb, psc, l, n, rows.n_ctx, dec_batch, dec_seq, y_c)

        merged = _merge(y_a, y_b, y_c, pb, wbr, rows, l, gate_col0)
        x = _outproj(merged, wout, x, mod, rows, l)
        x = _ffn(x, mod, norm_w4, w_g, w_u, wdn, rows, l, 1)

    y_prompt = _final_norm(x, final_norm_w, 0, rows.n_ctx).reshape(batch, seq, d)
    y_sample = _final_norm(x, final_norm_w, rows.n_ctx, rows.n_lat).reshape(dec_batch, dec_seq, d)
    new_c, new_n, new_m = new_state
    return (y_prompt, y_sample, new_cache[0], new_cache[1], new_c,
            new_n.reshape(batch, depth, N_DIR, MLSTM_HEADS, dh),
            new_m.reshape(batch, depth, N_DIR, MLSTM_HEADS))
```
